```python
import math
import jax
import jax.numpy as jnp
from jax import lax
import numpy as np

D_MODEL = 2048
BATCH = 16
SEQ = 2048
DEPTH = 1

BLK = 128
NEG_INF = -1e30
RMS_EPS = 1e-6
SUBLN_EPS = 1e-5
N_BUCKETS = 32
MAX_DISTANCE = 2048
A_HEADS = 8
A_HEAD_DIM = 128
DILATED_GROUPS = ((128, 1), (512, 4), (2048, 16))
N_DGROUPS = 3
B_HEADS_PER_GROUP = 8
B_HEAD_DIM = 64
N_BIAS_HEADS = A_HEADS + N_DGROUPS * B_HEADS_PER_GROUP
A_QK_WIDTH = A_HEADS * 2 * A_HEAD_DIM
A_V_WIDTH = A_HEADS * 2 * A_HEAD_DIM
B_WIDTH = N_DGROUPS * B_HEADS_PER_GROUP * B_HEAD_DIM
IN_WIDTH = 2 * A_QK_WIDTH + A_V_WIDTH + 3 * B_WIDTH
A_OUT_WIDTH = A_V_WIDTH
B_OUT_WIDTH = B_HEADS_PER_GROUP * B_HEAD_DIM
N_EXPERT_GROUPS = 4
EXPERTS_PER_GROUP = 8
N_EXPERTS = N_EXPERT_GROUPS * EXPERTS_PER_GROUP
TOP_K_FINE = 2
D_EXPERT = D_MODEL // 2
PLE_DIM = 256

kernel_name = 'hybrid_diffattn_dilated_hmoe_block'


def rmsnorm(x, g, eps=RMS_EPS):
    xf = x.astype(jnp.float32)
    y = xf * lax.rsqrt(jnp.mean(xf * xf, axis=-1, keepdims=True) + eps)
    return (y * g.astype(jnp.float32)).astype(x.dtype)


def t5_bucket(dist):
    dist = jnp.maximum(dist, 0)
    max_exact = N_BUCKETS // 2
    d_f = jnp.maximum(dist, 1).astype(jnp.float32)
    large = max_exact + (jnp.log(d_f / max_exact) / math.log(MAX_DISTANCE / max_exact)
                         * (N_BUCKETS - max_exact)).astype(jnp.int32)
    large = jnp.minimum(large, N_BUCKETS - 1)
    return jnp.where(dist < max_exact, dist, large)


def diff_attention(q, k, v, lam, bias_tab):
    S_ = q.shape[1]
    scale = A_HEAD_DIM ** -0.5
    outs = []
    for blk in range(S_ // BLK):
        s0 = blk * BLK
        kv_len = s0 + BLK
        qb = q[:, s0:s0 + BLK]
        kb = k[:, :kv_len]
        vb = v[:, :kv_len]
        s = jnp.einsum('bqhmd,bkhmd->bhmqk', qb, kb).astype(jnp.float32) * scale
        dist = (s0 + jnp.arange(BLK))[:, None] - jnp.arange(kv_len)[None, :]
        bias = bias_tab[t5_bucket(dist)].astype(jnp.float32).transpose(2, 0, 1)
        s = jnp.where((dist >= 0)[None, None, None], s + bias[None, :, None], NEG_INF)
        a = jax.nn.softmax(s, axis=-1)
        w = a[:, :, 0] - lam * a[:, :, 1]
        outs.append(jnp.einsum('bhqk,bkhe->bqhe', w.astype(v.dtype), vb))
    return jnp.concatenate(outs, axis=1)


def dilated_group(q, k, v, window, dilation, bias_tab):
    Bsz, S_, H, E = q.shape
    band = window // dilation
    L = S_ // dilation
    nq = -(-L // BLK)
    Lq = nq * BLK

    def streams(a):
        return a.reshape(Bsz, L, dilation, H, E).transpose(0, 2, 1, 3, 4)

    qs = jnp.pad(streams(q), ((0, 0), (0, 0), (0, Lq - L), (0, 0), (0, 0)))
    qs = qs.reshape(Bsz, dilation, nq, BLK, H, E)

    def windows(a):
        ap = jnp.pad(streams(a), ((0, 0), (0, 0), (BLK, Lq - L), (0, 0), (0, 0)))
        ap = ap.reshape(Bsz, dilation, nq + 1, BLK, H, E)
        return jnp.concatenate([ap[:, :, :-1], ap[:, :, 1:]], axis=3)

    kw = windows(k)
    vw = windows(v)
    s = jnp.einsum('bcnihe,bcnjhe->bcnhij', qs, kw).astype(jnp.float32) * (E ** -0.5)
    i = jnp.arange(BLK)[:, None]
    j = jnp.arange(2 * BLK)[None, :]
    rel = i - j + BLK
    kj = jnp.arange(nq)[:, None, None] * BLK + j[None] - BLK
    valid = (rel >= 0)[None] & (rel <= band)[None] & (kj >= 0) & (kj < L)
    bias = bias_tab[t5_bucket(rel * dilation)].astype(jnp.float32).transpose(2, 0, 1)
    s = jnp.where(valid[None, None, :, None], s + bias[None, None, None], NEG_INF)
    m = jnp.max(s, axis=-1, keepdims=True)
    e = jnp.exp(s - m)
    den = jnp.sum(e, axis=-1, keepdims=True)
    pr = e / den
    lse = (m + jnp.log(den))[..., 0]
    o = jnp.einsum('bcnhij,bcnjhe->bcnihe', pr.astype(v.dtype), vw)
    o = o.reshape(Bsz, dilation, Lq, H, E)[:, :, :L]
    o = o.transpose(0, 2, 1, 3, 4).reshape(Bsz, S_, H, E)
    lse = lse.transpose(0, 1, 2, 4, 3).reshape(Bsz, dilation, Lq, H)[:, :, :L]
    lse = lse.transpose(0, 2, 1, 3).reshape(Bsz, S_, H)
    return o, lse


def hier_moe(t, w_coarse, w_fine, w1, w3, w2):
    T, D = t.shape
    coarse = (t @ w_coarse).astype(jnp.float32)
    p_group = jax.nn.softmax(coarse, axis=-1)
    g_sel = jnp.argmax(coarse, axis=-1)
    pg_sel = jnp.take_along_axis(p_group, g_sel[:, None], axis=1)[:, 0]
    fine = jnp.einsum('td,gde->tge', t, w_fine).astype(jnp.float32)
    fine = jnp.take_along_axis(fine, g_sel[:, None, None], axis=1)[:, 0]
    top_v, top_i = lax.top_k(fine, TOP_K_FINE)
    gate = pg_sel[:, None] * jax.nn.softmax(top_v, axis=-1)
    expert_id = (g_sel[:, None] * EXPERTS_PER_GROUP + top_i).astype(jnp.int32)

    TK = T * TOP_K_FINE
    flat_e = expert_id.reshape(TK)
    flat_tok = jnp.repeat(jnp.arange(T, dtype=jnp.int32), TOP_K_FINE)
    flat_w = gate.reshape(TK).astype(t.dtype)
    order = jnp.argsort(flat_e)
    se = flat_e[order]
    counts = jnp.bincount(flat_e, length=N_EXPERTS)
    starts = jnp.cumsum(counts) - counts
    pcounts = ((counts + BLK - 1) // BLK) * BLK
    pends = jnp.cumsum(pcounts)
    pstarts = pends - pcounts
    dest = pstarts[se] + (jnp.arange(TK) - starts[se])
    n_blocks = TK // BLK + N_EXPERTS
    n_rows = n_blocks * BLK
    row_tok = jnp.full((n_rows,), T, jnp.int32).at[dest].set(flat_tok[order])
    row_w = jnp.zeros((n_rows,), t.dtype).at[dest].set(flat_w[order])
    blk_e = jnp.searchsorted(pends, jnp.arange(n_blocks) * BLK, side='right')
    blk_e = jnp.minimum(blk_e, N_EXPERTS - 1).astype(jnp.int32)
    t_pad = jnp.concatenate([t, jnp.zeros((1, D), t.dtype)], axis=0)
    xin = t_pad[row_tok].reshape(n_blocks, BLK, D)

    def expert_block(args):
        xb, e = args
        hdn = jax.nn.silu(xb @ w1[e]) * (xb @ w3[e])
        return hdn @ w2[e]

    yb = lax.map(expert_block, (xin, blk_e)).reshape(n_rows, D)
    y = jax.ops.segment_sum(yb * row_w[:, None], row_tok, num_segments=T + 1)[:T]
    return y


def setup_inputs(seed: int = 0) -> dict:
    key = jax.random.key(seed)
    ks = jax.random.split(key, 24)
    f32 = jnp.float32

    def nrm(k, shape, scale):
        return jax.random.normal(k, shape, f32) * scale

    def gain(k, shape):
        return 1.0 + 0.01 * jax.random.normal(k, shape, f32)

    return {
        'x': nrm(ks[0], (BATCH, SEQ, D_MODEL), 1.0),
        'p': nrm(ks[1], (DEPTH, BATCH, SEQ, PLE_DIM), 1.0),
        'rel_bias': nrm(ks[2], (N_BUCKETS, N_BIAS_HEADS), 0.2),
        'norm_mix_g': gain(ks[3], (DEPTH, D_MODEL)),
        'w_in': nrm(ks[4], (DEPTH, D_MODEL, IN_WIDTH), D_MODEL ** -0.5),
        'w_gate': nrm(ks[5], (DEPTH, D_MODEL, 2 * D_MODEL), D_MODEL ** -0.5),
        'lambda_q1': nrm(ks[6], (DEPTH, A_HEAD_DIM), 0.1),
        'lambda_k1': nrm(ks[7], (DEPTH, A_HEAD_DIM), 0.1),
        'lambda_q2': nrm(ks[8], (DEPTH, A_HEAD_DIM), 0.1),
        'lambda_k2': nrm(ks[9], (DEPTH, A_HEAD_DIM), 0.1),
        'subln_g': gain(ks[10], (DEPTH, 2 * A_HEAD_DIM)),
        'w_proj_a': nrm(ks[11], (DEPTH, A_OUT_WIDTH, D_MODEL), A_OUT_WIDTH ** -0.5),
        'w_proj_b': nrm(ks[12], (DEPTH, B_OUT_WIDTH, D_MODEL), B_OUT_WIDTH ** -0.5),
        'w_out': nrm(ks[13], (DEPTH, D_MODEL, D_MODEL), D_MODEL ** -0.5),
        'norm_ffn_g': gain(ks[14], (DEPTH, D_MODEL)),
        'w_coarse': nrm(ks[15], (DEPTH, D_MODEL, N_EXPERT_GROUPS), D_MODEL ** -0.5),
        'w_fine': nrm(ks[16], (DEPTH, N_EXPERT_GROUPS, D_MODEL, EXPERTS_PER_GROUP), D_MODEL ** -0.5),
        'w1': nrm(ks[17], (DEPTH, N_EXPERTS, D_MODEL, D_EXPERT), D_MODEL ** -0.5),
        'w3': nrm(ks[18], (DEPTH, N_EXPERTS, D_MODEL, D_EXPERT), D_MODEL ** -0.5),
        'w2': nrm(ks[19], (DEPTH, N_EXPERTS, D_EXPERT, D_MODEL), D_EXPERT ** -0.5),
        'norm_ple_g': gain(ks[20], (DEPTH, D_MODEL)),
        'w_ple_gate': nrm(ks[21], (DEPTH, D_MODEL, D_MODEL), D_MODEL ** -0.5),
        'w_ple_proj': nrm(ks[22], (DEPTH, PLE_DIM, D_MODEL), PLE_DIM ** -0.5),
        'final_norm_g': gain(ks[23], (D_MODEL,)),
    }


def reference(x, p, rel_bias, norm_mix_g, w_in, w_gate, lambda_q1, lambda_k1, lambda_q2,
              lambda_k2, subln_g, w_proj_a, w_proj_b, w_out, norm_ffn_g, w_coarse, w_fine,
              w1, w3, w2, norm_ple_g, w_ple_gate, w_ple_proj, final_norm_g):
    Bsz, S_, D = x.shape
    bias_a = rel_bias[:, :A_HEADS]
    for layer in range(DEPTH):
        h = rmsnorm(x, norm_mix_g[layer])
        z = h @ w_in[layer]
        o0 = 0
        qa = z[..., o0:o0 + A_QK_WIDTH].reshape(Bsz, S_, A_HEADS, 2, A_HEAD_DIM); o0 += A_QK_WIDTH
        ka = z[..., o0:o0 + A_QK_WIDTH].reshape(Bsz, S_, A_HEADS, 2, A_HEAD_DIM); o0 += A_QK_WIDTH
        va = z[..., o0:o0 + A_V_WIDTH].reshape(Bsz, S_, A_HEADS, 2 * A_HEAD_DIM); o0 += A_V_WIDTH
        qb = z[..., o0:o0 + B_WIDTH].reshape(Bsz, S_, N_DGROUPS, B_HEADS_PER_GROUP, B_HEAD_DIM); o0 += B_WIDTH
        kb = z[..., o0:o0 + B_WIDTH].reshape(Bsz, S_, N_DGROUPS, B_HEADS_PER_GROUP, B_HEAD_DIM); o0 += B_WIDTH
        vb = z[..., o0:o0 + B_WIDTH].reshape(Bsz, S_, N_DGROUPS, B_HEADS_PER_GROUP, B_HEAD_DIM)

        lam_init = 0.8 - 0.6 * math.exp(-0.3 * layer)
        lam = (jnp.exp(jnp.sum(lambda_q1[layer] * lambda_k1[layer]))
               - jnp.exp(jnp.sum(lambda_q2[layer] * lambda_k2[layer])) + lam_init).astype(jnp.float32)
        oa = diff_attention(qa, ka, va, lam, bias_a)
        oa = (rmsnorm(oa, subln_g[layer], SUBLN_EPS) * (1.0 - lam_init)).reshape(Bsz, S_, A_OUT_WIDTH)

        o_list = []
        lse_list = []
        for g, (window, dilation) in enumerate(DILATED_GROUPS):
            c0 = A_HEADS + g * B_HEADS_PER_GROUP
            og, lg = dilated_group(qb[:, :, g], kb[:, :, g], vb[:, :, g], window, dilation,
                                   rel_bias[:, c0:c0 + B_HEADS_PER_GROUP])
            o_list.append(og)
            lse_list.append(lg)
        alpha = jax.nn.softmax(jnp.stack(lse_list, axis=0), axis=0)
        ob = jnp.sum(alpha[..., None].astype(x.dtype) * jnp.stack(o_list, axis=0), axis=0)
        ob = ob.reshape(Bsz, S_, B_OUT_WIDTH)

        gates = jax.nn.sigmoid(h @ w_gate[layer])
        merged = gates[..., :D] * (oa @ w_proj_a[layer]) + gates[..., D:] * (ob @ w_proj_b[layer])
        x = x + merged @ w_out[layer]

        h2 = rmsnorm(x, norm_ffn_g[layer]).reshape(Bsz * S_, D)
        y = hier_moe(h2, w_coarse[layer], w_fine[layer], w1[layer], w3[layer], w2[layer])
        x = x + y.reshape(Bsz, S_, D)

        ple_gate = jax.nn.sigmoid(rmsnorm(x, norm_ple_g[layer]) @ w_ple_gate[layer])
        x = x + ple_gate * (p[layer] @ w_ple_proj[layer])
    return rmsnorm(x, final_norm_g)
```

```python
import functools
import math

import jax
import jax.numpy as jnp
from jax import lax
from jax.experimental import pallas as pl
from jax.experimental.pallas import tpu as pltpu

F32 = jnp.float32
BF16 = jnp.bfloat16

BLK = 128
NEG_INF = -1e30
RMS_EPS = 1e-6
SUBLN_EPS = 1e-5
N_BUCKETS = 32
MAX_DISTANCE = 2048
A_HEADS = 8
A_HEAD_DIM = 128
DILATED_GROUPS = ((128, 1), (512, 4), (2048, 16))
N_DGROUPS = 3
B_HEADS = 8
B_HEAD_DIM = 64
N_EXPERT_GROUPS = 4
EXPERTS_PER_GROUP = 8
N_EXPERTS = 32
TOP_K = 2

LANES = 128
VMEM_LIMIT = 56 * 1024 * 1024
ATT_BLK = 256
PERM_BLK = 256
GATE_COLS = 4096
PROJ_TN = 512
PROJ_TM = 1024
ROW_TM = 256
EXP_TM = 256
DISPATCH_TOK = 512
DMA_RING = 16
NORM_CHUNK = 128


def _t5_thresholds():
    max_exact = N_BUCKETS // 2
    out = []
    for k in range(1, N_BUCKETS - max_exact):
        out.append(int(math.ceil(max_exact * (MAX_DISTANCE / max_exact) ** (k / (N_BUCKETS - max_exact)))))
    return tuple(out)


T5_THRESHOLDS = _t5_thresholds()


def _cparams(sem, vmem=VMEM_LIMIT):
    return pltpu.CompilerParams(dimension_semantics=sem, vmem_limit_bytes=vmem)


def _mult(x, m):
    return x if isinstance(x, int) else pl.multiple_of(x, m)


def _sigmoid(x):
    return 1.0 / (1.0 + jnp.exp(-x))


def _dot(a, b, **kw):
    return jnp.dot(a, b, preferred_element_type=F32, **kw)


def _dot_nt(a, b, **kw):
    return lax.dot_general(a, b, (((1,), (1,)), ((), ())), preferred_element_type=F32, **kw)


def _lam_kernel(q1_ref, k1_ref, q2_ref, k2_ref, o_ref, *, lam_init):
    s1 = jnp.sum(q1_ref[...] * k1_ref[...], axis=-1, keepdims=True)
    s2 = jnp.sum(q2_ref[...] * k2_ref[...], axis=-1, keepdims=True)
    o_ref[...] = jnp.exp(s1) - jnp.exp(s2) + lam_init


def _lam_call(lq1, lk1, lq2, lk2, lam_init):
    return pl.pallas_call(
        functools.partial(_lam_kernel, lam_init=lam_init),
        out_shape=jax.ShapeDtypeStruct((1, 1), F32),
        name="lam",
    )(lq1, lk1, lq2, lk2)


def _bias_tile_kernel(tab_ref, o_ref, *, tq, tk, off_mult, off_add, dil, max_rel, head0):
    h = pl.program_id(0)
    n = pl.program_id(1)
    i = lax.broadcasted_iota(jnp.int32, (tq, tk), 0)
    j = lax.broadcasted_iota(jnp.int32, (tq, tk), 1)
    rel = i - j + (n * off_mult + off_add)
    dist = rel * dil
    large = jnp.full((tq, tk), N_BUCKETS // 2, jnp.int32)
    for thr in T5_THRESHOLDS:
        large = large + jnp.where(dist >= thr, 1, 0)
    bucket = jnp.where(dist < N_BUCKETS // 2, dist, large)
    acc = jnp.zeros((tq, tk), F32)
    for b in range(N_BUCKETS):
        acc = jnp.where(bucket == b, tab_ref[b, head0 + h], acc)
    valid = jnp.where(rel >= 0, jnp.where(rel <= max_rel, 1, 0), 0)
    o_ref[0, 0] = jnp.where(valid == 1, acc, NEG_INF)


def _bias_tiles_call(rel_bias, *, n_heads, n_off, tq, tk, off_mult, off_add, dil, max_rel, head0, name):
    kern = functools.partial(_bias_tile_kernel, tq=tq, tk=tk, off_mult=off_mult, off_add=off_add,
                             dil=dil, max_rel=max_rel, head0=head0)
    return pl.pallas_call(
        kern,
        grid=(n_heads, n_off),
        in_specs=[pl.BlockSpec(memory_space=pltpu.SMEM)],
        out_specs=pl.BlockSpec((1, 1, tq, tk), lambda h, n: (h, n, 0, 0)),
        out_shape=jax.ShapeDtypeStruct((n_heads, n_off, tq, tk), F32),
        compiler_params=_cparams(("parallel", "parallel")),
        name=name,
    )(rel_bias)


def _rmsnorm_rows(x_ref, g_ref, out_ref, eps):
    rows = x_ref.shape[0]
    g = g_ref[...]

    def body(c, carry):
        r0 = pl.multiple_of(c * NORM_CHUNK, NORM_CHUNK)
        x = x_ref[pl.ds(r0, NORM_CHUNK), :]
        ms = jnp.mean(x * x, axis=-1, keepdims=True)
        out_ref[pl.ds(r0, NORM_CHUNK), :] = ((x * lax.rsqrt(ms + eps)) * g).astype(out_ref.dtype)
        return carry

    lax.fori_loop(0, rows // NORM_CHUNK, body, 0)


def _rmsnorm_val(x, g, eps):
    ms = jnp.mean(x * x, axis=-1, keepdims=True)
    return (x * lax.rsqrt(ms + eps)) * g


def _inproj_kernel(x_ref, g_ref, w_ref, o_ref, h_ref, *, n_gate_blocks):
    j = pl.program_id(1)

    @pl.when(j == 0)
    def _():
        _rmsnorm_rows(x_ref, g_ref, h_ref, RMS_EPS)

    acc = _dot(h_ref[...], w_ref[...])

    @pl.when(j < n_gate_blocks)
    def _():
        o_ref[...] = _sigmoid(acc).astype(o_ref.dtype)

    @pl.when(j >= n_gate_blocks)
    def _():
        o_ref[...] = acc.astype(o_ref.dtype)


def _inproj_call(x2d, g, w_cat):
    t, d = x2d.shape
    n = w_cat.shape[1]
    tm = min(PROJ_TM, t)
    return pl.pallas_call(
        functools.partial(_inproj_kernel, n_gate_blocks=GATE_COLS // PROJ_TN),
        grid=(t // tm, n // PROJ_TN),
        in_specs=[
            pl.BlockSpec((tm, d), lambda i, j: (i, 0)),
            pl.BlockSpec((1, d), lambda i, j: (0, 0)),
            pl.BlockSpec((d, PROJ_TN), lambda i, j: (0, j)),
        ],
        out_specs=pl.BlockSpec((tm, PROJ_TN), lambda i, j: (i, j)),
        out_shape=jax.ShapeDtypeStruct((t, n), BF16),
        scratch_shapes=[pltpu.VMEM((tm, d), BF16)],
        compiler_params=_cparams(("parallel", "arbitrary")),
        name="inproj",
    )(x2d, g, w_cat)


def _diff_attn_kernel(lam_ref, q_ref, k_ref, v_ref, bias_ref, g_ref, o_ref, acc1, acc2, *, out_scale):
    qi = pl.program_id(2)
    scale = A_HEAD_DIM ** -0.5
    q1 = q_ref[:, :A_HEAD_DIM]
    q2 = q_ref[:, A_HEAD_DIM:]
    acc1[...] = jnp.zeros_like(acc1)
    acc2[...] = jnp.zeros_like(acc2)

    def one_map(q, kb, vb, bias, m, l, acc):
        s = _dot_nt(q, kb) * scale + bias
        m_new = jnp.maximum(m, jnp.max(s, axis=-1, keepdims=True))
        alpha = jnp.exp(m - m_new)
        p = jnp.exp(s - m_new)
        l_new = alpha * l + jnp.sum(p, axis=-1, keepdims=True)
        acc[...] = acc[...] * alpha + _dot(p.astype(BF16), vb)
        return m_new, l_new

    def body(ki, carry):
        m1, l1, m2, l2 = carry
        k0 = pl.multiple_of(ki * ATT_BLK, ATT_BLK)
        kb = k_ref[pl.ds(k0, ATT_BLK), :]
        vb = v_ref[pl.ds(k0, ATT_BLK), :]
        bias = bias_ref[0, qi - ki]
        m1, l1 = one_map(q1, kb[:, :A_HEAD_DIM], vb, bias, m1, l1, acc1)
        m2, l2 = one_map(q2, kb[:, A_HEAD_DIM:], vb, bias, m2, l2, acc2)
        return m1, l1, m2, l2

    minf = jnp.full((ATT_BLK, 1), -jnp.inf, F32)
    zero = jnp.zeros((ATT_BLK, 1), F32)
    m1, l1, m2, l2 = lax.fori_loop(0, qi + 1, body, (minf, zero, minf, zero))
    w = acc1[...] / l1 - lam_ref[0, 0] * (acc2[...] / l2)
    o_ref[...] = (_rmsnorm_val(w, g_ref[...], SUBLN_EPS) * out_scale).astype(o_ref.dtype)


def _diff_attn_call(zg, lam, bias_tiles, subln_g, *, batch, seq, out_scale):
    t = batch * seq
    nq = seq // ATT_BLK
    width = 2 * A_HEAD_DIM
    q_col = GATE_COLS // width
    k_col = q_col + A_HEADS
    v_col = k_col + A_HEADS
    return pl.pallas_call(
        functools.partial(_diff_attn_kernel, out_scale=out_scale),
        grid=(A_HEADS, batch, nq),
        in_specs=[
            pl.BlockSpec(memory_space=pltpu.SMEM),
            pl.BlockSpec((ATT_BLK, width), lambda h, b, i: (b * nq + i, q_col + h)),
            pl.BlockSpec((seq, width), lambda h, b, i: (b, k_col + h)),
            pl.BlockSpec((seq, width), lambda h, b, i: (b, v_col + h)),
            pl.BlockSpec((1, nq, ATT_BLK, ATT_BLK), lambda h, b, i: (h, 0, 0, 0)),
            pl.BlockSpec((1, width), lambda h, b, i: (0, 0)),
        ],
        out_specs=pl.BlockSpec((ATT_BLK, width), lambda h, b, i: (b * nq + i, h)),
        out_shape=jax.ShapeDtypeStruct((t, A_HEADS * width), BF16),
        scratch_shapes=[pltpu.VMEM((ATT_BLK, width), F32), pltpu.VMEM((ATT_BLK, width), F32)],
        compiler_params=_cparams(("parallel", "parallel", "arbitrary")),
        name="diff_attn",
    )(lam, zg, zg, zg, bias_tiles, subln_g)


def _perm_matrix(dil, inverse):
    w = PERM_BLK // dil
    shift = w.bit_length() - 1
    a = lax.broadcasted_iota(jnp.int32, (PERM_BLK, PERM_BLK), 0)
    b = lax.broadcasted_iota(jnp.int32, (PERM_BLK, PERM_BLK), 1)
    dst, src = (b, a) if inverse else (a, b)
    c = lax.shift_right_logical(dst, shift)
    ll = lax.bitwise_and(dst, w - 1)
    return jnp.where(src == ll * dil + c, jnp.float32(1), jnp.float32(0))


def _deinterleave(src_ref, dst_ref, perm, dil, seq):
    w = PERM_BLK // dil
    stream_len = seq // dil

    def body(b8, carry):
        r0 = pl.multiple_of(b8 * PERM_BLK, PERM_BLK)
        y = _dot(perm, src_ref[pl.ds(r0, PERM_BLK), :]).astype(dst_ref.dtype)
        for c in range(dil):
            d0 = pl.multiple_of(c * stream_len + b8 * w, w)
            dst_ref[pl.ds(d0, w), :] = y[c * w:(c + 1) * w, :]
        return carry

    lax.fori_loop(0, seq // PERM_BLK, body, 0)


def _interleave(src_ref, dst_ref, stack_ref, perm_inv, dil, seq, **dot_kw):
    w = PERM_BLK // dil
    stream_len = seq // dil

    def body(b8, carry):
        for c in range(dil):
            s0 = pl.multiple_of(c * stream_len + b8 * w, w)
            stack_ref[c * w:(c + 1) * w, :] = src_ref[pl.ds(s0, w), :]
        r0 = pl.multiple_of(b8 * PERM_BLK, PERM_BLK)
        dst_ref[pl.ds(r0, PERM_BLK), :] = _dot(perm_inv, stack_ref[...], **dot_kw).astype(dst_ref.dtype)
        return carry

    lax.fori_loop(0, seq // PERM_BLK, body, 0)


def _dil_window(q_src, k_src, v_src, o_dst, lse_dst, bias_ref, rq, rk, nk):
    scale = B_HEAD_DIM ** -0.5
    lane = lax.broadcasted_iota(jnp.int32, (BLK, LANES), 1)
    lse_acc = jnp.zeros((BLK, LANES), F32)
    for hh in range(B_HEADS):
        cols = slice(hh * B_HEAD_DIM, (hh + 1) * B_HEAD_DIM)
        qh = q_src[pl.ds(rq, BLK), cols]
        kh = k_src[pl.ds(rk, nk), cols]
        vh = v_src[pl.ds(rk, nk), cols]
        bias = bias_ref[0, hh, :, 2 * BLK - nk:]
        s = _dot_nt(qh, kh) * scale + bias
        m = jnp.max(s, axis=-1, keepdims=True)
        e = jnp.exp(s - m)
        den = jnp.sum(e, axis=-1, keepdims=True)
        o = _dot(e.astype(BF16), vh) / den
        o_dst[pl.ds(rq, BLK), cols] = o.astype(o_dst.dtype)
        lse_acc = jnp.where(lane == hh, m + jnp.log(den), lse_acc)
    lse_dst[pl.ds(rq, BLK), :] = lse_acc


def _dil_streams(q_src, k_src, v_src, o_dst, lse_dst, bias_ref, dil, seq):
    stream_len = seq // dil
    nq = stream_len // BLK

    def stream(c, carry):
        base = _mult(c * stream_len, BLK)
        _dil_window(q_src, k_src, v_src, o_dst, lse_dst, bias_ref, base, base, BLK)

        def qblock(n, carry2):
            rq = pl.multiple_of(base + n * BLK, BLK)
            _dil_window(q_src, k_src, v_src, o_dst, lse_dst, bias_ref, rq, rq - BLK, 2 * BLK)
            return carry2

        if nq > 1:
            lax.fori_loop(1, nq, qblock, 0)
        return carry

    if dil == 1:
        stream(0, 0)
    else:
        lax.fori_loop(0, dil, stream, 0)


def _dil_merge(o_src, lse_src, oacc, mrun, lrun, out_ref, first, last, seq):
    def body(ch, carry):
        r0 = pl.multiple_of(ch * ROW_TM, ROW_TM)
        rows = pl.ds(r0, ROW_TM)
        lse = lse_src[rows, :]
        if first:
            mrun[rows, :] = lse
            lrun[rows, :] = jnp.ones_like(lse)
            oacc[rows, :] = o_src[rows, :].astype(F32)
            return carry
        m_old = mrun[rows, :]
        m_new = jnp.maximum(m_old, lse)
        a = jnp.exp(m_old - m_new)
        bw = jnp.exp(lse - m_new)
        l_new = lrun[rows, :] * a + bw
        mrun[rows, :] = m_new
        lrun[rows, :] = l_new
        for hh in range(B_HEADS):
            cols = slice(hh * B_HEAD_DIM, (hh + 1) * B_HEAD_DIM)
            val = oacc[rows, cols] * a[:, hh:hh + 1] + o_src[rows, cols].astype(F32) * bw[:, hh:hh + 1]
            if last:
                out_ref[rows, cols] = (val / l_new[:, hh:hh + 1]).astype(out_ref.dtype)
            else:
                oacc[rows, cols] = val
        return carry

    lax.fori_loop(0, seq // ROW_TM, body, 0)


def _dilated_kernel(q_ref, k_ref, v_ref, bias_ref, out_ref,
                    qs, ks, vs, os_, lses, otok, lsetok, ostack, lstack, oacc, mrun, lrun, *, seq):
    g = pl.program_id(1)
    for gi, (_, dil) in enumerate(DILATED_GROUPS):

        @pl.when(g == gi)
        def _(gi=gi, dil=dil):
            first = gi == 0
            last = gi == N_DGROUPS - 1
            if dil == 1:
                _dil_streams(q_ref, k_ref, v_ref, otok, lsetok, bias_ref, 1, seq)
            else:
                perm = _perm_matrix(dil, inverse=False).astype(BF16)
                _deinterleave(q_ref, qs, perm, dil, seq)
                _deinterleave(k_ref, ks, perm, dil, seq)
                _deinterleave(v_ref, vs, perm, dil, seq)
                _dil_streams(qs, ks, vs, os_, lses, bias_ref, dil, seq)
                perm_inv = _perm_matrix(dil, inverse=True)
                _interleave(os_, otok, ostack, perm_inv.astype(BF16), dil, seq)
                _interleave(lses, lsetok, lstack, perm_inv, dil, seq, precision=lax.Precision.HIGHEST)
            _dil_merge(otok, lsetok, oacc, mrun, lrun, out_ref, first, last, seq)


def _dilated_call(zg, bias_tiles, *, batch, seq):
    width = B_HEADS * B_HEAD_DIM
    q_col = (GATE_COLS + 3 * A_HEADS * 2 * A_HEAD_DIM) // width
    k_col = q_col + N_DGROUPS
    v_col = k_col + N_DGROUPS
    return pl.pallas_call(
        functools.partial(_dilated_kernel, seq=seq),
        grid=(batch, N_DGROUPS),
        in_specs=[
            pl.BlockSpec((seq, width), lambda b, g: (b, q_col + g)),
            pl.BlockSpec((seq, width), lambda b, g: (b, k_col + g)),
            pl.BlockSpec((seq, width), lambda b, g: (b, v_col + g)),
            pl.BlockSpec((1, B_HEADS, BLK, 2 * BLK), lambda b, g: (g, 0, 0, 0)),
        ],
        out_specs=pl.BlockSpec((seq, width), lambda b, g: (b, 0)),
        out_shape=jax.ShapeDtypeStruct((batch * seq, width), BF16),
        scratch_shapes=[
            pltpu.VMEM((seq, width), BF16), pltpu.VMEM((seq, width), BF16), pltpu.VMEM((seq, width), BF16),
            pltpu.VMEM((seq, width), BF16), pltpu.VMEM((seq, LANES), F32),
            pltpu.VMEM((seq, width), BF16), pltpu.VMEM((seq, LANES), F32),
            pltpu.VMEM((PERM_BLK, width), BF16), pltpu.VMEM((PERM_BLK, LANES), F32),
            pltpu.VMEM((seq, width), F32), pltpu.VMEM((seq, LANES), F32), pltpu.VMEM((seq, LANES), F32),
        ],
        compiler_params=_cparams(("parallel", "arbitrary")),
        name="dilated_attn",
    )(zg, zg, zg, bias_tiles)


def _merge_router_kernel(oa_ref, ob_ref, gate_ref, x_ref, pa_ref, pb_ref, wo_ref, g_ref, wr_ref,
                         x1_ref, h2_ref, route_ref, cnt_ref, run_ref):
    tm, d = x_ref.shape

    @pl.when(pl.program_id(0) == 0)
    def _():
        run_ref[...] = jnp.zeros_like(run_ref)

    a = _dot(oa_ref[...], pa_ref[...])
    bm = _dot(ob_ref[...], pb_ref[...])
    merged = gate_ref[:, :d].astype(F32) * a + gate_ref[:, d:].astype(F32) * bm
    x1 = x_ref[...] + _dot(merged.astype(BF16), wo_ref[...])
    x1_ref[...] = x1
    h2 = _rmsnorm_val(x1, g_ref[...], RMS_EPS)
    h2_ref[...] = h2

    lt = _dot_nt(wr_ref[...], h2, precision=lax.Precision.HIGHEST)
    coarse = [lt[i:i + 1, :] for i in range(N_EXPERT_GROUPS)]
    best = coarse[0]
    gsel = jnp.zeros((1, tm), jnp.int32)
    for i in range(1, N_EXPERT_GROUPS):
        upd = coarse[i] > best
        gsel = jnp.where(upd, i, gsel)
        best = jnp.where(upd, coarse[i], best)
    den = jnp.exp(coarse[0] - best)
    for i in range(1, N_EXPERT_GROUPS):
        den = den + jnp.exp(coarse[i] - best)
    pg = 1.0 / den

    fine = []
    for k in range(EXPERTS_PER_GROUP):
        f = lt[N_EXPERT_GROUPS + k:N_EXPERT_GROUPS + k + 1, :]
        for gi in range(1, N_EXPERT_GROUPS):
            r = N_EXPERT_GROUPS + gi * EXPERTS_PER_GROUP + k
            f = jnp.where(gsel == gi, lt[r:r + 1, :], f)
        fine.append(f)
    v0 = fine[0]
    i0 = jnp.zeros((1, tm), jnp.int32)
    for k in range(1, EXPERTS_PER_GROUP):
        upd = fine[k] > v0
        i0 = jnp.where(upd, k, i0)
        v0 = jnp.where(upd, fine[k], v0)
    v1 = jnp.full((1, tm), -jnp.inf, F32)
    i1 = jnp.zeros((1, tm), jnp.int32)
    for k in range(EXPERTS_PER_GROUP):
        upd = jnp.where(i0 != k, jnp.where(fine[k] > v1, 1, 0), 0) == 1
        i1 = jnp.where(upd, k, i1)
        v1 = jnp.where(upd, fine[k], v1)
    e1w = jnp.exp(v1 - v0)
    gate0 = pg * (1.0 / (1.0 + e1w))
    gate1 = pg * (e1w / (1.0 + e1w))
    e0 = gsel * EXPERTS_PER_GROUP + i0
    e1 = gsel * EXPERTS_PER_GROUP + i1

    eidx = lax.broadcasted_iota(jnp.int32, (N_EXPERTS, tm), 0)
    oh0 = jnp.where(eidx == e0, jnp.float32(1), jnp.float32(0))
    oh1 = jnp.where(eidx == e1, jnp.float32(1), jnp.float32(0))
    ta = lax.broadcasted_iota(jnp.int32, (tm, tm), 0)
    tb = lax.broadcasted_iota(jnp.int32, (tm, tm), 1)
    before = jnp.where(ta < tb, jnp.float32(1), jnp.float32(0)).astype(BF16)
    pre0 = _dot(oh0.astype(BF16), before)
    pre1 = _dot(oh1.astype(BF16), before)
    run = run_ref[:, 0:1]
    tot0 = jnp.sum(oh0, axis=1, keepdims=True)
    tot1 = jnp.sum(oh1, axis=1, keepdims=True)
    rank0 = jnp.sum(oh0 * (run + pre0), axis=0, keepdims=True)
    rank1 = jnp.sum(oh1 * (run + tot0 + pre1), axis=0, keepdims=True)
    new_run = jnp.broadcast_to(run + tot0 + tot1, run_ref.shape)
    run_ref[...] = new_run
    cnt_ref[...] = new_run

    route_ref[0:1, :] = e0.astype(F32)
    route_ref[1:2, :] = e1.astype(F32)
    route_ref[2:3, :] = gate0
    route_ref[3:4, :] = gate1
    route_ref[4:5, :] = rank0
    route_ref[5:6, :] = rank1
    route_ref[6:8, :] = jnp.zeros((2, tm), F32)


def _merge_router_call(oa, ob, zg, x2d, pa, pb, wo, g, wr_t):
    t, d = x2d.shape
    tm = ROW_TM
    const = dict(pipeline_mode=pl.Buffered(1))
    return pl.pallas_call(
        _merge_router_kernel,
        grid=(t // tm,),
        in_specs=[
            pl.BlockSpec((tm, oa.shape[1]), lambda i: (i, 0)),
            pl.BlockSpec((tm, ob.shape[1]), lambda i: (i, 0)),
            pl.BlockSpec((tm, GATE_COLS), lambda i: (i, 0)),
            pl.BlockSpec((tm, d), lambda i: (i, 0)),
            pl.BlockSpec(pa.shape, lambda i: (0, 0), **const),
            pl.BlockSpec(pb.shape, lambda i: (0, 0), **const),
            pl.BlockSpec(wo.shape, lambda i: (0, 0), **const),
            pl.BlockSpec((1, d), lambda i: (0, 0)),
            pl.BlockSpec(wr_t.shape, lambda i: (0, 0), **const),
        ],
        out_specs=[
            pl.BlockSpec((tm, d), lambda i: (i, 0)),
            pl.BlockSpec((tm, d), lambda i: (i, 0)),
            pl.BlockSpec((8, tm), lambda i: (0, i)),
            pl.BlockSpec((N_EXPERTS, LANES), lambda i: (0, 0)),
        ],
        out_shape=[
            jax.ShapeDtypeStruct((t, d), F32),
            jax.ShapeDtypeStruct((t, d), F32),
            jax.ShapeDtypeStruct((8, t), F32),
            jax.ShapeDtypeStruct((N_EXPERTS, LANES), F32),
        ],
        scratch_shapes=[pltpu.VMEM((N_EXPERTS, LANES), F32)],
        compiler_params=_cparams(("arbitrary",)),
        name="merge_router",
    )(oa, ob, zg, x2d, pa, pb, wo, g, wr_t)


def _row_copy(src, s_row, dst, d_row, sem):
    return pltpu.make_async_copy(src.at[pl.ds(s_row, 1)], dst.at[pl.ds(d_row, 1)], sem)


def _dispatch_kernel(dest_ref, zstart_ref, zlen_ref, h_hbm, x_hbm, zero_ref, sem):
    step = pl.program_id(0)
    tok0 = step * DISPATCH_TOK
    n_copy = DISPATCH_TOK * TOP_K

    def issue(i, carry):
        t = lax.shift_right_logical(i, 1)
        k = lax.bitwise_and(i, 1)
        slot = lax.bitwise_and(i, DMA_RING - 1)

        @pl.when(i >= DMA_RING)
        def _():
            _row_copy(h_hbm, 0, x_hbm, 0, sem.at[slot]).wait()

        _row_copy(h_hbm, tok0 + t, x_hbm, dest_ref[0, k, t], sem.at[slot]).start()
        return carry

    lax.fori_loop(0, n_copy, issue, 0)

    def drain(i, carry):
        _row_copy(h_hbm, 0, x_hbm, 0, sem.at[i]).wait()
        return carry

    lax.fori_loop(0, DMA_RING, drain, 0)

    @pl.when(step == 0)
    def _():
        zero_ref[...] = jnp.zeros_like(zero_ref)

        def seg(e, carry):
            start = zstart_ref[e]

            def zrow(r, carry2):
                cp = _row_copy(zero_ref, 0, x_hbm, start + r, sem.at[0])
                cp.start()
                cp.wait()
                return carry2

            lax.fori_loop(0, zlen_ref[e], zrow, 0)
            return carry

        lax.fori_loop(0, N_EXPERTS + 1, seg, 0)


def _dispatch_call(dest_blocks, zstart, zlen, h2, n_rows):
    t, d = h2.shape
    return pl.pallas_call(
        _dispatch_kernel,
        grid=(t // DISPATCH_TOK,),
        in_specs=[
            pl.BlockSpec((1, TOP_K, DISPATCH_TOK), lambda i: (i, 0, 0), memory_space=pltpu.SMEM),
            pl.BlockSpec(memory_space=pltpu.SMEM),
            pl.BlockSpec(memory_space=pltpu.SMEM),
            pl.BlockSpec(memory_space=pl.ANY),
        ],
        out_specs=pl.BlockSpec(memory_space=pl.ANY),
        out_shape=jax.ShapeDtypeStruct((n_rows, d), F32),
        scratch_shapes=[pltpu.VMEM((8, d), F32), pltpu.SemaphoreType.DMA((DMA_RING,))],
        compiler_params=_cparams(("arbitrary",)),
        name="dispatch",
    )(dest_blocks, zstart, zlen, h2)


def _expert_kernel(blk_e_ref, blk_valid_ref, x_ref, w1_ref, w3_ref, w2_ref, y_ref):
    i = pl.program_id(0)

    @pl.when(blk_valid_ref[i] == 1)
    def _():
        x = x_ref[...].astype(BF16)
        a = _dot(x, w1_ref[0])
        b = _dot(x, w3_ref[0])
        hdn = (a * _sigmoid(a)) * b
        y_ref[...] = _dot(hdn.astype(BF16), w2_ref[0])

    @pl.when(blk_valid_ref[i] == 0)
    def _():
        y_ref[...] = jnp.zeros_like(y_ref)


def _expert_call(blk_e, blk_valid, xin, w1, w3, w2):
    n_rows, d = xin.shape
    de = w1.shape[2]
    grid_spec = pltpu.PrefetchScalarGridSpec(
        num_scalar_prefetch=2,
        grid=(n_rows // EXP_TM,),
        in_specs=[
            pl.BlockSpec((EXP_TM, d), lambda i, be, bv: (i, 0)),
            pl.BlockSpec((1, d, de), lambda i, be, bv: (be[i], 0, 0)),
            pl.BlockSpec((1, d, de), lambda i, be, bv: (be[i], 0, 0)),
            pl.BlockSpec((1, de, d), lambda i, be, bv: (be[i], 0, 0)),
        ],
        out_specs=pl.BlockSpec((EXP_TM, d), lambda i, be, bv: (i, 0)),
    )
    return pl.pallas_call(
        _expert_kernel,
        grid_spec=grid_spec,
        out_shape=jax.ShapeDtypeStruct((n_rows, d), F32),
        compiler_params=_cparams(("arbitrary",)),
        name="experts",
    )(blk_e, blk_valid, xin, w1, w3, w2)


def _combine_kernel(dest_ref, y_hbm, x1_ref, gt_ref, p_ref, gple_ref, wg_ref, wp_ref, gfin_ref,
                    o_ref, ybuf, sem, *, final_norm):
    tm = x1_ref.shape[0]

    def issue(t, carry):
        for k in range(TOP_K):
            _row_copy(y_hbm, dest_ref[0, k, t], ybuf.at[k], t, sem.at[k]).start()
        return carry

    lax.fori_loop(0, tm, issue, 0)
    pp = _dot(p_ref[...].astype(BF16), wp_ref[...])

    def drain(t, carry):
        for k in range(TOP_K):
            _row_copy(y_hbm, 0, ybuf.at[k], t, sem.at[k]).wait()
        return carry

    lax.fori_loop(0, tm, drain, 0)
    y = ybuf[0] * gt_ref[:, 2:3] + ybuf[1] * gt_ref[:, 3:4]
    x2 = x1_ref[...] + y
    hn = _rmsnorm_val(x2, gple_ref[...], RMS_EPS)
    gate = _sigmoid(_dot(hn.astype(BF16), wg_ref[...]))
    x3 = x2 + gate * pp
    if final_norm:
        o_ref[...] = _rmsnorm_val(x3, gfin_ref[...], RMS_EPS)
    else:
        o_ref[...] = x3


def _combine_call(dest_blocks, yb, x1, gates_t, p2d, g_ple, w_gate, w_proj, g_fin, final_norm):
    t, d = x1.shape
    tm = ROW_TM
    const = dict(pipeline_mode=pl.Buffered(1))
    return pl.pallas_call(
        functools.partial(_combine_kernel, final_norm=final_norm),
        grid=(t // tm,),
        in_specs=[
            pl.BlockSpec((1, TOP_K, tm), lambda i: (i, 0, 0), memory_space=pltpu.SMEM),
            pl.BlockSpec(memory_space=pl.ANY),
            pl.BlockSpec((tm, d), lambda i: (i, 0)),
            pl.BlockSpec((tm, 8), lambda i: (i, 0)),
            pl.BlockSpec((tm, p2d.shape[1]), lambda i: (i, 0)),
            pl.BlockSpec((1, d), lambda i: (0, 0)),
            pl.BlockSpec(w_gate.shape, lambda i: (0, 0), **const),
            pl.BlockSpec(w_proj.shape, lambda i: (0, 0), **const),
            pl.BlockSpec((1, d), lambda i: (0, 0)),
        ],
        out_specs=pl.BlockSpec((tm, d), lambda i: (i, 0)),
        out_shape=jax.ShapeDtypeStruct((t, d), F32),
        scratch_shapes=[pltpu.VMEM((TOP_K, tm, d), F32), pltpu.SemaphoreType.DMA((TOP_K,))],
        compiler_params=_cparams(("arbitrary",)),
        name="combine",
    )(dest_blocks, yb, x1, gates_t, p2d, g_ple, w_gate, w_proj, g_fin)


def _routing_plan(route, counts_f, t):
    counts = counts_f[:, 0].astype(jnp.int32)
    pcounts = ((counts + EXP_TM - 1) // EXP_TM) * EXP_TM
    pends = jnp.cumsum(pcounts)
    pstarts = pends - pcounts
    eid = route[0:2].astype(jnp.int32)
    rank = route[4:6].astype(jnp.int32)
    onehot = eid[..., None] == jnp.arange(N_EXPERTS, dtype=jnp.int32)
    dest = jnp.sum(jnp.where(onehot, pstarts, 0), axis=-1) + rank
    n_rows = TOP_K * t + N_EXPERTS * EXP_TM
    n_blocks = n_rows // EXP_TM
    blk_row = jnp.arange(n_blocks, dtype=jnp.int32) * EXP_TM
    blk_e = jnp.sum((pends[None, :] <= blk_row[:, None]).astype(jnp.int32), axis=1)
    blk_valid = (blk_row < pends[-1]).astype(jnp.int32)
    last_e = jnp.sum((pends <= pends[-1] - 1).astype(jnp.int32))
    blk_e = jnp.minimum(blk_e, last_e)
    zstart = jnp.concatenate([pstarts + counts, pends[-1:]]).astype(jnp.int32)
    zlen = jnp.concatenate([pcounts - counts, n_rows - pends[-1:]]).astype(jnp.int32)
    return dest, blk_e, blk_valid, zstart, zlen, n_rows


def _blocked(dest, tok_per_block):
    t = dest.shape[1]
    return dest.reshape(TOP_K, t // tok_per_block, tok_per_block).transpose(1, 0, 2)


def kernel(x, p, rel_bias, norm_mix_g, w_in, w_gate, lambda_q1, lambda_k1, lambda_q2, lambda_k2, subln_g,
           w_proj_a, w_proj_b, w_out, norm_ffn_g, w_coarse, w_fine, w1, w3, w2, norm_ple_g, w_ple_gate,
           w_ple_proj, final_norm_g):
    batch, seq, d = x.shape
    depth = w_in.shape[0]
    t = batch * seq
    assert seq % PERM_BLK == 0 and t % min(PROJ_TM, t) == 0
    assert all(seq // dil >= BLK and win // dil == BLK for win, dil in DILATED_GROUPS)

    nq = seq // ATT_BLK
    bias_a = _bias_tiles_call(rel_bias, n_heads=A_HEADS, n_off=nq, tq=ATT_BLK, tk=ATT_BLK, off_mult=ATT_BLK,
                              off_add=0, dil=1, max_rel=seq, head0=0, name="bias_diff")
    bias_b = jnp.concatenate([
        _bias_tiles_call(rel_bias, n_heads=B_HEADS, n_off=1, tq=BLK, tk=2 * BLK, off_mult=0, off_add=BLK,
                         dil=dil, max_rel=win // dil, head0=A_HEADS + gi * B_HEADS,
                         name="bias_dil%d" % gi).reshape(1, B_HEADS, BLK, 2 * BLK)
        for gi, (win, dil) in enumerate(DILATED_GROUPS)], axis=0)

    x2d = x.reshape(t, d)
    for layer in range(depth):
        lam_init = 0.8 - 0.6 * math.exp(-0.3 * layer)
        lam = _lam_call(lambda_q1[layer:layer + 1], lambda_k1[layer:layer + 1],
                        lambda_q2[layer:layer + 1], lambda_k2[layer:layer + 1], lam_init)
        w_cat = jnp.concatenate([w_gate[layer], w_in[layer]], axis=1).astype(BF16)
        zg = _inproj_call(x2d, norm_mix_g[layer:layer + 1], w_cat)
        oa = _diff_attn_call(zg, lam, bias_a, subln_g[layer:layer + 1], batch=batch, seq=seq,
                             out_scale=1.0 - lam_init)
        ob = _dilated_call(zg, bias_b, batch=batch, seq=seq)

        wr_t = jnp.concatenate([
            w_coarse[layer].T,
            w_fine[layer].transpose(0, 2, 1).reshape(N_EXPERTS, d),
            jnp.zeros((LANES - N_EXPERT_GROUPS - N_EXPERTS, d), F32)], axis=0)
        x1, h2, route, counts = _merge_router_call(
            oa, ob, zg, x2d, w_proj_a[layer].astype(BF16), w_proj_b[layer].astype(BF16),
            w_out[layer].astype(BF16), norm_ffn_g[layer:layer + 1], wr_t)

        dest, blk_e, blk_valid, zstart, zlen, n_rows = _routing_plan(route, counts, t)
        xin = _dispatch_call(_blocked(dest, DISPATCH_TOK), zstart, zlen, h2, n_rows)
        yb = _expert_call(blk_e, blk_valid, xin, w1[layer].astype(BF16), w3[layer].astype(BF16),
                          w2[layer].astype(BF16))
        x2d = _combine_call(_blocked(dest, ROW_TM), yb, x1, route.T, p[layer].reshape(t, -1),
                            norm_ple_g[layer:layer + 1], w_ple_gate[layer].astype(BF16),
                            w_ple_proj[layer].astype(BF16), final_norm_g.reshape(1, d),
                            final_norm=layer == depth - 1)
    return x2d.reshape(batch, seq, d)
```

```python
import functools
import math

import jax
import jax.numpy as jnp
from jax import lax
from jax.experimental import pallas as pl
from jax.experimental.pallas import tpu as pltpu

F32 = jnp.float32
BF16 = jnp.bfloat16

BLK = 128
NEG_INF = -1e30
RMS_EPS = 1e-6
SUBLN_EPS = 1e-5
N_BUCKETS = 32
MAX_DISTANCE = 2048
A_HEADS = 8
A_HEAD_DIM = 128
DILATED_GROUPS = ((128, 1), (512, 4), (2048, 16))
N_DGROUPS = 3
B_HEADS = 8
B_HEAD_DIM = 64
N_EXPERT_GROUPS = 4
EXPERTS_PER_GROUP = 8
N_EXPERTS = 32
TOP_K = 2

LANES = 128
VMEM_LIMIT = 56 * 1024 * 1024
ATT_BLK = 256
PERM_BLK = 256
GATE_COLS = 4096
PROJ_TN = 512
PROJ_TM = 1024
ROW_TM = 256
EXP_TM = 256
DISPATCH_TOK = 512
NORM_CHUNK = 128


def _t5_thresholds():
    max_exact = N_BUCKETS // 2
    out = []
    for k in range(1, N_BUCKETS - max_exact):
        out.append(int(math.ceil(max_exact * (MAX_DISTANCE / max_exact) ** (k / (N_BUCKETS - max_exact)))))
    return tuple(out)


T5_THRESHOLDS = _t5_thresholds()


def _cparams(sem, vmem=VMEM_LIMIT):
    return pltpu.CompilerParams(dimension_semantics=sem, vmem_limit_bytes=vmem)


def _mult(x, m):
    return x if isinstance(x, int) else pl.multiple_of(x, m)


def _sigmoid(x):
    return 1.0 / (1.0 + jnp.exp(-x))


def _dot(a, b, **kw):
    return jnp.dot(a, b, preferred_element_type=F32, **kw)


def _dot_nt(a, b, **kw):
    return lax.dot_general(a, b, (((1,), (1,)), ((), ())), preferred_element_type=F32, **kw)


def _lam_kernel(q1_ref, k1_ref, q2_ref, k2_ref, o_ref, *, lam_init):
    s1 = jnp.sum(q1_ref[...] * k1_ref[...], axis=-1, keepdims=True)
    s2 = jnp.sum(q2_ref[...] * k2_ref[...], axis=-1, keepdims=True)
    o_ref[...] = jnp.exp(s1) - jnp.exp(s2) + lam_init


def _lam_call(lq1, lk1, lq2, lk2, lam_init):
    return pl.pallas_call(
        functools.partial(_lam_kernel, lam_init=lam_init),
        out_shape=jax.ShapeDtypeStruct((1, 1), F32),
        name="lam",
    )(lq1, lk1, lq2, lk2)


def _bias_tile_kernel(tab_ref, o_ref, *, tq, tk, off_mult, off_add, dil, max_rel, head0):
    h = pl.program_id(0)
    n = pl.program_id(1)
    i = lax.broadcasted_iota(jnp.int32, (tq, tk), 0)
    j = lax.broadcasted_iota(jnp.int32, (tq, tk), 1)
    rel = i - j + (n * off_mult + off_add)
    dist = rel * dil
    large = jnp.full((tq, tk), N_BUCKETS // 2, jnp.int32)
    for thr in T5_THRESHOLDS:
        large = large + jnp.where(dist >= thr, 1, 0)
    bucket = jnp.where(dist < N_BUCKETS // 2, dist, large)
    acc = jnp.zeros((tq, tk), F32)
    for b in range(N_BUCKETS):
        acc = jnp.where(bucket == b, tab_ref[b, head0 + h], acc)
    valid = jnp.where(rel >= 0, jnp.where(rel <= max_rel, 1, 0), 0)
    o_ref[0, 0] = jnp.where(valid == 1, acc, NEG_INF)


def _bias_tiles_call(rel_bias, *, n_heads, n_off, tq, tk, off_mult, off_add, dil, max_rel, head0, name):
    kern = functools.partial(_bias_tile_kernel, tq=tq, tk=tk, off_mult=off_mult, off_add=off_add,
                             dil=dil, max_rel=max_rel, head0=head0)
    return pl.pallas_call(
        kern,
        grid=(n_heads, n_off),
        in_specs=[pl.BlockSpec(memory_space=pltpu.SMEM)],
        out_specs=pl.BlockSpec((1, 1, tq, tk), lambda h, n: (h, n, 0, 0)),
        out_shape=jax.ShapeDtypeStruct((n_heads, n_off, tq, tk), F32),
        compiler_params=_cparams(("parallel", "parallel")),
        name=name,
    )(rel_bias)


def _rmsnorm_rows(x_ref, g_ref, out_ref, eps):
    rows = x_ref.shape[0]
    g = g_ref[...]

    def body(c, carry):
        r0 = pl.multiple_of(c * NORM_CHUNK, NORM_CHUNK)
        x = x_ref[pl.ds(r0, NORM_CHUNK), :]
        ms = jnp.mean(x * x, axis=-1, keepdims=True)
        out_ref[pl.ds(r0, NORM_CHUNK), :] = ((x * lax.rsqrt(ms + eps)) * g).astype(out_ref.dtype)
        return carry

    lax.fori_loop(0, rows // NORM_CHUNK, body, 0)


def _rmsnorm_val(x, g, eps):
    ms = jnp.mean(x * x, axis=-1, keepdims=True)
    return (x * lax.rsqrt(ms + eps)) * g


def _inproj_kernel(x_ref, g_ref, w_ref, o_ref, h_ref, *, n_gate_blocks):
    j = pl.program_id(1)

    @pl.when(j == 0)
    def _():
        _rmsnorm_rows(x_ref, g_ref, h_ref, RMS_EPS)

    acc = _dot(h_ref[...], w_ref[...])

    @pl.when(j < n_gate_blocks)
    def _():
        o_ref[...] = _sigmoid(acc).astype(o_ref.dtype)

    @pl.when(j >= n_gate_blocks)
    def _():
        o_ref[...] = acc.astype(o_ref.dtype)


def _inproj_call(x2d, g, w_cat):
    t, d = x2d.shape
    n = w_cat.shape[1]
    tm = min(PROJ_TM, t)
    return pl.pallas_call(
        functools.partial(_inproj_kernel, n_gate_blocks=GATE_COLS // PROJ_TN),
        grid=(t // tm, n // PROJ_TN),
        in_specs=[
            pl.BlockSpec((tm, d), lambda i, j: (i, 0)),
            pl.BlockSpec((1, d), lambda i, j: (0, 0)),
            pl.BlockSpec((d, PROJ_TN), lambda i, j: (0, j)),
        ],
        out_specs=pl.BlockSpec((tm, PROJ_TN), lambda i, j: (i, j)),
        out_shape=jax.ShapeDtypeStruct((t, n), BF16),
        scratch_shapes=[pltpu.VMEM((tm, d), BF16)],
        compiler_params=_cparams(("parallel", "arbitrary")),
        name="inproj",
    )(x2d, g, w_cat)


def _diff_attn_kernel(lam_ref, q_ref, k_ref, v_ref, bias_ref, g_ref, o_ref, acc1, acc2, *, out_scale):
    qi = pl.program_id(2)
    scale = A_HEAD_DIM ** -0.5
    q1 = q_ref[:, :A_HEAD_DIM]
    q2 = q_ref[:, A_HEAD_DIM:]
    acc1[...] = jnp.zeros_like(acc1)
    acc2[...] = jnp.zeros_like(acc2)

    def one_map(q, kb, vb, bias, m, l, acc):
        s = _dot_nt(q, kb) * scale + bias
        m_new = jnp.maximum(m, jnp.max(s, axis=-1, keepdims=True))
        alpha = jnp.exp(m - m_new)
        p = jnp.exp(s - m_new)
        l_new = alpha * l + jnp.sum(p, axis=-1, keepdims=True)
        acc[...] = acc[...] * alpha + _dot(p.astype(BF16), vb)
        return m_new, l_new

    def body(ki, carry):
        m1, l1, m2, l2 = carry
        k0 = pl.multiple_of(ki * ATT_BLK, ATT_BLK)
        kb = k_ref[pl.ds(k0, ATT_BLK), :]
        vb = v_ref[pl.ds(k0, ATT_BLK), :]
        bias = bias_ref[0, qi - ki]
        m1, l1 = one_map(q1, kb[:, :A_HEAD_DIM], vb, bias, m1, l1, acc1)
        m2, l2 = one_map(q2, kb[:, A_HEAD_DIM:], vb, bias, m2, l2, acc2)
        return m1, l1, m2, l2

    minf = jnp.full((ATT_BLK, 1), -jnp.inf, F32)
    zero = jnp.zeros((ATT_BLK, 1), F32)
    m1, l1, m2, l2 = lax.fori_loop(0, qi + 1, body, (minf, zero, minf, zero))
    w = acc1[...] / l1 - lam_ref[0, 0] * (acc2[...] / l2)
    o_ref[...] = (_rmsnorm_val(w, g_ref[...], SUBLN_EPS) * out_scale).astype(o_ref.dtype)


def _diff_attn_call(zg, lam, bias_tiles, subln_g, *, batch, seq, out_scale):
    t = batch * seq
    nq = seq // ATT_BLK
    width = 2 * A_HEAD_DIM
    q_col = GATE_COLS // width
    k_col = q_col + A_HEADS
    v_col = k_col + A_HEADS
    return pl.pallas_call(
        functools.partial(_diff_attn_kernel, out_scale=out_scale),
        grid=(A_HEADS, batch, nq),
        in_specs=[
            pl.BlockSpec(memory_space=pltpu.SMEM),
            pl.BlockSpec((ATT_BLK, width), lambda h, b, i: (b * nq + i, q_col + h)),
            pl.BlockSpec((seq, width), lambda h, b, i: (b, k_col + h)),
            pl.BlockSpec((seq, width), lambda h, b, i: (b, v_col + h)),
            pl.BlockSpec((1, nq, ATT_BLK, ATT_BLK), lambda h, b, i: (h, 0, 0, 0)),
            pl.BlockSpec((1, width), lambda h, b, i: (0, 0)),
        ],
        out_specs=pl.BlockSpec((ATT_BLK, width), lambda h, b, i: (b * nq + i, h)),
        out_shape=jax.ShapeDtypeStruct((t, A_HEADS * width), BF16),
        scratch_shapes=[pltpu.VMEM((ATT_BLK, width), F32), pltpu.VMEM((ATT_BLK, width), F32)],
        compiler_params=_cparams(("parallel", "parallel", "arbitrary")),
        name="diff_attn",
    )(lam, zg, zg, zg, bias_tiles, subln_g)


def _perm_matrix(dil, inverse):
    w = PERM_BLK // dil
    shift = w.bit_length() - 1
    a = lax.broadcasted_iota(jnp.int32, (PERM_BLK, PERM_BLK), 0)
    b = lax.broadcasted_iota(jnp.int32, (PERM_BLK, PERM_BLK), 1)
    dst, src = (b, a) if inverse else (a, b)
    c = lax.shift_right_logical(dst, shift)
    ll = lax.bitwise_and(dst, w - 1)
    return jnp.where(src == ll * dil + c, jnp.float32(1), jnp.float32(0))


def _deinterleave(src_ref, dst_ref, perm, dil, seq):
    w = PERM_BLK // dil
    stream_len = seq // dil

    def body(b8, carry):
        r0 = pl.multiple_of(b8 * PERM_BLK, PERM_BLK)
        y = _dot(perm, src_ref[pl.ds(r0, PERM_BLK), :]).astype(dst_ref.dtype)
        for c in range(dil):
            d0 = pl.multiple_of(c * stream_len + b8 * w, w)
            dst_ref[pl.ds(d0, w), :] = y[c * w:(c + 1) * w, :]
        return carry

    lax.fori_loop(0, seq // PERM_BLK, body, 0)


def _interleave(src_ref, dst_ref, stack_ref, perm_inv, dil, seq, **dot_kw):
    w = PERM_BLK // dil
    stream_len = seq // dil

    def body(b8, carry):
        for c in range(dil):
            s0 = pl.multiple_of(c * stream_len + b8 * w, w)
            stack_ref[c * w:(c + 1) * w, :] = src_ref[pl.ds(s0, w), :]
        r0 = pl.multiple_of(b8 * PERM_BLK, PERM_BLK)
        dst_ref[pl.ds(r0, PERM_BLK), :] = _dot(perm_inv, stack_ref[...], **dot_kw).astype(dst_ref.dtype)
        return carry

    lax.fori_loop(0, seq // PERM_BLK, body, 0)


def _dil_window(q_src, k_src, v_src, o_dst, lse_dst, bias_ref, rq, rk, nk):
    scale = B_HEAD_DIM ** -0.5
    lane = lax.broadcasted_iota(jnp.int32, (BLK, LANES), 1)
    lse_acc = jnp.zeros((BLK, LANES), F32)
    for hh in range(B_HEADS):
        cols = slice(hh * B_HEAD_DIM, (hh + 1) * B_HEAD_DIM)
        qh = q_src[pl.ds(rq, BLK), cols]
        kh = k_src[pl.ds(rk, nk), cols]
        vh = v_src[pl.ds(rk, nk), cols]
        bias = bias_ref[0, hh, :, 2 * BLK - nk:]
        s = _dot_nt(qh, kh) * scale + bias
        m = jnp.max(s, axis=-1, keepdims=True)
        e = jnp.exp(s - m)
        den = jnp.sum(e, axis=-1, keepdims=True)
        o = _dot(e.astype(BF16), vh) / den
        o_dst[pl.ds(rq, BLK), cols] = o.astype(o_dst.dtype)
        lse_acc = jnp.where(lane == hh, m + jnp.log(den), lse_acc)
    lse_dst[pl.ds(rq, BLK), :] = lse_acc


def _dil_streams(q_src, k_src, v_src, o_dst, lse_dst, bias_ref, dil, seq):
    stream_len = seq // dil
    nq = stream_len // BLK

    def stream(c, carry):
        base = _mult(c * stream_len, BLK)
        _dil_window(q_src, k_src, v_src, o_dst, lse_dst, bias_ref, base, base, BLK)

        def qblock(n, carry2):
            rq = pl.multiple_of(base + n * BLK, BLK)
            _dil_window(q_src, k_src, v_src, o_dst, lse_dst, bias_ref, rq, rq - BLK, 2 * BLK)
            return carry2

        if nq > 1:
            lax.fori_loop(1, nq, qblock, 0)
        return carry

    if dil == 1:
        stream(0, 0)
    else:
        lax.fori_loop(0, dil, stream, 0)


def _dil_merge(o_src, lse_src, oacc, mrun, lrun, out_ref, first, last, seq):
    def body(ch, carry):
        r0 = pl.multiple_of(ch * ROW_TM, ROW_TM)
        rows = pl.ds(r0, ROW_TM)
        lse = lse_src[rows, :]
        if first:
            mrun[rows, :] = lse
            lrun[rows, :] = jnp.ones_like(lse)
            oacc[rows, :] = o_src[rows, :].astype(F32)
            return carry
        m_old = mrun[rows, :]
        m_new = jnp.maximum(m_old, lse)
        a = jnp.exp(m_old - m_new)
        bw = jnp.exp(lse - m_new)
        l_new = lrun[rows, :] * a + bw
        mrun[rows, :] = m_new
        lrun[rows, :] = l_new
        for hh in range(B_HEADS):
            cols = slice(hh * B_HEAD_DIM, (hh + 1) * B_HEAD_DIM)
            val = oacc[rows, cols] * a[:, hh:hh + 1] + o_src[rows, cols].astype(F32) * bw[:, hh:hh + 1]
            if last:
                out_ref[rows, cols] = (val / l_new[:, hh:hh + 1]).astype(out_ref.dtype)
            else:
                oacc[rows, cols] = val
        return carry

    lax.fori_loop(0, seq // ROW_TM, body, 0)


def _dilated_kernel(q_ref, k_ref, v_ref, bias_ref, out_ref,
                    qs, ks, vs, os_, lses, otok, lsetok, ostack, lstack, oacc, mrun, lrun, *, seq):
    g = pl.program_id(1)
    for gi, (_, dil) in enumerate(DILATED_GROUPS):

        @pl.when(g == gi)
        def _(gi=gi, dil=dil):
            first = gi == 0
            last = gi == N_DGROUPS - 1
            if dil == 1:
                _dil_streams(q_ref, k_ref, v_ref, otok, lsetok, bias_ref, 1, seq)
            else:
                perm = _perm_matrix(dil, inverse=False).astype(BF16)
                _deinterleave(q_ref, qs, perm, dil, seq)
                _deinterleave(k_ref, ks, perm, dil, seq)
                _deinterleave(v_ref, vs, perm, dil, seq)
                _dil_streams(qs, ks, vs, os_, lses, bias_ref, dil, seq)
                perm_inv = _perm_matrix(dil, inverse=True)
                _interleave(os_, otok, ostack, perm_inv.astype(BF16), dil, seq)
                _interleave(lses, lsetok, lstack, perm_inv, dil, seq, precision=lax.Precision.HIGHEST)
            _dil_merge(otok, lsetok, oacc, mrun, lrun, out_ref, first, last, seq)


def _dilated_call(zg, bias_tiles, *, batch, seq):
    width = B_HEADS * B_HEAD_DIM
    q_col = (GATE_COLS + 3 * A_HEADS * 2 * A_HEAD_DIM) // width
    k_col = q_col + N_DGROUPS
    v_col = k_col + N_DGROUPS
    return pl.pallas_call(
        functools.partial(_dilated_kernel, seq=seq),
        grid=(batch, N_DGROUPS),
        in_specs=[
            pl.BlockSpec((seq, width), lambda b, g: (b, q_col + g)),
            pl.BlockSpec((seq, width), lambda b, g: (b, k_col + g)),
            pl.BlockSpec((seq, width), lambda b, g: (b, v_col + g)),
            pl.BlockSpec((1, B_HEADS, BLK, 2 * BLK), lambda b, g: (g, 0, 0, 0)),
        ],
        out_specs=pl.BlockSpec((seq, width), lambda b, g: (b, 0)),
        out_shape=jax.ShapeDtypeStruct((batch * seq, width), BF16),
        scratch_shapes=[
            pltpu.VMEM((seq, width), BF16), pltpu.VMEM((seq, width), BF16), pltpu.VMEM((seq, width), BF16),
            pltpu.VMEM((seq, width), BF16), pltpu.VMEM((seq, LANES), F32),
            pltpu.VMEM((seq, width), BF16), pltpu.VMEM((seq, LANES), F32),
            pltpu.VMEM((PERM_BLK, width), BF16), pltpu.VMEM((PERM_BLK, LANES), F32),
            pltpu.VMEM((seq, width), F32), pltpu.VMEM((seq, LANES), F32), pltpu.VMEM((seq, LANES), F32),
        ],
        compiler_params=_cparams(("parallel", "arbitrary")),
        name="dilated_attn",
    )(zg, zg, zg, bias_tiles)


def _merge_router_kernel(oa_ref, ob_ref, gate_ref, x_ref, pa_ref, pb_ref, wo_ref, g_ref, wr_ref,
                         x1_ref, h2_ref, route_ref, cnt_ref, run_ref):
    tm, d = x_ref.shape

    @pl.when(pl.program_id(0) == 0)
    def _():
        run_ref[...] = jnp.zeros_like(run_ref)

    a = _dot(oa_ref[...], pa_ref[...])
    bm = _dot(ob_ref[...], pb_ref[...])
    merged = gate_ref[:, :d].astype(F32) * a + gate_ref[:, d:].astype(F32) * bm
    x1 = x_ref[...] + _dot(merged.astype(BF16), wo_ref[...])
    x1_ref[...] = x1
    h2 = _rmsnorm_val(x1, g_ref[...], RMS_EPS)
    h2_ref[...] = h2

    lt = _dot_nt(wr_ref[...], h2, precision=lax.Precision.HIGHEST)
    coarse = [lt[i:i + 1, :] for i in range(N_EXPERT_GROUPS)]
    best = coarse[0]
    gsel = jnp.zeros((1, tm), jnp.int32)
    for i in range(1, N_EXPERT_GROUPS):
        upd = coarse[i] > best
        gsel = jnp.where(upd, i, gsel)
        best = jnp.where(upd, coarse[i], best)
    den = jnp.exp(coarse[0] - best)
    for i in range(1, N_EXPERT_GROUPS):
        den = den + jnp.exp(coarse[i] - best)
    pg = 1.0 / den

    fine = []
    for k in range(EXPERTS_PER_GROUP):
        f = lt[N_EXPERT_GROUPS + k:N_EXPERT_GROUPS + k + 1, :]
        for gi in range(1, N_EXPERT_GROUPS):
            r = N_EXPERT_GROUPS + gi * EXPERTS_PER_GROUP + k
            f = jnp.where(gsel == gi, lt[r:r + 1, :], f)
        fine.append(f)
    v0 = fine[0]
    i0 = jnp.zeros((1, tm), jnp.int32)
    for k in range(1, EXPERTS_PER_GROUP):
        upd = fine[k] > v0
        i0 = jnp.where(upd, k, i0)
        v0 = jnp.where(upd, fine[k], v0)
    v1 = jnp.full((1, tm), -jnp.inf, F32)
    i1 = jnp.zeros((1, tm), jnp.int32)
    for k in range(EXPERTS_PER_GROUP):
        upd = jnp.where(i0 != k, jnp.where(fine[k] > v1, 1, 0), 0) == 1
        i1 = jnp.where(upd, k, i1)
        v1 = jnp.where(upd, fine[k], v1)
    e1w = jnp.exp(v1 - v0)
    gate0 = pg * (1.0 / (1.0 + e1w))
    gate1 = pg * (e1w / (1.0 + e1w))
    e0 = gsel * EXPERTS_PER_GROUP + i0
    e1 = gsel * EXPERTS_PER_GROUP + i1

    eidx = lax.broadcasted_iota(jnp.int32, (N_EXPERTS, tm), 0)
    oh0 = jnp.where(eidx == e0, jnp.float32(1), jnp.float32(0))
    oh1 = jnp.where(eidx == e1, jnp.float32(1), jnp.float32(0))
    ta = lax.broadcasted_iota(jnp.int32, (tm, tm), 0)
    tb = lax.broadcasted_iota(jnp.int32, (tm, tm), 1)
    before = jnp.where(ta < tb, jnp.float32(1), jnp.float32(0)).astype(BF16)
    pre0 = _dot(oh0.astype(BF16), before)
    pre1 = _dot(oh1.astype(BF16), before)
    run = run_ref[:, 0:1]
    tot0 = jnp.sum(oh0, axis=1, keepdims=True)
    tot1 = jnp.sum(oh1, axis=1, keepdims=True)
    rank0 = jnp.sum(oh0 * (run + pre0), axis=0, keepdims=True)
    rank1 = jnp.sum(oh1 * (run + tot0 + pre1), axis=0, keepdims=True)
    new_run = jnp.broadcast_to(run + tot0 + tot1, run_ref.shape)
    run_ref[...] = new_run
    cnt_ref[...] = new_run

    route_ref[0:1, :] = e0.astype(F32)
    route_ref[1:2, :] = e1.astype(F32)
    route_ref[2:3, :] = gate0
    route_ref[3:4, :] = gate1
    route_ref[4:5, :] = rank0
    route_ref[5:6, :] = rank1
    route_ref[6:8, :] = jnp.zeros((2, tm), F32)


def _merge_router_call(oa, ob, zg, x2d, pa, pb, wo, g, wr_t):
    t, d = x2d.shape
    tm = ROW_TM
    const = dict(pipeline_mode=pl.Buffered(1))
    return pl.pallas_call(
        _merge_router_kernel,
        grid=(t // tm,),
        in_specs=[
            pl.BlockSpec((tm, oa.shape[1]), lambda i: (i, 0)),
            pl.BlockSpec((tm, ob.shape[1]), lambda i: (i, 0)),
            pl.BlockSpec((tm, GATE_COLS), lambda i: (i, 0)),
            pl.BlockSpec((tm, d), lambda i: (i, 0)),
            pl.BlockSpec(pa.shape, lambda i: (0, 0), **const),
            pl.BlockSpec(pb.shape, lambda i: (0, 0), **const),
            pl.BlockSpec(wo.shape, lambda i: (0, 0), **const),
            pl.BlockSpec((1, d), lambda i: (0, 0)),
            pl.BlockSpec(wr_t.shape, lambda i: (0, 0), **const),
        ],
        out_specs=[
            pl.BlockSpec((tm, d), lambda i: (i, 0)),
            pl.BlockSpec((tm, d), lambda i: (i, 0)),
            pl.BlockSpec((8, tm), lambda i: (0, i)),
            pl.BlockSpec((N_EXPERTS, LANES), lambda i: (0, 0)),
        ],
        out_shape=[
            jax.ShapeDtypeStruct((t, d), F32),
            jax.ShapeDtypeStruct((t, d), F32),
            jax.ShapeDtypeStruct((8, t), F32),
            jax.ShapeDtypeStruct((N_EXPERTS, LANES), F32),
        ],
        scratch_shapes=[pltpu.VMEM((N_EXPERTS, LANES), F32)],
        compiler_params=_cparams(("arbitrary",)),
        name="merge_router",
    )(oa, ob, zg, x2d, pa, pb, wo, g, wr_t)


def _row_copy(src, s_row, dst, d_row, sem):
    return pltpu.make_async_copy(src.at[pl.ds(s_row, 1)], dst.at[pl.ds(d_row, 1)], sem)


def _dispatch_kernel(dest_ref, zstart_ref, zlen_ref, h_hbm, x_hbm, zero_ref, sem):
    step = pl.program_id(0)
    tok0 = step * DISPATCH_TOK

    def issue(t, carry):
        for k in range(TOP_K):
            _row_copy(h_hbm, tok0 + t, x_hbm, dest_ref[0, k, t], sem.at[k]).start()
        return carry

    lax.fori_loop(0, DISPATCH_TOK, issue, 0)

    def drain(t, carry):
        for k in range(TOP_K):
            _row_copy(h_hbm, 0, x_hbm, 0, sem.at[k]).wait()
        return carry

    lax.fori_loop(0, DISPATCH_TOK, drain, 0)

    @pl.when(step == 0)
    def _():
        zero_ref[...] = jnp.zeros_like(zero_ref)

        def seg_issue(e, carry):
            start = zstart_ref[e]

            def zrow(r, carry2):
                _row_copy(zero_ref, 0, x_hbm, start + r, sem.at[0]).start()
                return carry2

            lax.fori_loop(0, zlen_ref[e], zrow, 0)
            return carry

        lax.fori_loop(0, N_EXPERTS + 1, seg_issue, 0)

        def seg_drain(e, carry):
            def zrow(r, carry2):
                _row_copy(zero_ref, 0, x_hbm, 0, sem.at[0]).wait()
                return carry2

            lax.fori_loop(0, zlen_ref[e], zrow, 0)
            return carry

        lax.fori_loop(0, N_EXPERTS + 1, seg_drain, 0)


def _dispatch_call(dest_blocks, zstart, zlen, h2, n_rows):
    t, d = h2.shape
    return pl.pallas_call(
        _dispatch_kernel,
        grid=(t // DISPATCH_TOK,),
        in_specs=[
            pl.BlockSpec((1, TOP_K, DISPATCH_TOK), lambda i: (i, 0, 0), memory_space=pltpu.SMEM),
            pl.BlockSpec(memory_space=pltpu.SMEM),
            pl.BlockSpec(memory_space=pltpu.SMEM),
            pl.BlockSpec(memory_space=pl.ANY),
        ],
        out_specs=pl.BlockSpec(memory_space=pl.ANY),
        out_shape=jax.ShapeDtypeStruct((n_rows, d), F32),
        scratch_shapes=[pltpu.VMEM((8, d), F32), pltpu.SemaphoreType.DMA((TOP_K,))],
        compiler_params=_cparams(("arbitrary",)),
        name="dispatch",
    )(dest_blocks, zstart, zlen, h2)


def _expert_kernel(blk_e_ref, blk_valid_ref, x_ref, w1_ref, w3_ref, w2_ref, y_ref):
    i = pl.program_id(0)

    @pl.when(blk_valid_ref[i] == 1)
    def _():
        x = x_ref[...].astype(BF16)
        a = _dot(x, w1_ref[0])
        b = _dot(x, w3_ref[0])
        hdn = (a * _sigmoid(a)) * b
        y_ref[...] = _dot(hdn.astype(BF16), w2_ref[0])

    @pl.when(blk_valid_ref[i] == 0)
    def _():
        y_ref[...] = jnp.zeros_like(y_ref)


def _expert_call(blk_e, blk_valid, xin, w1, w3, w2):
    n_rows, d = xin.shape
    de = w1.shape[2]
    grid_spec = pltpu.PrefetchScalarGridSpec(
        num_scalar_prefetch=2,
        grid=(n_rows // EXP_TM,),
        in_specs=[
            pl.BlockSpec((EXP_TM, d), lambda i, be, bv: (i, 0)),
            pl.BlockSpec((1, d, de), lambda i, be, bv: (be[i], 0, 0)),
            pl.BlockSpec((1, d, de), lambda i, be, bv: (be[i], 0, 0)),
            pl.BlockSpec((1, de, d), lambda i, be, bv: (be[i], 0, 0)),
        ],
        out_specs=pl.BlockSpec((EXP_TM, d), lambda i, be, bv: (i, 0)),
    )
    return pl.pallas_call(
        _expert_kernel,
        grid_spec=grid_spec,
        out_shape=jax.ShapeDtypeStruct((n_rows, d), F32),
        compiler_params=_cparams(("arbitrary",)),
        name="experts",
    )(blk_e, blk_valid, xin, w1, w3, w2)


def _combine_kernel(dest_ref, y_hbm, x1_ref, gt_ref, p_ref, gple_ref, wg_ref, wp_ref, gfin_ref,
                    o_ref, ybuf, sem, *, final_norm):
    tm = x1_ref.shape[0]

    def issue(t, carry):
        for k in range(TOP_K):
            _row_copy(y_hbm, dest_ref[0, k, t], ybuf.at[k], t, sem.at[k]).start()
        return carry

    lax.fori_loop(0, tm, issue, 0)
    pp = _dot(p_ref[...].astype(BF16), wp_ref[...])

    def drain(t, carry):
        for k in range(TOP_K):
            _row_copy(y_hbm, 0, ybuf.at[k], t, sem.at[k]).wait()
        return carry

    lax.fori_loop(0, tm, drain, 0)
    y = ybuf[0] * gt_ref[:, 2:3] + ybuf[1] * gt_ref[:, 3:4]
    x2 = x1_ref[...] + y
    hn = _rmsnorm_val(x2, gple_ref[...], RMS_EPS)
    gate = _sigmoid(_dot(hn.astype(BF16), wg_ref[...]))
    x3 = x2 + gate * pp
    if final_norm:
        o_ref[...] = _rmsnorm_val(x3, gfin_ref[...], RMS_EPS)
    else:
        o_ref[...] = x3


def _combine_call(dest_blocks, yb, x1, gates_t, p2d, g_ple, w_gate, w_proj, g_fin, final_norm):
    t, d = x1.shape
    tm = ROW_TM
    const = dict(pipeline_mode=pl.Buffered(1))
    return pl.pallas_call(
        functools.partial(_combine_kernel, final_norm=final_norm),
        grid=(t // tm,),
        in_specs=[
            pl.BlockSpec((1, TOP_K, tm), lambda i: (i, 0, 0), memory_space=pltpu.SMEM),
            pl.BlockSpec(memory_space=pl.ANY),
            pl.BlockSpec((tm, d), lambda i: (i, 0)),
            pl.BlockSpec((tm, 8), lambda i: (i, 0)),
            pl.BlockSpec((tm, p2d.shape[1]), lambda i: (i, 0)),
            pl.BlockSpec((1, d), lambda i: (0, 0)),
            pl.BlockSpec(w_gate.shape, lambda i: (0, 0), **const),
            pl.BlockSpec(w_proj.shape, lambda i: (0, 0), **const),
            pl.BlockSpec((1, d), lambda i: (0, 0)),
        ],
        out_specs=pl.BlockSpec((tm, d), lambda i: (i, 0)),
        out_shape=jax.ShapeDtypeStruct((t, d), F32),
        scratch_shapes=[pltpu.VMEM((TOP_K, tm, d), F32), pltpu.SemaphoreType.DMA((TOP_K,))],
        compiler_params=_cparams(("arbitrary",)),
        name="combine",
    )(dest_blocks, yb, x1, gates_t, p2d, g_ple, w_gate, w_proj, g_fin)


def _routing_plan(route, counts_f, t):
    counts = counts_f[:, 0].astype(jnp.int32)
    pcounts = ((counts + EXP_TM - 1) // EXP_TM) * EXP_TM
    pends = jnp.cumsum(pcounts)
    pstarts = pends - pcounts
    eid = route[0:2].astype(jnp.int32)
    rank = route[4:6].astype(jnp.int32)
    onehot = eid[..., None] == jnp.arange(N_EXPERTS, dtype=jnp.int32)
    dest = jnp.sum(jnp.where(onehot, pstarts, 0), axis=-1) + rank
    n_rows = TOP_K * t + N_EXPERTS * EXP_TM
    n_blocks = n_rows // EXP_TM
    blk_row = jnp.arange(n_blocks, dtype=jnp.int32) * EXP_TM
    blk_e = jnp.sum((pends[None, :] <= blk_row[:, None]).astype(jnp.int32), axis=1)
    blk_valid = (blk_row < pends[-1]).astype(jnp.int32)
    last_e = jnp.sum((pends <= pends[-1] - 1).astype(jnp.int32))
    blk_e = jnp.minimum(blk_e, last_e)
    zstart = jnp.concatenate([pstarts + counts, pends[-1:]]).astype(jnp.int32)
    zlen = jnp.concatenate([pcounts - counts, n_rows - pends[-1:]]).astype(jnp.int32)
    return dest, blk_e, blk_valid, zstart, zlen, n_rows


def _blocked(dest, tok_per_block):
    t = dest.shape[1]
    return dest.reshape(TOP_K, t // tok_per_block, tok_per_block).transpose(1, 0, 2)


def kernel(x, p, rel_bias, norm_mix_g, w_in, w_gate, lambda_q1, lambda_k1, lambda_q2, lambda_k2, subln_g,
           w_proj_a, w_proj_b, w_out, norm_ffn_g, w_coarse, w_fine, w1, w3, w2, norm_ple_g, w_ple_gate,
           w_ple_proj, final_norm_g):
    batch, seq, d = x.shape
    depth = w_in.shape[0]
    t = batch * seq
    assert seq % PERM_BLK == 0 and t % min(PROJ_TM, t) == 0
    assert all(seq // dil >= BLK and win // dil == BLK for win, dil in DILATED_GROUPS)

    nq = seq // ATT_BLK
    bias_a = _bias_tiles_call(rel_bias, n_heads=A_HEADS, n_off=nq, tq=ATT_BLK, tk=ATT_BLK, off_mult=ATT_BLK,
                              off_add=0, dil=1, max_rel=seq, head0=0, name="bias_diff")
    bias_b = jnp.concatenate([
        _bias_tiles_call(rel_bias, n_heads=B_HEADS, n_off=1, tq=BLK, tk=2 * BLK, off_mult=0, off_add=BLK,
                         dil=dil, max_rel=win // dil, head0=A_HEADS + gi * B_HEADS,
                         name="bias_dil%d" % gi).reshape(1, B_HEADS, BLK, 2 * BLK)
        for gi, (win, dil) in enumerate(DILATED_GROUPS)], axis=0)

    x2d = x.reshape(t, d)
    for layer in range(depth):
        lam_init = 0.8 - 0.6 * math.exp(-0.3 * layer)
        lam = _lam_call(lambda_q1[layer:layer + 1], lambda_k1[layer:layer + 1],
                        lambda_q2[layer:layer + 1], lambda_k2[layer:layer + 1], lam_init)
        w_cat = jnp.concatenate([w_gate[layer], w_in[layer]], axis=1).astype(BF16)
        zg = _inproj_call(x2d, norm_mix_g[layer:layer + 1], w_cat)
        oa = _diff_attn_call(zg, lam, bias_a, subln_g[layer:layer + 1], batch=batch, seq=seq,
                             out_scale=1.0 - lam_init)
        ob = _dilated_call(zg, bias_b, batch=batch, seq=seq)

        wr_t = jnp.concatenate([
            w_coarse[layer].T,
            w_fine[layer].transpose(0, 2, 1).reshape(N_EXPERTS, d),
            jnp.zeros((LANES - N_EXPERT_GROUPS - N_EXPERTS, d), F32)], axis=0)
        x1, h2, route, counts = _merge_router_call(
            oa, ob, zg, x2d, w_proj_a[layer].astype(BF16), w_proj_b[layer].astype(BF16),
            w_out[layer].astype(BF16), norm_ffn_g[layer:layer + 1], wr_t)

        dest, blk_e, blk_valid, zstart, zlen, n_rows = _routing_plan(route, counts, t)
        xin = _dispatch_call(_blocked(dest, DISPATCH_TOK), zstart, zlen, h2, n_rows)
        yb = _expert_call(blk_e, blk_valid, xin, w1[layer].astype(BF16), w3[layer].astype(BF16),
                          w2[layer].astype(BF16))
        x2d = _combine_call(_blocked(dest, ROW_TM), yb, x1, route.T, p[layer].reshape(t, -1),
                            norm_ple_g[layer:layer + 1], w_ple_gate[layer].astype(BF16),
                            w_ple_proj[layer].astype(BF16), final_norm_g.reshape(1, d),
                            final_norm=layer == depth - 1)
    return x2d.reshape(batch, seq, d)
```

```python
import functools
import math

import jax
import jax.numpy as jnp
from jax import lax
from jax.experimental import pallas as pl
from jax.experimental.pallas import tpu as pltpu

F32 = jnp.float32
BF16 = jnp.bfloat16

BLK = 128
NEG_INF = -1e30
RMS_EPS = 1e-6
SUBLN_EPS = 1e-5
N_BUCKETS = 32
MAX_DISTANCE = 2048
A_HEADS = 8
A_HEAD_DIM = 128
DILATED_GROUPS = ((128, 1), (512, 4), (2048, 16))
N_DGROUPS = 3
B_HEADS = 8
B_HEAD_DIM = 64
N_EXPERT_GROUPS = 4
EXPERTS_PER_GROUP = 8
N_EXPERTS = 32
TOP_K = 2

LANES = 128
VMEM_LIMIT = 56 * 1024 * 1024
ATT_BLK = 256
PERM_BLK = 256
GATE_COLS = 4096
PROJ_TN = 512
PROJ_TM = 1024
ROW_TM = 256
EXP_TM = 256
DISPATCH_TOK = 512
NORM_CHUNK = 128


def _t5_thresholds():
    max_exact = N_BUCKETS // 2
    out = []
    for k in range(1, N_BUCKETS - max_exact):
        out.append(int(math.ceil(max_exact * (MAX_DISTANCE / max_exact) ** (k / (N_BUCKETS - max_exact)))))
    return tuple(out)


T5_THRESHOLDS = _t5_thresholds()


def _cparams(sem, vmem=VMEM_LIMIT):
    return pltpu.CompilerParams(dimension_semantics=sem, vmem_limit_bytes=vmem)


def _mult(x, m):
    return x if isinstance(x, int) else pl.multiple_of(x, m)


def _sigmoid(x):
    return 1.0 / (1.0 + jnp.exp(-x))


def _dot(a, b, **kw):
    return jnp.dot(a, b, preferred_element_type=F32, **kw)


def _dot_nt(a, b, **kw):
    return lax.dot_general(a, b, (((1,), (1,)), ((), ())), preferred_element_type=F32, **kw)


def _lam_kernel(q1_ref, k1_ref, q2_ref, k2_ref, o_ref, *, lam_init):
    s1 = jnp.sum(q1_ref[...] * k1_ref[...], axis=-1, keepdims=True)
    s2 = jnp.sum(q2_ref[...] * k2_ref[...], axis=-1, keepdims=True)
    o_ref[...] = jnp.exp(s1) - jnp.exp(s2) + lam_init


def _lam_call(lq1, lk1, lq2, lk2, lam_init):
    return pl.pallas_call(
        functools.partial(_lam_kernel, lam_init=lam_init),
        out_shape=jax.ShapeDtypeStruct((1, 1), F32),
        name="lam",
    )(lq1, lk1, lq2, lk2)


def _bias_tile_kernel(tab_ref, o_ref, *, tq, tk, off_mult, off_add, dil, max_rel, head0, key_major, mult):
    h = pl.program_id(0)
    n = pl.program_id(1)
    i = lax.broadcasted_iota(jnp.int32, (tq, tk), 1 if key_major else 0)
    j = lax.broadcasted_iota(jnp.int32, (tq, tk), 0 if key_major else 1)
    rel = i - j + (n * off_mult + off_add)
    dist = rel * dil
    large = jnp.full((tq, tk), N_BUCKETS // 2, jnp.int32)
    for thr in T5_THRESHOLDS:
        large = large + jnp.where(dist >= thr, 1, 0)
    bucket = jnp.where(dist < N_BUCKETS // 2, dist, large)
    acc = jnp.zeros((tq, tk), F32)
    for b in range(N_BUCKETS):
        acc = jnp.where(bucket == b, tab_ref[b, head0 + h], acc)
    valid = jnp.where(rel >= 0, jnp.where(rel <= max_rel, 1, 0), 0)
    o_ref[0, 0] = jnp.where(valid == 1, acc * mult, NEG_INF)


def _bias_tiles_call(rel_bias, *, n_heads, n_off, tq, tk, off_mult, off_add, dil, max_rel, head0, name,
                     key_major=False, mult=1.0):
    kern = functools.partial(_bias_tile_kernel, tq=tq, tk=tk, off_mult=off_mult, off_add=off_add,
                             dil=dil, max_rel=max_rel, head0=head0, key_major=key_major, mult=mult)
    return pl.pallas_call(
        kern,
        grid=(n_heads, n_off),
        in_specs=[pl.BlockSpec(memory_space=pltpu.SMEM)],
        out_specs=pl.BlockSpec((1, 1, tq, tk), lambda h, n: (h, n, 0, 0)),
        out_shape=jax.ShapeDtypeStruct((n_heads, n_off, tq, tk), F32),
        compiler_params=_cparams(("parallel", "parallel")),
        name=name,
    )(rel_bias)


def _rmsnorm_rows(x_ref, g_ref, out_ref, eps):
    rows = x_ref.shape[0]
    g = g_ref[...]

    def body(c, carry):
        r0 = pl.multiple_of(c * NORM_CHUNK, NORM_CHUNK)
        x = x_ref[pl.ds(r0, NORM_CHUNK), :]
        ms = jnp.mean(x * x, axis=-1, keepdims=True)
        out_ref[pl.ds(r0, NORM_CHUNK), :] = ((x * lax.rsqrt(ms + eps)) * g).astype(out_ref.dtype)
        return carry

    lax.fori_loop(0, rows // NORM_CHUNK, body, 0)


def _rmsnorm_val(x, g, eps):
    ms = jnp.mean(x * x, axis=-1, keepdims=True)
    return (x * lax.rsqrt(ms + eps)) * g


def _inproj_kernel(x_ref, g_ref, w_ref, o_ref, h_ref, *, n_gate_blocks):
    j = pl.program_id(1)

    @pl.when(j == 0)
    def _():
        _rmsnorm_rows(x_ref, g_ref, h_ref, RMS_EPS)

    acc = _dot(h_ref[...], w_ref[...])

    @pl.when(j < n_gate_blocks)
    def _():
        o_ref[...] = _sigmoid(acc).astype(o_ref.dtype)

    @pl.when(j >= n_gate_blocks)
    def _():
        o_ref[...] = acc.astype(o_ref.dtype)


def _inproj_call(x2d, g, w_cat):
    t, d = x2d.shape
    n = w_cat.shape[1]
    tm = min(PROJ_TM, t)
    return pl.pallas_call(
        functools.partial(_inproj_kernel, n_gate_blocks=GATE_COLS // PROJ_TN),
        grid=(t // tm, n // PROJ_TN),
        in_specs=[
            pl.BlockSpec((tm, d), lambda i, j: (i, 0)),
            pl.BlockSpec((1, d), lambda i, j: (0, 0)),
            pl.BlockSpec((d, PROJ_TN), lambda i, j: (0, j)),
        ],
        out_specs=pl.BlockSpec((tm, PROJ_TN), lambda i, j: (i, j)),
        out_shape=jax.ShapeDtypeStruct((t, n), BF16),
        scratch_shapes=[pltpu.VMEM((tm, d), BF16)],
        compiler_params=_cparams(("parallel", "arbitrary")),
        name="inproj",
    )(x2d, g, w_cat)


def _diff_attn_kernel(lam_ref, q_ref, k_ref, v_ref, bias_ref, g_ref, o_ref, vt_ref, tbuf_a, tbuf_b,
                      acc1, acc2, *,
                      out_scale, n_blk):
    qi = pl.program_id(2)
    c = (A_HEAD_DIM ** -0.5) * math.log2(math.e)

    @pl.when(qi == 0)
    def _():
        def transpose_block(b, carry):
            r0 = pl.multiple_of(b * ATT_BLK, ATT_BLK)
            vt_ref[b] = v_ref[pl.ds(r0, ATT_BLK), :].astype(F32).T.astype(BF16)
            return carry

        lax.fori_loop(0, n_blk, transpose_block, 0)

    q1 = q_ref[:, :A_HEAD_DIM]
    q2 = q_ref[:, A_HEAD_DIM:]
    acc1[...] = jnp.zeros_like(acc1)
    acc2[...] = jnp.zeros_like(acc2)

    def scores(ki, dst):
        k0 = pl.multiple_of(ki * ATT_BLK, ATT_BLK)
        kb = k_ref[pl.ds(k0, ATT_BLK), :]
        bias = bias_ref[0, qi - ki]
        dst[0] = _dot_nt(kb[:, :A_HEAD_DIM], q1) + bias
        dst[1] = _dot_nt(kb[:, A_HEAD_DIM:], q2) + bias

    def one_map(t, vt, m, l, acc):
        m_new = jnp.maximum(m, jnp.max(t, axis=0, keepdims=True))
        alpha = jnp.exp2((m - m_new) * c)
        p = jnp.exp2((t - m_new) * c)
        l_new = alpha * l + jnp.sum(p, axis=0, keepdims=True)
        acc[...] = acc[...] * alpha + _dot(vt, p.astype(BF16))
        return m_new, l_new

    scores(0, tbuf_a)

    def step(ki, carry, cur, nxt):
        m1, l1, m2, l2 = carry
        scores(jnp.minimum(ki + 1, qi), nxt)
        vt = vt_ref[ki]
        m1, l1 = one_map(cur[0], vt, m1, l1, acc1)
        m2, l2 = one_map(cur[1], vt, m2, l2, acc2)
        return m1, l1, m2, l2

    def body(ki, carry):
        return lax.cond(lax.bitwise_and(ki, 1) == 0,
                        lambda cr: step(ki, cr, tbuf_a, tbuf_b),
                        lambda cr: step(ki, cr, tbuf_b, tbuf_a), carry)

    minf = jnp.full((1, ATT_BLK), -jnp.inf, F32)
    zero = jnp.zeros((1, ATT_BLK), F32)
    m1, l1, m2, l2 = lax.fori_loop(0, qi + 1, body, (minf, zero, minf, zero))
    w = acc1[...] / l1 - lam_ref[0, 0] * (acc2[...] / l2)
    ms = jnp.mean(w * w, axis=0, keepdims=True)
    y = ((w * lax.rsqrt(ms + SUBLN_EPS)) * g_ref[...]) * out_scale
    o_ref[...] = y.T.astype(o_ref.dtype)


def _diff_attn_call(zg, lam, bias_tiles, subln_g, *, batch, seq, out_scale):
    t = batch * seq
    nq = seq // ATT_BLK
    width = 2 * A_HEAD_DIM
    q_col = GATE_COLS // width
    k_col = q_col + A_HEADS
    v_col = k_col + A_HEADS
    return pl.pallas_call(
        functools.partial(_diff_attn_kernel, out_scale=out_scale, n_blk=nq),
        grid=(A_HEADS, batch, nq),
        in_specs=[
            pl.BlockSpec(memory_space=pltpu.SMEM),
            pl.BlockSpec((ATT_BLK, width), lambda h, b, i: (b * nq + i, q_col + h)),
            pl.BlockSpec((seq, width), lambda h, b, i: (b, k_col + h)),
            pl.BlockSpec((seq, width), lambda h, b, i: (b, v_col + h)),
            pl.BlockSpec((1, nq, ATT_BLK, ATT_BLK), lambda h, b, i: (h, 0, 0, 0)),
            pl.BlockSpec((width, 1), lambda h, b, i: (0, 0)),
        ],
        out_specs=pl.BlockSpec((ATT_BLK, width), lambda h, b, i: (b * nq + i, h)),
        out_shape=jax.ShapeDtypeStruct((t, A_HEADS * width), BF16),
        scratch_shapes=[pltpu.VMEM((nq, width, ATT_BLK), BF16),
                        pltpu.VMEM((2, ATT_BLK, ATT_BLK), F32), pltpu.VMEM((2, ATT_BLK, ATT_BLK), F32),
                        pltpu.VMEM((width, ATT_BLK), F32), pltpu.VMEM((width, ATT_BLK), F32)],
        compiler_params=_cparams(("parallel", "parallel", "arbitrary")),
        name="diff_attn",
    )(lam, zg, zg, zg, bias_tiles, subln_g.reshape(width, 1))


def _perm_matrix(dil, inverse):
    w = PERM_BLK // dil
    shift = w.bit_length() - 1
    a = lax.broadcasted_iota(jnp.int32, (PERM_BLK, PERM_BLK), 0)
    b = lax.broadcasted_iota(jnp.int32, (PERM_BLK, PERM_BLK), 1)
    dst, src = (b, a) if inverse else (a, b)
    c = lax.shift_right_logical(dst, shift)
    ll = lax.bitwise_and(dst, w - 1)
    return jnp.where(src == ll * dil + c, jnp.float32(1), jnp.float32(0))


def _deinterleave(src_ref, dst_ref, perm, dil, seq):
    w = PERM_BLK // dil
    stream_len = seq // dil

    def body(b8, carry):
        r0 = pl.multiple_of(b8 * PERM_BLK, PERM_BLK)
        y = _dot(perm, src_ref[pl.ds(r0, PERM_BLK), :]).astype(dst_ref.dtype)
        for c in range(dil):
            d0 = pl.multiple_of(c * stream_len + b8 * w, w)
            dst_ref[pl.ds(d0, w), :] = y[c * w:(c + 1) * w, :]
        return carry

    lax.fori_loop(0, seq // PERM_BLK, body, 0)


def _interleave(src_ref, dst_ref, stack_ref, perm_inv, dil, seq, **dot_kw):
    w = PERM_BLK // dil
    stream_len = seq // dil

    def body(b8, carry):
        for c in range(dil):
            s0 = pl.multiple_of(c * stream_len + b8 * w, w)
            stack_ref[c * w:(c + 1) * w, :] = src_ref[pl.ds(s0, w), :]
        r0 = pl.multiple_of(b8 * PERM_BLK, PERM_BLK)
        dst_ref[pl.ds(r0, PERM_BLK), :] = _dot(perm_inv, stack_ref[...], **dot_kw).astype(dst_ref.dtype)
        return carry

    lax.fori_loop(0, seq // PERM_BLK, body, 0)


def _dil_window(q_src, k_src, v_src, o_dst, lse_dst, bias_ref, rq, rk, nk):
    scale = B_HEAD_DIM ** -0.5
    lane = lax.broadcasted_iota(jnp.int32, (BLK, LANES), 1)
    lse_acc = jnp.zeros((BLK, LANES), F32)
    for hh in range(B_HEADS):
        cols = slice(hh * B_HEAD_DIM, (hh + 1) * B_HEAD_DIM)
        qh = q_src[pl.ds(rq, BLK), cols]
        kh = k_src[pl.ds(rk, nk), cols]
        vh = v_src[pl.ds(rk, nk), cols]
        bias = bias_ref[0, hh, :, 2 * BLK - nk:]
        s = _dot_nt(qh, kh) * scale + bias
        m = jnp.max(s, axis=-1, keepdims=True)
        e = jnp.exp(s - m)
        den = jnp.sum(e, axis=-1, keepdims=True)
        o = _dot(e.astype(BF16), vh) / den
        o_dst[pl.ds(rq, BLK), cols] = o.astype(o_dst.dtype)
        lse_acc = jnp.where(lane == hh, m + jnp.log(den), lse_acc)
    lse_dst[pl.ds(rq, BLK), :] = lse_acc


def _dil_streams(q_src, k_src, v_src, o_dst, lse_dst, bias_ref, dil, seq):
    stream_len = seq // dil
    nq = stream_len // BLK

    def stream(c, carry):
        base = _mult(c * stream_len, BLK)
        _dil_window(q_src, k_src, v_src, o_dst, lse_dst, bias_ref, base, base, BLK)

        def qblock(n, carry2):
            rq = pl.multiple_of(base + n * BLK, BLK)
            _dil_window(q_src, k_src, v_src, o_dst, lse_dst, bias_ref, rq, rq - BLK, 2 * BLK)
            return carry2

        if nq > 1:
            lax.fori_loop(1, nq, qblock, 0)
        return carry

    if dil == 1:
        stream(0, 0)
    else:
        lax.fori_loop(0, dil, stream, 0)


def _dil_merge(o_src, lse_src, oacc, mrun, lrun, out_ref, first, last, seq):
    def body(ch, carry):
        r0 = pl.multiple_of(ch * ROW_TM, ROW_TM)
        rows = pl.ds(r0, ROW_TM)
        lse = lse_src[rows, :]
        if first:
            mrun[rows, :] = lse
            lrun[rows, :] = jnp.ones_like(lse)
            oacc[rows, :] = o_src[rows, :].astype(F32)
            return carry
        m_old = mrun[rows, :]
        m_new = jnp.maximum(m_old, lse)
        a = jnp.exp(m_old - m_new)
        bw = jnp.exp(lse - m_new)
        l_new = lrun[rows, :] * a + bw
        mrun[rows, :] = m_new
        lrun[rows, :] = l_new
        for hh in range(B_HEADS):
            cols = slice(hh * B_HEAD_DIM, (hh + 1) * B_HEAD_DIM)
            val = oacc[rows, cols] * a[:, hh:hh + 1] + o_src[rows, cols].astype(F32) * bw[:, hh:hh + 1]
            if last:
                out_ref[rows, cols] = (val / l_new[:, hh:hh + 1]).astype(out_ref.dtype)
            else:
                oacc[rows, cols] = val
        return carry

    lax.fori_loop(0, seq // ROW_TM, body, 0)


def _dilated_kernel(q_ref, k_ref, v_ref, bias_ref, out_ref,
                    qs, ks, vs, os_, lses, otok, lsetok, ostack, lstack, oacc, mrun, lrun, *, seq):
    g = pl.program_id(1)
    for gi, (_, dil) in enumerate(DILATED_GROUPS):

        @pl.when(g == gi)
        def _(gi=gi, dil=dil):
            first = gi == 0
            last = gi == N_DGROUPS - 1
            if dil == 1:
                _dil_streams(q_ref, k_ref, v_ref, otok, lsetok, bias_ref, 1, seq)
            else:
                perm = _perm_matrix(dil, inverse=False).astype(BF16)
                _deinterleave(q_ref, qs, perm, dil, seq)
                _deinterleave(k_ref, ks, perm, dil, seq)
                _deinterleave(v_ref, vs, perm, dil, seq)
                _dil_streams(qs, ks, vs, os_, lses, bias_ref, dil, seq)
                perm_inv = _perm_matrix(dil, inverse=True)
                _interleave(os_, otok, ostack, perm_inv.astype(BF16), dil, seq)
                _interleave(lses, lsetok, lstack, perm_inv, dil, seq, precision=lax.Precision.HIGHEST)
            _dil_merge(otok, lsetok, oacc, mrun, lrun, out_ref, first, last, seq)


def _dilated_call(zg, bias_tiles, *, batch, seq):
    width = B_HEADS * B_HEAD_DIM
    q_col = (GATE_COLS + 3 * A_HEADS * 2 * A_HEAD_DIM) // width
    k_col = q_col + N_DGROUPS
    v_col = k_col + N_DGROUPS
    return pl.pallas_call(
        functools.partial(_dilated_kernel, seq=seq),
        grid=(batch, N_DGROUPS),
        in_specs=[
            pl.BlockSpec((seq, width), lambda b, g: (b, q_col + g)),
            pl.BlockSpec((seq, width), lambda b, g: (b, k_col + g)),
            pl.BlockSpec((seq, width), lambda b, g: (b, v_col + g)),
            pl.BlockSpec((1, B_HEADS, BLK, 2 * BLK), lambda b, g: (g, 0, 0, 0)),
        ],
        out_specs=pl.BlockSpec((seq, width), lambda b, g: (b, 0)),
        out_shape=jax.ShapeDtypeStruct((batch * seq, width), BF16),
        scratch_shapes=[
            pltpu.VMEM((seq, width), BF16), pltpu.VMEM((seq, width), BF16), pltpu.VMEM((seq, width), BF16),
            pltpu.VMEM((seq, width), BF16), pltpu.VMEM((seq, LANES), F32),
            pltpu.VMEM((seq, width), BF16), pltpu.VMEM((seq, LANES), F32),
            pltpu.VMEM((PERM_BLK, width), BF16), pltpu.VMEM((PERM_BLK, LANES), F32),
            pltpu.VMEM((seq, width), F32), pltpu.VMEM((seq, LANES), F32), pltpu.VMEM((seq, LANES), F32),
        ],
        compiler_params=_cparams(("parallel", "arbitrary")),
        name="dilated_attn",
    )(zg, zg, zg, bias_tiles)


def _merge_router_kernel(oa_ref, ob_ref, gate_ref, x_ref, pa_ref, pb_ref, wo_ref, g_ref, wr_ref,
                         x1_ref, h2_ref, route_ref, cnt_ref, run_ref):
    tm, d = x_ref.shape

    @pl.when(pl.program_id(0) == 0)
    def _():
        run_ref[...] = jnp.zeros_like(run_ref)

    a = _dot(oa_ref[...], pa_ref[...])
    bm = _dot(ob_ref[...], pb_ref[...])
    merged = gate_ref[:, :d].astype(F32) * a + gate_ref[:, d:].astype(F32) * bm
    x1 = x_ref[...] + _dot(merged.astype(BF16), wo_ref[...])
    x1_ref[...] = x1
    h2 = _rmsnorm_val(x1, g_ref[...], RMS_EPS)
    h2_ref[...] = h2

    lt = _dot_nt(wr_ref[...], h2, precision=lax.Precision.HIGHEST)
    coarse = [lt[i:i + 1, :] for i in range(N_EXPERT_GROUPS)]
    best = coarse[0]
    gsel = jnp.zeros((1, tm), jnp.int32)
    for i in range(1, N_EXPERT_GROUPS):
        upd = coarse[i] > best
        gsel = jnp.where(upd, i, gsel)
        best = jnp.where(upd, coarse[i], best)
    den = jnp.exp(coarse[0] - best)
    for i in range(1, N_EXPERT_GROUPS):
        den = den + jnp.exp(coarse[i] - best)
    pg = 1.0 / den

    fine = []
    for k in range(EXPERTS_PER_GROUP):
        f = lt[N_EXPERT_GROUPS + k:N_EXPERT_GROUPS + k + 1, :]
        for gi in range(1, N_EXPERT_GROUPS):
            r = N_EXPERT_GROUPS + gi * EXPERTS_PER_GROUP + k
            f = jnp.where(gsel == gi, lt[r:r + 1, :], f)
        fine.append(f)
    v0 = fine[0]
    i0 = jnp.zeros((1, tm), jnp.int32)
    for k in range(1, EXPERTS_PER_GROUP):
        upd = fine[k] > v0
        i0 = jnp.where(upd, k, i0)
        v0 = jnp.where(upd, fine[k], v0)
    v1 = jnp.full((1, tm), -jnp.inf, F32)
    i1 = jnp.zeros((1, tm), jnp.int32)
    for k in range(EXPERTS_PER_GROUP):
        upd = jnp.where(i0 != k, jnp.where(fine[k] > v1, 1, 0), 0) == 1
        i1 = jnp.where(upd, k, i1)
        v1 = jnp.where(upd, fine[k], v1)
    e1w = jnp.exp(v1 - v0)
    gate0 = pg * (1.0 / (1.0 + e1w))
    gate1 = pg * (e1w / (1.0 + e1w))
    e0 = gsel * EXPERTS_PER_GROUP + i0
    e1 = gsel * EXPERTS_PER_GROUP + i1

    eidx = lax.broadcasted_iota(jnp.int32, (N_EXPERTS, tm), 0)
    oh0 = jnp.where(eidx == e0, jnp.float32(1), jnp.float32(0))
    oh1 = jnp.where(eidx == e1, jnp.float32(1), jnp.float32(0))
    ta = lax.broadcasted_iota(jnp.int32, (tm, tm), 0)
    tb = lax.broadcasted_iota(jnp.int32, (tm, tm), 1)
    before = jnp.where(ta < tb, jnp.float32(1), jnp.float32(0)).astype(BF16)
    pre0 = _dot(oh0.astype(BF16), before)
    pre1 = _dot(oh1.astype(BF16), before)
    run = run_ref[:, 0:1]
    tot0 = jnp.sum(oh0, axis=1, keepdims=True)
    tot1 = jnp.sum(oh1, axis=1, keepdims=True)
    rank0 = jnp.sum(oh0 * (run + pre0), axis=0, keepdims=True)
    rank1 = jnp.sum(oh1 * (run + tot0 + pre1), axis=0, keepdims=True)
    new_run = jnp.broadcast_to(run + tot0 + tot1, run_ref.shape)
    run_ref[...] = new_run
    cnt_ref[...] = new_run

    route_ref[0:1, :] = e0.astype(F32)
    route_ref[1:2, :] = e1.astype(F32)
    route_ref[2:3, :] = gate0
    route_ref[3:4, :] = gate1
    route_ref[4:5, :] = rank0
    route_ref[5:6, :] = rank1
    route_ref[6:8, :] = jnp.zeros((2, tm), F32)


def _merge_router_call(oa, ob, zg, x2d, pa, pb, wo, g, wr_t):
    t, d = x2d.shape
    tm = ROW_TM
    const = dict(pipeline_mode=pl.Buffered(1))
    return pl.pallas_call(
        _merge_router_kernel,
        grid=(t // tm,),
        in_specs=[
            pl.BlockSpec((tm, oa.shape[1]), lambda i: (i, 0)),
            pl.BlockSpec((tm, ob.shape[1]), lambda i: (i, 0)),
            pl.BlockSpec((tm, GATE_COLS), lambda i: (i, 0)),
            pl.BlockSpec((tm, d), lambda i: (i, 0)),
            pl.BlockSpec(pa.shape, lambda i: (0, 0), **const),
            pl.BlockSpec(pb.shape, lambda i: (0, 0), **const),
            pl.BlockSpec(wo.shape, lambda i: (0, 0), **const),
            pl.BlockSpec((1, d), lambda i: (0, 0)),
            pl.BlockSpec(wr_t.shape, lambda i: (0, 0), **const),
        ],
        out_specs=[
            pl.BlockSpec((tm, d), lambda i: (i, 0)),
            pl.BlockSpec((tm, d), lambda i: (i, 0)),
            pl.BlockSpec((8, tm), lambda i: (0, i)),
            pl.BlockSpec((N_EXPERTS, LANES), lambda i: (0, 0)),
        ],
        out_shape=[
            jax.ShapeDtypeStruct((t, d), F32),
            jax.ShapeDtypeStruct((t, d), F32),
            jax.ShapeDtypeStruct((8, t), F32),
            jax.ShapeDtypeStruct((N_EXPERTS, LANES), F32),
        ],
        scratch_shapes=[pltpu.VMEM((N_EXPERTS, LANES), F32)],
        compiler_params=_cparams(("arbitrary",)),
        name="merge_router",
    )(oa, ob, zg, x2d, pa, pb, wo, g, wr_t)


def _row_copy(src, s_row, dst, d_row, sem):
    return pltpu.make_async_copy(src.at[pl.ds(s_row, 1)], dst.at[pl.ds(d_row, 1)], sem)


def _dispatch_kernel(dest_ref, zstart_ref, zlen_ref, h_ref, x_hbm, zero_ref, sem, zsem):
    step = pl.program_id(0)

    def issue(t, carry):
        for k in range(TOP_K):
            _row_copy(h_ref, t, x_hbm, dest_ref[0, k, t], sem.at[k]).start()
        return carry

    lax.fori_loop(0, DISPATCH_TOK, issue, 0)

    @pl.when(step == 0)
    def _():
        zero_ref[...] = jnp.zeros_like(zero_ref)

        def group_copy(r8):
            return pltpu.make_async_copy(zero_ref, x_hbm.at[pl.ds(pl.multiple_of(r8, 8), 8)], zsem.at[1])

        def segment(e, wait):
            start = zstart_ref[e]
            n_head = jnp.minimum(lax.bitwise_and(-start, 7), zlen_ref[e])
            n_group = lax.shift_right_logical(zlen_ref[e] - n_head, 3)

            def head(r, carry):
                cp = _row_copy(zero_ref, 0, x_hbm, 0 if wait else start + r, zsem.at[0])
                cp.wait() if wait else cp.start()
                return carry

            def group(j, carry):
                cp = group_copy(0 if wait else start + n_head + 8 * j)
                cp.wait() if wait else cp.start()
                return carry

            lax.fori_loop(0, n_head, head, 0)
            lax.fori_loop(0, n_group, group, 0)

        lax.fori_loop(0, N_EXPERTS + 1, lambda e, c: (segment(e, False), c)[1], 0)
        lax.fori_loop(0, N_EXPERTS + 1, lambda e, c: (segment(e, True), c)[1], 0)

    def drain(t, carry):
        for k in range(TOP_K):
            _row_copy(h_ref, 0, x_hbm, 0, sem.at[k]).wait()
        return carry

    lax.fori_loop(0, DISPATCH_TOK, drain, 0)


def _dispatch_call(dest_blocks, zstart, zlen, h2, n_rows):
    t, d = h2.shape
    return pl.pallas_call(
        _dispatch_kernel,
        grid=(t // DISPATCH_TOK,),
        in_specs=[
            pl.BlockSpec((1, TOP_K, DISPATCH_TOK), lambda i: (i, 0, 0), memory_space=pltpu.SMEM),
            pl.BlockSpec(memory_space=pltpu.SMEM),
            pl.BlockSpec(memory_space=pltpu.SMEM),
            pl.BlockSpec((DISPATCH_TOK, d), lambda i: (i, 0)),
        ],
        out_specs=pl.BlockSpec(memory_space=pl.ANY),
        out_shape=jax.ShapeDtypeStruct((n_rows, d), F32),
        scratch_shapes=[pltpu.VMEM((8, d), F32), pltpu.SemaphoreType.DMA((TOP_K,)),
                        pltpu.SemaphoreType.DMA((2,))],
        compiler_params=_cparams(("arbitrary",)),
        name="dispatch",
    )(dest_blocks, zstart, zlen, h2)


def _expert_kernel(blk_e_ref, blk_valid_ref, x_ref, w1_ref, w3_ref, w2_ref, y_ref):
    i = pl.program_id(0)

    @pl.when(blk_valid_ref[i] == 1)
    def _():
        x = x_ref[...].astype(BF16)
        a = _dot(x, w1_ref[0])
        b = _dot(x, w3_ref[0])
        hdn = (a * _sigmoid(a)) * b
        y_ref[...] = _dot(hdn.astype(BF16), w2_ref[0])

    @pl.when(blk_valid_ref[i] == 0)
    def _():
        y_ref[...] = jnp.zeros_like(y_ref)


def _expert_call(blk_e, blk_valid, xin, w1, w3, w2):
    n_rows, d = xin.shape
    de = w1.shape[2]
    grid_spec = pltpu.PrefetchScalarGridSpec(
        num_scalar_prefetch=2,
        grid=(n_rows // EXP_TM,),
        in_specs=[
            pl.BlockSpec((EXP_TM, d), lambda i, be, bv: (i, 0)),
            pl.BlockSpec((1, d, de), lambda i, be, bv: (be[i], 0, 0)),
            pl.BlockSpec((1, d, de), lambda i, be, bv: (be[i], 0, 0)),
            pl.BlockSpec((1, de, d), lambda i, be, bv: (be[i], 0, 0)),
        ],
        out_specs=pl.BlockSpec((EXP_TM, d), lambda i, be, bv: (i, 0)),
    )
    return pl.pallas_call(
        _expert_kernel,
        grid_spec=grid_spec,
        out_shape=jax.ShapeDtypeStruct((n_rows, d), F32),
        compiler_params=_cparams(("arbitrary",)),
        name="experts",
    )(blk_e, blk_valid, xin, w1, w3, w2)


def _combine_kernel(dest_ref, y_hbm, x1_ref, gt_ref, p_ref, gple_ref, wg_ref, wp_ref, gfin_ref,
                    o_ref, ybuf, sem, *, final_norm):
    tm = x1_ref.shape[0]

    def issue(t, carry):
        for k in range(TOP_K):
            _row_copy(y_hbm, dest_ref[0, k, t], ybuf.at[k], t, sem.at[k]).start()
        return carry

    lax.fori_loop(0, tm, issue, 0)
    pp = _dot(p_ref[...].astype(BF16), wp_ref[...])

    def drain(t, carry):
        for k in range(TOP_K):
            _row_copy(y_hbm, 0, ybuf.at[k], t, sem.at[k]).wait()
        return carry

    lax.fori_loop(0, tm, drain, 0)
    y = ybuf[0] * gt_ref[:, 2:3] + ybuf[1] * gt_ref[:, 3:4]
    x2 = x1_ref[...] + y
    hn = _rmsnorm_val(x2, gple_ref[...], RMS_EPS)
    gate = _sigmoid(_dot(hn.astype(BF16), wg_ref[...]))
    x3 = x2 + gate * pp
    if final_norm:
        o_ref[...] = _rmsnorm_val(x3, gfin_ref[...], RMS_EPS)
    else:
        o_ref[...] = x3


def _combine_call(dest_blocks, yb, x1, gates_t, p2d, g_ple, w_gate, w_proj, g_fin, final_norm):
    t, d = x1.shape
    tm = ROW_TM
    const = dict(pipeline_mode=pl.Buffered(1))
    return pl.pallas_call(
        functools.partial(_combine_kernel, final_norm=final_norm),
        grid=(t // tm,),
        in_specs=[
            pl.BlockSpec((1, TOP_K, tm), lambda i: (i, 0, 0), memory_space=pltpu.SMEM),
            pl.BlockSpec(memory_space=pl.ANY),
            pl.BlockSpec((tm, d), lambda i: (i, 0)),
            pl.BlockSpec((tm, 8), lambda i: (i, 0)),
            pl.BlockSpec((tm, p2d.shape[1]), lambda i: (i, 0)),
            pl.BlockSpec((1, d), lambda i: (0, 0)),
            pl.BlockSpec(w_gate.shape, lambda i: (0, 0), **const),
            pl.BlockSpec(w_proj.shape, lambda i: (0, 0), **const),
            pl.BlockSpec((1, d), lambda i: (0, 0)),
        ],
        out_specs=pl.BlockSpec((tm, d), lambda i: (i, 0)),
        out_shape=jax.ShapeDtypeStruct((t, d), F32),
        scratch_shapes=[pltpu.VMEM((TOP_K, tm, d), F32), pltpu.SemaphoreType.DMA((TOP_K,))],
        compiler_params=_cparams(("arbitrary",)),
        name="combine",
    )(dest_blocks, yb, x1, gates_t, p2d, g_ple, w_gate, w_proj, g_fin)


def _routing_plan(route, counts_f, t):
    counts = counts_f[:, 0].astype(jnp.int32)
    pcounts = ((counts + EXP_TM - 1) // EXP_TM) * EXP_TM
    pends = jnp.cumsum(pcounts)
    pstarts = pends - pcounts
    eid = route[0:2].astype(jnp.int32)
    rank = route[4:6].astype(jnp.int32)
    onehot = eid[..., None] == jnp.arange(N_EXPERTS, dtype=jnp.int32)
    dest = jnp.sum(jnp.where(onehot, pstarts, 0), axis=-1) + rank
    n_rows = TOP_K * t + N_EXPERTS * EXP_TM
    n_blocks = n_rows // EXP_TM
    blk_row = jnp.arange(n_blocks, dtype=jnp.int32) * EXP_TM
    blk_e = jnp.sum((pends[None, :] <= blk_row[:, None]).astype(jnp.int32), axis=1)
    blk_valid = (blk_row < pends[-1]).astype(jnp.int32)
    last_e = jnp.sum((pends <= pends[-1] - 1).astype(jnp.int32))
    blk_e = jnp.minimum(blk_e, last_e)
    zstart = jnp.concatenate([pstarts + counts, pends[-1:]]).astype(jnp.int32)
    zlen = jnp.concatenate([pcounts - counts, n_rows - pends[-1:]]).astype(jnp.int32)
    return dest, blk_e, blk_valid, zstart, zlen, n_rows


def _blocked(dest, tok_per_block):
    t = dest.shape[1]
    return dest.reshape(TOP_K, t // tok_per_block, tok_per_block).transpose(1, 0, 2)


def kernel(x, p, rel_bias, norm_mix_g, w_in, w_gate, lambda_q1, lambda_k1, lambda_q2, lambda_k2, subln_g,
           w_proj_a, w_proj_b, w_out, norm_ffn_g, w_coarse, w_fine, w1, w3, w2, norm_ple_g, w_ple_gate,
           w_ple_proj, final_norm_g):
    batch, seq, d = x.shape
    depth = w_in.shape[0]
    t = batch * seq
    assert seq % PERM_BLK == 0 and t % min(PROJ_TM, t) == 0
    assert all(seq // dil >= BLK and win // dil == BLK for win, dil in DILATED_GROUPS)

    nq = seq // ATT_BLK
    bias_a = _bias_tiles_call(rel_bias, n_heads=A_HEADS, n_off=nq, tq=ATT_BLK, tk=ATT_BLK, off_mult=ATT_BLK,
                              off_add=0, dil=1, max_rel=seq, head0=0, name="bias_diff",
                              key_major=True, mult=A_HEAD_DIM ** 0.5)
    bias_b = jnp.concatenate([
        _bias_tiles_call(rel_bias, n_heads=B_HEADS, n_off=1, tq=BLK, tk=2 * BLK, off_mult=0, off_add=BLK,
                         dil=dil, max_rel=win // dil, head0=A_HEADS + gi * B_HEADS,
                         name="bias_dil%d" % gi).reshape(1, B_HEADS, BLK, 2 * BLK)
        for gi, (win, dil) in enumerate(DILATED_GROUPS)], axis=0)

    x2d = x.reshape(t, d)
    for layer in range(depth):
        lam_init = 0.8 - 0.6 * math.exp(-0.3 * layer)
        lam = _lam_call(lambda_q1[layer:layer + 1], lambda_k1[layer:layer + 1],
                        lambda_q2[layer:layer + 1], lambda_k2[layer:layer + 1], lam_init)
        w_cat = jnp.concatenate([w_gate[layer], w_in[layer]], axis=1).astype(BF16)
        zg = _inproj_call(x2d, norm_mix_g[layer:layer + 1], w_cat)
        oa = _diff_attn_call(zg, lam, bias_a, subln_g[layer:layer + 1], batch=batch, seq=seq,
                             out_scale=1.0 - lam_init)
        ob = _dilated_call(zg, bias_b, batch=batch, seq=seq)

        wr_t = jnp.concatenate([
            w_coarse[layer].T,
            w_fine[layer].transpose(0, 2, 1).reshape(N_EXPERTS, d),
            jnp.zeros((LANES - N_EXPERT_GROUPS - N_EXPERTS, d), F32)], axis=0)
        x1, h2, route, counts = _merge_router_call(
            oa, ob, zg, x2d, w_proj_a[layer].astype(BF16), w_proj_b[layer].astype(BF16),
            w_out[layer].astype(BF16), norm_ffn_g[layer:layer + 1], wr_t)

        dest, blk_e, blk_valid, zstart, zlen, n_rows = _routing_plan(route, counts, t)
        xin = _dispatch_call(_blocked(dest, DISPATCH_TOK), zstart, zlen, h2, n_rows)
        yb = _expert_call(blk_e, blk_valid, xin, w1[layer].astype(BF16), w3[layer].astype(BF16),
                          w2[layer].astype(BF16))
        x2d = _combine_call(_blocked(dest, ROW_TM), yb, x1, route.T, p[layer].reshape(t, -1),
                            norm_ple_g[layer:layer + 1], w_ple_gate[layer].astype(BF16),
                            w_ple_proj[layer].astype(BF16), final_norm_g.reshape(1, d),
                            final_norm=layer == depth - 1)
    return x2d.reshape(batch, seq, d)
```

```python
import functools
import math

import jax
import jax.numpy as jnp
from jax import lax
from jax.experimental import pallas as pl
from jax.experimental.pallas import tpu as pltpu

F32 = jnp.float32
BF16 = jnp.bfloat16

BLK = 128
NEG_INF = -1e30
RMS_EPS = 1e-6
SUBLN_EPS = 1e-5
N_BUCKETS = 32
MAX_DISTANCE = 2048
A_HEADS = 8
A_HEAD_DIM = 128
DILATED_GROUPS = ((128, 1), (512, 4), (2048, 16))
N_DGROUPS = 3
B_HEADS = 8
B_HEAD_DIM = 64
N_EXPERT_GROUPS = 4
EXPERTS_PER_GROUP = 8
N_EXPERTS = 32
TOP_K = 2

LANES = 128
VMEM_LIMIT = 56 * 1024 * 1024
ATT_BLK = 256
PERM_BLK = 256
GATE_COLS = 4096
PROJ_TN = 512
PROJ_TM = 1024
ROW_TM = 256
EXP_TM = 256
DISPATCH_TOK = 512
NORM_CHUNK = 128


def _t5_thresholds():
    max_exact = N_BUCKETS // 2
    out = []
    for k in range(1, N_BUCKETS - max_exact):
        out.append(int(math.ceil(max_exact * (MAX_DISTANCE / max_exact) ** (k / (N_BUCKETS - max_exact)))))
    return tuple(out)


T5_THRESHOLDS = _t5_thresholds()


def _cparams(sem, vmem=VMEM_LIMIT):
    return pltpu.CompilerParams(dimension_semantics=sem, vmem_limit_bytes=vmem)


def _mult(x, m):
    return x if isinstance(x, int) else pl.multiple_of(x, m)


def _sigmoid(x):
    return 1.0 / (1.0 + jnp.exp(-x))


def _dot(a, b, **kw):
    return jnp.dot(a, b, preferred_element_type=F32, **kw)


def _dot_nt(a, b, **kw):
    return lax.dot_general(a, b, (((1,), (1,)), ((), ())), preferred_element_type=F32, **kw)


def _lam_kernel(q1_ref, k1_ref, q2_ref, k2_ref, o_ref, *, lam_init):
    s1 = jnp.sum(q1_ref[...] * k1_ref[...], axis=-1, keepdims=True)
    s2 = jnp.sum(q2_ref[...] * k2_ref[...], axis=-1, keepdims=True)
    o_ref[...] = jnp.exp(s1) - jnp.exp(s2) + lam_init


def _lam_call(lq1, lk1, lq2, lk2, lam_init):
    return pl.pallas_call(
        functools.partial(_lam_kernel, lam_init=lam_init),
        out_shape=jax.ShapeDtypeStruct((1, 1), F32),
        name="lam",
    )(lq1, lk1, lq2, lk2)


def _bias_tile_kernel(tab_ref, o_ref, *, tq, tk, off_mult, off_add, dil, max_rel, head0, key_major, mult):
    h = pl.program_id(0)
    n = pl.program_id(1)
    i = lax.broadcasted_iota(jnp.int32, (tq, tk), 1 if key_major else 0)
    j = lax.broadcasted_iota(jnp.int32, (tq, tk), 0 if key_major else 1)
    rel = i - j + (n * off_mult + off_add)
    dist = rel * dil
    large = jnp.full((tq, tk), N_BUCKETS // 2, jnp.int32)
    for thr in T5_THRESHOLDS:
        large = large + jnp.where(dist >= thr, 1, 0)
    bucket = jnp.where(dist < N_BUCKETS // 2, dist, large)
    acc = jnp.zeros((tq, tk), F32)
    for b in range(N_BUCKETS):
        acc = jnp.where(bucket == b, tab_ref[b, head0 + h], acc)
    valid = jnp.where(rel >= 0, jnp.where(rel <= max_rel, 1, 0), 0)
    o_ref[0, 0] = jnp.where(valid == 1, acc * mult, NEG_INF)


def _bias_tiles_call(rel_bias, *, n_heads, n_off, tq, tk, off_mult, off_add, dil, max_rel, head0, name,
                     key_major=False, mult=1.0):
    kern = functools.partial(_bias_tile_kernel, tq=tq, tk=tk, off_mult=off_mult, off_add=off_add,
                             dil=dil, max_rel=max_rel, head0=head0, key_major=key_major, mult=mult)
    return pl.pallas_call(
        kern,
        grid=(n_heads, n_off),
        in_specs=[pl.BlockSpec(memory_space=pltpu.SMEM)],
        out_specs=pl.BlockSpec((1, 1, tq, tk), lambda h, n: (h, n, 0, 0)),
        out_shape=jax.ShapeDtypeStruct((n_heads, n_off, tq, tk), F32),
        compiler_params=_cparams(("parallel", "parallel")),
        name=name,
    )(rel_bias)


def _rmsnorm_rows(x_ref, g_ref, out_ref, eps):
    rows = x_ref.shape[0]
    g = g_ref[...]

    def body(c, carry):
        r0 = pl.multiple_of(c * NORM_CHUNK, NORM_CHUNK)
        x = x_ref[pl.ds(r0, NORM_CHUNK), :]
        ms = jnp.mean(x * x, axis=-1, keepdims=True)
        out_ref[pl.ds(r0, NORM_CHUNK), :] = ((x * lax.rsqrt(ms + eps)) * g).astype(out_ref.dtype)
        return carry

    lax.fori_loop(0, rows // NORM_CHUNK, body, 0)


def _rmsnorm_val(x, g, eps):
    ms = jnp.mean(x * x, axis=-1, keepdims=True)
    return (x * lax.rsqrt(ms + eps)) * g


def _inproj_kernel(x_ref, g_ref, w_ref, o_ref, h_ref, *, n_gate_blocks):
    j = pl.program_id(1)

    @pl.when(j == 0)
    def _():
        _rmsnorm_rows(x_ref, g_ref, h_ref, RMS_EPS)

    acc = _dot(h_ref[...], w_ref[...])

    @pl.when(j < n_gate_blocks)
    def _():
        o_ref[...] = _sigmoid(acc).astype(o_ref.dtype)

    @pl.when(j >= n_gate_blocks)
    def _():
        o_ref[...] = acc.astype(o_ref.dtype)


def _inproj_call(x2d, g, w_cat):
    t, d = x2d.shape
    n = w_cat.shape[1]
    tm = min(PROJ_TM, t)
    return pl.pallas_call(
        functools.partial(_inproj_kernel, n_gate_blocks=GATE_COLS // PROJ_TN),
        grid=(t // tm, n // PROJ_TN),
        in_specs=[
            pl.BlockSpec((tm, d), lambda i, j: (i, 0)),
            pl.BlockSpec((1, d), lambda i, j: (0, 0)),
            pl.BlockSpec((d, PROJ_TN), lambda i, j: (0, j)),
        ],
        out_specs=pl.BlockSpec((tm, PROJ_TN), lambda i, j: (i, j)),
        out_shape=jax.ShapeDtypeStruct((t, n), BF16),
        scratch_shapes=[pltpu.VMEM((tm, d), BF16)],
        compiler_params=_cparams(("parallel", "arbitrary")),
        name="inproj",
    )(x2d, g, w_cat)


def _diff_attn_kernel(lam_ref, q_ref, k_ref, v_ref, bias_ref, g_ref, o_ref, vt_ref, tbuf_a, tbuf_b,
                      acc1, acc2, *,
                      out_scale, n_blk):
    qi = pl.program_id(2)
    c = (A_HEAD_DIM ** -0.5) * math.log2(math.e)

    @pl.when(qi == 0)
    def _():
        def transpose_block(b, carry):
            r0 = pl.multiple_of(b * ATT_BLK, ATT_BLK)
            vt_ref[b] = v_ref[pl.ds(r0, ATT_BLK), :].astype(F32).T.astype(BF16)
            return carry

        lax.fori_loop(0, n_blk, transpose_block, 0)

    q1 = q_ref[:, :A_HEAD_DIM]
    q2 = q_ref[:, A_HEAD_DIM:]
    acc1[...] = jnp.zeros_like(acc1)
    acc2[...] = jnp.zeros_like(acc2)

    def scores(ki, dst):
        k0 = pl.multiple_of(ki * ATT_BLK, ATT_BLK)
        kb = k_ref[pl.ds(k0, ATT_BLK), :]
        bias = bias_ref[0, qi - ki]
        dst[0] = _dot_nt(kb[:, :A_HEAD_DIM], q1) + bias
        dst[1] = _dot_nt(kb[:, A_HEAD_DIM:], q2) + bias

    def one_map(t, vt, m, l, acc):
        m_new = jnp.maximum(m, jnp.max(t, axis=0, keepdims=True))
        alpha = jnp.exp2((m - m_new) * c)
        p = jnp.exp2((t - m_new) * c)
        l_new = alpha * l + jnp.sum(p, axis=0, keepdims=True)
        acc[...] = acc[...] * alpha + _dot(vt, p.astype(BF16))
        return m_new, l_new

    scores(0, tbuf_a)

    def step(ki, carry, cur, nxt):
        m1, l1, m2, l2 = carry
        scores(jnp.minimum(ki + 1, qi), nxt)
        vt = vt_ref[ki]
        m1, l1 = one_map(cur[0], vt, m1, l1, acc1)
        m2, l2 = one_map(cur[1], vt, m2, l2, acc2)
        return m1, l1, m2, l2

    def body(ki, carry):
        return lax.cond(lax.bitwise_and(ki, 1) == 0,
                        lambda cr: step(ki, cr, tbuf_a, tbuf_b),
                        lambda cr: step(ki, cr, tbuf_b, tbuf_a), carry)

    minf = jnp.full((1, ATT_BLK), -jnp.inf, F32)
    zero = jnp.zeros((1, ATT_BLK), F32)
    m1, l1, m2, l2 = lax.fori_loop(0, qi + 1, body, (minf, zero, minf, zero))
    w = acc1[...] / l1 - lam_ref[0, 0] * (acc2[...] / l2)
    ms = jnp.mean(w * w, axis=0, keepdims=True)
    y = ((w * lax.rsqrt(ms + SUBLN_EPS)) * g_ref[...]) * out_scale
    o_ref[...] = y.T.astype(o_ref.dtype)


def _diff_attn_call(zg, lam, bias_tiles, subln_g, *, batch, seq, out_scale):
    t = batch * seq
    nq = seq // ATT_BLK
    width = 2 * A_HEAD_DIM
    q_col = GATE_COLS // width
    k_col = q_col + A_HEADS
    v_col = k_col + A_HEADS
    return pl.pallas_call(
        functools.partial(_diff_attn_kernel, out_scale=out_scale, n_blk=nq),
        grid=(A_HEADS, batch, nq),
        in_specs=[
            pl.BlockSpec(memory_space=pltpu.SMEM),
            pl.BlockSpec((ATT_BLK, width), lambda h, b, i: (b * nq + i, q_col + h)),
            pl.BlockSpec((seq, width), lambda h, b, i: (b, k_col + h)),
            pl.BlockSpec((seq, width), lambda h, b, i: (b, v_col + h)),
            pl.BlockSpec((1, nq, ATT_BLK, ATT_BLK), lambda h, b, i: (h, 0, 0, 0)),
            pl.BlockSpec((width, 1), lambda h, b, i: (0, 0)),
        ],
        out_specs=pl.BlockSpec((ATT_BLK, width), lambda h, b, i: (b * nq + i, h)),
        out_shape=jax.ShapeDtypeStruct((t, A_HEADS * width), BF16),
        scratch_shapes=[pltpu.VMEM((nq, width, ATT_BLK), BF16),
                        pltpu.VMEM((2, ATT_BLK, ATT_BLK), F32), pltpu.VMEM((2, ATT_BLK, ATT_BLK), F32),
                        pltpu.VMEM((width, ATT_BLK), F32), pltpu.VMEM((width, ATT_BLK), F32)],
        compiler_params=_cparams(("parallel", "parallel", "arbitrary")),
        name="diff_attn",
    )(lam, zg, zg, zg, bias_tiles, subln_g.reshape(width, 1))


def _perm_matrix(dil, inverse):
    w = PERM_BLK // dil
    shift = w.bit_length() - 1
    a = lax.broadcasted_iota(jnp.int32, (PERM_BLK, PERM_BLK), 0)
    b = lax.broadcasted_iota(jnp.int32, (PERM_BLK, PERM_BLK), 1)
    dst, src = (b, a) if inverse else (a, b)
    c = lax.shift_right_logical(dst, shift)
    ll = lax.bitwise_and(dst, w - 1)
    return jnp.where(src == ll * dil + c, jnp.float32(1), jnp.float32(0))


def _deinterleave(pairs, perm, dil, seq):
    w = PERM_BLK // dil
    stream_len = seq // dil

    def body(b8, carry):
        r0 = pl.multiple_of(b8 * PERM_BLK, PERM_BLK)
        ys = [_dot(perm, src[pl.ds(r0, PERM_BLK), :]).astype(dst.dtype) for src, dst in pairs]
        for y, (_, dst) in zip(ys, pairs):
            for c in range(dil):
                d0 = pl.multiple_of(c * stream_len + b8 * w, w)
                dst[pl.ds(d0, w), :] = y[c * w:(c + 1) * w, :]
        return carry

    lax.fori_loop(0, seq // PERM_BLK, body, 0)


def _interleave(items, dil, seq):
    w = PERM_BLK // dil
    stream_len = seq // dil

    def body(b8, carry):
        for src, _, stack, _, _ in items:
            for c in range(dil):
                s0 = pl.multiple_of(c * stream_len + b8 * w, w)
                stack[c * w:(c + 1) * w, :] = src[pl.ds(s0, w), :]
        outs = [_dot(pinv, stack[...], **kw) for _, _, stack, pinv, kw in items]
        r0 = pl.multiple_of(b8 * PERM_BLK, PERM_BLK)
        for out, (_, dst, _, _, _) in zip(outs, items):
            dst[pl.ds(r0, PERM_BLK), :] = out.astype(dst.dtype)
        return carry

    lax.fori_loop(0, seq // PERM_BLK, body, 0)


def _transpose_blocks(src_ref, vt_ref, seq):
    def body(blk, carry):
        r0 = pl.multiple_of(blk * BLK, BLK)
        vt_ref[blk] = src_ref[pl.ds(r0, BLK), :].astype(F32).T.astype(vt_ref.dtype)
        return carry

    lax.fori_loop(0, seq // BLK, body, 0)


def _dil_window(q_src, k_src, vt_ref, o_dst, lse_dst, bias_ref, ot_ref, lt_ref, rq, nk):
    scale = B_HEAD_DIM ** -0.5
    c = scale * math.log2(math.e)
    qb = rq // BLK if isinstance(rq, int) else lax.shift_right_logical(rq, BLK.bit_length() - 1)
    rk = rq - (nk - BLK)
    tiles = []
    for hh in range(B_HEADS):
        cols = slice(hh * B_HEAD_DIM, (hh + 1) * B_HEAD_DIM)
        qh = q_src[pl.ds(rq, BLK), cols]
        kh = k_src[pl.ds(rk, nk), cols]
        tiles.append(_dot_nt(kh, qh) + bias_ref[0, hh, 2 * BLK - nk:, :])
    for hh in range(B_HEADS):
        cols = slice(hh * B_HEAD_DIM, (hh + 1) * B_HEAD_DIM)
        t = tiles[hh]
        m = jnp.max(t, axis=0, keepdims=True)
        p = jnp.exp2((t - m) * c)
        den = jnp.sum(p, axis=0, keepdims=True)
        if nk == BLK:
            vth = vt_ref[qb, cols, :]
        else:
            vth = jnp.concatenate([vt_ref[qb - 1, cols, :], vt_ref[qb, cols, :]], axis=1)
        ot_ref[cols, :] = _dot(vth, p.astype(BF16)) / den
        lt_ref[hh:hh + 1, :] = m * scale + jnp.log(den)
    o_dst[pl.ds(rq, BLK), :] = ot_ref[...].T.astype(o_dst.dtype)
    lse_dst[pl.ds(rq, BLK), :] = lt_ref[...].T


def _dil_streams(q_src, k_src, vt_ref, o_dst, lse_dst, bias_ref, ot_ref, lt_ref, dil, seq):
    stream_len = seq // dil
    nq = stream_len // BLK

    def stream(c, carry):
        base = _mult(c * stream_len, BLK)
        _dil_window(q_src, k_src, vt_ref, o_dst, lse_dst, bias_ref, ot_ref, lt_ref, base, BLK)

        def qblock(n, carry2):
            rq = pl.multiple_of(base + n * BLK, BLK)
            _dil_window(q_src, k_src, vt_ref, o_dst, lse_dst, bias_ref, ot_ref, lt_ref, rq, 2 * BLK)
            return carry2

        if nq > 1:
            lax.fori_loop(1, nq, qblock, 0)
        return carry

    if dil == 1:
        stream(0, 0)
    else:
        lax.fori_loop(0, dil, stream, 0)


def _head_expand(x, expand):
    hi = x.astype(BF16)
    lo = (x - hi.astype(F32)).astype(BF16)
    return _dot(hi, expand) + _dot(lo, expand)


def _dil_merge(o_src, lse_src, oacc, mrun, lrun, out_ref, first, last, seq):
    er = lax.broadcasted_iota(jnp.int32, (LANES, B_HEADS * B_HEAD_DIM), 0)
    ec = lax.broadcasted_iota(jnp.int32, (LANES, B_HEADS * B_HEAD_DIM), 1)
    expand = jnp.where(lax.shift_right_logical(ec, B_HEAD_DIM.bit_length() - 1) == er,
                       jnp.float32(1), jnp.float32(0)).astype(BF16)

    def body(ch, carry):
        r0 = pl.multiple_of(ch * ROW_TM, ROW_TM)
        rows = pl.ds(r0, ROW_TM)
        lse = lse_src[rows, :]
        if first:
            mrun[rows, :] = lse
            lrun[rows, :] = jnp.ones_like(lse)
            oacc[rows, :] = o_src[rows, :].astype(F32)
            return carry
        m_old = mrun[rows, :]
        m_new = jnp.maximum(m_old, lse)
        a = jnp.exp(m_old - m_new)
        bw = jnp.exp(lse - m_new)
        l_new = lrun[rows, :] * a + bw
        val = oacc[rows, :] * _head_expand(a, expand) + o_src[rows, :].astype(F32) * _head_expand(bw, expand)
        if last:
            out_ref[rows, :] = (val * _head_expand(1.0 / l_new, expand)).astype(out_ref.dtype)
        else:
            mrun[rows, :] = m_new
            lrun[rows, :] = l_new
            oacc[rows, :] = val
        return carry

    lax.fori_loop(0, seq // ROW_TM, body, 0)


def _dilated_kernel(q_ref, k_ref, v_ref, bias_ref, out_ref,
                    qs, ks, vs, vt, os_, lses, otok, lsetok, ostack, lstack, ot, lt, oacc, mrun, lrun,
                    *, seq):
    g = pl.program_id(1)

    @pl.when(g == 0)
    def _():
        lt[...] = jnp.zeros_like(lt)

    for gi, (_, dil) in enumerate(DILATED_GROUPS):

        @pl.when(g == gi)
        def _(gi=gi, dil=dil):
            first = gi == 0
            last = gi == N_DGROUPS - 1
            if dil == 1:
                _transpose_blocks(v_ref, vt, seq)
                _dil_streams(q_ref, k_ref, vt, otok, lsetok, bias_ref, ot, lt, 1, seq)
            else:
                perm = _perm_matrix(dil, inverse=False).astype(BF16)
                _deinterleave([(q_ref, qs), (k_ref, ks), (v_ref, vs)], perm, dil, seq)
                _transpose_blocks(vs, vt, seq)
                _dil_streams(qs, ks, vt, os_, lses, bias_ref, ot, lt, dil, seq)
                perm_inv = _perm_matrix(dil, inverse=True)
                _interleave([(os_, otok, ostack, perm_inv.astype(BF16), {}),
                             (lses, lsetok, lstack, perm_inv, dict(precision=lax.Precision.HIGHEST))],
                            dil, seq)
            _dil_merge(otok, lsetok, oacc, mrun, lrun, out_ref, first, last, seq)


def _dilated_call(zg, bias_tiles, *, batch, seq):
    width = B_HEADS * B_HEAD_DIM
    q_col = (GATE_COLS + 3 * A_HEADS * 2 * A_HEAD_DIM) // width
    k_col = q_col + N_DGROUPS
    v_col = k_col + N_DGROUPS
    return pl.pallas_call(
        functools.partial(_dilated_kernel, seq=seq),
        grid=(batch, N_DGROUPS),
        in_specs=[
            pl.BlockSpec((seq, width), lambda b, g: (b, q_col + g)),
            pl.BlockSpec((seq, width), lambda b, g: (b, k_col + g)),
            pl.BlockSpec((seq, width), lambda b, g: (b, v_col + g)),
            pl.BlockSpec((1, B_HEADS, 2 * BLK, BLK), lambda b, g: (g, 0, 0, 0)),
        ],
        out_specs=pl.BlockSpec((seq, width), lambda b, g: (b, 0)),
        out_shape=jax.ShapeDtypeStruct((batch * seq, width), BF16),
        scratch_shapes=[
            pltpu.VMEM((seq, width), BF16), pltpu.VMEM((seq, width), BF16), pltpu.VMEM((seq, width), BF16),
            pltpu.VMEM((seq // BLK, width, BLK), BF16),
            pltpu.VMEM((seq, width), BF16), pltpu.VMEM((seq, LANES), F32),
            pltpu.VMEM((seq, width), BF16), pltpu.VMEM((seq, LANES), F32),
            pltpu.VMEM((PERM_BLK, width), BF16), pltpu.VMEM((PERM_BLK, LANES), F32),
            pltpu.VMEM((width, BLK), F32), pltpu.VMEM((LANES, BLK), F32),
            pltpu.VMEM((seq, width), F32), pltpu.VMEM((seq, LANES), F32), pltpu.VMEM((seq, LANES), F32),
        ],
        compiler_params=_cparams(("parallel", "arbitrary")),
        name="dilated_attn",
    )(zg, zg, zg, bias_tiles)


def _merge_router_kernel(oa_ref, ob_ref, gate_ref, x_ref, pa_ref, pb_ref, wo_ref, g_ref, wr_ref,
                         x1_ref, h2_ref, route_ref, cnt_ref, run_ref):
    tm, d = x_ref.shape

    @pl.when(pl.program_id(0) == 0)
    def _():
        run_ref[...] = jnp.zeros_like(run_ref)

    a = _dot(oa_ref[...], pa_ref[...])
    bm = _dot(ob_ref[...], pb_ref[...])
    merged = gate_ref[:, :d].astype(F32) * a + gate_ref[:, d:].astype(F32) * bm
    x1 = x_ref[...] + _dot(merged.astype(BF16), wo_ref[...])
    x1_ref[...] = x1
    h2 = _rmsnorm_val(x1, g_ref[...], RMS_EPS)
    h2_ref[...] = h2

    h_hi = h2.astype(BF16)
    h_lo = (h2 - h_hi.astype(F32)).astype(BF16)
    lt = _dot_nt(wr_ref[0], h_hi) + (_dot_nt(wr_ref[1], h_hi) + _dot_nt(wr_ref[0], h_lo))
    coarse = [lt[i:i + 1, :] for i in range(N_EXPERT_GROUPS)]
    best = coarse[0]
    gsel = jnp.zeros((1, tm), jnp.int32)
    for i in range(1, N_EXPERT_GROUPS):
        upd = coarse[i] > best
        gsel = jnp.where(upd, i, gsel)
        best = jnp.where(upd, coarse[i], best)
    den = jnp.exp(coarse[0] - best)
    for i in range(1, N_EXPERT_GROUPS):
        den = den + jnp.exp(coarse[i] - best)
    pg = 1.0 / den

    fine = []
    for k in range(EXPERTS_PER_GROUP):
        f = lt[N_EXPERT_GROUPS + k:N_EXPERT_GROUPS + k + 1, :]
        for gi in range(1, N_EXPERT_GROUPS):
            r = N_EXPERT_GROUPS + gi * EXPERTS_PER_GROUP + k
            f = jnp.where(gsel == gi, lt[r:r + 1, :], f)
        fine.append(f)
    v0 = fine[0]
    i0 = jnp.zeros((1, tm), jnp.int32)
    for k in range(1, EXPERTS_PER_GROUP):
        upd = fine[k] > v0
        i0 = jnp.where(upd, k, i0)
        v0 = jnp.where(upd, fine[k], v0)
    v1 = jnp.full((1, tm), -jnp.inf, F32)
    i1 = jnp.zeros((1, tm), jnp.int32)
    for k in range(EXPERTS_PER_GROUP):
        upd = jnp.where(i0 != k, jnp.where(fine[k] > v1, 1, 0), 0) == 1
        i1 = jnp.where(upd, k, i1)
        v1 = jnp.where(upd, fine[k], v1)
    e1w = jnp.exp(v1 - v0)
    gate0 = pg * (1.0 / (1.0 + e1w))
    gate1 = pg * (e1w / (1.0 + e1w))
    e0 = gsel * EXPERTS_PER_GROUP + i0
    e1 = gsel * EXPERTS_PER_GROUP + i1

    eidx = lax.broadcasted_iota(jnp.int32, (N_EXPERTS, tm), 0)
    oh0 = jnp.where(eidx == e0, jnp.float32(1), jnp.float32(0))
    oh1 = jnp.where(eidx == e1, jnp.float32(1), jnp.float32(0))
    ta = lax.broadcasted_iota(jnp.int32, (tm, tm), 0)
    tb = lax.broadcasted_iota(jnp.int32, (tm, tm), 1)
    before = jnp.where(ta < tb, jnp.float32(1), jnp.float32(0)).astype(BF16)
    pre0 = _dot(oh0.astype(BF16), before)
    pre1 = _dot(oh1.astype(BF16), before)
    run = run_ref[:, 0:1]
    tot0 = jnp.sum(oh0, axis=1, keepdims=True)
    tot1 = jnp.sum(oh1, axis=1, keepdims=True)
    rank0 = jnp.sum(oh0 * (run + pre0), axis=0, keepdims=True)
    rank1 = jnp.sum(oh1 * (run + tot0 + pre1), axis=0, keepdims=True)
    new_run = jnp.broadcast_to(run + tot0 + tot1, run_ref.shape)
    run_ref[...] = new_run
    cnt_ref[...] = new_run

    route_ref[0:1, :] = e0.astype(F32)
    route_ref[1:2, :] = e1.astype(F32)
    route_ref[2:3, :] = gate0
    route_ref[3:4, :] = gate1
    route_ref[4:5, :] = rank0
    route_ref[5:6, :] = rank1
    route_ref[6:8, :] = jnp.zeros((2, tm), F32)


def _merge_router_call(oa, ob, zg, x2d, pa, pb, wo, g, wr_t):
    t, d = x2d.shape
    tm = ROW_TM
    const = dict(pipeline_mode=pl.Buffered(1))
    return pl.pallas_call(
        _merge_router_kernel,
        grid=(t // tm,),
        in_specs=[
            pl.BlockSpec((tm, oa.shape[1]), lambda i: (i, 0)),
            pl.BlockSpec((tm, ob.shape[1]), lambda i: (i, 0)),
            pl.BlockSpec((tm, GATE_COLS), lambda i: (i, 0)),
            pl.BlockSpec((tm, d), lambda i: (i, 0)),
            pl.BlockSpec(pa.shape, lambda i: (0, 0), **const),
            pl.BlockSpec(pb.shape, lambda i: (0, 0), **const),
            pl.BlockSpec(wo.shape, lambda i: (0, 0), **const),
            pl.BlockSpec((1, d), lambda i: (0, 0)),
            pl.BlockSpec(wr_t.shape, lambda i: (0, 0, 0), **const),
        ],
        out_specs=[
            pl.BlockSpec((tm, d), lambda i: (i, 0)),
            pl.BlockSpec((tm, d), lambda i: (i, 0)),
            pl.BlockSpec((8, tm), lambda i: (0, i)),
            pl.BlockSpec((N_EXPERTS, LANES), lambda i: (0, 0)),
        ],
        out_shape=[
            jax.ShapeDtypeStruct((t, d), F32),
            jax.ShapeDtypeStruct((t, d), F32),
            jax.ShapeDtypeStruct((8, t), F32),
            jax.ShapeDtypeStruct((N_EXPERTS, LANES), F32),
        ],
        scratch_shapes=[pltpu.VMEM((N_EXPERTS, LANES), F32)],
        compiler_params=_cparams(("arbitrary",)),
        name="merge_router",
    )(oa, ob, zg, x2d, pa, pb, wo, g, wr_t)


def _row_copy(src, s_row, dst, d_row, sem):
    return pltpu.make_async_copy(src.at[pl.ds(s_row, 1)], dst.at[pl.ds(d_row, 1)], sem)


def _dispatch_kernel(dest_ref, zstart_ref, zlen_ref, h_ref, x_hbm, zero_ref, sem, zsem):
    step = pl.program_id(0)

    def issue(t, carry):
        for k in range(TOP_K):
            _row_copy(h_ref, t, x_hbm, dest_ref[0, k, t], sem.at[k]).start()
        return carry

    lax.fori_loop(0, DISPATCH_TOK, issue, 0, unroll=8)

    @pl.when(step == 0)
    def _():
        zero_ref[...] = jnp.zeros_like(zero_ref)

        def group_copy(r8):
            return pltpu.make_async_copy(zero_ref, x_hbm.at[pl.ds(pl.multiple_of(r8, 8), 8)], zsem.at[1])

        def segment(e, wait):
            start = zstart_ref[e]
            n_head = jnp.minimum(lax.bitwise_and(-start, 7), zlen_ref[e])
            n_group = lax.shift_right_logical(zlen_ref[e] - n_head, 3)

            def head(r, carry):
                cp = _row_copy(zero_ref, 0, x_hbm, 0 if wait else start + r, zsem.at[0])
                cp.wait() if wait else cp.start()
                return carry

            def group(j, carry):
                cp = group_copy(0 if wait else start + n_head + 8 * j)
                cp.wait() if wait else cp.start()
                return carry

            lax.fori_loop(0, n_head, head, 0)
            lax.fori_loop(0, n_group, group, 0)

        lax.fori_loop(0, N_EXPERTS + 1, lambda e, c: (segment(e, False), c)[1], 0)
        lax.fori_loop(0, N_EXPERTS + 1, lambda e, c: (segment(e, True), c)[1], 0)

    for k in range(TOP_K):
        pltpu.make_async_copy(h_ref, x_hbm.at[pl.ds(0, DISPATCH_TOK)], sem.at[k]).wait()


def _dispatch_call(dest_blocks, zstart, zlen, h2, n_rows):
    t, d = h2.shape
    return pl.pallas_call(
        _dispatch_kernel,
        grid=(t // DISPATCH_TOK,),
        in_specs=[
            pl.BlockSpec((1, TOP_K, DISPATCH_TOK), lambda i: (i, 0, 0), memory_space=pltpu.SMEM),
            pl.BlockSpec(memory_space=pltpu.SMEM),
            pl.BlockSpec(memory_space=pltpu.SMEM),
            pl.BlockSpec((DISPATCH_TOK, d), lambda i: (i, 0)),
        ],
        out_specs=pl.BlockSpec(memory_space=pl.ANY),
        out_shape=jax.ShapeDtypeStruct((n_rows, d), F32),
        scratch_shapes=[pltpu.VMEM((8, d), F32), pltpu.SemaphoreType.DMA((TOP_K,)),
                        pltpu.SemaphoreType.DMA((2,))],
        compiler_params=_cparams(("arbitrary",)),
        name="dispatch",
    )(dest_blocks, zstart, zlen, h2)


def _expert_kernel(blk_e_ref, blk_valid_ref, x_ref, w1_ref, w3_ref, w2_ref, y_ref):
    i = pl.program_id(0)

    @pl.when(blk_valid_ref[i] == 1)
    def _():
        x = x_ref[...].astype(BF16)
        a = _dot(x, w1_ref[0])
        b = _dot(x, w3_ref[0])
        hdn = (a * _sigmoid(a)) * b
        y_ref[...] = _dot(hdn.astype(BF16), w2_ref[0])

    @pl.when(blk_valid_ref[i] == 0)
    def _():
        y_ref[...] = jnp.zeros_like(y_ref)


def _expert_call(blk_e, blk_valid, xin, w1, w3, w2):
    n_rows, d = xin.shape
    de = w1.shape[2]
    grid_spec = pltpu.PrefetchScalarGridSpec(
        num_scalar_prefetch=2,
        grid=(n_rows // EXP_TM,),
        in_specs=[
            pl.BlockSpec((EXP_TM, d), lambda i, be, bv: (i, 0)),
            pl.BlockSpec((1, d, de), lambda i, be, bv: (be[i], 0, 0)),
            pl.BlockSpec((1, d, de), lambda i, be, bv: (be[i], 0, 0)),
            pl.BlockSpec((1, de, d), lambda i, be, bv: (be[i], 0, 0)),
        ],
        out_specs=pl.BlockSpec((EXP_TM, d), lambda i, be, bv: (i, 0)),
    )
    return pl.pallas_call(
        _expert_kernel,
        grid_spec=grid_spec,
        out_shape=jax.ShapeDtypeStruct((n_rows, d), F32),
        compiler_params=_cparams(("arbitrary",)),
        name="experts",
    )(blk_e, blk_valid, xin, w1, w3, w2)


def _combine_kernel(dest_ref, y_hbm, x1_ref, gt_ref, p_ref, gple_ref, wg_ref, wp_ref, gfin_ref,
                    o_ref, ybuf, sem, *, final_norm):
    tm = x1_ref.shape[0]

    def issue(t, carry):
        for k in range(TOP_K):
            _row_copy(y_hbm, dest_ref[0, k, t], ybuf.at[k], t, sem.at[k]).start()
        return carry

    lax.fori_loop(0, tm, issue, 0, unroll=8)
    pp = _dot(p_ref[...].astype(BF16), wp_ref[...])

    for k in range(TOP_K):
        pltpu.make_async_copy(y_hbm.at[pl.ds(0, tm)], ybuf.at[k], sem.at[k]).wait()
    y = ybuf[0] * gt_ref[:, 2:3] + ybuf[1] * gt_ref[:, 3:4]
    x2 = x1_ref[...] + y
    hn = _rmsnorm_val(x2, gple_ref[...], RMS_EPS)
    gate = _sigmoid(_dot(hn.astype(BF16), wg_ref[...]))
    x3 = x2 + gate * pp
    if final_norm:
        o_ref[...] = _rmsnorm_val(x3, gfin_ref[...], RMS_EPS)
    else:
        o_ref[...] = x3


def _combine_call(dest_blocks, yb, x1, gates_t, p2d, g_ple, w_gate, w_proj, g_fin, final_norm):
    t, d = x1.shape
    tm = ROW_TM
    const = dict(pipeline_mode=pl.Buffered(1))
    return pl.pallas_call(
        functools.partial(_combine_kernel, final_norm=final_norm),
        grid=(t // tm,),
        in_specs=[
            pl.BlockSpec((1, TOP_K, tm), lambda i: (i, 0, 0), memory_space=pltpu.SMEM),
            pl.BlockSpec(memory_space=pl.ANY),
            pl.BlockSpec((tm, d), lambda i: (i, 0)),
            pl.BlockSpec((tm, 8), lambda i: (i, 0)),
            pl.BlockSpec((tm, p2d.shape[1]), lambda i: (i, 0)),
            pl.BlockSpec((1, d), lambda i: (0, 0)),
            pl.BlockSpec(w_gate.shape, lambda i: (0, 0), **const),
            pl.BlockSpec(w_proj.shape, lambda i: (0, 0), **const),
            pl.BlockSpec((1, d), lambda i: (0, 0)),
        ],
        out_specs=pl.BlockSpec((tm, d), lambda i: (i, 0)),
        out_shape=jax.ShapeDtypeStruct((t, d), F32),
        scratch_shapes=[pltpu.VMEM((TOP_K, tm, d), F32), pltpu.SemaphoreType.DMA((TOP_K,))],
        compiler_params=_cparams(("arbitrary",)),
        name="combine",
    )(dest_blocks, yb, x1, gates_t, p2d, g_ple, w_gate, w_proj, g_fin)


def _routing_plan(route, counts_f, t):
    counts = counts_f[:, 0].astype(jnp.int32)
    pcounts = ((counts + EXP_TM - 1) // EXP_TM) * EXP_TM
    pends = jnp.cumsum(pcounts)
    pstarts = pends - pcounts
    eid = route[0:2].astype(jnp.int32)
    rank = route[4:6].astype(jnp.int32)
    onehot = eid[..., None] == jnp.arange(N_EXPERTS, dtype=jnp.int32)
    dest = jnp.sum(jnp.where(onehot, pstarts, 0), axis=-1) + rank
    n_rows = TOP_K * t + N_EXPERTS * EXP_TM
    n_blocks = n_rows // EXP_TM
    blk_row = jnp.arange(n_blocks, dtype=jnp.int32) * EXP_TM
    blk_e = jnp.sum((pends[None, :] <= blk_row[:, None]).astype(jnp.int32), axis=1)
    blk_valid = (blk_row < pends[-1]).astype(jnp.int32)
    last_e = jnp.sum((pends <= pends[-1] - 1).astype(jnp.int32))
    blk_e = jnp.minimum(blk_e, last_e)
    zstart = jnp.concatenate([pstarts + counts, pends[-1:]]).astype(jnp.int32)
    zlen = jnp.concatenate([pcounts - counts, n_rows - pends[-1:]]).astype(jnp.int32)
    return dest, blk_e, blk_valid, zstart, zlen, n_rows


def _blocked(dest, tok_per_block):
    t = dest.shape[1]
    return dest.reshape(TOP_K, t // tok_per_block, tok_per_block).transpose(1, 0, 2)


def kernel(x, p, rel_bias, norm_mix_g, w_in, w_gate, lambda_q1, lambda_k1, lambda_q2, lambda_k2, subln_g,
           w_proj_a, w_proj_b, w_out, norm_ffn_g, w_coarse, w_fine, w1, w3, w2, norm_ple_g, w_ple_gate,
           w_ple_proj, final_norm_g):
    batch, seq, d = x.shape
    depth = w_in.shape[0]
    t = batch * seq
    assert seq % PERM_BLK == 0 and t % min(PROJ_TM, t) == 0
    assert all(seq // dil >= BLK and win // dil == BLK for win, dil in DILATED_GROUPS)

    nq = seq // ATT_BLK
    bias_a = _bias_tiles_call(rel_bias, n_heads=A_HEADS, n_off=nq, tq=ATT_BLK, tk=ATT_BLK, off_mult=ATT_BLK,
                              off_add=0, dil=1, max_rel=seq, head0=0, name="bias_diff",
                              key_major=True, mult=A_HEAD_DIM ** 0.5)
    bias_b = jnp.concatenate([
        _bias_tiles_call(rel_bias, n_heads=B_HEADS, n_off=1, tq=2 * BLK, tk=BLK, off_mult=0, off_add=BLK,
                         dil=dil, max_rel=win // dil, head0=A_HEADS + gi * B_HEADS, name="bias_dil%d" % gi,
                         key_major=True, mult=B_HEAD_DIM ** 0.5).reshape(1, B_HEADS, 2 * BLK, BLK)
        for gi, (win, dil) in enumerate(DILATED_GROUPS)], axis=0)

    x2d = x.reshape(t, d)
    for layer in range(depth):
        lam_init = 0.8 - 0.6 * math.exp(-0.3 * layer)
        lam = _lam_call(lambda_q1[layer:layer + 1], lambda_k1[layer:layer + 1],
                        lambda_q2[layer:layer + 1], lambda_k2[layer:layer + 1], lam_init)
        w_cat = jnp.concatenate([w_gate[layer], w_in[layer]], axis=1).astype(BF16)
        zg = _inproj_call(x2d, norm_mix_g[layer:layer + 1], w_cat)
        oa = _diff_attn_call(zg, lam, bias_a, subln_g[layer:layer + 1], batch=batch, seq=seq,
                             out_scale=1.0 - lam_init)
        ob = _dilated_call(zg, bias_b, batch=batch, seq=seq)

        wr_t = jnp.concatenate([
            w_coarse[layer].T,
            w_fine[layer].transpose(0, 2, 1).reshape(N_EXPERTS, d),
            jnp.zeros((LANES - N_EXPERT_GROUPS - N_EXPERTS, d), F32)], axis=0)
        wr_hi = wr_t.astype(BF16)
        wr_t = jnp.stack([wr_hi, (wr_t - wr_hi.astype(F32)).astype(BF16)], axis=0)
        x1, h2, route, counts = _merge_router_call(
            oa, ob, zg, x2d, w_proj_a[layer].astype(BF16), w_proj_b[layer].astype(BF16),
            w_out[layer].astype(BF16), norm_ffn_g[layer:layer + 1], wr_t)

        dest, blk_e, blk_valid, zstart, zlen, n_rows = _routing_plan(route, counts, t)
        xin = _dispatch_call(_blocked(dest, DISPATCH_TOK), zstart, zlen, h2, n_rows)
        yb = _expert_call(blk_e, blk_valid, xin, w1[layer].astype(BF16), w3[layer].astype(BF16),
                          w2[layer].astype(BF16))
        x2d = _combine_call(_blocked(dest, ROW_TM), yb, x1, route.T, p[layer].reshape(t, -1),
                            norm_ple_g[layer:layer + 1], w_ple_gate[layer].astype(BF16),
                            w_ple_proj[layer].astype(BF16), final_norm_g.reshape(1, d),
                            final_norm=layer == depth - 1)
    return x2d.reshape(batch, seq, d)
```

```python
import functools
import math

import jax
import jax.numpy as jnp
from jax import lax
from jax.experimental import pallas as pl
from jax.experimental.pallas import tpu as pltpu

F32 = jnp.float32
BF16 = jnp.bfloat16

BLK = 128
NEG_INF = -1e30
RMS_EPS = 1e-6
SUBLN_EPS = 1e-5
N_BUCKETS = 32
MAX_DISTANCE = 2048
A_HEADS = 8
A_HEAD_DIM = 128
DILATED_GROUPS = ((128, 1), (512, 4), (2048, 16))
N_DGROUPS = 3
B_HEADS = 8
B_HEAD_DIM = 64
N_EXPERT_GROUPS = 4
EXPERTS_PER_GROUP = 8
N_EXPERTS = 32
TOP_K = 2

LANES = 128
VMEM_LIMIT = 56 * 1024 * 1024
ATT_BLK = 256
PERM_BLK = 256
GATE_COLS = 4096
PROJ_TN = 512
PROJ_TM = 2048
ROW_TM = 256
EXP_TM = 256
DISPATCH_TOK = 512
NORM_CHUNK = 128


def _t5_thresholds():
    max_exact = N_BUCKETS // 2
    out = []
    for k in range(1, N_BUCKETS - max_exact):
        out.append(int(math.ceil(max_exact * (MAX_DISTANCE / max_exact) ** (k / (N_BUCKETS - max_exact)))))
    return tuple(out)


T5_THRESHOLDS = _t5_thresholds()


def _cparams(sem, vmem=VMEM_LIMIT):
    return pltpu.CompilerParams(dimension_semantics=sem, vmem_limit_bytes=vmem)


def _mult(x, m):
    return x if isinstance(x, int) else pl.multiple_of(x, m)


def _sigmoid(x):
    return 0.5 * jnp.tanh(0.5 * x) + 0.5


def _dot(a, b, **kw):
    return jnp.dot(a, b, preferred_element_type=F32, **kw)


def _dot_nt(a, b, **kw):
    return lax.dot_general(a, b, (((1,), (1,)), ((), ())), preferred_element_type=F32, **kw)


def _lam_kernel(q1_ref, k1_ref, q2_ref, k2_ref, o_ref, *, lam_init):
    s1 = jnp.sum(q1_ref[...] * k1_ref[...], axis=-1, keepdims=True)
    s2 = jnp.sum(q2_ref[...] * k2_ref[...], axis=-1, keepdims=True)
    o_ref[...] = jnp.exp(s1) - jnp.exp(s2) + lam_init


def _lam_call(lq1, lk1, lq2, lk2, lam_init):
    return pl.pallas_call(
        functools.partial(_lam_kernel, lam_init=lam_init),
        out_shape=jax.ShapeDtypeStruct((1, 1), F32),
        name="lam",
    )(lq1, lk1, lq2, lk2)


def _bias_tile_kernel(tab_ref, o_ref, *, tq, tk, off_mult, off_add, dil, max_rel, head0, key_major, mult):
    h = pl.program_id(0)
    n = pl.program_id(1)
    i = lax.broadcasted_iota(jnp.int32, (tq, tk), 1 if key_major else 0)
    j = lax.broadcasted_iota(jnp.int32, (tq, tk), 0 if key_major else 1)
    rel = i - j + (n * off_mult + off_add)
    dist = rel * dil
    large = jnp.full((tq, tk), N_BUCKETS // 2, jnp.int32)
    for thr in T5_THRESHOLDS:
        large = large + jnp.where(dist >= thr, 1, 0)
    bucket = jnp.where(dist < N_BUCKETS // 2, dist, large)
    acc = jnp.zeros((tq, tk), F32)
    for b in range(N_BUCKETS):
        acc = jnp.where(bucket == b, tab_ref[b, head0 + h], acc)
    valid = jnp.where(rel >= 0, jnp.where(rel <= max_rel, 1, 0), 0)
    o_ref[0, 0] = jnp.where(valid == 1, acc * mult, NEG_INF)


def _bias_tiles_call(rel_bias, *, n_heads, n_off, tq, tk, off_mult, off_add, dil, max_rel, head0, name,
                     key_major=False, mult=1.0):
    kern = functools.partial(_bias_tile_kernel, tq=tq, tk=tk, off_mult=off_mult, off_add=off_add,
                             dil=dil, max_rel=max_rel, head0=head0, key_major=key_major, mult=mult)
    return pl.pallas_call(
        kern,
        grid=(n_heads, n_off),
        in_specs=[pl.BlockSpec(memory_space=pltpu.SMEM)],
        out_specs=pl.BlockSpec((1, 1, tq, tk), lambda h, n: (h, n, 0, 0)),
        out_shape=jax.ShapeDtypeStruct((n_heads, n_off, tq, tk), F32),
        compiler_params=_cparams(("parallel", "parallel")),
        name=name,
    )(rel_bias)


def _rmsnorm_rows(x_ref, g_ref, out_ref, eps):
    rows = x_ref.shape[0]
    g = g_ref[...]

    def body(c, carry):
        r0 = pl.multiple_of(c * NORM_CHUNK, NORM_CHUNK)
        x = x_ref[pl.ds(r0, NORM_CHUNK), :]
        ms = jnp.mean(x * x, axis=-1, keepdims=True)
        out_ref[pl.ds(r0, NORM_CHUNK), :] = ((x * lax.rsqrt(ms + eps)) * g).astype(out_ref.dtype)
        return carry

    lax.fori_loop(0, rows // NORM_CHUNK, body, 0)


def _rmsnorm_val(x, g, eps):
    ms = jnp.mean(x * x, axis=-1, keepdims=True)
    return (x * lax.rsqrt(ms + eps)) * g


def _inproj_kernel(x_ref, g_ref, w_ref, o_ref, h_ref, *, n_gate_blocks):
    j = pl.program_id(1)

    @pl.when(j == 0)
    def _():
        _rmsnorm_rows(x_ref, g_ref, h_ref, RMS_EPS)

    acc = _dot(h_ref[...], w_ref[...])

    @pl.when(j < n_gate_blocks)
    def _():
        o_ref[...] = _sigmoid(acc).astype(o_ref.dtype)

    @pl.when(j >= n_gate_blocks)
    def _():
        o_ref[...] = acc.astype(o_ref.dtype)


def _inproj_call(x2d, g, w_cat):
    t, d = x2d.shape
    n = w_cat.shape[1]
    tm = min(PROJ_TM, t)
    return pl.pallas_call(
        functools.partial(_inproj_kernel, n_gate_blocks=GATE_COLS // PROJ_TN),
        grid=(t // tm, n // PROJ_TN),
        in_specs=[
            pl.BlockSpec((tm, d), lambda i, j: (i, 0)),
            pl.BlockSpec((1, d), lambda i, j: (0, 0)),
            pl.BlockSpec((d, PROJ_TN), lambda i, j: (0, j)),
        ],
        out_specs=pl.BlockSpec((tm, PROJ_TN), lambda i, j: (i, j)),
        out_shape=jax.ShapeDtypeStruct((t, n), BF16),
        scratch_shapes=[pltpu.VMEM((tm, d), BF16)],
        compiler_params=_cparams(("parallel", "arbitrary")),
        name="inproj",
    )(x2d, g, w_cat)


def _diff_attn_kernel(lam_ref, q_ref, k_ref, v_ref, bias_ref, g_ref, o_ref, vt_ref, tbuf_a, tbuf_b,
                      acc1, acc2, *,
                      out_scale, n_blk):
    qi = pl.program_id(2)
    c = (A_HEAD_DIM ** -0.5) * math.log2(math.e)

    @pl.when(qi == 0)
    def _():
        def transpose_block(b, carry):
            r0 = pl.multiple_of(b * ATT_BLK, ATT_BLK)
            vt_ref[b] = v_ref[pl.ds(r0, ATT_BLK), :].astype(F32).T.astype(BF16)
            return carry

        lax.fori_loop(0, n_blk, transpose_block, 0)

    q1 = q_ref[:, :A_HEAD_DIM]
    q2 = q_ref[:, A_HEAD_DIM:]
    acc1[...] = jnp.zeros_like(acc1)
    acc2[...] = jnp.zeros_like(acc2)

    def scores(ki, dst):
        k0 = pl.multiple_of(ki * ATT_BLK, ATT_BLK)
        kb = k_ref[pl.ds(k0, ATT_BLK), :]
        bias = bias_ref[0, qi - ki]
        dst[0] = _dot_nt(kb[:, :A_HEAD_DIM], q1) + bias
        dst[1] = _dot_nt(kb[:, A_HEAD_DIM:], q2) + bias

    def softmax(t, m, l):
        m_new = jnp.maximum(m, jnp.max(t, axis=0, keepdims=True))
        alpha = jnp.exp2((m - m_new) * c)
        p = jnp.exp2((t - m_new) * c)
        return m_new, alpha * l + jnp.sum(p, axis=0, keepdims=True), alpha, p.astype(BF16)

    scores(0, tbuf_a)

    def step(ki, carry, cur, nxt):
        m1, l1, m2, l2 = carry
        m1, l1, a1, p1 = softmax(cur[0], m1, l1)
        m2, l2, a2, p2 = softmax(cur[1], m2, l2)
        scores(jnp.minimum(ki + 1, qi), nxt)
        vt = vt_ref[ki]
        acc1[...] = acc1[...] * a1 + _dot(vt, p1)
        acc2[...] = acc2[...] * a2 + _dot(vt, p2)
        return m1, l1, m2, l2

    def body(ki, carry):
        return lax.cond(lax.bitwise_and(ki, 1) == 0,
                        lambda cr: step(ki, cr, tbuf_a, tbuf_b),
                        lambda cr: step(ki, cr, tbuf_b, tbuf_a), carry)

    minf = jnp.full((1, ATT_BLK), -jnp.inf, F32)
    zero = jnp.zeros((1, ATT_BLK), F32)
    m1, l1, m2, l2 = lax.fori_loop(0, qi + 1, body, (minf, zero, minf, zero))
    w = acc1[...] / l1 - lam_ref[0, 0] * (acc2[...] / l2)
    ms = jnp.mean(w * w, axis=0, keepdims=True)
    y = ((w * lax.rsqrt(ms + SUBLN_EPS)) * g_ref[...]) * out_scale
    o_ref[...] = y.T.astype(o_ref.dtype)


def _diff_attn_call(zg, lam, bias_tiles, subln_g, *, batch, seq, out_scale):
    t = batch * seq
    nq = seq // ATT_BLK
    width = 2 * A_HEAD_DIM
    q_col = GATE_COLS // width
    k_col = q_col + A_HEADS
    v_col = k_col + A_HEADS
    return pl.pallas_call(
        functools.partial(_diff_attn_kernel, out_scale=out_scale, n_blk=nq),
        grid=(A_HEADS, batch, nq),
        in_specs=[
            pl.BlockSpec(memory_space=pltpu.SMEM),
            pl.BlockSpec((ATT_BLK, width), lambda h, b, i: (b * nq + i, q_col + h)),
            pl.BlockSpec((seq, width), lambda h, b, i: (b, k_col + h)),
            pl.BlockSpec((seq, width), lambda h, b, i: (b, v_col + h)),
            pl.BlockSpec((1, nq, ATT_BLK, ATT_BLK), lambda h, b, i: (h, 0, 0, 0)),
            pl.BlockSpec((width, 1), lambda h, b, i: (0, 0)),
        ],
        out_specs=pl.BlockSpec((ATT_BLK, width), lambda h, b, i: (b * nq + i, h)),
        out_shape=jax.ShapeDtypeStruct((t, A_HEADS * width), BF16),
        scratch_shapes=[pltpu.VMEM((nq, width, ATT_BLK), BF16),
                        pltpu.VMEM((2, ATT_BLK, ATT_BLK), F32), pltpu.VMEM((2, ATT_BLK, ATT_BLK), F32),
                        pltpu.VMEM((width, ATT_BLK), F32), pltpu.VMEM((width, ATT_BLK), F32)],
        compiler_params=_cparams(("parallel", "parallel", "arbitrary")),
        name="diff_attn",
    )(lam, zg, zg, zg, bias_tiles, subln_g.reshape(width, 1))


def _perm_matrix(dil, inverse):
    w = PERM_BLK // dil
    shift = w.bit_length() - 1
    a = lax.broadcasted_iota(jnp.int32, (PERM_BLK, PERM_BLK), 0)
    b = lax.broadcasted_iota(jnp.int32, (PERM_BLK, PERM_BLK), 1)
    dst, src = (b, a) if inverse else (a, b)
    c = lax.shift_right_logical(dst, shift)
    ll = lax.bitwise_and(dst, w - 1)
    return jnp.where(src == ll * dil + c, jnp.float32(1), jnp.float32(0))


def _deinterleave(pairs, perm, dil, seq):
    w = PERM_BLK // dil
    stream_len = seq // dil

    def body(b8, carry):
        r0 = pl.multiple_of(b8 * PERM_BLK, PERM_BLK)
        ys = [_dot(perm, src[pl.ds(r0, PERM_BLK), :]).astype(dst.dtype) for src, dst in pairs]
        for y, (_, dst) in zip(ys, pairs):
            for c in range(dil):
                d0 = pl.multiple_of(c * stream_len + b8 * w, w)
                dst[pl.ds(d0, w), :] = y[c * w:(c + 1) * w, :]
        return carry

    lax.fori_loop(0, seq // PERM_BLK, body, 0)


def _interleave(items, dil, seq):
    w = PERM_BLK // dil
    stream_len = seq // dil

    def body(b8, carry):
        for src, _, stack, _, _ in items:
            for c in range(dil):
                s0 = pl.multiple_of(c * stream_len + b8 * w, w)
                stack[c * w:(c + 1) * w, :] = src[pl.ds(s0, w), :]
        outs = [_dot(pinv, stack[...], **kw) for _, _, stack, pinv, kw in items]
        r0 = pl.multiple_of(b8 * PERM_BLK, PERM_BLK)
        for out, (_, dst, _, _, _) in zip(outs, items):
            dst[pl.ds(r0, PERM_BLK), :] = out.astype(dst.dtype)
        return carry

    lax.fori_loop(0, seq // PERM_BLK, body, 0)


def _transpose_blocks(src_ref, vt_ref, seq):
    def body(blk, carry):
        r0 = pl.multiple_of(blk * BLK, BLK)
        vt_ref[blk] = src_ref[pl.ds(r0, BLK), :].astype(F32).T.astype(vt_ref.dtype)
        return carry

    lax.fori_loop(0, seq // BLK, body, 0)


def _dil_window(q_src, k_src, vt_ref, o_dst, lse_dst, bias_ref, ot_ref, lt_ref, rq, nk):
    scale = B_HEAD_DIM ** -0.5
    c = scale * math.log2(math.e)
    qb = rq // BLK if isinstance(rq, int) else lax.shift_right_logical(rq, BLK.bit_length() - 1)
    rk = rq - (nk - BLK)
    tiles = []
    for hh in range(B_HEADS):
        cols = slice(hh * B_HEAD_DIM, (hh + 1) * B_HEAD_DIM)
        qh = q_src[pl.ds(rq, BLK), cols]
        kh = k_src[pl.ds(rk, nk), cols]
        tiles.append(_dot_nt(kh, qh) + bias_ref[0, hh, 2 * BLK - nk:, :])
    for hh in range(B_HEADS):
        cols = slice(hh * B_HEAD_DIM, (hh + 1) * B_HEAD_DIM)
        t = tiles[hh]
        m = jnp.max(t, axis=0, keepdims=True)
        p = jnp.exp2((t - m) * c)
        den = jnp.sum(p, axis=0, keepdims=True)
        if nk == BLK:
            vth = vt_ref[qb, cols, :]
        else:
            vth = jnp.concatenate([vt_ref[qb - 1, cols, :], vt_ref[qb, cols, :]], axis=1)
        ot_ref[cols, :] = _dot(vth, p.astype(BF16)) / den
        lt_ref[hh:hh + 1, :] = m * scale + jnp.log(den)
    o_dst[pl.ds(rq, BLK), :] = ot_ref[...].T.astype(o_dst.dtype)
    lse_dst[pl.ds(rq, BLK), :] = lt_ref[...].T


def _dil_streams(q_src, k_src, vt_ref, o_dst, lse_dst, bias_ref, ot_ref, lt_ref, dil, seq):
    stream_len = seq // dil
    nq = stream_len // BLK

    def stream(c, carry):
        base = _mult(c * stream_len, BLK)
        _dil_window(q_src, k_src, vt_ref, o_dst, lse_dst, bias_ref, ot_ref, lt_ref, base, BLK)

        def qblock(n, carry2):
            rq = pl.multiple_of(base + n * BLK, BLK)
            _dil_window(q_src, k_src, vt_ref, o_dst, lse_dst, bias_ref, ot_ref, lt_ref, rq, 2 * BLK)
            return carry2

        if nq > 1:
            lax.fori_loop(1, nq, qblock, 0)
        return carry

    if dil == 1:
        stream(0, 0)
    else:
        lax.fori_loop(0, dil, stream, 0)


def _head_expand(x, expand):
    hi = x.astype(BF16)
    lo = (x - hi.astype(F32)).astype(BF16)
    return _dot(hi, expand) + _dot(lo, expand)


def _dil_merge(o_src, lse_src, oacc, mrun, lrun, out_ref, first, last, seq):
    er = lax.broadcasted_iota(jnp.int32, (LANES, B_HEADS * B_HEAD_DIM), 0)
    ec = lax.broadcasted_iota(jnp.int32, (LANES, B_HEADS * B_HEAD_DIM), 1)
    expand = jnp.where(lax.shift_right_logical(ec, B_HEAD_DIM.bit_length() - 1) == er,
                       jnp.float32(1), jnp.float32(0)).astype(BF16)

    def body(ch, carry):
        r0 = pl.multiple_of(ch * ROW_TM, ROW_TM)
        rows = pl.ds(r0, ROW_TM)
        lse = lse_src[rows, :]
        if first:
            mrun[rows, :] = lse
            lrun[rows, :] = jnp.ones_like(lse)
            oacc[rows, :] = o_src[rows, :].astype(F32)
            return carry
        m_old = mrun[rows, :]
        m_new = jnp.maximum(m_old, lse)
        a = jnp.exp(m_old - m_new)
        bw = jnp.exp(lse - m_new)
        l_new = lrun[rows, :] * a + bw
        val = oacc[rows, :] * _head_expand(a, expand) + o_src[rows, :].astype(F32) * _head_expand(bw, expand)
        if last:
            out_ref[rows, :] = (val * _head_expand(1.0 / l_new, expand)).astype(out_ref.dtype)
        else:
            mrun[rows, :] = m_new
            lrun[rows, :] = l_new
            oacc[rows, :] = val
        return carry

    lax.fori_loop(0, seq // ROW_TM, body, 0)


def _dilated_kernel(q_ref, k_ref, v_ref, bias_ref, out_ref,
                    qs, ks, vs, vt, os_, lses, otok, lsetok, ostack, lstack, ot, lt, oacc, mrun, lrun,
                    *, seq):
    g = pl.program_id(1)

    @pl.when(g == 0)
    def _():
        lt[...] = jnp.zeros_like(lt)

    for gi, (_, dil) in enumerate(DILATED_GROUPS):

        @pl.when(g == gi)
        def _(gi=gi, dil=dil):
            first = gi == 0
            last = gi == N_DGROUPS - 1
            if dil == 1:
                _transpose_blocks(v_ref, vt, seq)
                _dil_streams(q_ref, k_ref, vt, otok, lsetok, bias_ref, ot, lt, 1, seq)
            else:
                perm = _perm_matrix(dil, inverse=False).astype(BF16)
                _deinterleave([(q_ref, qs), (k_ref, ks), (v_ref, vs)], perm, dil, seq)
                _transpose_blocks(vs, vt, seq)
                _dil_streams(qs, ks, vt, os_, lses, bias_ref, ot, lt, dil, seq)
                perm_inv = _perm_matrix(dil, inverse=True)
                _interleave([(os_, otok, ostack, perm_inv.astype(BF16), {}),
                             (lses, lsetok, lstack, perm_inv, dict(precision=lax.Precision.HIGHEST))],
                            dil, seq)
            _dil_merge(otok, lsetok, oacc, mrun, lrun, out_ref, first, last, seq)


def _dilated_call(zg, bias_tiles, *, batch, seq):
    width = B_HEADS * B_HEAD_DIM
    q_col = (GATE_COLS + 3 * A_HEADS * 2 * A_HEAD_DIM) // width
    k_col = q_col + N_DGROUPS
    v_col = k_col + N_DGROUPS
    return pl.pallas_call(
        functools.partial(_dilated_kernel, seq=seq),
        grid=(batch, N_DGROUPS),
        in_specs=[
            pl.BlockSpec((seq, width), lambda b, g: (b, q_col + g)),
            pl.BlockSpec((seq, width), lambda b, g: (b, k_col + g)),
            pl.BlockSpec((seq, width), lambda b, g: (b, v_col + g)),
            pl.BlockSpec((1, B_HEADS, 2 * BLK, BLK), lambda b, g: (g, 0, 0, 0)),
        ],
        out_specs=pl.BlockSpec((seq, width), lambda b, g: (b, 0)),
        out_shape=jax.ShapeDtypeStruct((batch * seq, width), BF16),
        scratch_shapes=[
            pltpu.VMEM((seq, width), BF16), pltpu.VMEM((seq, width), BF16), pltpu.VMEM((seq, width), BF16),
            pltpu.VMEM((seq // BLK, width, BLK), BF16),
            pltpu.VMEM((seq, width), BF16), pltpu.VMEM((seq, LANES), F32),
            pltpu.VMEM((seq, width), BF16), pltpu.VMEM((seq, LANES), F32),
            pltpu.VMEM((PERM_BLK, width), BF16), pltpu.VMEM((PERM_BLK, LANES), F32),
            pltpu.VMEM((width, BLK), F32), pltpu.VMEM((LANES, BLK), F32),
            pltpu.VMEM((seq, width), F32), pltpu.VMEM((seq, LANES), F32), pltpu.VMEM((seq, LANES), F32),
        ],
        compiler_params=_cparams(("parallel", "arbitrary")),
        name="dilated_attn",
    )(zg, zg, zg, bias_tiles)


def _merge_router_kernel(oa_ref, ob_ref, gate_ref, x_ref, pa_ref, pb_ref, wo_ref, g_ref, wr_ref,
                         x1_ref, h2_ref, route_ref, cnt_ref, run_ref):
    tm, d = x_ref.shape

    @pl.when(pl.program_id(0) == 0)
    def _():
        run_ref[...] = jnp.zeros_like(run_ref)

    a = _dot(oa_ref[...], pa_ref[...])
    bm = _dot(ob_ref[...], pb_ref[...])
    merged = gate_ref[:, :d].astype(F32) * a + gate_ref[:, d:].astype(F32) * bm
    x1 = x_ref[...] + _dot(merged.astype(BF16), wo_ref[...])
    x1_ref[...] = x1
    h2 = _rmsnorm_val(x1, g_ref[...], RMS_EPS)
    h2_ref[...] = h2

    h_hi = h2.astype(BF16)
    h_lo = (h2 - h_hi.astype(F32)).astype(BF16)
    lt = _dot_nt(wr_ref[0], h_hi) + (_dot_nt(wr_ref[1], h_hi) + _dot_nt(wr_ref[0], h_lo))
    coarse = [lt[i:i + 1, :] for i in range(N_EXPERT_GROUPS)]
    best = coarse[0]
    gsel = jnp.zeros((1, tm), jnp.int32)
    for i in range(1, N_EXPERT_GROUPS):
        upd = coarse[i] > best
        gsel = jnp.where(upd, i, gsel)
        best = jnp.where(upd, coarse[i], best)
    den = jnp.exp(coarse[0] - best)
    for i in range(1, N_EXPERT_GROUPS):
        den = den + jnp.exp(coarse[i] - best)
    pg = 1.0 / den

    fine = []
    for k in range(EXPERTS_PER_GROUP):
        f = lt[N_EXPERT_GROUPS + k:N_EXPERT_GROUPS + k + 1, :]
        for gi in range(1, N_EXPERT_GROUPS):
            r = N_EXPERT_GROUPS + gi * EXPERTS_PER_GROUP + k
            f = jnp.where(gsel == gi, lt[r:r + 1, :], f)
        fine.append(f)
    v0 = fine[0]
    i0 = jnp.zeros((1, tm), jnp.int32)
    for k in range(1, EXPERTS_PER_GROUP):
        upd = fine[k] > v0
        i0 = jnp.where(upd, k, i0)
        v0 = jnp.where(upd, fine[k], v0)
    v1 = jnp.full((1, tm), -jnp.inf, F32)
    i1 = jnp.zeros((1, tm), jnp.int32)
    for k in range(EXPERTS_PER_GROUP):
        upd = jnp.where(i0 != k, jnp.where(fine[k] > v1, 1, 0), 0) == 1
        i1 = jnp.where(upd, k, i1)
        v1 = jnp.where(upd, fine[k], v1)
    e1w = jnp.exp(v1 - v0)
    gate0 = pg * (1.0 / (1.0 + e1w))
    gate1 = pg * (e1w / (1.0 + e1w))
    e0 = gsel * EXPERTS_PER_GROUP + i0
    e1 = gsel * EXPERTS_PER_GROUP + i1

    eidx = lax.broadcasted_iota(jnp.int32, (N_EXPERTS, tm), 0)
    oh0 = jnp.where(eidx == e0, jnp.float32(1), jnp.float32(0))
    oh1 = jnp.where(eidx == e1, jnp.float32(1), jnp.float32(0))
    ta = lax.broadcasted_iota(jnp.int32, (tm, tm), 0)
    tb = lax.broadcasted_iota(jnp.int32, (tm, tm), 1)
    before = jnp.where(ta < tb, jnp.float32(1), jnp.float32(0)).astype(BF16)
    pre0 = _dot(oh0.astype(BF16), before)
    pre1 = _dot(oh1.astype(BF16), before)
    run = run_ref[:, 0:1]
    tot0 = jnp.sum(oh0, axis=1, keepdims=True)
    tot1 = jnp.sum(oh1, axis=1, keepdims=True)
    rank0 = jnp.sum(oh0 * (run + pre0), axis=0, keepdims=True)
    rank1 = jnp.sum(oh1 * (run + tot0 + pre1), axis=0, keepdims=True)
    new_run = jnp.broadcast_to(run + tot0 + tot1, run_ref.shape)
    run_ref[...] = new_run
    cnt_ref[...] = new_run

    route_ref[0:1, :] = e0.astype(F32)
    route_ref[1:2, :] = e1.astype(F32)
    route_ref[2:3, :] = gate0
    route_ref[3:4, :] = gate1
    route_ref[4:5, :] = rank0
    route_ref[5:6, :] = rank1
    route_ref[6:8, :] = jnp.zeros((2, tm), F32)


def _merge_router_call(oa, ob, zg, x2d, pa, pb, wo, g, wr_t):
    t, d = x2d.shape
    tm = ROW_TM
    const = dict(pipeline_mode=pl.Buffered(1))
    return pl.pallas_call(
        _merge_router_kernel,
        grid=(t // tm,),
        in_specs=[
            pl.BlockSpec((tm, oa.shape[1]), lambda i: (i, 0)),
            pl.BlockSpec((tm, ob.shape[1]), lambda i: (i, 0)),
            pl.BlockSpec((tm, GATE_COLS), lambda i: (i, 0)),
            pl.BlockSpec((tm, d), lambda i: (i, 0)),
            pl.BlockSpec(pa.shape, lambda i: (0, 0), **const),
            pl.BlockSpec(pb.shape, lambda i: (0, 0), **const),
            pl.BlockSpec(wo.shape, lambda i: (0, 0), **const),
            pl.BlockSpec((1, d), lambda i: (0, 0)),
            pl.BlockSpec(wr_t.shape, lambda i: (0, 0, 0), **const),
        ],
        out_specs=[
            pl.BlockSpec((tm, d), lambda i: (i, 0)),
            pl.BlockSpec((tm, d), lambda i: (i, 0)),
            pl.BlockSpec((8, tm), lambda i: (0, i)),
            pl.BlockSpec((N_EXPERTS, LANES), lambda i: (0, 0)),
        ],
        out_shape=[
            jax.ShapeDtypeStruct((t, d), F32),
            jax.ShapeDtypeStruct((t, d), F32),
            jax.ShapeDtypeStruct((8, t), F32),
            jax.ShapeDtypeStruct((N_EXPERTS, LANES), F32),
        ],
        scratch_shapes=[pltpu.VMEM((N_EXPERTS, LANES), F32)],
        compiler_params=_cparams(("arbitrary",)),
        name="merge_router",
    )(oa, ob, zg, x2d, pa, pb, wo, g, wr_t)


def _row_copy(src, s_row, dst, d_row, sem):
    return pltpu.make_async_copy(src.at[pl.ds(s_row, 1)], dst.at[pl.ds(d_row, 1)], sem)


def _dispatch_kernel(dest_ref, zstart_ref, zlen_ref, h_ref, x_hbm, zero_ref, sem, zsem):
    step = pl.program_id(0)

    def issue(t, carry):
        for k in range(TOP_K):
            _row_copy(h_ref, t, x_hbm, dest_ref[0, k, t], sem.at[k]).start()
        return carry

    lax.fori_loop(0, DISPATCH_TOK, issue, 0, unroll=8)

    @pl.when(step == 0)
    def _():
        zero_ref[...] = jnp.zeros_like(zero_ref)

        def group_copy(r8):
            return pltpu.make_async_copy(zero_ref, x_hbm.at[pl.ds(pl.multiple_of(r8, 8), 8)], zsem.at[1])

        def segment(e, wait):
            start = zstart_ref[e]
            n_head = jnp.minimum(lax.bitwise_and(-start, 7), zlen_ref[e])
            n_group = lax.shift_right_logical(zlen_ref[e] - n_head, 3)

            def head(r, carry):
                cp = _row_copy(zero_ref, 0, x_hbm, 0 if wait else start + r, zsem.at[0])
                cp.wait() if wait else cp.start()
                return carry

            def group(j, carry):
                cp = group_copy(0 if wait else start + n_head + 8 * j)
                cp.wait() if wait else cp.start()
                return carry

            lax.fori_loop(0, n_head, head, 0)
            lax.fori_loop(0, n_group, group, 0)

        lax.fori_loop(0, N_EXPERTS + 1, lambda e, c: (segment(e, False), c)[1], 0)
        lax.fori_loop(0, N_EXPERTS + 1, lambda e, c: (segment(e, True), c)[1], 0)

    for k in range(TOP_K):
        pltpu.make_async_copy(h_ref, x_hbm.at[pl.ds(0, DISPATCH_TOK)], sem.at[k]).wait()


def _dispatch_call(dest_blocks, zstart, zlen, h2, n_rows):
    t, d = h2.shape
    return pl.pallas_call(
        _dispatch_kernel,
        grid=(t // DISPATCH_TOK,),
        in_specs=[
            pl.BlockSpec((1, TOP_K, DISPATCH_TOK), lambda i: (i, 0, 0), memory_space=pltpu.SMEM),
            pl.BlockSpec(memory_space=pltpu.SMEM),
            pl.BlockSpec(memory_space=pltpu.SMEM),
            pl.BlockSpec((DISPATCH_TOK, d), lambda i: (i, 0)),
        ],
        out_specs=pl.BlockSpec(memory_space=pl.ANY),
        out_shape=jax.ShapeDtypeStruct((n_rows, d), F32),
        scratch_shapes=[pltpu.VMEM((8, d), F32), pltpu.SemaphoreType.DMA((TOP_K,)),
                        pltpu.SemaphoreType.DMA((2,))],
        compiler_params=_cparams(("arbitrary",)),
        name="dispatch",
    )(dest_blocks, zstart, zlen, h2)


def _expert_kernel(blk_e_ref, blk_valid_ref, x_ref, w1_ref, w3_ref, w2_ref, y_ref):
    i = pl.program_id(0)

    @pl.when(blk_valid_ref[i] == 1)
    def _():
        x = x_ref[...].astype(BF16)
        a = _dot(x, w1_ref[0])
        b = _dot(x, w3_ref[0])
        hdn = (a * _sigmoid(a)) * b
        y_ref[...] = _dot(hdn.astype(BF16), w2_ref[0])

    @pl.when(blk_valid_ref[i] == 0)
    def _():
        y_ref[...] = jnp.zeros_like(y_ref)


def _expert_call(blk_e, blk_valid, xin, w1, w3, w2):
    n_rows, d = xin.shape
    de = w1.shape[2]
    grid_spec = pltpu.PrefetchScalarGridSpec(
        num_scalar_prefetch=2,
        grid=(n_rows // EXP_TM,),
        in_specs=[
            pl.BlockSpec((EXP_TM, d), lambda i, be, bv: (i, 0)),
            pl.BlockSpec((1, d, de), lambda i, be, bv: (be[i], 0, 0)),
            pl.BlockSpec((1, d, de), lambda i, be, bv: (be[i], 0, 0)),
            pl.BlockSpec((1, de, d), lambda i, be, bv: (be[i], 0, 0)),
        ],
        out_specs=pl.BlockSpec((EXP_TM, d), lambda i, be, bv: (i, 0)),
    )
    return pl.pallas_call(
        _expert_kernel,
        grid_spec=grid_spec,
        out_shape=jax.ShapeDtypeStruct((n_rows, d), F32),
        compiler_params=_cparams(("arbitrary",)),
        name="experts",
    )(blk_e, blk_valid, xin, w1, w3, w2)


def _combine_kernel(dest_ref, dnext_ref, y_hbm, x1_ref, gt_ref, p_ref, gple_ref, wg_ref, wp_ref, gfin_ref,
                    o_ref, ybuf, sem, *, final_norm, n_steps):
    tm = x1_ref.shape[0]
    i = pl.program_id(0)
    slot = lax.bitwise_and(i, 1)

    def gather(d_ref, s):
        def issue(t, carry):
            for k in range(TOP_K):
                _row_copy(y_hbm, d_ref[0, k, t], ybuf.at[s, k], t, sem.at[s, k]).start()
            return carry

        lax.fori_loop(0, tm, issue, 0, unroll=8)

    @pl.when(i == 0)
    def _():
        gather(dest_ref, 0)

    @pl.when(i + 1 < n_steps)
    def _():
        gather(dnext_ref, 1 - slot)

    pp = _dot(p_ref[...].astype(BF16), wp_ref[...])

    for k in range(TOP_K):
        pltpu.make_async_copy(y_hbm.at[pl.ds(0, tm)], ybuf.at[slot, k], sem.at[slot, k]).wait()
    y = ybuf[slot, 0] * gt_ref[:, 2:3] + ybuf[slot, 1] * gt_ref[:, 3:4]
    x2 = x1_ref[...] + y
    hn = _rmsnorm_val(x2, gple_ref[...], RMS_EPS)
    gate = _sigmoid(_dot(hn.astype(BF16), wg_ref[...]))
    x3 = x2 + gate * pp
    if final_norm:
        o_ref[...] = _rmsnorm_val(x3, gfin_ref[...], RMS_EPS)
    else:
        o_ref[...] = x3


def _combine_call(dest_blocks, yb, x1, gates_t, p2d, g_ple, w_gate, w_proj, g_fin, final_norm):
    t, d = x1.shape
    tm = ROW_TM
    const = dict(pipeline_mode=pl.Buffered(1))
    return pl.pallas_call(
        functools.partial(_combine_kernel, final_norm=final_norm, n_steps=t // tm),
        grid=(t // tm,),
        in_specs=[
            pl.BlockSpec((1, TOP_K, tm), lambda i: (i, 0, 0), memory_space=pltpu.SMEM),
            pl.BlockSpec((1, TOP_K, tm), lambda i: (jnp.minimum(i + 1, t // tm - 1), 0, 0),
                         memory_space=pltpu.SMEM),
            pl.BlockSpec(memory_space=pl.ANY),
            pl.BlockSpec((tm, d), lambda i: (i, 0)),
            pl.BlockSpec((tm, 8), lambda i: (i, 0)),
            pl.BlockSpec((tm, p2d.shape[1]), lambda i: (i, 0)),
            pl.BlockSpec((1, d), lambda i: (0, 0)),
            pl.BlockSpec(w_gate.shape, lambda i: (0, 0), **const),
            pl.BlockSpec(w_proj.shape, lambda i: (0, 0), **const),
            pl.BlockSpec((1, d), lambda i: (0, 0)),
        ],
        out_specs=pl.BlockSpec((tm, d), lambda i: (i, 0)),
        out_shape=jax.ShapeDtypeStruct((t, d), F32),
        scratch_shapes=[pltpu.VMEM((2, TOP_K, tm, d), F32), pltpu.SemaphoreType.DMA((2, TOP_K))],
        compiler_params=_cparams(("arbitrary",)),
        name="combine",
    )(dest_blocks, dest_blocks, yb, x1, gates_t, p2d, g_ple, w_gate, w_proj, g_fin)


def _routing_plan(route, counts_f, t):
    counts = counts_f[:, 0].astype(jnp.int32)
    pcounts = ((counts + EXP_TM - 1) // EXP_TM) * EXP_TM
    pends = jnp.cumsum(pcounts)
    pstarts = pends - pcounts
    eid = route[0:2].astype(jnp.int32)
    rank = route[4:6].astype(jnp.int32)
    onehot = eid[..., None] == jnp.arange(N_EXPERTS, dtype=jnp.int32)
    dest = jnp.sum(jnp.where(onehot, pstarts, 0), axis=-1) + rank
    n_rows = TOP_K * t + N_EXPERTS * EXP_TM
    n_blocks = n_rows // EXP_TM
    blk_row = jnp.arange(n_blocks, dtype=jnp.int32) * EXP_TM
    blk_e = jnp.sum((pends[None, :] <= blk_row[:, None]).astype(jnp.int32), axis=1)
    blk_valid = (blk_row < pends[-1]).astype(jnp.int32)
    last_e = jnp.sum((pends <= pends[-1] - 1).astype(jnp.int32))
    blk_e = jnp.minimum(blk_e, last_e)
    zstart = jnp.concatenate([pstarts + counts, pends[-1:]]).astype(jnp.int32)
    zlen = jnp.concatenate([pcounts - counts, n_rows - pends[-1:]]).astype(jnp.int32)
    return dest, blk_e, blk_valid, zstart, zlen, n_rows


def _blocked(dest, tok_per_block):
    t = dest.shape[1]
    return dest.reshape(TOP_K, t // tok_per_block, tok_per_block).transpose(1, 0, 2)


def kernel(x, p, rel_bias, norm_mix_g, w_in, w_gate, lambda_q1, lambda_k1, lambda_q2, lambda_k2, subln_g,
           w_proj_a, w_proj_b, w_out, norm_ffn_g, w_coarse, w_fine, w1, w3, w2, norm_ple_g, w_ple_gate,
           w_ple_proj, final_norm_g):
    batch, seq, d = x.shape
    depth = w_in.shape[0]
    t = batch * seq
    assert seq % PERM_BLK == 0 and t % min(PROJ_TM, t) == 0
    assert all(seq // dil >= BLK and win // dil == BLK for win, dil in DILATED_GROUPS)

    nq = seq // ATT_BLK
    bias_a = _bias_tiles_call(rel_bias, n_heads=A_HEADS, n_off=nq, tq=ATT_BLK, tk=ATT_BLK, off_mult=ATT_BLK,
                              off_add=0, dil=1, max_rel=seq, head0=0, name="bias_diff",
                              key_major=True, mult=A_HEAD_DIM ** 0.5)
    bias_b = jnp.concatenate([
        _bias_tiles_call(rel_bias, n_heads=B_HEADS, n_off=1, tq=2 * BLK, tk=BLK, off_mult=0, off_add=BLK,
                         dil=dil, max_rel=win // dil, head0=A_HEADS + gi * B_HEADS, name="bias_dil%d" % gi,
                         key_major=True, mult=B_HEAD_DIM ** 0.5).reshape(1, B_HEADS, 2 * BLK, BLK)
        for gi, (win, dil) in enumerate(DILATED_GROUPS)], axis=0)

    x2d = x.reshape(t, d)
    for layer in range(depth):
        lam_init = 0.8 - 0.6 * math.exp(-0.3 * layer)
        lam = _lam_call(lambda_q1[layer:layer + 1], lambda_k1[layer:layer + 1],
                        lambda_q2[layer:layer + 1], lambda_k2[layer:layer + 1], lam_init)
        w_cat = jnp.concatenate([w_gate[layer], w_in[layer]], axis=1).astype(BF16)
        zg = _inproj_call(x2d, norm_mix_g[layer:layer + 1], w_cat)
        oa = _diff_attn_call(zg, lam, bias_a, subln_g[layer:layer + 1], batch=batch, seq=seq,
                             out_scale=1.0 - lam_init)
        ob = _dilated_call(zg, bias_b, batch=batch, seq=seq)

        wr_t = jnp.concatenate([
            w_coarse[layer].T,
            w_fine[layer].transpose(0, 2, 1).reshape(N_EXPERTS, d),
            jnp.zeros((LANES - N_EXPERT_GROUPS - N_EXPERTS, d), F32)], axis=0)
        wr_hi = wr_t.astype(BF16)
        wr_t = jnp.stack([wr_hi, (wr_t - wr_hi.astype(F32)).astype(BF16)], axis=0)
        x1, h2, route, counts = _merge_router_call(
            oa, ob, zg, x2d, w_proj_a[layer].astype(BF16), w_proj_b[layer].astype(BF16),
            w_out[layer].astype(BF16), norm_ffn_g[layer:layer + 1], wr_t)

        dest, blk_e, blk_valid, zstart, zlen, n_rows = _routing_plan(route, counts, t)
        xin = _dispatch_call(_blocked(dest, DISPATCH_TOK), zstart, zlen, h2, n_rows)
        yb = _expert_call(blk_e, blk_valid, xin, w1[layer].astype(BF16), w3[layer].astype(BF16),
                          w2[layer].astype(BF16))
        x2d = _combine_call(_blocked(dest, ROW_TM), yb, x1, route.T, p[layer].reshape(t, -1),
                            norm_ple_g[layer:layer + 1], w_ple_gate[layer].astype(BF16),
                            w_ple_proj[layer].astype(BF16), final_norm_g.reshape(1, d),
                            final_norm=layer == depth - 1)
    return x2d.reshape(batch, seq, d)
```

```python
import functools
import math

import jax
import jax.numpy as jnp
from jax import lax
from jax.experimental import pallas as pl
from jax.experimental.pallas import tpu as pltpu

F32 = jnp.float32
BF16 = jnp.bfloat16

BLK = 128
NEG_INF = -1e30
RMS_EPS = 1e-6
SUBLN_EPS = 1e-5
N_BUCKETS = 32
MAX_DISTANCE = 2048
A_HEADS = 8
A_HEAD_DIM = 128
DILATED_GROUPS = ((128, 1), (512, 4), (2048, 16))
N_DGROUPS = 3
B_HEADS = 8
B_HEAD_DIM = 64
N_EXPERT_GROUPS = 4
EXPERTS_PER_GROUP = 8
N_EXPERTS = 32
TOP_K = 2

LANES = 128
VMEM_LIMIT = 56 * 1024 * 1024
ATT_BLK = 256
ATT_HEADS = 2
PERM_BLK = 256
GATE_COLS = 4096
PROJ_TN = 512
PROJ_TM = 2048
ROW_TM = 256
EXP_TM = 256
DISPATCH_TOK = 512
NORM_CHUNK = 128


def _t5_thresholds():
    max_exact = N_BUCKETS // 2
    out = []
    for k in range(1, N_BUCKETS - max_exact):
        out.append(int(math.ceil(max_exact * (MAX_DISTANCE / max_exact) ** (k / (N_BUCKETS - max_exact)))))
    return tuple(out)


T5_THRESHOLDS = _t5_thresholds()


def _cparams(sem, vmem=VMEM_LIMIT):
    return pltpu.CompilerParams(dimension_semantics=sem, vmem_limit_bytes=vmem)


def _mult(x, m):
    return x if isinstance(x, int) else pl.multiple_of(x, m)


def _sigmoid(x):
    return 0.5 * jnp.tanh(0.5 * x) + 0.5


def _dot(a, b, **kw):
    return jnp.dot(a, b, preferred_element_type=F32, **kw)


def _dot_nt(a, b, **kw):
    return lax.dot_general(a, b, (((1,), (1,)), ((), ())), preferred_element_type=F32, **kw)


def _lam_kernel(q1_ref, k1_ref, q2_ref, k2_ref, o_ref, *, lam_init):
    s1 = jnp.sum(q1_ref[...] * k1_ref[...], axis=-1, keepdims=True)
    s2 = jnp.sum(q2_ref[...] * k2_ref[...], axis=-1, keepdims=True)
    o_ref[...] = jnp.exp(s1) - jnp.exp(s2) + lam_init


def _lam_call(lq1, lk1, lq2, lk2, lam_init):
    return pl.pallas_call(
        functools.partial(_lam_kernel, lam_init=lam_init),
        out_shape=jax.ShapeDtypeStruct((1, 1), F32),
        name="lam",
    )(lq1, lk1, lq2, lk2)


def _bias_tile_kernel(tab_ref, o_ref, *, tq, tk, off_mult, off_add, dil, max_rel, head0, key_major, mult):
    h = pl.program_id(0)
    n = pl.program_id(1)
    i = lax.broadcasted_iota(jnp.int32, (tq, tk), 1 if key_major else 0)
    j = lax.broadcasted_iota(jnp.int32, (tq, tk), 0 if key_major else 1)
    rel = i - j + (n * off_mult + off_add)
    dist = rel * dil
    large = jnp.full((tq, tk), N_BUCKETS // 2, jnp.int32)
    for thr in T5_THRESHOLDS:
        large = large + jnp.where(dist >= thr, 1, 0)
    bucket = jnp.where(dist < N_BUCKETS // 2, dist, large)
    acc = jnp.zeros((tq, tk), F32)
    for b in range(N_BUCKETS):
        acc = jnp.where(bucket == b, tab_ref[b, head0 + h], acc)
    valid = jnp.where(rel >= 0, jnp.where(rel <= max_rel, 1, 0), 0)
    o_ref[0, 0] = jnp.where(valid == 1, acc * mult, NEG_INF)


def _bias_tiles_call(rel_bias, *, n_heads, n_off, tq, tk, off_mult, off_add, dil, max_rel, head0, name,
                     key_major=False, mult=1.0):
    kern = functools.partial(_bias_tile_kernel, tq=tq, tk=tk, off_mult=off_mult, off_add=off_add,
                             dil=dil, max_rel=max_rel, head0=head0, key_major=key_major, mult=mult)
    return pl.pallas_call(
        kern,
        grid=(n_heads, n_off),
        in_specs=[pl.BlockSpec(memory_space=pltpu.SMEM)],
        out_specs=pl.BlockSpec((1, 1, tq, tk), lambda h, n: (h, n, 0, 0)),
        out_shape=jax.ShapeDtypeStruct((n_heads, n_off, tq, tk), F32),
        compiler_params=_cparams(("parallel", "parallel")),
        name=name,
    )(rel_bias)


def _rmsnorm_rows(x_ref, g_ref, out_ref, eps):
    rows = x_ref.shape[0]
    g = g_ref[...]

    def body(c, carry):
        r0 = pl.multiple_of(c * NORM_CHUNK, NORM_CHUNK)
        x = x_ref[pl.ds(r0, NORM_CHUNK), :]
        ms = jnp.mean(x * x, axis=-1, keepdims=True)
        out_ref[pl.ds(r0, NORM_CHUNK), :] = ((x * lax.rsqrt(ms + eps)) * g).astype(out_ref.dtype)
        return carry

    lax.fori_loop(0, rows // NORM_CHUNK, body, 0)


def _rmsnorm_val(x, g, eps):
    ms = jnp.mean(x * x, axis=-1, keepdims=True)
    return (x * lax.rsqrt(ms + eps)) * g


def _inproj_kernel(x_ref, g_ref, w_ref, o_ref, h_ref, *, n_gate_blocks):
    j = pl.program_id(1)

    @pl.when(j == 0)
    def _():
        _rmsnorm_rows(x_ref, g_ref, h_ref, RMS_EPS)

    acc = _dot(h_ref[...], w_ref[...])

    @pl.when(j < n_gate_blocks)
    def _():
        o_ref[...] = _sigmoid(acc).astype(o_ref.dtype)

    @pl.when(j >= n_gate_blocks)
    def _():
        o_ref[...] = acc.astype(o_ref.dtype)


def _inproj_call(x2d, g, w_cat):
    t, d = x2d.shape
    n = w_cat.shape[1]
    tm = min(PROJ_TM, t)
    return pl.pallas_call(
        functools.partial(_inproj_kernel, n_gate_blocks=GATE_COLS // PROJ_TN),
        grid=(t // tm, n // PROJ_TN),
        in_specs=[
            pl.BlockSpec((tm, d), lambda i, j: (i, 0)),
            pl.BlockSpec((1, d), lambda i, j: (0, 0)),
            pl.BlockSpec((d, PROJ_TN), lambda i, j: (0, j)),
        ],
        out_specs=pl.BlockSpec((tm, PROJ_TN), lambda i, j: (i, j)),
        out_shape=jax.ShapeDtypeStruct((t, n), BF16),
        scratch_shapes=[pltpu.VMEM((tm, d), BF16)],
        compiler_params=_cparams(("parallel", "arbitrary")),
        name="inproj",
    )(x2d, g, w_cat)


def _diff_attn_kernel(lam_ref, q_ref, k_ref, v_ref, bias_ref, g_ref, o_ref, vt_ref, tbuf_a, tbuf_b, acc, *,
                      out_scale, n_blk):
    qi = pl.program_id(2)
    width = 2 * A_HEAD_DIM
    n_map = 2 * ATT_HEADS
    c = (A_HEAD_DIM ** -0.5) * math.log2(math.e)

    @pl.when(qi == 0)
    def _():
        def transpose_block(b, carry):
            r0 = pl.multiple_of(b * ATT_BLK, ATT_BLK)
            for hp in range(ATT_HEADS):
                vb = v_ref[pl.ds(r0, ATT_BLK), hp * width:(hp + 1) * width]
                vt_ref[hp, b] = vb.astype(F32).T.astype(BF16)
            return carry

        lax.fori_loop(0, n_blk, transpose_block, 0)

    qs = [q_ref[:, mi * A_HEAD_DIM:(mi + 1) * A_HEAD_DIM] for mi in range(n_map)]
    acc[...] = jnp.zeros_like(acc)

    def scores(ki, dst):
        k0 = pl.multiple_of(ki * ATT_BLK, ATT_BLK)
        for mi in range(n_map):
            kb = k_ref[pl.ds(k0, ATT_BLK), mi * A_HEAD_DIM:(mi + 1) * A_HEAD_DIM]
            dst[mi] = _dot_nt(kb, qs[mi]) + bias_ref[mi // 2, qi - ki]

    def softmax(t, m, l):
        m_new = jnp.maximum(m, jnp.max(t, axis=0, keepdims=True))
        alpha = jnp.exp2((m - m_new) * c)
        p = jnp.exp2((t - m_new) * c)
        return m_new, alpha * l + jnp.sum(p, axis=0, keepdims=True), alpha, p.astype(BF16)

    scores(0, tbuf_a)

    def step(ki, carry, cur, nxt):
        ms, ls = carry
        stats = [softmax(cur[mi], ms[mi], ls[mi]) for mi in range(n_map)]
        scores(jnp.minimum(ki + 1, qi), nxt)
        for mi in range(n_map):
            acc[mi] = acc[mi] * stats[mi][2] + _dot(vt_ref[mi // 2, ki], stats[mi][3])
        return tuple(st[0] for st in stats), tuple(st[1] for st in stats)

    def body(ki, carry):
        return lax.cond(lax.bitwise_and(ki, 1) == 0,
                        lambda cr: step(ki, cr, tbuf_a, tbuf_b),
                        lambda cr: step(ki, cr, tbuf_b, tbuf_a), carry)

    minf = (jnp.full((1, ATT_BLK), -jnp.inf, F32),) * n_map
    zero = (jnp.zeros((1, ATT_BLK), F32),) * n_map
    _, ls = lax.fori_loop(0, qi + 1, body, (minf, zero))
    for hp in range(ATT_HEADS):
        w = acc[2 * hp] / ls[2 * hp] - lam_ref[0, 0] * (acc[2 * hp + 1] / ls[2 * hp + 1])
        ms = jnp.mean(w * w, axis=0, keepdims=True)
        y = ((w * lax.rsqrt(ms + SUBLN_EPS)) * g_ref[...]) * out_scale
        o_ref[:, hp * width:(hp + 1) * width] = y.T.astype(o_ref.dtype)


def _diff_attn_call(zg, lam, bias_tiles, subln_g, *, batch, seq, out_scale):
    t = batch * seq
    nq = seq // ATT_BLK
    width = 2 * A_HEAD_DIM
    blk_w = ATT_HEADS * width
    q_col = GATE_COLS // blk_w
    k_col = q_col + A_HEADS // ATT_HEADS
    v_col = k_col + A_HEADS // ATT_HEADS
    return pl.pallas_call(
        functools.partial(_diff_attn_kernel, out_scale=out_scale, n_blk=nq),
        grid=(A_HEADS // ATT_HEADS, batch, nq),
        in_specs=[
            pl.BlockSpec(memory_space=pltpu.SMEM),
            pl.BlockSpec((ATT_BLK, blk_w), lambda h, b, i: (b * nq + i, q_col + h)),
            pl.BlockSpec((seq, blk_w), lambda h, b, i: (b, k_col + h)),
            pl.BlockSpec((seq, blk_w), lambda h, b, i: (b, v_col + h)),
            pl.BlockSpec((ATT_HEADS, nq, ATT_BLK, ATT_BLK), lambda h, b, i: (h, 0, 0, 0)),
            pl.BlockSpec((width, 1), lambda h, b, i: (0, 0)),
        ],
        out_specs=pl.BlockSpec((ATT_BLK, blk_w), lambda h, b, i: (b * nq + i, h)),
        out_shape=jax.ShapeDtypeStruct((t, A_HEADS * width), BF16),
        scratch_shapes=[pltpu.VMEM((ATT_HEADS, nq, width, ATT_BLK), BF16),
                        pltpu.VMEM((2 * ATT_HEADS, ATT_BLK, ATT_BLK), F32),
                        pltpu.VMEM((2 * ATT_HEADS, ATT_BLK, ATT_BLK), F32),
                        pltpu.VMEM((2 * ATT_HEADS, width, ATT_BLK), F32)],
        compiler_params=_cparams(("parallel", "parallel", "arbitrary")),
        name="diff_attn",
    )(lam, zg, zg, zg, bias_tiles, subln_g.reshape(width, 1))


def _perm_matrix(dil, inverse):
    w = PERM_BLK // dil
    shift = w.bit_length() - 1
    a = lax.broadcasted_iota(jnp.int32, (PERM_BLK, PERM_BLK), 0)
    b = lax.broadcasted_iota(jnp.int32, (PERM_BLK, PERM_BLK), 1)
    dst, src = (b, a) if inverse else (a, b)
    c = lax.shift_right_logical(dst, shift)
    ll = lax.bitwise_and(dst, w - 1)
    return jnp.where(src == ll * dil + c, jnp.float32(1), jnp.float32(0))


def _deinterleave(pairs, perm, dil, seq):
    w = PERM_BLK // dil
    stream_len = seq // dil

    def body(b8, carry):
        r0 = pl.multiple_of(b8 * PERM_BLK, PERM_BLK)
        ys = [_dot(perm, src[pl.ds(r0, PERM_BLK), :]).astype(dst.dtype) for src, dst in pairs]
        for y, (_, dst) in zip(ys, pairs):
            for c in range(dil):
                d0 = pl.multiple_of(c * stream_len + b8 * w, w)
                dst[pl.ds(d0, w), :] = y[c * w:(c + 1) * w, :]
        return carry

    lax.fori_loop(0, seq // PERM_BLK, body, 0)


def _interleave(items, dil, seq):
    w = PERM_BLK // dil
    stream_len = seq // dil

    def body(b8, carry):
        for src, _, stack, _, _ in items:
            for c in range(dil):
                s0 = pl.multiple_of(c * stream_len + b8 * w, w)
                stack[c * w:(c + 1) * w, :] = src[pl.ds(s0, w), :]
        outs = [_dot(pinv, stack[...], **kw) for _, _, stack, pinv, kw in items]
        r0 = pl.multiple_of(b8 * PERM_BLK, PERM_BLK)
        for out, (_, dst, _, _, _) in zip(outs, items):
            dst[pl.ds(r0, PERM_BLK), :] = out.astype(dst.dtype)
        return carry

    lax.fori_loop(0, seq // PERM_BLK, body, 0)


def _transpose_blocks(src_ref, vt_ref, seq):
    def body(blk, carry):
        r0 = pl.multiple_of(blk * BLK, BLK)
        vt_ref[blk] = src_ref[pl.ds(r0, BLK), :].astype(F32).T.astype(vt_ref.dtype)
        return carry

    lax.fori_loop(0, seq // BLK, body, 0)


def _dil_window(q_src, k_src, vt_ref, o_dst, lse_dst, bias_ref, ot_ref, lt_ref, rq, nk):
    scale = B_HEAD_DIM ** -0.5
    c = scale * math.log2(math.e)
    qb = rq // BLK if isinstance(rq, int) else lax.shift_right_logical(rq, BLK.bit_length() - 1)
    rk = rq - (nk - BLK)
    tiles = []
    for hh in range(B_HEADS):
        cols = slice(hh * B_HEAD_DIM, (hh + 1) * B_HEAD_DIM)
        qh = q_src[pl.ds(rq, BLK), cols]
        kh = k_src[pl.ds(rk, nk), cols]
        tiles.append(_dot_nt(kh, qh) + bias_ref[0, hh, 2 * BLK - nk:, :])
    for hh in range(B_HEADS):
        cols = slice(hh * B_HEAD_DIM, (hh + 1) * B_HEAD_DIM)
        t = tiles[hh]
        m = jnp.max(t, axis=0, keepdims=True)
        p = jnp.exp2((t - m) * c)
        den = jnp.sum(p, axis=0, keepdims=True)
        if nk == BLK:
            vth = vt_ref[qb, cols, :]
        else:
            vth = jnp.concatenate([vt_ref[qb - 1, cols, :], vt_ref[qb, cols, :]], axis=1)
        ot_ref[cols, :] = _dot(vth, p.astype(BF16)) / den
        lt_ref[hh:hh + 1, :] = m * scale + jnp.log(den)
    o_dst[pl.ds(rq, BLK), :] = ot_ref[...].T.astype(o_dst.dtype)
    lse_dst[pl.ds(rq, BLK), :] = lt_ref[...].T


def _dil_streams(q_src, k_src, vt_ref, o_dst, lse_dst, bias_ref, ot_ref, lt_ref, dil, seq):
    stream_len = seq // dil
    nq = stream_len // BLK

    def stream(c, carry):
        base = _mult(c * stream_len, BLK)
        _dil_window(q_src, k_src, vt_ref, o_dst, lse_dst, bias_ref, ot_ref, lt_ref, base, BLK)

        def qblock(n, carry2):
            rq = pl.multiple_of(base + n * BLK, BLK)
            _dil_window(q_src, k_src, vt_ref, o_dst, lse_dst, bias_ref, ot_ref, lt_ref, rq, 2 * BLK)
            return carry2

        if nq > 1:
            lax.fori_loop(1, nq, qblock, 0)
        return carry

    if dil == 1:
        stream(0, 0)
    else:
        lax.fori_loop(0, dil, stream, 0)


def _head_expand(x, expand):
    hi = x.astype(BF16)
    lo = (x - hi.astype(F32)).astype(BF16)
    return _dot(hi, expand) + _dot(lo, expand)


def _dil_merge(o_src, lse_src, oacc, mrun, lrun, out_ref, first, last, seq):
    er = lax.broadcasted_iota(jnp.int32, (LANES, B_HEADS * B_HEAD_DIM), 0)
    ec = lax.broadcasted_iota(jnp.int32, (LANES, B_HEADS * B_HEAD_DIM), 1)
    expand = jnp.where(lax.shift_right_logical(ec, B_HEAD_DIM.bit_length() - 1) == er,
                       jnp.float32(1), jnp.float32(0)).astype(BF16)

    def body(ch, carry):
        r0 = pl.multiple_of(ch * ROW_TM, ROW_TM)
        rows = pl.ds(r0, ROW_TM)
        lse = lse_src[rows, :]
        if first:
            mrun[rows, :] = lse
            lrun[rows, :] = jnp.ones_like(lse)
            oacc[rows, :] = o_src[rows, :].astype(F32)
            return carry
        m_old = mrun[rows, :]
        m_new = jnp.maximum(m_old, lse)
        a = jnp.exp(m_old - m_new)
        bw = jnp.exp(lse - m_new)
        l_new = lrun[rows, :] * a + bw
        val = oacc[rows, :] * _head_expand(a, expand) + o_src[rows, :].astype(F32) * _head_expand(bw, expand)
        if last:
            out_ref[rows, :] = (val * _head_expand(1.0 / l_new, expand)).astype(out_ref.dtype)
        else:
            mrun[rows, :] = m_new
            lrun[rows, :] = l_new
            oacc[rows, :] = val
        return carry

    lax.fori_loop(0, seq // ROW_TM, body, 0)


def _dilated_kernel(q_ref, k_ref, v_ref, bias_ref, out_ref,
                    qs, ks, vs, vt, os_, lses, otok, lsetok, ostack, lstack, ot, lt, oacc, mrun, lrun,
                    *, seq):
    g = pl.program_id(1)

    @pl.when(g == 0)
    def _():
        lt[...] = jnp.zeros_like(lt)

    for gi, (_, dil) in enumerate(DILATED_GROUPS):

        @pl.when(g == gi)
        def _(gi=gi, dil=dil):
            first = gi == 0
            last = gi == N_DGROUPS - 1
            if dil == 1:
                _transpose_blocks(v_ref, vt, seq)
                _dil_streams(q_ref, k_ref, vt, otok, lsetok, bias_ref, ot, lt, 1, seq)
            else:
                perm = _perm_matrix(dil, inverse=False).astype(BF16)
                _deinterleave([(q_ref, qs), (k_ref, ks), (v_ref, vs)], perm, dil, seq)
                _transpose_blocks(vs, vt, seq)
                _dil_streams(qs, ks, vt, os_, lses, bias_ref, ot, lt, dil, seq)
                perm_inv = _perm_matrix(dil, inverse=True)
                _interleave([(os_, otok, ostack, perm_inv.astype(BF16), {}),
                             (lses, lsetok, lstack, perm_inv, dict(precision=lax.Precision.HIGHEST))],
                            dil, seq)
            _dil_merge(otok, lsetok, oacc, mrun, lrun, out_ref, first, last, seq)


def _dilated_call(zg, bias_tiles, *, batch, seq):
    width = B_HEADS * B_HEAD_DIM
    q_col = (GATE_COLS + 3 * A_HEADS * 2 * A_HEAD_DIM) // width
    k_col = q_col + N_DGROUPS
    v_col = k_col + N_DGROUPS
    return pl.pallas_call(
        functools.partial(_dilated_kernel, seq=seq),
        grid=(batch, N_DGROUPS),
        in_specs=[
            pl.BlockSpec((seq, width), lambda b, g: (b, q_col + g)),
            pl.BlockSpec((seq, width), lambda b, g: (b, k_col + g)),
            pl.BlockSpec((seq, width), lambda b, g: (b, v_col + g)),
            pl.BlockSpec((1, B_HEADS, 2 * BLK, BLK), lambda b, g: (g, 0, 0, 0)),
        ],
        out_specs=pl.BlockSpec((seq, width), lambda b, g: (b, 0)),
        out_shape=jax.ShapeDtypeStruct((batch * seq, width), BF16),
        scratch_shapes=[
            pltpu.VMEM((seq, width), BF16), pltpu.VMEM((seq, width), BF16), pltpu.VMEM((seq, width), BF16),
            pltpu.VMEM((seq // BLK, width, BLK), BF16),
            pltpu.VMEM((seq, width), BF16), pltpu.VMEM((seq, LANES), F32),
            pltpu.VMEM((seq, width), BF16), pltpu.VMEM((seq, LANES), F32),
            pltpu.VMEM((PERM_BLK, width), BF16), pltpu.VMEM((PERM_BLK, LANES), F32),
            pltpu.VMEM((width, BLK), F32), pltpu.VMEM((LANES, BLK), F32),
            pltpu.VMEM((seq, width), F32), pltpu.VMEM((seq, LANES), F32), pltpu.VMEM((seq, LANES), F32),
        ],
        compiler_params=_cparams(("parallel", "arbitrary")),
        name="dilated_attn",
    )(zg, zg, zg, bias_tiles)


def _merge_router_kernel(oa_ref, ob_ref, gate_ref, x_ref, pa_ref, pb_ref, wo_ref, g_ref, wr_ref,
                         x1_ref, h2_ref, route_ref, cnt_ref, run_ref):
    tm, d = x_ref.shape

    @pl.when(pl.program_id(0) == 0)
    def _():
        run_ref[...] = jnp.zeros_like(run_ref)

    a = _dot(oa_ref[...], pa_ref[...])
    bm = _dot(ob_ref[...], pb_ref[...])
    merged = gate_ref[:, :d].astype(F32) * a + gate_ref[:, d:].astype(F32) * bm
    x1 = x_ref[...] + _dot(merged.astype(BF16), wo_ref[...])
    x1_ref[...] = x1
    h2 = _rmsnorm_val(x1, g_ref[...], RMS_EPS)
    h2_ref[...] = h2

    h_hi = h2.astype(BF16)
    h_lo = (h2 - h_hi.astype(F32)).astype(BF16)
    lt = _dot_nt(wr_ref[0], h_hi) + (_dot_nt(wr_ref[1], h_hi) + _dot_nt(wr_ref[0], h_lo))
    coarse = [lt[i:i + 1, :] for i in range(N_EXPERT_GROUPS)]
    best = coarse[0]
    gsel = jnp.zeros((1, tm), jnp.int32)
    for i in range(1, N_EXPERT_GROUPS):
        upd = coarse[i] > best
        gsel = jnp.where(upd, i, gsel)
        best = jnp.where(upd, coarse[i], best)
    den = jnp.exp(coarse[0] - best)
    for i in range(1, N_EXPERT_GROUPS):
        den = den + jnp.exp(coarse[i] - best)
    pg = 1.0 / den

    fine = []
    for k in range(EXPERTS_PER_GROUP):
        f = lt[N_EXPERT_GROUPS + k:N_EXPERT_GROUPS + k + 1, :]
        for gi in range(1, N_EXPERT_GROUPS):
            r = N_EXPERT_GROUPS + gi * EXPERTS_PER_GROUP + k
            f = jnp.where(gsel == gi, lt[r:r + 1, :], f)
        fine.append(f)
    v0 = fine[0]
    i0 = jnp.zeros((1, tm), jnp.int32)
    for k in range(1, EXPERTS_PER_GROUP):
        upd = fine[k] > v0
        i0 = jnp.where(upd, k, i0)
        v0 = jnp.where(upd, fine[k], v0)
    v1 = jnp.full((1, tm), -jnp.inf, F32)
    i1 = jnp.zeros((1, tm), jnp.int32)
    for k in range(EXPERTS_PER_GROUP):
        upd = jnp.where(i0 != k, jnp.where(fine[k] > v1, 1, 0), 0) == 1
        i1 = jnp.where(upd, k, i1)
        v1 = jnp.where(upd, fine[k], v1)
    e1w = jnp.exp(v1 - v0)
    gate0 = pg * (1.0 / (1.0 + e1w))
    gate1 = pg * (e1w / (1.0 + e1w))
    e0 = gsel * EXPERTS_PER_GROUP + i0
    e1 = gsel * EXPERTS_PER_GROUP + i1

    eidx = lax.broadcasted_iota(jnp.int32, (N_EXPERTS, tm), 0)
    oh0 = jnp.where(eidx == e0, jnp.float32(1), jnp.float32(0))
    oh1 = jnp.where(eidx == e1, jnp.float32(1), jnp.float32(0))
    ta = lax.broadcasted_iota(jnp.int32, (tm, tm), 0)
    tb = lax.broadcasted_iota(jnp.int32, (tm, tm), 1)
    before = jnp.where(ta < tb, jnp.float32(1), jnp.float32(0)).astype(BF16)
    pre0 = _dot(oh0.astype(BF16), before)
    pre1 = _dot(oh1.astype(BF16), before)
    run = run_ref[:, 0:1]
    tot0 = jnp.sum(oh0, axis=1, keepdims=True)
    tot1 = jnp.sum(oh1, axis=1, keepdims=True)
    rank0 = jnp.sum(oh0 * (run + pre0), axis=0, keepdims=True)
    rank1 = jnp.sum(oh1 * (run + tot0 + pre1), axis=0, keepdims=True)
    new_run = jnp.broadcast_to(run + tot0 + tot1, run_ref.shape)
    run_ref[...] = new_run
    cnt_ref[...] = new_run

    route_ref[0:1, :] = e0.astype(F32)
    route_ref[1:2, :] = e1.astype(F32)
    route_ref[2:3, :] = gate0
    route_ref[3:4, :] = gate1
    route_ref[4:5, :] = rank0
    route_ref[5:6, :] = rank1
    route_ref[6:8, :] = jnp.zeros((2, tm), F32)


def _merge_router_call(oa, ob, zg, x2d, pa, pb, wo, g, wr_t):
    t, d = x2d.shape
    tm = ROW_TM
    const = dict(pipeline_mode=pl.Buffered(1))
    return pl.pallas_call(
        _merge_router_kernel,
        grid=(t // tm,),
        in_specs=[
            pl.BlockSpec((tm, oa.shape[1]), lambda i: (i, 0)),
            pl.BlockSpec((tm, ob.shape[1]), lambda i: (i, 0)),
            pl.BlockSpec((tm, GATE_COLS), lambda i: (i, 0)),
            pl.BlockSpec((tm, d), lambda i: (i, 0)),
            pl.BlockSpec(pa.shape, lambda i: (0, 0), **const),
            pl.BlockSpec(pb.shape, lambda i: (0, 0), **const),
            pl.BlockSpec(wo.shape, lambda i: (0, 0), **const),
            pl.BlockSpec((1, d), lambda i: (0, 0)),
            pl.BlockSpec(wr_t.shape, lambda i: (0, 0, 0), **const),
        ],
        out_specs=[
            pl.BlockSpec((tm, d), lambda i: (i, 0)),
            pl.BlockSpec((tm, d), lambda i: (i, 0)),
            pl.BlockSpec((8, tm), lambda i: (0, i)),
            pl.BlockSpec((N_EXPERTS, LANES), lambda i: (0, 0)),
        ],
        out_shape=[
            jax.ShapeDtypeStruct((t, d), F32),
            jax.ShapeDtypeStruct((t, d), F32),
            jax.ShapeDtypeStruct((8, t), F32),
            jax.ShapeDtypeStruct((N_EXPERTS, LANES), F32),
        ],
        scratch_shapes=[pltpu.VMEM((N_EXPERTS, LANES), F32)],
        compiler_params=_cparams(("arbitrary",)),
        name="merge_router",
    )(oa, ob, zg, x2d, pa, pb, wo, g, wr_t)


def _row_copy(src, s_row, dst, d_row, sem):
    return pltpu.make_async_copy(src.at[pl.ds(s_row, 1)], dst.at[pl.ds(d_row, 1)], sem)


def _dispatch_kernel(dest_ref, zstart_ref, zlen_ref, h_ref, x_hbm, zero_ref, sem, zsem):
    step = pl.program_id(0)

    def issue(t, carry):
        for k in range(TOP_K):
            _row_copy(h_ref, t, x_hbm, dest_ref[0, k, t], sem.at[k]).start()
        return carry

    lax.fori_loop(0, DISPATCH_TOK, issue, 0, unroll=8)

    @pl.when(step == 0)
    def _():
        zero_ref[...] = jnp.zeros_like(zero_ref)

        def group_copy(r8):
            return pltpu.make_async_copy(zero_ref, x_hbm.at[pl.ds(pl.multiple_of(r8, 8), 8)], zsem.at[1])

        def segment(e, wait):
            start = zstart_ref[e]
            n_head = jnp.minimum(lax.bitwise_and(-start, 7), zlen_ref[e])
            n_group = lax.shift_right_logical(zlen_ref[e] - n_head, 3)

            def head(r, carry):
                cp = _row_copy(zero_ref, 0, x_hbm, 0 if wait else start + r, zsem.at[0])
                cp.wait() if wait else cp.start()
                return carry

            def group(j, carry):
                cp = group_copy(0 if wait else start + n_head + 8 * j)
                cp.wait() if wait else cp.start()
                return carry

            lax.fori_loop(0, n_head, head, 0)
            lax.fori_loop(0, n_group, group, 0)

        lax.fori_loop(0, N_EXPERTS + 1, lambda e, c: (segment(e, False), c)[1], 0)
        lax.fori_loop(0, N_EXPERTS + 1, lambda e, c: (segment(e, True), c)[1], 0)

    for k in range(TOP_K):
        pltpu.make_async_copy(h_ref, x_hbm.at[pl.ds(0, DISPATCH_TOK)], sem.at[k]).wait()


def _dispatch_call(dest_blocks, zstart, zlen, h2, n_rows):
    t, d = h2.shape
    return pl.pallas_call(
        _dispatch_kernel,
        grid=(t // DISPATCH_TOK,),
        in_specs=[
            pl.BlockSpec((1, TOP_K, DISPATCH_TOK), lambda i: (i, 0, 0), memory_space=pltpu.SMEM),
            pl.BlockSpec(memory_space=pltpu.SMEM),
            pl.BlockSpec(memory_space=pltpu.SMEM),
            pl.BlockSpec((DISPATCH_TOK, d), lambda i: (i, 0)),
        ],
        out_specs=pl.BlockSpec(memory_space=pl.ANY),
        out_shape=jax.ShapeDtypeStruct((n_rows, d), F32),
        scratch_shapes=[pltpu.VMEM((8, d), F32), pltpu.SemaphoreType.DMA((TOP_K,)),
                        pltpu.SemaphoreType.DMA((2,))],
        compiler_params=_cparams(("arbitrary",)),
        name="dispatch",
    )(dest_blocks, zstart, zlen, h2)


def _expert_kernel(blk_e_ref, blk_valid_ref, x_ref, w1_ref, w3_ref, w2_ref, y_ref):
    i = pl.program_id(0)

    @pl.when(blk_valid_ref[i] == 1)
    def _():
        x = x_ref[...].astype(BF16)
        a = _dot(x, w1_ref[0])
        b = _dot(x, w3_ref[0])
        hdn = (a * _sigmoid(a)) * b
        y_ref[...] = _dot(hdn.astype(BF16), w2_ref[0])

    @pl.when(blk_valid_ref[i] == 0)
    def _():
        y_ref[...] = jnp.zeros_like(y_ref)


def _expert_call(blk_e, blk_valid, xin, w1, w3, w2):
    n_rows, d = xin.shape
    de = w1.shape[2]
    grid_spec = pltpu.PrefetchScalarGridSpec(
        num_scalar_prefetch=2,
        grid=(n_rows // EXP_TM,),
        in_specs=[
            pl.BlockSpec((EXP_TM, d), lambda i, be, bv: (i, 0)),
            pl.BlockSpec((1, d, de), lambda i, be, bv: (be[i], 0, 0)),
            pl.BlockSpec((1, d, de), lambda i, be, bv: (be[i], 0, 0)),
            pl.BlockSpec((1, de, d), lambda i, be, bv: (be[i], 0, 0)),
        ],
        out_specs=pl.BlockSpec((EXP_TM, d), lambda i, be, bv: (i, 0)),
    )
    return pl.pallas_call(
        _expert_kernel,
        grid_spec=grid_spec,
        out_shape=jax.ShapeDtypeStruct((n_rows, d), F32),
        compiler_params=_cparams(("arbitrary",)),
        name="experts",
    )(blk_e, blk_valid, xin, w1, w3, w2)


def _combine_kernel(dest_ref, dnext_ref, y_hbm, x1_ref, gt_ref, p_ref, gple_ref, wg_ref, wp_ref, gfin_ref,
                    o_ref, ybuf, sem, *, final_norm, n_steps):
    tm = x1_ref.shape[0]
    i = pl.program_id(0)
    slot = lax.bitwise_and(i, 1)

    def gather(d_ref, s):
        def issue(t, carry):
            for k in range(TOP_K):
                _row_copy(y_hbm, d_ref[0, k, t], ybuf.at[s, k], t, sem.at[s, k]).start()
            return carry

        lax.fori_loop(0, tm, issue, 0, unroll=8)

    @pl.when(i == 0)
    def _():
        gather(dest_ref, 0)

    @pl.when(i + 1 < n_steps)
    def _():
        gather(dnext_ref, 1 - slot)

    pp = _dot(p_ref[...].astype(BF16), wp_ref[...])

    for k in range(TOP_K):
        pltpu.make_async_copy(y_hbm.at[pl.ds(0, tm)], ybuf.at[slot, k], sem.at[slot, k]).wait()
    y = ybuf[slot, 0] * gt_ref[:, 2:3] + ybuf[slot, 1] * gt_ref[:, 3:4]
    x2 = x1_ref[...] + y
    hn = _rmsnorm_val(x2, gple_ref[...], RMS_EPS)
    gate = _sigmoid(_dot(hn.astype(BF16), wg_ref[...]))
    x3 = x2 + gate * pp
    if final_norm:
        o_ref[...] = _rmsnorm_val(x3, gfin_ref[...], RMS_EPS)
    else:
        o_ref[...] = x3


def _combine_call(dest_blocks, yb, x1, gates_t, p2d, g_ple, w_gate, w_proj, g_fin, final_norm):
    t, d = x1.shape
    tm = ROW_TM
    const = dict(pipeline_mode=pl.Buffered(1))
    return pl.pallas_call(
        functools.partial(_combine_kernel, final_norm=final_norm, n_steps=t // tm),
        grid=(t // tm,),
        in_specs=[
            pl.BlockSpec((1, TOP_K, tm), lambda i: (i, 0, 0), memory_space=pltpu.SMEM),
            pl.BlockSpec((1, TOP_K, tm), lambda i: (jnp.minimum(i + 1, t // tm - 1), 0, 0),
                         memory_space=pltpu.SMEM),
            pl.BlockSpec(memory_space=pl.ANY),
            pl.BlockSpec((tm, d), lambda i: (i, 0)),
            pl.BlockSpec((tm, 8), lambda i: (i, 0)),
            pl.BlockSpec((tm, p2d.shape[1]), lambda i: (i, 0)),
            pl.BlockSpec((1, d), lambda i: (0, 0)),
            pl.BlockSpec(w_gate.shape, lambda i: (0, 0), **const),
            pl.BlockSpec(w_proj.shape, lambda i: (0, 0), **const),
            pl.BlockSpec((1, d), lambda i: (0, 0)),
        ],
        out_specs=pl.BlockSpec((tm, d), lambda i: (i, 0)),
        out_shape=jax.ShapeDtypeStruct((t, d), F32),
        scratch_shapes=[pltpu.VMEM((2, TOP_K, tm, d), F32), pltpu.SemaphoreType.DMA((2, TOP_K))],
        compiler_params=_cparams(("arbitrary",)),
        name="combine",
    )(dest_blocks, dest_blocks, yb, x1, gates_t, p2d, g_ple, w_gate, w_proj, g_fin)


def _routing_plan(route, counts_f, t):
    counts = counts_f[:, 0].astype(jnp.int32)
    pcounts = ((counts + EXP_TM - 1) // EXP_TM) * EXP_TM
    pends = jnp.cumsum(pcounts)
    pstarts = pends - pcounts
    eid = route[0:2].astype(jnp.int32)
    rank = route[4:6].astype(jnp.int32)
    onehot = eid[..., None] == jnp.arange(N_EXPERTS, dtype=jnp.int32)
    dest = jnp.sum(jnp.where(onehot, pstarts, 0), axis=-1) + rank
    n_rows = TOP_K * t + N_EXPERTS * EXP_TM
    n_blocks = n_rows // EXP_TM
    blk_row = jnp.arange(n_blocks, dtype=jnp.int32) * EXP_TM
    blk_e = jnp.sum((pends[None, :] <= blk_row[:, None]).astype(jnp.int32), axis=1)
    blk_valid = (blk_row < pends[-1]).astype(jnp.int32)
    last_e = jnp.sum((pends <= pends[-1] - 1).astype(jnp.int32))
    blk_e = jnp.minimum(blk_e, last_e)
    zstart = jnp.concatenate([pstarts + counts, pends[-1:]]).astype(jnp.int32)
    zlen = jnp.concatenate([pcounts - counts, n_rows - pends[-1:]]).astype(jnp.int32)
    return dest, blk_e, blk_valid, zstart, zlen, n_rows


def _blocked(dest, tok_per_block):
    t = dest.shape[1]
    return dest.reshape(TOP_K, t // tok_per_block, tok_per_block).transpose(1, 0, 2)


def kernel(x, p, rel_bias, norm_mix_g, w_in, w_gate, lambda_q1, lambda_k1, lambda_q2, lambda_k2, subln_g,
           w_proj_a, w_proj_b, w_out, norm_ffn_g, w_coarse, w_fine, w1, w3, w2, norm_ple_g, w_ple_gate,
           w_ple_proj, final_norm_g):
    batch, seq, d = x.shape
    depth = w_in.shape[0]
    t = batch * seq
    assert seq % PERM_BLK == 0 and t % min(PROJ_TM, t) == 0
    assert all(seq // dil >= BLK and win // dil == BLK for win, dil in DILATED_GROUPS)

    nq = seq // ATT_BLK
    bias_a = _bias_tiles_call(rel_bias, n_heads=A_HEADS, n_off=nq, tq=ATT_BLK, tk=ATT_BLK, off_mult=ATT_BLK,
                              off_add=0, dil=1, max_rel=seq, head0=0, name="bias_diff",
                              key_major=True, mult=A_HEAD_DIM ** 0.5)
    bias_b = jnp.concatenate([
        _bias_tiles_call(rel_bias, n_heads=B_HEADS, n_off=1, tq=2 * BLK, tk=BLK, off_mult=0, off_add=BLK,
                         dil=dil, max_rel=win // dil, head0=A_HEADS + gi * B_HEADS, name="bias_dil%d" % gi,
                         key_major=True, mult=B_HEAD_DIM ** 0.5).reshape(1, B_HEADS, 2 * BLK, BLK)
        for gi, (win, dil) in enumerate(DILATED_GROUPS)], axis=0)

    x2d = x.reshape(t, d)
    for layer in range(depth):
        lam_init = 0.8 - 0.6 * math.exp(-0.3 * layer)
        lam = _lam_call(lambda_q1[layer:layer + 1], lambda_k1[layer:layer + 1],
                        lambda_q2[layer:layer + 1], lambda_k2[layer:layer + 1], lam_init)
        w_cat = jnp.concatenate([w_gate[layer], w_in[layer]], axis=1).astype(BF16)
        zg = _inproj_call(x2d, norm_mix_g[layer:layer + 1], w_cat)
        oa = _diff_attn_call(zg, lam, bias_a, subln_g[layer:layer + 1], batch=batch, seq=seq,
                             out_scale=1.0 - lam_init)
        ob = _dilated_call(zg, bias_b, batch=batch, seq=seq)

        wr_t = jnp.concatenate([
            w_coarse[layer].T,
            w_fine[layer].transpose(0, 2, 1).reshape(N_EXPERTS, d),
            jnp.zeros((LANES - N_EXPERT_GROUPS - N_EXPERTS, d), F32)], axis=0)
        wr_hi = wr_t.astype(BF16)
        wr_t = jnp.stack([wr_hi, (wr_t - wr_hi.astype(F32)).astype(BF16)], axis=0)
        x1, h2, route, counts = _merge_router_call(
            oa, ob, zg, x2d, w_proj_a[layer].astype(BF16), w_proj_b[layer].astype(BF16),
            w_out[layer].astype(BF16), norm_ffn_g[layer:layer + 1], wr_t)

        dest, blk_e, blk_valid, zstart, zlen, n_rows = _routing_plan(route, counts, t)
        xin = _dispatch_call(_blocked(dest, DISPATCH_TOK), zstart, zlen, h2, n_rows)
        yb = _expert_call(blk_e, blk_valid, xin, w1[layer].astype(BF16), w3[layer].astype(BF16),
                          w2[layer].astype(BF16))
        x2d = _combine_call(_blocked(dest, ROW_TM), yb, x1, route.T, p[layer].reshape(t, -1),
                            norm_ple_g[layer:layer + 1], w_ple_gate[layer].astype(BF16),
                            w_ple_proj[layer].astype(BF16), final_norm_g.reshape(1, d),
                            final_norm=layer == depth - 1)
    return x2d.reshape(batch, seq, d)
```

```python
import functools
import math

import jax
import jax.numpy as jnp
from jax import lax
from jax.experimental import pallas as pl
from jax.experimental.pallas import tpu as pltpu

F32 = jnp.float32
BF16 = jnp.bfloat16

BLK = 128
NEG_INF = -1e30
RMS_EPS = 1e-6
SUBLN_EPS = 1e-5
N_BUCKETS = 32
MAX_DISTANCE = 2048
A_HEADS = 8
A_HEAD_DIM = 128
DILATED_GROUPS = ((128, 1), (512, 4), (2048, 16))
N_DGROUPS = 3
B_HEADS = 8
B_HEAD_DIM = 64
N_EXPERT_GROUPS = 4
EXPERTS_PER_GROUP = 8
N_EXPERTS = 32
TOP_K = 2

LANES = 128
VMEM_LIMIT = 56 * 1024 * 1024
ATT_BLK = 256
ATT_HEADS = 2
PERM_BLK = 256
GATE_COLS = 4096
PROJ_TN = 512
PROJ_TM = 2048
ROW_TM = 256
EXP_TM = 256
DISPATCH_TOK = 512
NORM_CHUNK = 128
CAST_SLAB_BYTES = 1 << 20


def _t5_thresholds():
    max_exact = N_BUCKETS // 2
    out = []
    for k in range(1, N_BUCKETS - max_exact):
        out.append(int(math.ceil(max_exact * (MAX_DISTANCE / max_exact) ** (k / (N_BUCKETS - max_exact)))))
    return tuple(out)


T5_THRESHOLDS = _t5_thresholds()


def _cparams(sem, vmem=VMEM_LIMIT):
    return pltpu.CompilerParams(dimension_semantics=sem, vmem_limit_bytes=vmem)


def _mult(x, m):
    return x if isinstance(x, int) else pl.multiple_of(x, m)


def _sigmoid(x):
    return 0.5 * jnp.tanh(0.5 * x) + 0.5


def _dot(a, b, **kw):
    return jnp.dot(a, b, preferred_element_type=F32, **kw)


def _dot_nt(a, b, **kw):
    return lax.dot_general(a, b, (((1,), (1,)), ((), ())), preferred_element_type=F32, **kw)


def _lam_kernel(q1_ref, k1_ref, q2_ref, k2_ref, o_ref, *, lam_init):
    s1 = jnp.sum(q1_ref[...] * k1_ref[...], axis=-1, keepdims=True)
    s2 = jnp.sum(q2_ref[...] * k2_ref[...], axis=-1, keepdims=True)
    o_ref[...] = jnp.exp(s1) - jnp.exp(s2) + lam_init


def _lam_call(lq1, lk1, lq2, lk2, lam_init):
    return pl.pallas_call(
        functools.partial(_lam_kernel, lam_init=lam_init),
        out_shape=jax.ShapeDtypeStruct((1, 1), F32),
        name="lam",
    )(lq1, lk1, lq2, lk2)


def _bias_tile_kernel(tab_ref, o_ref, *, tq, tk, off_mult, off_add, dil, max_rel, head0, key_major, mult):
    h = pl.program_id(0)
    n = pl.program_id(1)
    i = lax.broadcasted_iota(jnp.int32, (tq, tk), 1 if key_major else 0)
    j = lax.broadcasted_iota(jnp.int32, (tq, tk), 0 if key_major else 1)
    rel = i - j + (n * off_mult + off_add)
    dist = rel * dil
    large = jnp.full((tq, tk), N_BUCKETS // 2, jnp.int32)
    for thr in T5_THRESHOLDS:
        large = large + jnp.where(dist >= thr, 1, 0)
    bucket = jnp.where(dist < N_BUCKETS // 2, dist, large)
    acc = jnp.zeros((tq, tk), F32)
    for b in range(N_BUCKETS):
        acc = jnp.where(bucket == b, tab_ref[b, head0 + h], acc)
    valid = jnp.where(rel >= 0, jnp.where(rel <= max_rel, 1, 0), 0)
    o_ref[0, 0] = jnp.where(valid == 1, acc * mult, NEG_INF)


def _bias_tiles_call(rel_bias, *, n_heads, n_off, tq, tk, off_mult, off_add, dil, max_rel, head0, name,
                     key_major=False, mult=1.0):
    kern = functools.partial(_bias_tile_kernel, tq=tq, tk=tk, off_mult=off_mult, off_add=off_add,
                             dil=dil, max_rel=max_rel, head0=head0, key_major=key_major, mult=mult)
    return pl.pallas_call(
        kern,
        grid=(n_heads, n_off),
        in_specs=[pl.BlockSpec(memory_space=pltpu.SMEM)],
        out_specs=pl.BlockSpec((1, 1, tq, tk), lambda h, n: (h, n, 0, 0)),
        out_shape=jax.ShapeDtypeStruct((n_heads, n_off, tq, tk), F32),
        compiler_params=_cparams(("parallel", "parallel")),
        name=name,
    )(rel_bias)


def _rmsnorm_rows(x_ref, g_ref, out_ref, eps):
    rows = x_ref.shape[0]
    g = g_ref[...]

    def body(c, carry):
        r0 = pl.multiple_of(c * NORM_CHUNK, NORM_CHUNK)
        x = x_ref[pl.ds(r0, NORM_CHUNK), :]
        ms = jnp.mean(x * x, axis=-1, keepdims=True)
        out_ref[pl.ds(r0, NORM_CHUNK), :] = ((x * lax.rsqrt(ms + eps)) * g).astype(out_ref.dtype)
        return carry

    lax.fori_loop(0, rows // NORM_CHUNK, body, 0)


def _rmsnorm_val(x, g, eps):
    ms = jnp.mean(x * x, axis=-1, keepdims=True)
    return (x * lax.rsqrt(ms + eps)) * g


def _inproj_kernel(x_ref, g_ref, w_ref, o_ref, h_ref, *, n_gate_blocks):
    j = pl.program_id(1)

    @pl.when(j == 0)
    def _():
        _rmsnorm_rows(x_ref, g_ref, h_ref, RMS_EPS)

    acc = _dot(h_ref[...], w_ref[...])

    @pl.when(j < n_gate_blocks)
    def _():
        o_ref[...] = _sigmoid(acc).astype(o_ref.dtype)

    @pl.when(j >= n_gate_blocks)
    def _():
        o_ref[...] = acc.astype(o_ref.dtype)


def _inproj_call(x2d, g, w_cat):
    t, d = x2d.shape
    n = w_cat.shape[1]
    tm = min(PROJ_TM, t)
    return pl.pallas_call(
        functools.partial(_inproj_kernel, n_gate_blocks=GATE_COLS // PROJ_TN),
        grid=(t // tm, n // PROJ_TN),
        in_specs=[
            pl.BlockSpec((tm, d), lambda i, j: (i, 0)),
            pl.BlockSpec((1, d), lambda i, j: (0, 0)),
            pl.BlockSpec((d, PROJ_TN), lambda i, j: (0, j)),
        ],
        out_specs=pl.BlockSpec((tm, PROJ_TN), lambda i, j: (i, j)),
        out_shape=jax.ShapeDtypeStruct((t, n), BF16),
        scratch_shapes=[pltpu.VMEM((tm, d), BF16)],
        compiler_params=_cparams(("parallel", "arbitrary")),
        name="inproj",
    )(x2d, g, w_cat)


def _diff_attn_kernel(lam_ref, q_ref, k_ref, v_ref, bias_ref, g_ref, *rest, out_scale, n_blk, n_cast):
    cast_in = rest[:n_cast]
    o_ref = rest[n_cast]
    cast_out = rest[n_cast + 1:2 * n_cast + 1]
    vt_ref, tbuf_a, tbuf_b, acc = rest[2 * n_cast + 1:]
    for src, dst in zip(cast_in, cast_out):
        dst[...] = src[...].astype(dst.dtype)
    qi = pl.program_id(2)
    width = 2 * A_HEAD_DIM
    n_map = 2 * ATT_HEADS
    c = (A_HEAD_DIM ** -0.5) * math.log2(math.e)

    @pl.when(qi == 0)
    def _():
        def transpose_block(b, carry):
            r0 = pl.multiple_of(b * ATT_BLK, ATT_BLK)
            for hp in range(ATT_HEADS):
                vb = v_ref[pl.ds(r0, ATT_BLK), hp * width:(hp + 1) * width]
                vt_ref[hp, b] = vb.astype(F32).T.astype(BF16)
            return carry

        lax.fori_loop(0, n_blk, transpose_block, 0)

    qs = [q_ref[:, mi * A_HEAD_DIM:(mi + 1) * A_HEAD_DIM] for mi in range(n_map)]
    acc[...] = jnp.zeros_like(acc)

    def scores(ki, dst):
        k0 = pl.multiple_of(ki * ATT_BLK, ATT_BLK)
        for mi in range(n_map):
            kb = k_ref[pl.ds(k0, ATT_BLK), mi * A_HEAD_DIM:(mi + 1) * A_HEAD_DIM]
            dst[mi] = _dot_nt(kb, qs[mi]) + bias_ref[mi // 2, qi - ki]

    def softmax(t, m, l):
        m_new = jnp.maximum(m, jnp.max(t, axis=0, keepdims=True))
        alpha = jnp.exp2((m - m_new) * c)
        p = jnp.exp2((t - m_new) * c)
        return m_new, alpha * l + jnp.sum(p, axis=0, keepdims=True), alpha, p.astype(BF16)

    scores(0, tbuf_a)

    def step(ki, carry, cur, nxt):
        ms, ls = carry
        stats = [softmax(cur[mi], ms[mi], ls[mi]) for mi in range(n_map)]
        scores(jnp.minimum(ki + 1, qi), nxt)
        for mi in range(n_map):
            acc[mi] = acc[mi] * stats[mi][2] + _dot(vt_ref[mi // 2, ki], stats[mi][3])
        return tuple(st[0] for st in stats), tuple(st[1] for st in stats)

    def body(ki, carry):
        return lax.cond(lax.bitwise_and(ki, 1) == 0,
                        lambda cr: step(ki, cr, tbuf_a, tbuf_b),
                        lambda cr: step(ki, cr, tbuf_b, tbuf_a), carry)

    minf = (jnp.full((1, ATT_BLK), -jnp.inf, F32),) * n_map
    zero = (jnp.zeros((1, ATT_BLK), F32),) * n_map
    _, ls = lax.fori_loop(0, qi + 1, body, (minf, zero))
    for hp in range(ATT_HEADS):
        w = acc[2 * hp] / ls[2 * hp] - lam_ref[0, 0] * (acc[2 * hp + 1] / ls[2 * hp + 1])
        ms = jnp.mean(w * w, axis=0, keepdims=True)
        y = ((w * lax.rsqrt(ms + SUBLN_EPS)) * g_ref[...]) * out_scale
        o_ref[:, hp * width:(hp + 1) * width] = y.T.astype(o_ref.dtype)


def _diff_attn_call(zg, lam, bias_tiles, subln_g, passengers, *, batch, seq, out_scale):
    t = batch * seq
    nq = seq // ATT_BLK
    width = 2 * A_HEAD_DIM
    blk_w = ATT_HEADS * width
    q_col = GATE_COLS // blk_w
    k_col = q_col + A_HEADS // ATT_HEADS
    v_col = k_col + A_HEADS // ATT_HEADS
    n_steps = (A_HEADS // ATT_HEADS) * batch * nq
    riders = [w for w in passengers
              if w.shape[0] % (n_steps * 16) == 0 and w.size * 4 // n_steps <= CAST_SLAB_BYTES]

    def slab_spec(w):
        return pl.BlockSpec((w.shape[0] // n_steps, w.shape[1]), lambda h, b, i: ((h * batch + b) * nq + i, 0))

    outs = pl.pallas_call(
        functools.partial(_diff_attn_kernel, out_scale=out_scale, n_blk=nq, n_cast=len(riders)),
        grid=(A_HEADS // ATT_HEADS, batch, nq),
        in_specs=[
            pl.BlockSpec(memory_space=pltpu.SMEM),
            pl.BlockSpec((ATT_BLK, blk_w), lambda h, b, i: (b * nq + i, q_col + h)),
            pl.BlockSpec((seq, blk_w), lambda h, b, i: (b, k_col + h)),
            pl.BlockSpec((seq, blk_w), lambda h, b, i: (b, v_col + h)),
            pl.BlockSpec((ATT_HEADS, nq, ATT_BLK, ATT_BLK), lambda h, b, i: (h, 0, 0, 0)),
            pl.BlockSpec((width, 1), lambda h, b, i: (0, 0)),
        ] + [slab_spec(w) for w in riders],
        out_specs=[pl.BlockSpec((ATT_BLK, blk_w), lambda h, b, i: (b * nq + i, h))]
        + [slab_spec(w) for w in riders],
        out_shape=[jax.ShapeDtypeStruct((t, A_HEADS * width), BF16)]
        + [jax.ShapeDtypeStruct(w.shape, BF16) for w in riders],
        scratch_shapes=[pltpu.VMEM((ATT_HEADS, nq, width, ATT_BLK), BF16),
                        pltpu.VMEM((2 * ATT_HEADS, ATT_BLK, ATT_BLK), F32),
                        pltpu.VMEM((2 * ATT_HEADS, ATT_BLK, ATT_BLK), F32),
                        pltpu.VMEM((2 * ATT_HEADS, width, ATT_BLK), F32)],
        compiler_params=_cparams(("parallel", "parallel", "arbitrary")),
        name="diff_attn",
    )(lam, zg, zg, zg, bias_tiles, subln_g.reshape(width, 1), *riders)
    cast = iter(outs[1:])
    return outs[0], [next(cast) if any(w is r for r in riders) else w.astype(BF16) for w in passengers]


def _perm_matrix(dil, inverse):
    w = PERM_BLK // dil
    shift = w.bit_length() - 1
    a = lax.broadcasted_iota(jnp.int32, (PERM_BLK, PERM_BLK), 0)
    b = lax.broadcasted_iota(jnp.int32, (PERM_BLK, PERM_BLK), 1)
    dst, src = (b, a) if inverse else (a, b)
    c = lax.shift_right_logical(dst, shift)
    ll = lax.bitwise_and(dst, w - 1)
    return jnp.where(src == ll * dil + c, jnp.float32(1), jnp.float32(0))


def _deinterleave(pairs, perm, dil, seq):
    w = PERM_BLK // dil
    stream_len = seq // dil

    def body(b8, carry):
        r0 = pl.multiple_of(b8 * PERM_BLK, PERM_BLK)
        ys = [_dot(perm, src[pl.ds(r0, PERM_BLK), :]).astype(dst.dtype) for src, dst in pairs]
        for y, (_, dst) in zip(ys, pairs):
            for c in range(dil):
                d0 = pl.multiple_of(c * stream_len + b8 * w, w)
                dst[pl.ds(d0, w), :] = y[c * w:(c + 1) * w, :]
        return carry

    lax.fori_loop(0, seq // PERM_BLK, body, 0)


def _interleave(items, dil, seq):
    w = PERM_BLK // dil
    stream_len = seq // dil

    def body(b8, carry):
        for src, _, stack, _, _ in items:
            for c in range(dil):
                s0 = pl.multiple_of(c * stream_len + b8 * w, w)
                stack[c * w:(c + 1) * w, :] = src[pl.ds(s0, w), :]
        outs = [_dot(pinv, stack[...], **kw) for _, _, stack, pinv, kw in items]
        r0 = pl.multiple_of(b8 * PERM_BLK, PERM_BLK)
        for out, (_, dst, _, _, _) in zip(outs, items):
            dst[pl.ds(r0, PERM_BLK), :] = out.astype(dst.dtype)
        return carry

    lax.fori_loop(0, seq // PERM_BLK, body, 0)


def _transpose_blocks(src_ref, vt_ref, seq):
    def body(blk, carry):
        r0 = pl.multiple_of(blk * BLK, BLK)
        vt_ref[blk] = src_ref[pl.ds(r0, BLK), :].astype(F32).T.astype(vt_ref.dtype)
        return carry

    lax.fori_loop(0, seq // BLK, body, 0)


def _dil_window(q_src, k_src, vt_ref, o_dst, lse_dst, bias_ref, ot_ref, lt_ref, rq, nk):
    scale = B_HEAD_DIM ** -0.5
    c = scale * math.log2(math.e)
    qb = rq // BLK if isinstance(rq, int) else lax.shift_right_logical(rq, BLK.bit_length() - 1)
    rk = rq - (nk - BLK)
    tiles = []
    for hh in range(B_HEADS):
        cols = slice(hh * B_HEAD_DIM, (hh + 1) * B_HEAD_DIM)
        qh = q_src[pl.ds(rq, BLK), cols]
        kh = k_src[pl.ds(rk, nk), cols]
        tiles.append(_dot_nt(kh, qh) + bias_ref[0, hh, 2 * BLK - nk:, :])
    for hh in range(B_HEADS):
        cols = slice(hh * B_HEAD_DIM, (hh + 1) * B_HEAD_DIM)
        t = tiles[hh]
        m = jnp.max(t, axis=0, keepdims=True)
        p = jnp.exp2((t - m) * c)
        den = jnp.sum(p, axis=0, keepdims=True)
        if nk == BLK:
            vth = vt_ref[qb, cols, :]
        else:
            vth = jnp.concatenate([vt_ref[qb - 1, cols, :], vt_ref[qb, cols, :]], axis=1)
        ot_ref[cols, :] = _dot(vth, p.astype(BF16)) / den
        lt_ref[hh:hh + 1, :] = m * scale + jnp.log(den)
    o_dst[pl.ds(rq, BLK), :] = ot_ref[...].T.astype(o_dst.dtype)
    lse_dst[pl.ds(rq, BLK), :] = lt_ref[...].T


def _dil_streams(q_src, k_src, vt_ref, o_dst, lse_dst, bias_ref, ot_ref, lt_ref, dil, seq):
    stream_len = seq // dil
    nq = stream_len // BLK

    def stream(c, carry):
        base = _mult(c * stream_len, BLK)
        _dil_window(q_src, k_src, vt_ref, o_dst, lse_dst, bias_ref, ot_ref, lt_ref, base, BLK)

        def qblock(n, carry2):
            rq = pl.multiple_of(base + n * BLK, BLK)
            _dil_window(q_src, k_src, vt_ref, o_dst, lse_dst, bias_ref, ot_ref, lt_ref, rq, 2 * BLK)
            return carry2

        if nq > 1:
            lax.fori_loop(1, nq, qblock, 0)
        return carry

    if dil == 1:
        stream(0, 0)
    else:
        lax.fori_loop(0, dil, stream, 0)


def _head_expand(x, expand):
    hi = x.astype(BF16)
    lo = (x - hi.astype(F32)).astype(BF16)
    return _dot(jnp.concatenate([hi, lo], axis=1), expand)


def _dil_merge(o_src, lse_src, oacc, mrun, lrun, out_ref, first, last, seq):
    er = lax.broadcasted_iota(jnp.int32, (2 * LANES, B_HEADS * B_HEAD_DIM), 0)
    ec = lax.broadcasted_iota(jnp.int32, (2 * LANES, B_HEADS * B_HEAD_DIM), 1)
    expand = jnp.where(lax.shift_right_logical(ec, B_HEAD_DIM.bit_length() - 1) == lax.bitwise_and(er, LANES - 1),
                       jnp.float32(1), jnp.float32(0)).astype(BF16)

    def body(ch, carry):
        r0 = pl.multiple_of(ch * ROW_TM, ROW_TM)
        rows = pl.ds(r0, ROW_TM)
        lse = lse_src[rows, :]
        if first:
            mrun[rows, :] = lse
            lrun[rows, :] = jnp.ones_like(lse)
            oacc[rows, :] = o_src[rows, :].astype(F32)
            return carry
        m_old = mrun[rows, :]
        m_new = jnp.maximum(m_old, lse)
        a = jnp.exp(m_old - m_new)
        bw = jnp.exp(lse - m_new)
        l_new = lrun[rows, :] * a + bw
        if last:
            a = a / l_new
            bw = bw / l_new
        val = oacc[rows, :] * _head_expand(a, expand) + o_src[rows, :].astype(F32) * _head_expand(bw, expand)
        if last:
            out_ref[rows, :] = val.astype(out_ref.dtype)
        else:
            mrun[rows, :] = m_new
            lrun[rows, :] = l_new
            oacc[rows, :] = val
        return carry

    lax.fori_loop(0, seq // ROW_TM, body, 0)


def _dilated_kernel(q_ref, k_ref, v_ref, bias_ref, out_ref,
                    qs, ks, vs, vt, os_, lses, otok, lsetok, ostack, lstack, ot, lt, oacc, mrun, lrun,
                    *, seq):
    g = pl.program_id(1)

    @pl.when(g == 0)
    def _():
        lt[...] = jnp.zeros_like(lt)

    for gi, (_, dil) in enumerate(DILATED_GROUPS):

        @pl.when(g == gi)
        def _(gi=gi, dil=dil):
            first = gi == 0
            last = gi == N_DGROUPS - 1
            if dil == 1:
                _transpose_blocks(v_ref, vt, seq)
                _dil_streams(q_ref, k_ref, vt, otok, lsetok, bias_ref, ot, lt, 1, seq)
            else:
                perm = _perm_matrix(dil, inverse=False).astype(BF16)
                _deinterleave([(q_ref, qs), (k_ref, ks), (v_ref, vs)], perm, dil, seq)
                _transpose_blocks(vs, vt, seq)
                _dil_streams(qs, ks, vt, os_, lses, bias_ref, ot, lt, dil, seq)
                perm_inv = _perm_matrix(dil, inverse=True)
                _interleave([(os_, otok, ostack, perm_inv.astype(BF16), {}),
                             (lses, lsetok, lstack, perm_inv, dict(precision=lax.Precision.HIGHEST))],
                            dil, seq)
            _dil_merge(otok, lsetok, oacc, mrun, lrun, out_ref, first, last, seq)


def _dilated_call(zg, bias_tiles, *, batch, seq):
    width = B_HEADS * B_HEAD_DIM
    q_col = (GATE_COLS + 3 * A_HEADS * 2 * A_HEAD_DIM) // width
    k_col = q_col + N_DGROUPS
    v_col = k_col + N_DGROUPS
    return pl.pallas_call(
        functools.partial(_dilated_kernel, seq=seq),
        grid=(batch, N_DGROUPS),
        in_specs=[
            pl.BlockSpec((seq, width), lambda b, g: (b, q_col + g)),
            pl.BlockSpec((seq, width), lambda b, g: (b, k_col + g)),
            pl.BlockSpec((seq, width), lambda b, g: (b, v_col + g)),
            pl.BlockSpec((1, B_HEADS, 2 * BLK, BLK), lambda b, g: (g, 0, 0, 0)),
        ],
        out_specs=pl.BlockSpec((seq, width), lambda b, g: (b, 0)),
        out_shape=jax.ShapeDtypeStruct((batch * seq, width), BF16),
        scratch_shapes=[
            pltpu.VMEM((seq, width), BF16), pltpu.VMEM((seq, width), BF16), pltpu.VMEM((seq, width), BF16),
            pltpu.VMEM((seq // BLK, width, BLK), BF16),
            pltpu.VMEM((seq, width), BF16), pltpu.VMEM((seq, LANES), F32),
            pltpu.VMEM((seq, width), BF16), pltpu.VMEM((seq, LANES), F32),
            pltpu.VMEM((PERM_BLK, width), BF16), pltpu.VMEM((PERM_BLK, LANES), F32),
            pltpu.VMEM((width, BLK), F32), pltpu.VMEM((LANES, BLK), F32),
            pltpu.VMEM((seq, width), F32), pltpu.VMEM((seq, LANES), F32), pltpu.VMEM((seq, LANES), F32),
        ],
        compiler_params=_cparams(("parallel", "arbitrary")),
        name="dilated_attn",
    )(zg, zg, zg, bias_tiles)


def _merge_router_kernel(oa_ref, ob_ref, gate_ref, x_ref, pa_ref, pb_ref, wo_ref, g_ref, wr_ref,
                         x1_ref, h2_ref, route_ref, cnt_ref, run_ref):
    tm, d = x_ref.shape

    @pl.when(pl.program_id(0) == 0)
    def _():
        run_ref[...] = jnp.zeros_like(run_ref)

    a = _dot(oa_ref[...], pa_ref[...])
    bm = _dot(ob_ref[...], pb_ref[...])
    merged = gate_ref[:, :d].astype(F32) * a + gate_ref[:, d:].astype(F32) * bm
    x1 = x_ref[...] + _dot(merged.astype(BF16), wo_ref[...])
    x1_ref[...] = x1
    h2 = _rmsnorm_val(x1, g_ref[...], RMS_EPS)
    h2_ref[...] = h2

    h_hi = h2.astype(BF16)
    h_lo = (h2 - h_hi.astype(F32)).astype(BF16)
    lt = _dot_nt(wr_ref[0], h_hi) + (_dot_nt(wr_ref[1], h_hi) + _dot_nt(wr_ref[0], h_lo))
    coarse = [lt[i:i + 1, :] for i in range(N_EXPERT_GROUPS)]
    best = coarse[0]
    gsel = jnp.zeros((1, tm), jnp.int32)
    for i in range(1, N_EXPERT_GROUPS):
        upd = coarse[i] > best
        gsel = jnp.where(upd, i, gsel)
        best = jnp.where(upd, coarse[i], best)
    den = jnp.exp(coarse[0] - best)
    for i in range(1, N_EXPERT_GROUPS):
        den = den + jnp.exp(coarse[i] - best)
    pg = 1.0 / den

    fine = []
    for k in range(EXPERTS_PER_GROUP):
        f = lt[N_EXPERT_GROUPS + k:N_EXPERT_GROUPS + k + 1, :]
        for gi in range(1, N_EXPERT_GROUPS):
            r = N_EXPERT_GROUPS + gi * EXPERTS_PER_GROUP + k
            f = jnp.where(gsel == gi, lt[r:r + 1, :], f)
        fine.append(f)
    v0 = fine[0]
    i0 = jnp.zeros((1, tm), jnp.int32)
    for k in range(1, EXPERTS_PER_GROUP):
        upd = fine[k] > v0
        i0 = jnp.where(upd, k, i0)
        v0 = jnp.where(upd, fine[k], v0)
    v1 = jnp.full((1, tm), -jnp.inf, F32)
    i1 = jnp.zeros((1, tm), jnp.int32)
    for k in range(EXPERTS_PER_GROUP):
        upd = jnp.where(i0 != k, jnp.where(fine[k] > v1, 1, 0), 0) == 1
        i1 = jnp.where(upd, k, i1)
        v1 = jnp.where(upd, fine[k], v1)
    e1w = jnp.exp(v1 - v0)
    gate0 = pg * (1.0 / (1.0 + e1w))
    gate1 = pg * (e1w / (1.0 + e1w))
    e0 = gsel * EXPERTS_PER_GROUP + i0
    e1 = gsel * EXPERTS_PER_GROUP + i1

    eidx = lax.broadcasted_iota(jnp.int32, (N_EXPERTS, tm), 0)
    oh0 = jnp.where(eidx == e0, jnp.float32(1), jnp.float32(0))
    oh1 = jnp.where(eidx == e1, jnp.float32(1), jnp.float32(0))
    ta = lax.broadcasted_iota(jnp.int32, (tm, tm), 0)
    tb = lax.broadcasted_iota(jnp.int32, (tm, tm), 1)
    before = jnp.where(ta < tb, jnp.float32(1), jnp.float32(0)).astype(BF16)
    pre0 = _dot(oh0.astype(BF16), before)
    pre1 = _dot(oh1.astype(BF16), before)
    run = run_ref[:, 0:1]
    tot0 = jnp.sum(oh0, axis=1, keepdims=True)
    tot1 = jnp.sum(oh1, axis=1, keepdims=True)
    rank0 = jnp.sum(oh0 * (run + pre0), axis=0, keepdims=True)
    rank1 = jnp.sum(oh1 * (run + tot0 + pre1), axis=0, keepdims=True)
    new_run = jnp.broadcast_to(run + tot0 + tot1, run_ref.shape)
    run_ref[...] = new_run
    cnt_ref[...] = new_run

    route_ref[0:1, :] = e0.astype(F32)
    route_ref[1:2, :] = e1.astype(F32)
    route_ref[2:3, :] = gate0
    route_ref[3:4, :] = gate1
    route_ref[4:5, :] = rank0
    route_ref[5:6, :] = rank1
    route_ref[6:8, :] = jnp.zeros((2, tm), F32)


def _merge_router_call(oa, ob, zg, x2d, pa, pb, wo, g, wr_t):
    t, d = x2d.shape
    tm = ROW_TM
    const = dict(pipeline_mode=pl.Buffered(1))
    return pl.pallas_call(
        _merge_router_kernel,
        grid=(t // tm,),
        in_specs=[
            pl.BlockSpec((tm, oa.shape[1]), lambda i: (i, 0)),
            pl.BlockSpec((tm, ob.shape[1]), lambda i: (i, 0)),
            pl.BlockSpec((tm, GATE_COLS), lambda i: (i, 0)),
            pl.BlockSpec((tm, d), lambda i: (i, 0)),
            pl.BlockSpec(pa.shape, lambda i: (0, 0), **const),
            pl.BlockSpec(pb.shape, lambda i: (0, 0), **const),
            pl.BlockSpec(wo.shape, lambda i: (0, 0), **const),
            pl.BlockSpec((1, d), lambda i: (0, 0)),
            pl.BlockSpec(wr_t.shape, lambda i: (0, 0, 0), **const),
        ],
        out_specs=[
            pl.BlockSpec((tm, d), lambda i: (i, 0)),
            pl.BlockSpec((tm, d), lambda i: (i, 0)),
            pl.BlockSpec((8, tm), lambda i: (0, i)),
            pl.BlockSpec((N_EXPERTS, LANES), lambda i: (0, 0)),
        ],
        out_shape=[
            jax.ShapeDtypeStruct((t, d), F32),
            jax.ShapeDtypeStruct((t, d), F32),
            jax.ShapeDtypeStruct((8, t), F32),
            jax.ShapeDtypeStruct((N_EXPERTS, LANES), F32),
        ],
        scratch_shapes=[pltpu.VMEM((N_EXPERTS, LANES), F32)],
        compiler_params=_cparams(("arbitrary",)),
        name="merge_router",
    )(oa, ob, zg, x2d, pa, pb, wo, g, wr_t)


def _row_copy(src, s_row, dst, d_row, sem):
    return pltpu.make_async_copy(src.at[pl.ds(s_row, 1)], dst.at[pl.ds(d_row, 1)], sem)


def _dispatch_kernel(dest_ref, zstart_ref, zlen_ref, h_ref, x_hbm, zero_ref, sem, zsem):
    step = pl.program_id(0)

    def issue(t, carry):
        for k in range(TOP_K):
            _row_copy(h_ref, t, x_hbm, dest_ref[0, k, t], sem.at[k]).start()
        return carry

    lax.fori_loop(0, DISPATCH_TOK, issue, 0, unroll=8)

    @pl.when(step == 0)
    def _():
        zero_ref[...] = jnp.zeros_like(zero_ref)

        def group_copy(r8):
            return pltpu.make_async_copy(zero_ref, x_hbm.at[pl.ds(pl.multiple_of(r8, 8), 8)], zsem.at[1])

        def segment(e, wait):
            start = zstart_ref[e]
            n_head = jnp.minimum(lax.bitwise_and(-start, 7), zlen_ref[e])
            n_group = lax.shift_right_logical(zlen_ref[e] - n_head, 3)

            def head(r, carry):
                cp = _row_copy(zero_ref, 0, x_hbm, 0 if wait else start + r, zsem.at[0])
                cp.wait() if wait else cp.start()
                return carry

            def group(j, carry):
                cp = group_copy(0 if wait else start + n_head + 8 * j)
                cp.wait() if wait else cp.start()
                return carry

            lax.fori_loop(0, n_head, head, 0)
            lax.fori_loop(0, n_group, group, 0)

        lax.fori_loop(0, N_EXPERTS + 1, lambda e, c: (segment(e, False), c)[1], 0)
        lax.fori_loop(0, N_EXPERTS + 1, lambda e, c: (segment(e, True), c)[1], 0)

    for k in range(TOP_K):
        pltpu.make_async_copy(h_ref, x_hbm.at[pl.ds(0, DISPATCH_TOK)], sem.at[k]).wait()


def _dispatch_call(dest_blocks, zstart, zlen, h2, n_rows):
    t, d = h2.shape
    return pl.pallas_call(
        _dispatch_kernel,
        grid=(t // DISPATCH_TOK,),
        in_specs=[
            pl.BlockSpec((1, TOP_K, DISPATCH_TOK), lambda i: (i, 0, 0), memory_space=pltpu.SMEM),
            pl.BlockSpec(memory_space=pltpu.SMEM),
            pl.BlockSpec(memory_space=pltpu.SMEM),
            pl.BlockSpec((DISPATCH_TOK, d), lambda i: (i, 0)),
        ],
        out_specs=pl.BlockSpec(memory_space=pl.ANY),
        out_shape=jax.ShapeDtypeStruct((n_rows, d), F32),
        scratch_shapes=[pltpu.VMEM((8, d), F32), pltpu.SemaphoreType.DMA((TOP_K,)),
                        pltpu.SemaphoreType.DMA((2,))],
        compiler_params=_cparams(("arbitrary",)),
        name="dispatch",
    )(dest_blocks, zstart, zlen, h2)


def _expert_kernel(blk_e_ref, blk_valid_ref, x_ref, w1_ref, w3_ref, w2_ref, y_ref):
    i = pl.program_id(0)

    @pl.when(blk_valid_ref[i] == 1)
    def _():
        x = x_ref[...].astype(BF16)
        a = _dot(x, w1_ref[0])
        b = _dot(x, w3_ref[0])
        hdn = (a * _sigmoid(a)) * b
        y_ref[...] = _dot(hdn.astype(BF16), w2_ref[0])

    @pl.when(blk_valid_ref[i] == 0)
    def _():
        y_ref[...] = jnp.zeros_like(y_ref)


def _expert_call(blk_e, blk_valid, xin, w1, w3, w2):
    n_rows, d = xin.shape
    de = w1.shape[2]
    grid_spec = pltpu.PrefetchScalarGridSpec(
        num_scalar_prefetch=2,
        grid=(n_rows // EXP_TM,),
        in_specs=[
            pl.BlockSpec((EXP_TM, d), lambda i, be, bv: (i, 0)),
            pl.BlockSpec((1, d, de), lambda i, be, bv: (be[i], 0, 0)),
            pl.BlockSpec((1, d, de), lambda i, be, bv: (be[i], 0, 0)),
            pl.BlockSpec((1, de, d), lambda i, be, bv: (be[i], 0, 0)),
        ],
        out_specs=pl.BlockSpec((EXP_TM, d), lambda i, be, bv: (i, 0)),
    )
    return pl.pallas_call(
        _expert_kernel,
        grid_spec=grid_spec,
        out_shape=jax.ShapeDtypeStruct((n_rows, d), F32),
        compiler_params=_cparams(("arbitrary",)),
        name="experts",
    )(blk_e, blk_valid, xin, w1, w3, w2)


def _combine_kernel(dest_ref, dnext_ref, y_hbm, x1_ref, gt_ref, p_ref, gple_ref, wg_ref, wp_ref, gfin_ref,
                    o_ref, ybuf, sem, *, final_norm, n_steps):
    tm = x1_ref.shape[0]
    i = pl.program_id(0)
    slot = lax.bitwise_and(i, 1)

    def gather(d_ref, s):
        def issue(t, carry):
            for k in range(TOP_K):
                _row_copy(y_hbm, d_ref[0, k, t], ybuf.at[s, k], t, sem.at[s, k]).start()
            return carry

        lax.fori_loop(0, tm, issue, 0, unroll=8)

    @pl.when(i == 0)
    def _():
        gather(dest_ref, 0)

    @pl.when(i + 1 < n_steps)
    def _():
        gather(dnext_ref, 1 - slot)

    pp = _dot(p_ref[...].astype(BF16), wp_ref[...])

    for k in range(TOP_K):
        pltpu.make_async_copy(y_hbm.at[pl.ds(0, tm)], ybuf.at[slot, k], sem.at[slot, k]).wait()
    y = ybuf[slot, 0] * gt_ref[:, 2:3] + ybuf[slot, 1] * gt_ref[:, 3:4]
    x2 = x1_ref[...] + y
    hn = _rmsnorm_val(x2, gple_ref[...], RMS_EPS)
    gate = _sigmoid(_dot(hn.astype(BF16), wg_ref[...]))
    x3 = x2 + gate * pp
    if final_norm:
        o_ref[...] = _rmsnorm_val(x3, gfin_ref[...], RMS_EPS)
    else:
        o_ref[...] = x3


def _combine_call(dest_blocks, yb, x1, gates_t, p2d, g_ple, w_gate, w_proj, g_fin, final_norm):
    t, d = x1.shape
    tm = ROW_TM
    const = dict(pipeline_mode=pl.Buffered(1))
    return pl.pallas_call(
        functools.partial(_combine_kernel, final_norm=final_norm, n_steps=t // tm),
        grid=(t // tm,),
        in_specs=[
            pl.BlockSpec((1, TOP_K, tm), lambda i: (i, 0, 0), memory_space=pltpu.SMEM),
            pl.BlockSpec((1, TOP_K, tm), lambda i: (jnp.minimum(i + 1, t // tm - 1), 0, 0),
                         memory_space=pltpu.SMEM),
            pl.BlockSpec(memory_space=pl.ANY),
            pl.BlockSpec((tm, d), lambda i: (i, 0)),
            pl.BlockSpec((tm, 8), lambda i: (i, 0)),
            pl.BlockSpec((tm, p2d.shape[1]), lambda i: (i, 0)),
            pl.BlockSpec((1, d), lambda i: (0, 0)),
            pl.BlockSpec(w_gate.shape, lambda i: (0, 0), **const),
            pl.BlockSpec(w_proj.shape, lambda i: (0, 0), **const),
            pl.BlockSpec((1, d), lambda i: (0, 0)),
        ],
        out_specs=pl.BlockSpec((tm, d), lambda i: (i, 0)),
        out_shape=jax.ShapeDtypeStruct((t, d), F32),
        scratch_shapes=[pltpu.VMEM((2, TOP_K, tm, d), F32), pltpu.SemaphoreType.DMA((2, TOP_K))],
        compiler_params=_cparams(("arbitrary",)),
        name="combine",
    )(dest_blocks, dest_blocks, yb, x1, gates_t, p2d, g_ple, w_gate, w_proj, g_fin)


def _routing_plan(route, counts_f, t):
    counts = counts_f[:, 0].astype(jnp.int32)
    pcounts = ((counts + EXP_TM - 1) // EXP_TM) * EXP_TM
    pends = jnp.cumsum(pcounts)
    pstarts = pends - pcounts
    eid = route[0:2].astype(jnp.int32)
    rank = route[4:6].astype(jnp.int32)
    onehot = eid[..., None] == jnp.arange(N_EXPERTS, dtype=jnp.int32)
    dest = jnp.sum(jnp.where(onehot, pstarts, 0), axis=-1) + rank
    n_rows = TOP_K * t + N_EXPERTS * EXP_TM
    n_blocks = n_rows // EXP_TM
    blk_row = jnp.arange(n_blocks, dtype=jnp.int32) * EXP_TM
    blk_e = jnp.sum((pends[None, :] <= blk_row[:, None]).astype(jnp.int32), axis=1)
    blk_valid = (blk_row < pends[-1]).astype(jnp.int32)
    last_e = jnp.sum((pends <= pends[-1] - 1).astype(jnp.int32))
    blk_e = jnp.minimum(blk_e, last_e)
    zstart = jnp.concatenate([pstarts + counts, pends[-1:]]).astype(jnp.int32)
    zlen = jnp.concatenate([pcounts - counts, n_rows - pends[-1:]]).astype(jnp.int32)
    return dest, blk_e, blk_valid, zstart, zlen, n_rows


def _blocked(dest, tok_per_block):
    t = dest.shape[1]
    return dest.reshape(TOP_K, t // tok_per_block, tok_per_block).transpose(1, 0, 2)


def kernel(x, p, rel_bias, norm_mix_g, w_in, w_gate, lambda_q1, lambda_k1, lambda_q2, lambda_k2, subln_g,
           w_proj_a, w_proj_b, w_out, norm_ffn_g, w_coarse, w_fine, w1, w3, w2, norm_ple_g, w_ple_gate,
           w_ple_proj, final_norm_g):
    batch, seq, d = x.shape
    depth = w_in.shape[0]
    t = batch * seq
    assert seq % PERM_BLK == 0 and t % min(PROJ_TM, t) == 0
    assert all(seq // dil >= BLK and win // dil == BLK for win, dil in DILATED_GROUPS)

    nq = seq // ATT_BLK
    bias_a = _bias_tiles_call(rel_bias, n_heads=A_HEADS, n_off=nq, tq=ATT_BLK, tk=ATT_BLK, off_mult=ATT_BLK,
                              off_add=0, dil=1, max_rel=seq, head0=0, name="bias_diff",
                              key_major=True, mult=A_HEAD_DIM ** 0.5)
    bias_b = jnp.concatenate([
        _bias_tiles_call(rel_bias, n_heads=B_HEADS, n_off=1, tq=2 * BLK, tk=BLK, off_mult=0, off_add=BLK,
                         dil=dil, max_rel=win // dil, head0=A_HEADS + gi * B_HEADS, name="bias_dil%d" % gi,
                         key_major=True, mult=B_HEAD_DIM ** 0.5).reshape(1, B_HEADS, 2 * BLK, BLK)
        for gi, (win, dil) in enumerate(DILATED_GROUPS)], axis=0)

    x2d = x.reshape(t, d)
    for layer in range(depth):
        lam_init = 0.8 - 0.6 * math.exp(-0.3 * layer)
        lam = _lam_call(lambda_q1[layer:layer + 1], lambda_k1[layer:layer + 1],
                        lambda_q2[layer:layer + 1], lambda_k2[layer:layer + 1], lam_init)
        w_cat = jnp.concatenate([w_gate[layer], w_in[layer]], axis=1).astype(BF16)
        zg = _inproj_call(x2d, norm_mix_g[layer:layer + 1], w_cat)
        n_exp, _, d_exp = w1[layer].shape
        oa, (w1b, w3b, w2b) = _diff_attn_call(
            zg, lam, bias_a, subln_g[layer:layer + 1],
            [w1[layer].reshape(n_exp * d, d_exp), w3[layer].reshape(n_exp * d, d_exp),
             w2[layer].reshape(n_exp * d_exp, d)],
            batch=batch, seq=seq, out_scale=1.0 - lam_init)
        ob = _dilated_call(zg, bias_b, batch=batch, seq=seq)

        wr_t = jnp.concatenate([
            w_coarse[layer].T,
            w_fine[layer].transpose(0, 2, 1).reshape(N_EXPERTS, d),
            jnp.zeros((LANES - N_EXPERT_GROUPS - N_EXPERTS, d), F32)], axis=0)
        wr_hi = wr_t.astype(BF16)
        wr_t = jnp.stack([wr_hi, (wr_t - wr_hi.astype(F32)).astype(BF16)], axis=0)
        x1, h2, route, counts = _merge_router_call(
            oa, ob, zg, x2d, w_proj_a[layer].astype(BF16), w_proj_b[layer].astype(BF16),
            w_out[layer].astype(BF16), norm_ffn_g[layer:layer + 1], wr_t)

        dest, blk_e, blk_valid, zstart, zlen, n_rows = _routing_plan(route, counts, t)
        xin = _dispatch_call(_blocked(dest, DISPATCH_TOK), zstart, zlen, h2, n_rows)
        yb = _expert_call(blk_e, blk_valid, xin, w1b.reshape(n_exp, d, d_exp), w3b.reshape(n_exp, d, d_exp),
                          w2b.reshape(n_exp, d_exp, d))
        x2d = _combine_call(_blocked(dest, ROW_TM), yb, x1, route.T, p[layer].reshape(t, -1),
                            norm_ple_g[layer:layer + 1], w_ple_gate[layer].astype(BF16),
                            w_ple_proj[layer].astype(BF16), final_norm_g.reshape(1, d),
                            final_norm=layer == depth - 1)
    return x2d.reshape(batch, seq, d)
```

```python
import functools
import math

import jax
import jax.numpy as jnp
from jax import lax
from jax.experimental import pallas as pl
from jax.experimental.pallas import tpu as pltpu

F32 = jnp.float32
BF16 = jnp.bfloat16

BLK = 128
NEG_INF = -1e30
RMS_EPS = 1e-6
SUBLN_EPS = 1e-5
N_BUCKETS = 32
MAX_DISTANCE = 2048
A_HEADS = 8
A_HEAD_DIM = 128
DILATED_GROUPS = ((128, 1), (512, 4), (2048, 16))
N_DGROUPS = 3
B_HEADS = 8
B_HEAD_DIM = 64
N_EXPERT_GROUPS = 4
EXPERTS_PER_GROUP = 8
N_EXPERTS = 32
TOP_K = 2

LANES = 128
VMEM_LIMIT = 56 * 1024 * 1024
ATT_BLK = 256
ATT_HEADS = 2
PERM_BLK = 256
GATE_COLS = 4096
PROJ_TN = 512
PROJ_TM = 2048
ROW_TM = 256
EXP_TM = 256
DISPATCH_TOK = 512
NORM_CHUNK = 128
CAST_SLAB_BYTES = 1 << 20


def _t5_thresholds():
    max_exact = N_BUCKETS // 2
    out = []
    for k in range(1, N_BUCKETS - max_exact):
        out.append(int(math.ceil(max_exact * (MAX_DISTANCE / max_exact) ** (k / (N_BUCKETS - max_exact)))))
    return tuple(out)


T5_THRESHOLDS = _t5_thresholds()


def _cparams(sem, vmem=VMEM_LIMIT):
    return pltpu.CompilerParams(dimension_semantics=sem, vmem_limit_bytes=vmem)


def _mult(x, m):
    return x if isinstance(x, int) else pl.multiple_of(x, m)


def _sigmoid(x):
    return 0.5 * jnp.tanh(0.5 * x) + 0.5


def _dot(a, b, **kw):
    return jnp.dot(a, b, preferred_element_type=F32, **kw)


def _dot_nt(a, b, **kw):
    return lax.dot_general(a, b, (((1,), (1,)), ((), ())), preferred_element_type=F32, **kw)


def _lam_kernel(q1_ref, k1_ref, q2_ref, k2_ref, o_ref, *, lam_init):
    s1 = jnp.sum(q1_ref[...] * k1_ref[...], axis=-1, keepdims=True)
    s2 = jnp.sum(q2_ref[...] * k2_ref[...], axis=-1, keepdims=True)
    o_ref[...] = jnp.exp(s1) - jnp.exp(s2) + lam_init


def _lam_call(lq1, lk1, lq2, lk2, lam_init):
    return pl.pallas_call(
        functools.partial(_lam_kernel, lam_init=lam_init),
        out_shape=jax.ShapeDtypeStruct((1, 1), F32),
        name="lam",
    )(lq1, lk1, lq2, lk2)


def _bias_tile_kernel(tab_ref, o_ref, *, tq, tk, off_mult, off_add, dil, max_rel, head0, key_major, mult):
    h = pl.program_id(0)
    n = pl.program_id(1)
    i = lax.broadcasted_iota(jnp.int32, (tq, tk), 1 if key_major else 0)
    j = lax.broadcasted_iota(jnp.int32, (tq, tk), 0 if key_major else 1)
    rel = i - j + (n * off_mult + off_add)
    dist = rel * dil
    large = jnp.full((tq, tk), N_BUCKETS // 2, jnp.int32)
    for thr in T5_THRESHOLDS:
        large = large + jnp.where(dist >= thr, 1, 0)
    bucket = jnp.where(dist < N_BUCKETS // 2, dist, large)
    acc = jnp.zeros((tq, tk), F32)
    for b in range(N_BUCKETS):
        acc = jnp.where(bucket == b, tab_ref[b, head0 + h], acc)
    valid = jnp.where(rel >= 0, jnp.where(rel <= max_rel, 1, 0), 0)
    o_ref[0, 0] = jnp.where(valid == 1, acc * mult, NEG_INF)


def _bias_tiles_call(rel_bias, *, n_heads, n_off, tq, tk, off_mult, off_add, dil, max_rel, head0, name,
                     key_major=False, mult=1.0):
    kern = functools.partial(_bias_tile_kernel, tq=tq, tk=tk, off_mult=off_mult, off_add=off_add,
                             dil=dil, max_rel=max_rel, head0=head0, key_major=key_major, mult=mult)
    return pl.pallas_call(
        kern,
        grid=(n_heads, n_off),
        in_specs=[pl.BlockSpec(memory_space=pltpu.SMEM)],
        out_specs=pl.BlockSpec((1, 1, tq, tk), lambda h, n: (h, n, 0, 0)),
        out_shape=jax.ShapeDtypeStruct((n_heads, n_off, tq, tk), F32),
        compiler_params=_cparams(("parallel", "parallel")),
        name=name,
    )(rel_bias)


def _rmsnorm_rows(x_ref, g_ref, out_ref, eps):
    rows = x_ref.shape[0]
    g = g_ref[...]

    def body(c, carry):
        r0 = pl.multiple_of(c * NORM_CHUNK, NORM_CHUNK)
        x = x_ref[pl.ds(r0, NORM_CHUNK), :]
        ms = jnp.mean(x * x, axis=-1, keepdims=True)
        out_ref[pl.ds(r0, NORM_CHUNK), :] = ((x * lax.rsqrt(ms + eps)) * g).astype(out_ref.dtype)
        return carry

    lax.fori_loop(0, rows // NORM_CHUNK, body, 0)


def _rmsnorm_val(x, g, eps):
    ms = jnp.mean(x * x, axis=-1, keepdims=True)
    return (x * lax.rsqrt(ms + eps)) * g


def _inproj_kernel(x_ref, g_ref, w_ref, o_ref, h_ref, *, n_gate_blocks):
    j = pl.program_id(1)

    @pl.when(j == 0)
    def _():
        _rmsnorm_rows(x_ref, g_ref, h_ref, RMS_EPS)

    acc = _dot(h_ref[...], w_ref[...])

    @pl.when(j < n_gate_blocks)
    def _():
        o_ref[...] = _sigmoid(acc).astype(o_ref.dtype)

    @pl.when(j >= n_gate_blocks)
    def _():
        o_ref[...] = acc.astype(o_ref.dtype)


def _inproj_call(x2d, g, w_cat):
    t, d = x2d.shape
    n = w_cat.shape[1]
    tm = min(PROJ_TM, t)
    return pl.pallas_call(
        functools.partial(_inproj_kernel, n_gate_blocks=GATE_COLS // PROJ_TN),
        grid=(t // tm, n // PROJ_TN),
        in_specs=[
            pl.BlockSpec((tm, d), lambda i, j: (i, 0)),
            pl.BlockSpec((1, d), lambda i, j: (0, 0)),
            pl.BlockSpec((d, PROJ_TN), lambda i, j: (0, j)),
        ],
        out_specs=pl.BlockSpec((tm, PROJ_TN), lambda i, j: (i, j)),
        out_shape=jax.ShapeDtypeStruct((t, n), BF16),
        scratch_shapes=[pltpu.VMEM((tm, d), BF16)],
        compiler_params=_cparams(("parallel", "arbitrary")),
        name="inproj",
    )(x2d, g, w_cat)


def _diff_attn_kernel(lam_ref, q_ref, k_ref, v_ref, bias_ref, g_ref, *rest, out_scale, n_blk, n_cast):
    cast_in = rest[:n_cast]
    o_ref = rest[n_cast]
    cast_out = rest[n_cast + 1:2 * n_cast + 1]
    vt_ref, tbuf_a, tbuf_b, acc = rest[2 * n_cast + 1:]
    for src, dst in zip(cast_in, cast_out):
        dst[...] = src[...].astype(dst.dtype)
    qi = pl.program_id(2)
    width = 2 * A_HEAD_DIM
    n_map = 2 * ATT_HEADS
    c = (A_HEAD_DIM ** -0.5) * math.log2(math.e)

    @pl.when(qi == 0)
    def _():
        def transpose_block(b, carry):
            r0 = pl.multiple_of(b * ATT_BLK, ATT_BLK)
            for hp in range(ATT_HEADS):
                vb = v_ref[pl.ds(r0, ATT_BLK), hp * width:(hp + 1) * width]
                vt_ref[hp, b] = vb.astype(F32).T.astype(BF16)
            return carry

        lax.fori_loop(0, n_blk, transpose_block, 0)

    qs = [q_ref[:, mi * A_HEAD_DIM:(mi + 1) * A_HEAD_DIM] for mi in range(n_map)]
    acc[...] = jnp.zeros_like(acc)

    def scores(ki, dst):
        k0 = pl.multiple_of(ki * ATT_BLK, ATT_BLK)
        for mi in range(n_map):
            kb = k_ref[pl.ds(k0, ATT_BLK), mi * A_HEAD_DIM:(mi + 1) * A_HEAD_DIM]
            dst[mi] = _dot_nt(kb, qs[mi]) + bias_ref[mi // 2, qi - ki]

    def softmax(t, m, l):
        m_new = jnp.maximum(m, jnp.max(t, axis=0, keepdims=True))
        alpha = jnp.exp2((m - m_new) * c)
        p = jnp.exp2((t - m_new) * c)
        return m_new, alpha * l + jnp.sum(p, axis=0, keepdims=True), alpha, p.astype(BF16)

    scores(0, tbuf_a)

    def step(ki, carry, cur, nxt):
        ms, ls = carry
        stats = [softmax(cur[mi], ms[mi], ls[mi]) for mi in range(n_map)]
        scores(jnp.minimum(ki + 1, qi), nxt)
        for mi in range(n_map):
            acc[mi] = acc[mi] * stats[mi][2] + _dot(vt_ref[mi // 2, ki], stats[mi][3])
        return tuple(st[0] for st in stats), tuple(st[1] for st in stats)

    def body(ki, carry):
        return lax.cond(lax.bitwise_and(ki, 1) == 0,
                        lambda cr: step(ki, cr, tbuf_a, tbuf_b),
                        lambda cr: step(ki, cr, tbuf_b, tbuf_a), carry)

    minf = (jnp.full((1, ATT_BLK), -jnp.inf, F32),) * n_map
    zero = (jnp.zeros((1, ATT_BLK), F32),) * n_map
    _, ls = lax.fori_loop(0, qi + 1, body, (minf, zero))
    for hp in range(ATT_HEADS):
        w = acc[2 * hp] / ls[2 * hp] - lam_ref[0, 0] * (acc[2 * hp + 1] / ls[2 * hp + 1])
        ms = jnp.mean(w * w, axis=0, keepdims=True)
        y = ((w * lax.rsqrt(ms + SUBLN_EPS)) * g_ref[...]) * out_scale
        o_ref[:, hp * width:(hp + 1) * width] = y.T.astype(o_ref.dtype)


def _diff_attn_call(zg, lam, bias_tiles, subln_g, passengers, *, batch, seq, out_scale):
    t = batch * seq
    nq = seq // ATT_BLK
    width = 2 * A_HEAD_DIM
    blk_w = ATT_HEADS * width
    q_col = GATE_COLS // blk_w
    k_col = q_col + A_HEADS // ATT_HEADS
    v_col = k_col + A_HEADS // ATT_HEADS
    n_steps = (A_HEADS // ATT_HEADS) * batch * nq
    riders = [w for w in passengers
              if w.shape[0] % (n_steps * 16) == 0 and w.size * 4 // n_steps <= CAST_SLAB_BYTES]

    def slab_spec(w):
        return pl.BlockSpec((w.shape[0] // n_steps, w.shape[1]), lambda h, b, i: ((h * batch + b) * nq + i, 0))

    outs = pl.pallas_call(
        functools.partial(_diff_attn_kernel, out_scale=out_scale, n_blk=nq, n_cast=len(riders)),
        grid=(A_HEADS // ATT_HEADS, batch, nq),
        in_specs=[
            pl.BlockSpec(memory_space=pltpu.SMEM),
            pl.BlockSpec((ATT_BLK, blk_w), lambda h, b, i: (b * nq + i, q_col + h)),
            pl.BlockSpec((seq, blk_w), lambda h, b, i: (b, k_col + h)),
            pl.BlockSpec((seq, blk_w), lambda h, b, i: (b, v_col + h)),
            pl.BlockSpec((ATT_HEADS, nq, ATT_BLK, ATT_BLK), lambda h, b, i: (h, 0, 0, 0)),
            pl.BlockSpec((width, 1), lambda h, b, i: (0, 0)),
        ] + [slab_spec(w) for w in riders],
        out_specs=[pl.BlockSpec((ATT_BLK, blk_w), lambda h, b, i: (b * nq + i, h))]
        + [slab_spec(w) for w in riders],
        out_shape=[jax.ShapeDtypeStruct((t, A_HEADS * width), BF16)]
        + [jax.ShapeDtypeStruct(w.shape, BF16) for w in riders],
        scratch_shapes=[pltpu.VMEM((ATT_HEADS, nq, width, ATT_BLK), BF16),
                        pltpu.VMEM((2 * ATT_HEADS, ATT_BLK, ATT_BLK), F32),
                        pltpu.VMEM((2 * ATT_HEADS, ATT_BLK, ATT_BLK), F32),
                        pltpu.VMEM((2 * ATT_HEADS, width, ATT_BLK), F32)],
        compiler_params=_cparams(("parallel", "parallel", "arbitrary")),
        name="diff_attn",
    )(lam, zg, zg, zg, bias_tiles, subln_g.reshape(width, 1), *riders)
    cast = iter(outs[1:])
    return outs[0], [next(cast) if any(w is r for r in riders) else w.astype(BF16) for w in passengers]


def _perm_matrix(dil, inverse):
    w = PERM_BLK // dil
    shift = w.bit_length() - 1
    a = lax.broadcasted_iota(jnp.int32, (PERM_BLK, PERM_BLK), 0)
    b = lax.broadcasted_iota(jnp.int32, (PERM_BLK, PERM_BLK), 1)
    dst, src = (b, a) if inverse else (a, b)
    c = lax.shift_right_logical(dst, shift)
    ll = lax.bitwise_and(dst, w - 1)
    return jnp.where(src == ll * dil + c, jnp.float32(1), jnp.float32(0))


def _deinterleave(pairs, perm, dil, seq):
    w = PERM_BLK // dil
    stream_len = seq // dil

    def body(b8, carry):
        r0 = pl.multiple_of(b8 * PERM_BLK, PERM_BLK)
        ys = [_dot(perm, src[pl.ds(r0, PERM_BLK), :]).astype(dst.dtype) for src, dst in pairs]
        for y, (_, dst) in zip(ys, pairs):
            for c in range(dil):
                d0 = pl.multiple_of(c * stream_len + b8 * w, w)
                dst[pl.ds(d0, w), :] = y[c * w:(c + 1) * w, :]
        return carry

    lax.fori_loop(0, seq // PERM_BLK, body, 0)


def _interleave(items, dil, seq):
    w = PERM_BLK // dil
    stream_len = seq // dil

    def body(b8, carry):
        for src, _, stack, _, _ in items:
            for c in range(dil):
                s0 = pl.multiple_of(c * stream_len + b8 * w, w)
                stack[c * w:(c + 1) * w, :] = src[pl.ds(s0, w), :]
        outs = [_dot(pinv, stack[...], **kw) for _, _, stack, pinv, kw in items]
        r0 = pl.multiple_of(b8 * PERM_BLK, PERM_BLK)
        for out, (_, dst, _, _, _) in zip(outs, items):
            dst[pl.ds(r0, PERM_BLK), :] = out.astype(dst.dtype)
        return carry

    lax.fori_loop(0, seq // PERM_BLK, body, 0)


def _transpose_blocks(src_ref, vt_ref, seq):
    def body(blk, carry):
        r0 = pl.multiple_of(blk * BLK, BLK)
        vt_ref[blk] = src_ref[pl.ds(r0, BLK), :].astype(F32).T.astype(vt_ref.dtype)
        return carry

    lax.fori_loop(0, seq // BLK, body, 0)


def _dil_windows(q_src, k_src, vt_ref, o_dst, lse_dst, bias_ref, ot_ref, lt_ref, windows):
    scale = B_HEAD_DIM ** -0.5
    c = scale * math.log2(math.e)
    tiles = []
    for rq, nk in windows:
        rk = rq - (nk - BLK)
        for hh in range(B_HEADS):
            cols = slice(hh * B_HEAD_DIM, (hh + 1) * B_HEAD_DIM)
            qh = q_src[pl.ds(rq, BLK), cols]
            kh = k_src[pl.ds(rk, nk), cols]
            tiles.append(_dot_nt(kh, qh) + bias_ref[0, hh, 2 * BLK - nk:, :])
    for wi, (rq, nk) in enumerate(windows):
        qb = rq // BLK if isinstance(rq, int) else lax.shift_right_logical(rq, BLK.bit_length() - 1)
        for hh in range(B_HEADS):
            cols = slice(hh * B_HEAD_DIM, (hh + 1) * B_HEAD_DIM)
            t = tiles[wi * B_HEADS + hh]
            m = jnp.max(t, axis=0, keepdims=True)
            p = jnp.exp2((t - m) * c)
            den = jnp.sum(p, axis=0, keepdims=True)
            if nk == BLK:
                vth = vt_ref[qb, cols, :]
            else:
                vth = jnp.concatenate([vt_ref[qb - 1, cols, :], vt_ref[qb, cols, :]], axis=1)
            ot_ref[wi, cols, :] = _dot(vth, p.astype(BF16)) / den
            lt_ref[wi, hh:hh + 1, :] = m * scale + jnp.log(den)
        o_dst[pl.ds(rq, BLK), :] = ot_ref[wi].T.astype(o_dst.dtype)
        lse_dst[pl.ds(rq, BLK), :] = lt_ref[wi].T


def _dil_streams(q_src, k_src, vt_ref, o_dst, lse_dst, bias_ref, ot_ref, lt_ref, dil, seq):
    stream_len = seq // dil
    nq = stream_len // BLK

    def run(windows):
        _dil_windows(q_src, k_src, vt_ref, o_dst, lse_dst, bias_ref, ot_ref, lt_ref, windows)

    if nq == 1:
        def stream_pair(c2, carry):
            base = pl.multiple_of(c2 * (2 * stream_len), BLK)
            run([(base, BLK), (pl.multiple_of(base + stream_len, BLK), BLK)])
            return carry

        lax.fori_loop(0, dil // 2, stream_pair, 0)
        return

    def stream(c, carry):
        base = _mult(c * stream_len, BLK)
        n_full = nq - 1
        if n_full % 2 == 1:
            run([(base, BLK), (_mult(base + n_full * BLK, BLK), 2 * BLK)])
            n_full -= 1
        else:
            run([(base, BLK)])

        def qpair(j, carry2):
            rq = pl.multiple_of(base + (1 + 2 * j) * BLK, BLK)
            run([(rq, 2 * BLK), (pl.multiple_of(rq + BLK, BLK), 2 * BLK)])
            return carry2

        if n_full:
            lax.fori_loop(0, n_full // 2, qpair, 0)
        return carry

    if dil == 1:
        stream(0, 0)
    else:
        lax.fori_loop(0, dil, stream, 0)


def _head_expand(x, expand):
    hi = x.astype(BF16)
    lo = (x - hi.astype(F32)).astype(BF16)
    return _dot(jnp.concatenate([hi, lo], axis=1), expand)


def _dil_merge(o_src, lse_src, oacc, mrun, lrun, out_ref, first, last, seq):
    er = lax.broadcasted_iota(jnp.int32, (2 * LANES, B_HEADS * B_HEAD_DIM), 0)
    ec = lax.broadcasted_iota(jnp.int32, (2 * LANES, B_HEADS * B_HEAD_DIM), 1)
    expand = jnp.where(lax.shift_right_logical(ec, B_HEAD_DIM.bit_length() - 1) == lax.bitwise_and(er, LANES - 1),
                       jnp.float32(1), jnp.float32(0)).astype(BF16)

    def body(ch, carry):
        r0 = pl.multiple_of(ch * ROW_TM, ROW_TM)
        rows = pl.ds(r0, ROW_TM)
        lse = lse_src[rows, :]
        if first:
            mrun[rows, :] = lse
            lrun[rows, :] = jnp.ones_like(lse)
            oacc[rows, :] = o_src[rows, :].astype(F32)
            return carry
        m_old = mrun[rows, :]
        m_new = jnp.maximum(m_old, lse)
        a = jnp.exp(m_old - m_new)
        bw = jnp.exp(lse - m_new)
        l_new = lrun[rows, :] * a + bw
        if last:
            a = a / l_new
            bw = bw / l_new
        val = oacc[rows, :] * _head_expand(a, expand) + o_src[rows, :].astype(F32) * _head_expand(bw, expand)
        if last:
            out_ref[rows, :] = val.astype(out_ref.dtype)
        else:
            mrun[rows, :] = m_new
            lrun[rows, :] = l_new
            oacc[rows, :] = val
        return carry

    lax.fori_loop(0, seq // ROW_TM, body, 0)


def _dilated_kernel(q_ref, k_ref, v_ref, bias_ref, out_ref,
                    qs, ks, vs, vt, os_, lses, otok, lsetok, ostack, lstack, ot, lt, oacc, mrun, lrun,
                    *, seq):
    g = pl.program_id(1)

    @pl.when(g == 0)
    def _():
        lt[...] = jnp.zeros_like(lt)

    for gi, (_, dil) in enumerate(DILATED_GROUPS):

        @pl.when(g == gi)
        def _(gi=gi, dil=dil):
            first = gi == 0
            last = gi == N_DGROUPS - 1
            if dil == 1:
                _transpose_blocks(v_ref, vt, seq)
                _dil_streams(q_ref, k_ref, vt, otok, lsetok, bias_ref, ot, lt, 1, seq)
            else:
                perm = _perm_matrix(dil, inverse=False).astype(BF16)
                _deinterleave([(q_ref, qs), (k_ref, ks), (v_ref, vs)], perm, dil, seq)
                _transpose_blocks(vs, vt, seq)
                _dil_streams(qs, ks, vt, os_, lses, bias_ref, ot, lt, dil, seq)
                perm_inv = _perm_matrix(dil, inverse=True)
                _interleave([(os_, otok, ostack, perm_inv.astype(BF16), {}),
                             (lses, lsetok, lstack, perm_inv, dict(precision=lax.Precision.HIGHEST))],
                            dil, seq)
            _dil_merge(otok, lsetok, oacc, mrun, lrun, out_ref, first, last, seq)


def _dilated_call(zg, bias_tiles, *, batch, seq):
    width = B_HEADS * B_HEAD_DIM
    q_col = (GATE_COLS + 3 * A_HEADS * 2 * A_HEAD_DIM) // width
    k_col = q_col + N_DGROUPS
    v_col = k_col + N_DGROUPS
    return pl.pallas_call(
        functools.partial(_dilated_kernel, seq=seq),
        grid=(batch, N_DGROUPS),
        in_specs=[
            pl.BlockSpec((seq, width), lambda b, g: (b, q_col + g)),
            pl.BlockSpec((seq, width), lambda b, g: (b, k_col + g)),
            pl.BlockSpec((seq, width), lambda b, g: (b, v_col + g)),
            pl.BlockSpec((1, B_HEADS, 2 * BLK, BLK), lambda b, g: (g, 0, 0, 0)),
        ],
        out_specs=pl.BlockSpec((seq, width), lambda b, g: (b, 0)),
        out_shape=jax.ShapeDtypeStruct((batch * seq, width), BF16),
        scratch_shapes=[
            pltpu.VMEM((seq, width), BF16), pltpu.VMEM((seq, width), BF16), pltpu.VMEM((seq, width), BF16),
            pltpu.VMEM((seq // BLK, width, BLK), BF16),
            pltpu.VMEM((seq, width), BF16), pltpu.VMEM((seq, LANES), F32),
            pltpu.VMEM((seq, width), BF16), pltpu.VMEM((seq, LANES), F32),
            pltpu.VMEM((PERM_BLK, width), BF16), pltpu.VMEM((PERM_BLK, LANES), F32),
            pltpu.VMEM((2, width, BLK), F32), pltpu.VMEM((2, LANES, BLK), F32),
            pltpu.VMEM((seq, width), F32), pltpu.VMEM((seq, LANES), F32), pltpu.VMEM((seq, LANES), F32),
        ],
        compiler_params=_cparams(("parallel", "arbitrary")),
        name="dilated_attn",
    )(zg, zg, zg, bias_tiles)


def _merge_router_kernel(oa_ref, ob_ref, gate_ref, x_ref, pa_ref, pb_ref, wo_ref, g_ref, wr_ref,
                         x1_ref, h2_ref, route_ref, cnt_ref, run_ref):
    tm, d = x_ref.shape

    @pl.when(pl.program_id(0) == 0)
    def _():
        run_ref[...] = jnp.zeros_like(run_ref)

    a = _dot(oa_ref[...], pa_ref[...])
    bm = _dot(ob_ref[...], pb_ref[...])
    merged = gate_ref[:, :d].astype(F32) * a + gate_ref[:, d:].astype(F32) * bm
    x1 = x_ref[...] + _dot(merged.astype(BF16), wo_ref[...])
    x1_ref[...] = x1
    h2 = _rmsnorm_val(x1, g_ref[...], RMS_EPS)
    h2_ref[...] = h2

    h_hi = h2.astype(BF16)
    h_lo = (h2 - h_hi.astype(F32)).astype(BF16)
    lt = (_dot(h_hi, wr_ref[0]) + (_dot(h_hi, wr_ref[1]) + _dot(h_lo, wr_ref[0]))).T
    coarse = [lt[i:i + 1, :] for i in range(N_EXPERT_GROUPS)]
    best = coarse[0]
    gsel = jnp.zeros((1, tm), jnp.int32)
    for i in range(1, N_EXPERT_GROUPS):
        upd = coarse[i] > best
        gsel = jnp.where(upd, i, gsel)
        best = jnp.where(upd, coarse[i], best)
    den = jnp.exp(coarse[0] - best)
    for i in range(1, N_EXPERT_GROUPS):
        den = den + jnp.exp(coarse[i] - best)
    pg = 1.0 / den

    fine = []
    for k in range(EXPERTS_PER_GROUP):
        f = lt[N_EXPERT_GROUPS + k:N_EXPERT_GROUPS + k + 1, :]
        for gi in range(1, N_EXPERT_GROUPS):
            r = N_EXPERT_GROUPS + gi * EXPERTS_PER_GROUP + k
            f = jnp.where(gsel == gi, lt[r:r + 1, :], f)
        fine.append(f)
    v0 = fine[0]
    i0 = jnp.zeros((1, tm), jnp.int32)
    for k in range(1, EXPERTS_PER_GROUP):
        upd = fine[k] > v0
        i0 = jnp.where(upd, k, i0)
        v0 = jnp.where(upd, fine[k], v0)
    v1 = jnp.full((1, tm), -jnp.inf, F32)
    i1 = jnp.zeros((1, tm), jnp.int32)
    for k in range(EXPERTS_PER_GROUP):
        upd = jnp.where(i0 != k, jnp.where(fine[k] > v1, 1, 0), 0) == 1
        i1 = jnp.where(upd, k, i1)
        v1 = jnp.where(upd, fine[k], v1)
    e1w = jnp.exp(v1 - v0)
    gate0 = pg * (1.0 / (1.0 + e1w))
    gate1 = pg * (e1w / (1.0 + e1w))
    e0 = gsel * EXPERTS_PER_GROUP + i0
    e1 = gsel * EXPERTS_PER_GROUP + i1

    eidx = lax.broadcasted_iota(jnp.int32, (N_EXPERTS, tm), 0)
    oh0 = jnp.where(eidx == e0, jnp.float32(1), jnp.float32(0))
    oh1 = jnp.where(eidx == e1, jnp.float32(1), jnp.float32(0))
    ta = lax.broadcasted_iota(jnp.int32, (tm, tm), 0)
    tb = lax.broadcasted_iota(jnp.int32, (tm, tm), 1)
    before = jnp.where(ta < tb, jnp.float32(1), jnp.float32(0)).astype(BF16)
    pre0 = _dot(oh0.astype(BF16), before)
    pre1 = _dot(oh1.astype(BF16), before)
    run = run_ref[:, 0:1]
    tot0 = jnp.sum(oh0, axis=1, keepdims=True)
    tot1 = jnp.sum(oh1, axis=1, keepdims=True)
    rank0 = jnp.sum(oh0 * (run + pre0), axis=0, keepdims=True)
    rank1 = jnp.sum(oh1 * (run + tot0 + pre1), axis=0, keepdims=True)
    new_run = jnp.broadcast_to(run + tot0 + tot1, run_ref.shape)
    run_ref[...] = new_run
    cnt_ref[...] = new_run

    route_ref[0:1, :] = e0.astype(F32)
    route_ref[1:2, :] = e1.astype(F32)
    route_ref[2:3, :] = gate0
    route_ref[3:4, :] = gate1
    route_ref[4:5, :] = rank0
    route_ref[5:6, :] = rank1
    route_ref[6:8, :] = jnp.zeros((2, tm), F32)


def _merge_router_call(oa, ob, zg, x2d, pa, pb, wo, g, wr_t):
    t, d = x2d.shape
    tm = ROW_TM
    const = dict(pipeline_mode=pl.Buffered(1))
    return pl.pallas_call(
        _merge_router_kernel,
        grid=(t // tm,),
        in_specs=[
            pl.BlockSpec((tm, oa.shape[1]), lambda i: (i, 0)),
            pl.BlockSpec((tm, ob.shape[1]), lambda i: (i, 0)),
            pl.BlockSpec((tm, GATE_COLS), lambda i: (i, 0)),
            pl.BlockSpec((tm, d), lambda i: (i, 0)),
            pl.BlockSpec(pa.shape, lambda i: (0, 0), **const),
            pl.BlockSpec(pb.shape, lambda i: (0, 0), **const),
            pl.BlockSpec(wo.shape, lambda i: (0, 0), **const),
            pl.BlockSpec((1, d), lambda i: (0, 0)),
            pl.BlockSpec(wr_t.shape, lambda i: (0, 0, 0), **const),
        ],
        out_specs=[
            pl.BlockSpec((tm, d), lambda i: (i, 0)),
            pl.BlockSpec((tm, d), lambda i: (i, 0)),
            pl.BlockSpec((8, tm), lambda i: (0, i)),
            pl.BlockSpec((N_EXPERTS, LANES), lambda i: (0, 0)),
        ],
        out_shape=[
            jax.ShapeDtypeStruct((t, d), F32),
            jax.ShapeDtypeStruct((t, d), F32),
            jax.ShapeDtypeStruct((8, t), F32),
            jax.ShapeDtypeStruct((N_EXPERTS, LANES), F32),
        ],
        scratch_shapes=[pltpu.VMEM((N_EXPERTS, LANES), F32)],
        compiler_params=_cparams(("arbitrary",)),
        name="merge_router",
    )(oa, ob, zg, x2d, pa, pb, wo, g, wr_t)


def _row_copy(src, s_row, dst, d_row, sem):
    return pltpu.make_async_copy(src.at[pl.ds(s_row, 1)], dst.at[pl.ds(d_row, 1)], sem)


def _dispatch_kernel(dest_ref, zstart_ref, zlen_ref, h_ref, x_hbm, zero_ref, sem, zsem):
    step = pl.program_id(0)

    def issue(t, carry):
        for k in range(TOP_K):
            _row_copy(h_ref, t, x_hbm, dest_ref[0, k, t], sem.at[k]).start()
        return carry

    lax.fori_loop(0, DISPATCH_TOK, issue, 0, unroll=8)

    @pl.when(step == 0)
    def _():
        zero_ref[...] = jnp.zeros_like(zero_ref)

        def group_copy(r8):
            return pltpu.make_async_copy(zero_ref, x_hbm.at[pl.ds(pl.multiple_of(r8, 8), 8)], zsem.at[1])

        def segment(e, wait):
            start = zstart_ref[e]
            n_head = jnp.minimum(lax.bitwise_and(-start, 7), zlen_ref[e])
            n_group = lax.shift_right_logical(zlen_ref[e] - n_head, 3)

            def head(r, carry):
                cp = _row_copy(zero_ref, 0, x_hbm, 0 if wait else start + r, zsem.at[0])
                cp.wait() if wait else cp.start()
                return carry

            def group(j, carry):
                cp = group_copy(0 if wait else start + n_head + 8 * j)
                cp.wait() if wait else cp.start()
                return carry

            lax.fori_loop(0, n_head, head, 0)
            lax.fori_loop(0, n_group, group, 0)

        lax.fori_loop(0, N_EXPERTS + 1, lambda e, c: (segment(e, False), c)[1], 0)
        lax.fori_loop(0, N_EXPERTS + 1, lambda e, c: (segment(e, True), c)[1], 0)

    for k in range(TOP_K):
        pltpu.make_async_copy(h_ref, x_hbm.at[pl.ds(0, DISPATCH_TOK)], sem.at[k]).wait()


def _dispatch_call(dest_blocks, zstart, zlen, h2, n_rows):
    t, d = h2.shape
    return pl.pallas_call(
        _dispatch_kernel,
        grid=(t // DISPATCH_TOK,),
        in_specs=[
            pl.BlockSpec((1, TOP_K, DISPATCH_TOK), lambda i: (i, 0, 0), memory_space=pltpu.SMEM),
            pl.BlockSpec(memory_space=pltpu.SMEM),
            pl.BlockSpec(memory_space=pltpu.SMEM),
            pl.BlockSpec((DISPATCH_TOK, d), lambda i: (i, 0)),
        ],
        out_specs=pl.BlockSpec(memory_space=pl.ANY),
        out_shape=jax.ShapeDtypeStruct((n_rows, d), F32),
        scratch_shapes=[pltpu.VMEM((8, d), F32), pltpu.SemaphoreType.DMA((TOP_K,)),
                        pltpu.SemaphoreType.DMA((2,))],
        compiler_params=_cparams(("arbitrary",)),
        name="dispatch",
    )(dest_blocks, zstart, zlen, h2)


def _expert_kernel(blk_e_ref, blk_valid_ref, x_ref, w1_ref, w3_ref, w2_ref, y_ref):
    i = pl.program_id(0)

    @pl.when(blk_valid_ref[i] == 1)
    def _():
        x = x_ref[...].astype(BF16)
        a = _dot(x, w1_ref[0])
        b = _dot(x, w3_ref[0])
        hdn = (a * _sigmoid(a)) * b
        y_ref[...] = _dot(hdn.astype(BF16), w2_ref[0])

    @pl.when(blk_valid_ref[i] == 0)
    def _():
        y_ref[...] = jnp.zeros_like(y_ref)


def _expert_call(blk_e, blk_valid, xin, w1, w3, w2):
    n_rows, d = xin.shape
    de = w1.shape[2]
    grid_spec = pltpu.PrefetchScalarGridSpec(
        num_scalar_prefetch=2,
        grid=(n_rows // EXP_TM,),
        in_specs=[
            pl.BlockSpec((EXP_TM, d), lambda i, be, bv: (i, 0)),
            pl.BlockSpec((1, d, de), lambda i, be, bv: (be[i], 0, 0)),
            pl.BlockSpec((1, d, de), lambda i, be, bv: (be[i], 0, 0)),
            pl.BlockSpec((1, de, d), lambda i, be, bv: (be[i], 0, 0)),
        ],
        out_specs=pl.BlockSpec((EXP_TM, d), lambda i, be, bv: (i, 0)),
    )
    return pl.pallas_call(
        _expert_kernel,
        grid_spec=grid_spec,
        out_shape=jax.ShapeDtypeStruct((n_rows, d), F32),
        compiler_params=_cparams(("arbitrary",)),
        name="experts",
    )(blk_e, blk_valid, xin, w1, w3, w2)


def _combine_kernel(dest_ref, dnext_ref, y_hbm, x1_ref, gt_ref, p_ref, gple_ref, wg_ref, wp_ref, gfin_ref,
                    o_ref, ybuf, sem, *, final_norm, n_steps):
    tm = x1_ref.shape[0]
    i = pl.program_id(0)
    slot = lax.bitwise_and(i, 1)

    def gather(d_ref, s):
        def issue(t, carry):
            for k in range(TOP_K):
                _row_copy(y_hbm, d_ref[0, k, t], ybuf.at[s, k], t, sem.at[s, k]).start()
            return carry

        lax.fori_loop(0, tm, issue, 0, unroll=8)

    @pl.when(i == 0)
    def _():
        gather(dest_ref, 0)

    @pl.when(i + 1 < n_steps)
    def _():
        gather(dnext_ref, 1 - slot)

    pp = _dot(p_ref[...].astype(BF16), wp_ref[...])

    for k in range(TOP_K):
        pltpu.make_async_copy(y_hbm.at[pl.ds(0, tm)], ybuf.at[slot, k], sem.at[slot, k]).wait()
    y = ybuf[slot, 0] * gt_ref[:, 2:3] + ybuf[slot, 1] * gt_ref[:, 3:4]
    x2 = x1_ref[...] + y
    hn = _rmsnorm_val(x2, gple_ref[...], RMS_EPS)
    gate = _sigmoid(_dot(hn.astype(BF16), wg_ref[...]))
    x3 = x2 + gate * pp
    if final_norm:
        o_ref[...] = _rmsnorm_val(x3, gfin_ref[...], RMS_EPS)
    else:
        o_ref[...] = x3


def _combine_call(dest_blocks, yb, x1, gates_t, p2d, g_ple, w_gate, w_proj, g_fin, final_norm):
    t, d = x1.shape
    tm = ROW_TM
    const = dict(pipeline_mode=pl.Buffered(1))
    return pl.pallas_call(
        functools.partial(_combine_kernel, final_norm=final_norm, n_steps=t // tm),
        grid=(t // tm,),
        in_specs=[
            pl.BlockSpec((1, TOP_K, tm), lambda i: (i, 0, 0), memory_space=pltpu.SMEM),
            pl.BlockSpec((1, TOP_K, tm), lambda i: (jnp.minimum(i + 1, t // tm - 1), 0, 0),
                         memory_space=pltpu.SMEM),
            pl.BlockSpec(memory_space=pl.ANY),
            pl.BlockSpec((tm, d), lambda i: (i, 0)),
            pl.BlockSpec((tm, 8), lambda i: (i, 0)),
            pl.BlockSpec((tm, p2d.shape[1]), lambda i: (i, 0)),
            pl.BlockSpec((1, d), lambda i: (0, 0)),
            pl.BlockSpec(w_gate.shape, lambda i: (0, 0), **const),
            pl.BlockSpec(w_proj.shape, lambda i: (0, 0), **const),
            pl.BlockSpec((1, d), lambda i: (0, 0)),
        ],
        out_specs=pl.BlockSpec((tm, d), lambda i: (i, 0)),
        out_shape=jax.ShapeDtypeStruct((t, d), F32),
        scratch_shapes=[pltpu.VMEM((2, TOP_K, tm, d), F32), pltpu.SemaphoreType.DMA((2, TOP_K))],
        compiler_params=_cparams(("arbitrary",)),
        name="combine",
    )(dest_blocks, dest_blocks, yb, x1, gates_t, p2d, g_ple, w_gate, w_proj, g_fin)


def _routing_plan(route, counts_f, t):
    counts = counts_f[:, 0].astype(jnp.int32)
    pcounts = ((counts + EXP_TM - 1) // EXP_TM) * EXP_TM
    pends = jnp.cumsum(pcounts)
    pstarts = pends - pcounts
    eid = route[0:2].astype(jnp.int32)
    rank = route[4:6].astype(jnp.int32)
    onehot = eid[..., None] == jnp.arange(N_EXPERTS, dtype=jnp.int32)
    dest = jnp.sum(jnp.where(onehot, pstarts, 0), axis=-1) + rank
    n_rows = TOP_K * t + N_EXPERTS * EXP_TM
    n_blocks = n_rows // EXP_TM
    blk_row = jnp.arange(n_blocks, dtype=jnp.int32) * EXP_TM
    blk_e = jnp.sum((pends[None, :] <= blk_row[:, None]).astype(jnp.int32), axis=1)
    blk_valid = (blk_row < pends[-1]).astype(jnp.int32)
    last_e = jnp.sum((pends <= pends[-1] - 1).astype(jnp.int32))
    blk_e = jnp.minimum(blk_e, last_e)
    zstart = jnp.concatenate([pstarts + counts, pends[-1:]]).astype(jnp.int32)
    zlen = jnp.concatenate([pcounts - counts, n_rows - pends[-1:]]).astype(jnp.int32)
    return dest, blk_e, blk_valid, zstart, zlen, n_rows


def _blocked(dest, tok_per_block):
    t = dest.shape[1]
    return dest.reshape(TOP_K, t // tok_per_block, tok_per_block).transpose(1, 0, 2)


def kernel(x, p, rel_bias, norm_mix_g, w_in, w_gate, lambda_q1, lambda_k1, lambda_q2, lambda_k2, subln_g,
           w_proj_a, w_proj_b, w_out, norm_ffn_g, w_coarse, w_fine, w1, w3, w2, norm_ple_g, w_ple_gate,
           w_ple_proj, final_norm_g):
    batch, seq, d = x.shape
    depth = w_in.shape[0]
    t = batch * seq
    assert seq % PERM_BLK == 0 and t % min(PROJ_TM, t) == 0
    assert all(seq // dil >= BLK and win // dil == BLK for win, dil in DILATED_GROUPS)

    nq = seq // ATT_BLK
    bias_a = _bias_tiles_call(rel_bias, n_heads=A_HEADS, n_off=nq, tq=ATT_BLK, tk=ATT_BLK, off_mult=ATT_BLK,
                              off_add=0, dil=1, max_rel=seq, head0=0, name="bias_diff",
                              key_major=True, mult=A_HEAD_DIM ** 0.5)
    bias_b = jnp.concatenate([
        _bias_tiles_call(rel_bias, n_heads=B_HEADS, n_off=1, tq=2 * BLK, tk=BLK, off_mult=0, off_add=BLK,
                         dil=dil, max_rel=win // dil, head0=A_HEADS + gi * B_HEADS, name="bias_dil%d" % gi,
                         key_major=True, mult=B_HEAD_DIM ** 0.5).reshape(1, B_HEADS, 2 * BLK, BLK)
        for gi, (win, dil) in enumerate(DILATED_GROUPS)], axis=0)

    x2d = x.reshape(t, d)
    for layer in range(depth):
        lam_init = 0.8 - 0.6 * math.exp(-0.3 * layer)
        lam = _lam_call(lambda_q1[layer:layer + 1], lambda_k1[layer:layer + 1],
                        lambda_q2[layer:layer + 1], lambda_k2[layer:layer + 1], lam_init)
        w_cat = jnp.concatenate([w_gate[layer], w_in[layer]], axis=1).astype(BF16)
        zg = _inproj_call(x2d, norm_mix_g[layer:layer + 1], w_cat)
        n_exp, _, d_exp = w1[layer].shape
        oa, (w1b, w3b, w2b) = _diff_attn_call(
            zg, lam, bias_a, subln_g[layer:layer + 1],
            [w1[layer].reshape(n_exp * d, d_exp), w3[layer].reshape(n_exp * d, d_exp),
             w2[layer].reshape(n_exp * d_exp, d)],
            batch=batch, seq=seq, out_scale=1.0 - lam_init)
        ob = _dilated_call(zg, bias_b, batch=batch, seq=seq)

        wr_t = jnp.concatenate([
            w_coarse[layer].T,
            w_fine[layer].transpose(0, 2, 1).reshape(N_EXPERTS, d),
            jnp.zeros((LANES - N_EXPERT_GROUPS - N_EXPERTS, d), F32)], axis=0)
        wr_t = wr_t.T
        wr_hi = wr_t.astype(BF16)
        wr_t = jnp.stack([wr_hi, (wr_t - wr_hi.astype(F32)).astype(BF16)], axis=0)
        x1, h2, route, counts = _merge_router_call(
            oa, ob, zg, x2d, w_proj_a[layer].astype(BF16), w_proj_b[layer].astype(BF16),
            w_out[layer].astype(BF16), norm_ffn_g[layer:layer + 1], wr_t)

        dest, blk_e, blk_valid, zstart, zlen, n_rows = _routing_plan(route, counts, t)
        xin = _dispatch_call(_blocked(dest, DISPATCH_TOK), zstart, zlen, h2, n_rows)
        yb = _expert_call(blk_e, blk_valid, xin, w1b.reshape(n_exp, d, d_exp), w3b.reshape(n_exp, d, d_exp),
                          w2b.reshape(n_exp, d_exp, d))
        x2d = _combine_call(_blocked(dest, ROW_TM), yb, x1, route.T, p[layer].reshape(t, -1),
                            norm_ple_g[layer:layer + 1], w_ple_gate[layer].astype(BF16),
                            w_ple_proj[layer].astype(BF16), final_norm_g.reshape(1, d),
                            final_norm=layer == depth - 1)
    return x2d.reshape(batch, seq, d)
```

```python
import functools
import math

import jax
import jax.numpy as jnp
from jax import lax
from jax.experimental import pallas as pl
from jax.experimental.pallas import tpu as pltpu

F32 = jnp.float32
BF16 = jnp.bfloat16

BLK = 128
NEG_INF = -1e30
RMS_EPS = 1e-6
SUBLN_EPS = 1e-5
N_BUCKETS = 32
MAX_DISTANCE = 2048
A_HEADS = 8
A_HEAD_DIM = 128
DILATED_GROUPS = ((128, 1), (512, 4), (2048, 16))
N_DGROUPS = 3
B_HEADS = 8
B_HEAD_DIM = 64
N_EXPERT_GROUPS = 4
EXPERTS_PER_GROUP = 8
N_EXPERTS = 32
TOP_K = 2

LANES = 128
VMEM_LIMIT = 56 * 1024 * 1024
ATT_BLK = 256
ATT_HEADS = 2
PERM_BLK = 256
GATE_COLS = 4096
PROJ_TN = 512
PROJ_TM = 2048
ROW_TM = 256
EXP_TM = 256
DISPATCH_TOK = 512
NORM_CHUNK = 128
CAST_SLAB_BYTES = 1 << 20


def _t5_thresholds():
    max_exact = N_BUCKETS // 2
    out = []
    for k in range(1, N_BUCKETS - max_exact):
        out.append(int(math.ceil(max_exact * (MAX_DISTANCE / max_exact) ** (k / (N_BUCKETS - max_exact)))))
    return tuple(out)


T5_THRESHOLDS = _t5_thresholds()


def _cparams(sem, vmem=VMEM_LIMIT):
    return pltpu.CompilerParams(dimension_semantics=sem, vmem_limit_bytes=vmem)


def _mult(x, m):
    return x if isinstance(x, int) else pl.multiple_of(x, m)


def _sigmoid(x):
    return 0.5 * jnp.tanh(0.5 * x) + 0.5


def _dot(a, b, **kw):
    return jnp.dot(a, b, preferred_element_type=F32, **kw)


def _dot_nt(a, b, **kw):
    return lax.dot_general(a, b, (((1,), (1,)), ((), ())), preferred_element_type=F32, **kw)


def _lam_kernel(q1_ref, k1_ref, q2_ref, k2_ref, o_ref, *, lam_init):
    s1 = jnp.sum(q1_ref[...] * k1_ref[...], axis=-1, keepdims=True)
    s2 = jnp.sum(q2_ref[...] * k2_ref[...], axis=-1, keepdims=True)
    o_ref[...] = jnp.exp(s1) - jnp.exp(s2) + lam_init


def _lam_call(lq1, lk1, lq2, lk2, lam_init):
    return pl.pallas_call(
        functools.partial(_lam_kernel, lam_init=lam_init),
        out_shape=jax.ShapeDtypeStruct((1, 1), F32),
        name="lam",
    )(lq1, lk1, lq2, lk2)


def _bias_tile_kernel(tab_ref, o_ref, *, tq, tk, off_mult, off_add, dil, max_rel, head0, key_major, mult):
    h = pl.program_id(0)
    n = pl.program_id(1)
    i = lax.broadcasted_iota(jnp.int32, (tq, tk), 1 if key_major else 0)
    j = lax.broadcasted_iota(jnp.int32, (tq, tk), 0 if key_major else 1)
    rel = i - j + (n * off_mult + off_add)
    dist = rel * dil
    large = jnp.full((tq, tk), N_BUCKETS // 2, jnp.int32)
    for thr in T5_THRESHOLDS:
        large = large + jnp.where(dist >= thr, 1, 0)
    bucket = jnp.where(dist < N_BUCKETS // 2, dist, large)
    acc = jnp.zeros((tq, tk), F32)
    for b in range(N_BUCKETS):
        acc = jnp.where(bucket == b, tab_ref[b, head0 + h], acc)
    valid = jnp.where(rel >= 0, jnp.where(rel <= max_rel, 1, 0), 0)
    o_ref[0, 0] = jnp.where(valid == 1, acc * mult, NEG_INF)


def _bias_tiles_call(rel_bias, *, n_heads, n_off, tq, tk, off_mult, off_add, dil, max_rel, head0, name,
                     key_major=False, mult=1.0):
    kern = functools.partial(_bias_tile_kernel, tq=tq, tk=tk, off_mult=off_mult, off_add=off_add,
                             dil=dil, max_rel=max_rel, head0=head0, key_major=key_major, mult=mult)
    return pl.pallas_call(
        kern,
        grid=(n_heads, n_off),
        in_specs=[pl.BlockSpec(memory_space=pltpu.SMEM)],
        out_specs=pl.BlockSpec((1, 1, tq, tk), lambda h, n: (h, n, 0, 0)),
        out_shape=jax.ShapeDtypeStruct((n_heads, n_off, tq, tk), F32),
        compiler_params=_cparams(("parallel", "parallel")),
        name=name,
    )(rel_bias)


def _rmsnorm_rows(x_ref, g_ref, out_ref, eps):
    rows = x_ref.shape[0]
    g = g_ref[...]

    def body(c, carry):
        r0 = pl.multiple_of(c * NORM_CHUNK, NORM_CHUNK)
        x = x_ref[pl.ds(r0, NORM_CHUNK), :]
        ms = jnp.mean(x * x, axis=-1, keepdims=True)
        out_ref[pl.ds(r0, NORM_CHUNK), :] = ((x * lax.rsqrt(ms + eps)) * g).astype(out_ref.dtype)
        return carry

    lax.fori_loop(0, rows // NORM_CHUNK, body, 0)


def _rmsnorm_val(x, g, eps):
    ms = jnp.mean(x * x, axis=-1, keepdims=True)
    return (x * lax.rsqrt(ms + eps)) * g


def _inproj_kernel(x_ref, g_ref, w_ref, o_ref, h_ref, *, n_gate_blocks):
    j = pl.program_id(1)

    @pl.when(j == 0)
    def _():
        _rmsnorm_rows(x_ref, g_ref, h_ref, RMS_EPS)

    acc = _dot(h_ref[...], w_ref[...])

    @pl.when(j < n_gate_blocks)
    def _():
        o_ref[...] = _sigmoid(acc).astype(o_ref.dtype)

    @pl.when(j >= n_gate_blocks)
    def _():
        o_ref[...] = acc.astype(o_ref.dtype)


def _inproj_call(x2d, g, w_cat):
    t, d = x2d.shape
    n = w_cat.shape[1]
    tm = min(PROJ_TM, t)
    return pl.pallas_call(
        functools.partial(_inproj_kernel, n_gate_blocks=GATE_COLS // PROJ_TN),
        grid=(t // tm, n // PROJ_TN),
        in_specs=[
            pl.BlockSpec((tm, d), lambda i, j: (i, 0)),
            pl.BlockSpec((1, d), lambda i, j: (0, 0)),
            pl.BlockSpec((d, PROJ_TN), lambda i, j: (0, j)),
        ],
        out_specs=pl.BlockSpec((tm, PROJ_TN), lambda i, j: (i, j)),
        out_shape=jax.ShapeDtypeStruct((t, n), BF16),
        scratch_shapes=[pltpu.VMEM((tm, d), BF16)],
        compiler_params=_cparams(("parallel", "arbitrary")),
        name="inproj",
    )(x2d, g, w_cat)


def _diff_attn_kernel(lam_ref, q_ref, k_ref, v_ref, bias_ref, g_ref, *rest, out_scale, n_blk, n_cast, n_step_cast):
    cast_in = rest[:n_cast]
    o_ref = rest[n_cast]
    cast_out = rest[n_cast + 1:2 * n_cast + 1]
    vt_ref, tbuf_a, tbuf_b, acc = rest[2 * n_cast + 1:]
    for src, dst in zip(cast_in[:n_step_cast], cast_out[:n_step_cast]):
        dst[...] = src[...].astype(dst.dtype)
    qi = pl.program_id(2)
    width = 2 * A_HEAD_DIM
    n_map = 2 * ATT_HEADS
    c = (A_HEAD_DIM ** -0.5) * math.log2(math.e)

    @pl.when(qi == 0)
    def _():
        def transpose_block(b, carry):
            r0 = pl.multiple_of(b * ATT_BLK, ATT_BLK)
            for hp in range(ATT_HEADS):
                vb = v_ref[pl.ds(r0, ATT_BLK), hp * width:(hp + 1) * width]
                vt_ref[hp, b] = vb.astype(F32).T.astype(BF16)
            return carry

        lax.fori_loop(0, n_blk, transpose_block, 0)
        for src, dst in zip(cast_in[n_step_cast:], cast_out[n_step_cast:]):
            dst[...] = src[...].astype(dst.dtype)

    qs = [q_ref[:, mi * A_HEAD_DIM:(mi + 1) * A_HEAD_DIM] for mi in range(n_map)]
    acc[...] = jnp.zeros_like(acc)

    def scores(ki, dst):
        k0 = pl.multiple_of(ki * ATT_BLK, ATT_BLK)
        for mi in range(n_map):
            kb = k_ref[pl.ds(k0, ATT_BLK), mi * A_HEAD_DIM:(mi + 1) * A_HEAD_DIM]
            dst[mi] = _dot_nt(kb, qs[mi]) + bias_ref[mi // 2, qi - ki]

    def softmax(t, m, l):
        m_new = jnp.maximum(m, jnp.max(t, axis=0, keepdims=True))
        alpha = jnp.exp2((m - m_new) * c)
        p = jnp.exp2((t - m_new) * c)
        return m_new, alpha * l + jnp.sum(p, axis=0, keepdims=True), alpha, p.astype(BF16)

    scores(0, tbuf_a)

    def step(ki, carry, cur, nxt):
        ms, ls = carry
        stats = [softmax(cur[mi], ms[mi], ls[mi]) for mi in range(n_map)]
        scores(jnp.minimum(ki + 1, qi), nxt)
        for mi in range(n_map):
            acc[mi] = acc[mi] * stats[mi][2] + _dot(vt_ref[mi // 2, ki], stats[mi][3])
        return tuple(st[0] for st in stats), tuple(st[1] for st in stats)

    def body(ki, carry):
        return lax.cond(lax.bitwise_and(ki, 1) == 0,
                        lambda cr: step(ki, cr, tbuf_a, tbuf_b),
                        lambda cr: step(ki, cr, tbuf_b, tbuf_a), carry)

    minf = (jnp.full((1, ATT_BLK), -jnp.inf, F32),) * n_map
    zero = (jnp.zeros((1, ATT_BLK), F32),) * n_map
    _, ls = lax.fori_loop(0, qi + 1, body, (minf, zero))
    for hp in range(ATT_HEADS):
        w = acc[2 * hp] / ls[2 * hp] - lam_ref[0, 0] * (acc[2 * hp + 1] / ls[2 * hp + 1])
        ms = jnp.mean(w * w, axis=0, keepdims=True)
        y = ((w * lax.rsqrt(ms + SUBLN_EPS)) * g_ref[...]) * out_scale
        o_ref[:, hp * width:(hp + 1) * width] = y.T.astype(o_ref.dtype)


def _diff_attn_call(zg, lam, bias_tiles, subln_g, passengers, *, batch, seq, out_scale):
    t = batch * seq
    nq = seq // ATT_BLK
    width = 2 * A_HEAD_DIM
    blk_w = ATT_HEADS * width
    q_col = GATE_COLS // blk_w
    k_col = q_col + A_HEADS // ATT_HEADS
    v_col = k_col + A_HEADS // ATT_HEADS
    n_outer = (A_HEADS // ATT_HEADS) * batch
    n_steps = n_outer * nq

    def rides(w, n):
        return w.shape[0] % (n * 16) == 0 and w.size * 4 // n <= CAST_SLAB_BYTES

    step_riders = [w for w in passengers if rides(w, n_steps)]
    outer_riders = [w for w in passengers if not rides(w, n_steps) and rides(w, n_outer)]
    riders = step_riders + outer_riders

    def slab_spec(w):
        if any(w is r for r in step_riders):
            return pl.BlockSpec((w.shape[0] // n_steps, w.shape[1]),
                                lambda h, b, i: ((h * batch + b) * nq + i, 0))
        return pl.BlockSpec((w.shape[0] // n_outer, w.shape[1]), lambda h, b, i: (h * batch + b, 0))

    outs = pl.pallas_call(
        functools.partial(_diff_attn_kernel, out_scale=out_scale, n_blk=nq, n_cast=len(riders),
                          n_step_cast=len(step_riders)),
        grid=(A_HEADS // ATT_HEADS, batch, nq),
        in_specs=[
            pl.BlockSpec(memory_space=pltpu.SMEM),
            pl.BlockSpec((ATT_BLK, blk_w), lambda h, b, i: (b * nq + i, q_col + h)),
            pl.BlockSpec((seq, blk_w), lambda h, b, i: (b, k_col + h)),
            pl.BlockSpec((seq, blk_w), lambda h, b, i: (b, v_col + h)),
            pl.BlockSpec((ATT_HEADS, nq, ATT_BLK, ATT_BLK), lambda h, b, i: (h, 0, 0, 0)),
            pl.BlockSpec((width, 1), lambda h, b, i: (0, 0)),
        ] + [slab_spec(w) for w in riders],
        out_specs=[pl.BlockSpec((ATT_BLK, blk_w), lambda h, b, i: (b * nq + i, h))]
        + [slab_spec(w) for w in riders],
        out_shape=[jax.ShapeDtypeStruct((t, A_HEADS * width), BF16)]
        + [jax.ShapeDtypeStruct(w.shape, BF16) for w in riders],
        scratch_shapes=[pltpu.VMEM((ATT_HEADS, nq, width, ATT_BLK), BF16),
                        pltpu.VMEM((2 * ATT_HEADS, ATT_BLK, ATT_BLK), F32),
                        pltpu.VMEM((2 * ATT_HEADS, ATT_BLK, ATT_BLK), F32),
                        pltpu.VMEM((2 * ATT_HEADS, width, ATT_BLK), F32)],
        compiler_params=_cparams(("parallel", "parallel", "arbitrary")),
        name="diff_attn",
    )(lam, zg, zg, zg, bias_tiles, subln_g.reshape(width, 1), *riders)
    cast = dict(zip(map(id, riders), outs[1:]))
    return outs[0], [cast[id(w)] if id(w) in cast else w.astype(BF16) for w in passengers]


def _perm_matrix(dil, inverse):
    w = PERM_BLK // dil
    shift = w.bit_length() - 1
    a = lax.broadcasted_iota(jnp.int32, (PERM_BLK, PERM_BLK), 0)
    b = lax.broadcasted_iota(jnp.int32, (PERM_BLK, PERM_BLK), 1)
    dst, src = (b, a) if inverse else (a, b)
    c = lax.shift_right_logical(dst, shift)
    ll = lax.bitwise_and(dst, w - 1)
    return jnp.where(src == ll * dil + c, jnp.float32(1), jnp.float32(0))


def _deinterleave(pairs, perm, dil, seq):
    w = PERM_BLK // dil
    stream_len = seq // dil

    def body(b8, carry):
        r0 = pl.multiple_of(b8 * PERM_BLK, PERM_BLK)
        ys = [_dot(perm, src[pl.ds(r0, PERM_BLK), :]).astype(dst.dtype) for src, dst in pairs]
        for y, (_, dst) in zip(ys, pairs):
            for c in range(dil):
                d0 = pl.multiple_of(c * stream_len + b8 * w, w)
                dst[pl.ds(d0, w), :] = y[c * w:(c + 1) * w, :]
        return carry

    lax.fori_loop(0, seq // PERM_BLK, body, 0)


def _interleave(items, dil, seq):
    w = PERM_BLK // dil
    stream_len = seq // dil

    def body(b8, carry):
        for src, _, stack, _, _ in items:
            for c in range(dil):
                s0 = pl.multiple_of(c * stream_len + b8 * w, w)
                stack[c * w:(c + 1) * w, :] = src[pl.ds(s0, w), :]
        outs = [_dot(pinv, stack[...], **kw) for _, _, stack, pinv, kw in items]
        r0 = pl.multiple_of(b8 * PERM_BLK, PERM_BLK)
        for out, (_, dst, _, _, _) in zip(outs, items):
            dst[pl.ds(r0, PERM_BLK), :] = out.astype(dst.dtype)
        return carry

    lax.fori_loop(0, seq // PERM_BLK, body, 0)


def _transpose_blocks(src_ref, vt_ref, seq):
    def body(blk, carry):
        r0 = pl.multiple_of(blk * BLK, BLK)
        vt_ref[blk] = src_ref[pl.ds(r0, BLK), :].astype(F32).T.astype(vt_ref.dtype)
        return carry

    lax.fori_loop(0, seq // BLK, body, 0)


def _dil_windows(q_src, k_src, vt_ref, o_dst, lse_dst, bias_ref, ot_ref, lt_ref, windows):
    scale = B_HEAD_DIM ** -0.5
    c = scale * math.log2(math.e)
    tiles = []
    for rq, nk in windows:
        rk = rq - (nk - BLK)
        for hh in range(B_HEADS):
            cols = slice(hh * B_HEAD_DIM, (hh + 1) * B_HEAD_DIM)
            qh = q_src[pl.ds(rq, BLK), cols]
            kh = k_src[pl.ds(rk, nk), cols]
            tiles.append(_dot_nt(kh, qh) + bias_ref[0, hh, 2 * BLK - nk:, :])
    for wi, (rq, nk) in enumerate(windows):
        qb = rq // BLK if isinstance(rq, int) else lax.shift_right_logical(rq, BLK.bit_length() - 1)
        for hh in range(B_HEADS):
            cols = slice(hh * B_HEAD_DIM, (hh + 1) * B_HEAD_DIM)
            t = tiles[wi * B_HEADS + hh]
            m = jnp.max(t, axis=0, keepdims=True)
            p = jnp.exp2((t - m) * c)
            den = jnp.sum(p, axis=0, keepdims=True)
            if nk == BLK:
                vth = vt_ref[qb, cols, :]
            else:
                vth = jnp.concatenate([vt_ref[qb - 1, cols, :], vt_ref[qb, cols, :]], axis=1)
            ot_ref[wi, cols, :] = _dot(vth, p.astype(BF16)) / den
            lt_ref[wi, hh:hh + 1, :] = m * scale + jnp.log(den)
        o_dst[pl.ds(rq, BLK), :] = ot_ref[wi].T.astype(o_dst.dtype)
        lse_dst[pl.ds(rq, BLK), :] = lt_ref[wi].T


def _dil_streams(q_src, k_src, vt_ref, o_dst, lse_dst, bias_ref, ot_ref, lt_ref, dil, seq):
    stream_len = seq // dil
    nq = stream_len // BLK

    def run(windows):
        _dil_windows(q_src, k_src, vt_ref, o_dst, lse_dst, bias_ref, ot_ref, lt_ref, windows)

    if nq == 1:
        def stream_pair(c2, carry):
            base = pl.multiple_of(c2 * (2 * stream_len), BLK)
            run([(base, BLK), (pl.multiple_of(base + stream_len, BLK), BLK)])
            return carry

        lax.fori_loop(0, dil // 2, stream_pair, 0)
        return

    def stream(c, carry):
        base = _mult(c * stream_len, BLK)
        n_full = nq - 1
        if n_full % 2 == 1:
            run([(base, BLK), (_mult(base + n_full * BLK, BLK), 2 * BLK)])
            n_full -= 1
        else:
            run([(base, BLK)])

        def qpair(j, carry2):
            rq = pl.multiple_of(base + (1 + 2 * j) * BLK, BLK)
            run([(rq, 2 * BLK), (pl.multiple_of(rq + BLK, BLK), 2 * BLK)])
            return carry2

        if n_full:
            lax.fori_loop(0, n_full // 2, qpair, 0)
        return carry

    if dil == 1:
        stream(0, 0)
    else:
        lax.fori_loop(0, dil, stream, 0)


def _head_expand(x, expand):
    hi = x.astype(BF16)
    lo = (x - hi.astype(F32)).astype(BF16)
    return _dot(jnp.concatenate([hi, lo], axis=1), expand)


def _dil_merge(o_src, lse_src, oacc, mrun, lrun, out_ref, first, last, seq):
    er = lax.broadcasted_iota(jnp.int32, (2 * LANES, B_HEADS * B_HEAD_DIM), 0)
    ec = lax.broadcasted_iota(jnp.int32, (2 * LANES, B_HEADS * B_HEAD_DIM), 1)
    expand = jnp.where(lax.shift_right_logical(ec, B_HEAD_DIM.bit_length() - 1) == lax.bitwise_and(er, LANES - 1),
                       jnp.float32(1), jnp.float32(0)).astype(BF16)

    def body(ch, carry):
        r0 = pl.multiple_of(ch * ROW_TM, ROW_TM)
        rows = pl.ds(r0, ROW_TM)
        lse = lse_src[rows, :]
        if first:
            mrun[rows, :] = lse
            lrun[rows, :] = jnp.ones_like(lse)
            oacc[rows, :] = o_src[rows, :].astype(F32)
            return carry
        m_old = mrun[rows, :]
        m_new = jnp.maximum(m_old, lse)
        a = jnp.exp(m_old - m_new)
        bw = jnp.exp(lse - m_new)
        l_new = lrun[rows, :] * a + bw
        if last:
            a = a / l_new
            bw = bw / l_new
        val = oacc[rows, :] * _head_expand(a, expand) + o_src[rows, :].astype(F32) * _head_expand(bw, expand)
        if last:
            out_ref[rows, :] = val.astype(out_ref.dtype)
        else:
            mrun[rows, :] = m_new
            lrun[rows, :] = l_new
            oacc[rows, :] = val
        return carry

    lax.fori_loop(0, seq // ROW_TM, body, 0)


def _dilated_kernel(q_ref, k_ref, v_ref, bias_ref, out_ref,
                    qs, ks, vs, vt, os_, lses, otok, lsetok, ostack, lstack, ot, lt, oacc, mrun, lrun,
                    *, seq):
    g = pl.program_id(1)

    @pl.when(g == 0)
    def _():
        lt[...] = jnp.zeros_like(lt)

    for gi, (_, dil) in enumerate(DILATED_GROUPS):

        @pl.when(g == gi)
        def _(gi=gi, dil=dil):
            first = gi == 0
            last = gi == N_DGROUPS - 1
            if dil == 1:
                _transpose_blocks(v_ref, vt, seq)
                _dil_streams(q_ref, k_ref, vt, otok, lsetok, bias_ref, ot, lt, 1, seq)
            else:
                perm = _perm_matrix(dil, inverse=False).astype(BF16)
                _deinterleave([(q_ref, qs), (k_ref, ks), (v_ref, vs)], perm, dil, seq)
                _transpose_blocks(vs, vt, seq)
                _dil_streams(qs, ks, vt, os_, lses, bias_ref, ot, lt, dil, seq)
                perm_inv = _perm_matrix(dil, inverse=True)
                _interleave([(os_, otok, ostack, perm_inv.astype(BF16), {}),
                             (lses, lsetok, lstack, perm_inv, dict(precision=lax.Precision.HIGHEST))],
                            dil, seq)
            _dil_merge(otok, lsetok, oacc, mrun, lrun, out_ref, first, last, seq)


def _dilated_call(zg, bias_tiles, *, batch, seq):
    width = B_HEADS * B_HEAD_DIM
    q_col = (GATE_COLS + 3 * A_HEADS * 2 * A_HEAD_DIM) // width
    k_col = q_col + N_DGROUPS
    v_col = k_col + N_DGROUPS
    return pl.pallas_call(
        functools.partial(_dilated_kernel, seq=seq),
        grid=(batch, N_DGROUPS),
        in_specs=[
            pl.BlockSpec((seq, width), lambda b, g: (b, q_col + g)),
            pl.BlockSpec((seq, width), lambda b, g: (b, k_col + g)),
            pl.BlockSpec((seq, width), lambda b, g: (b, v_col + g)),
            pl.BlockSpec((1, B_HEADS, 2 * BLK, BLK), lambda b, g: (g, 0, 0, 0)),
        ],
        out_specs=pl.BlockSpec((seq, width), lambda b, g: (b, 0)),
        out_shape=jax.ShapeDtypeStruct((batch * seq, width), BF16),
        scratch_shapes=[
            pltpu.VMEM((seq, width), BF16), pltpu.VMEM((seq, width), BF16), pltpu.VMEM((seq, width), BF16),
            pltpu.VMEM((seq // BLK, width, BLK), BF16),
            pltpu.VMEM((seq, width), BF16), pltpu.VMEM((seq, LANES), F32),
            pltpu.VMEM((seq, width), BF16), pltpu.VMEM((seq, LANES), F32),
            pltpu.VMEM((PERM_BLK, width), BF16), pltpu.VMEM((PERM_BLK, LANES), F32),
            pltpu.VMEM((2, width, BLK), F32), pltpu.VMEM((2, LANES, BLK), F32),
            pltpu.VMEM((seq, width), F32), pltpu.VMEM((seq, LANES), F32), pltpu.VMEM((seq, LANES), F32),
        ],
        compiler_params=_cparams(("parallel", "arbitrary")),
        name="dilated_attn",
    )(zg, zg, zg, bias_tiles)


def _merge_router_kernel(oa_ref, ob_ref, gate_ref, x_ref, pa_ref, pb_ref, wo_ref, g_ref, wr_ref,
                         x1_ref, h2_ref, route_ref, cnt_ref, run_ref):
    tm, d = x_ref.shape

    @pl.when(pl.program_id(0) == 0)
    def _():
        run_ref[...] = jnp.zeros_like(run_ref)

    a = _dot(oa_ref[...], pa_ref[...])
    bm = _dot(ob_ref[...], pb_ref[...])
    merged = gate_ref[:, :d].astype(F32) * a + gate_ref[:, d:].astype(F32) * bm
    x1 = x_ref[...] + _dot(merged.astype(BF16), wo_ref[...])
    x1_ref[...] = x1
    h2 = _rmsnorm_val(x1, g_ref[...], RMS_EPS)
    h2_ref[...] = h2

    h_hi = h2.astype(BF16)
    h_lo = (h2 - h_hi.astype(F32)).astype(BF16)
    lt = (_dot(h_hi, wr_ref[0]) + (_dot(h_hi, wr_ref[1]) + _dot(h_lo, wr_ref[0]))).T
    coarse = [lt[i:i + 1, :] for i in range(N_EXPERT_GROUPS)]
    best = coarse[0]
    gsel = jnp.zeros((1, tm), jnp.int32)
    for i in range(1, N_EXPERT_GROUPS):
        upd = coarse[i] > best
        gsel = jnp.where(upd, i, gsel)
        best = jnp.where(upd, coarse[i], best)
    den = jnp.exp(coarse[0] - best)
    for i in range(1, N_EXPERT_GROUPS):
        den = den + jnp.exp(coarse[i] - best)
    pg = 1.0 / den

    fine = []
    for k in range(EXPERTS_PER_GROUP):
        f = lt[N_EXPERT_GROUPS + k:N_EXPERT_GROUPS + k + 1, :]
        for gi in range(1, N_EXPERT_GROUPS):
            r = N_EXPERT_GROUPS + gi * EXPERTS_PER_GROUP + k
            f = jnp.where(gsel == gi, lt[r:r + 1, :], f)
        fine.append(f)
    v0 = fine[0]
    i0 = jnp.zeros((1, tm), jnp.int32)
    for k in range(1, EXPERTS_PER_GROUP):
        upd = fine[k] > v0
        i0 = jnp.where(upd, k, i0)
        v0 = jnp.where(upd, fine[k], v0)
    v1 = jnp.full((1, tm), -jnp.inf, F32)
    i1 = jnp.zeros((1, tm), jnp.int32)
    for k in range(EXPERTS_PER_GROUP):
        upd = jnp.where(i0 != k, jnp.where(fine[k] > v1, 1, 0), 0) == 1
        i1 = jnp.where(upd, k, i1)
        v1 = jnp.where(upd, fine[k], v1)
    e1w = jnp.exp(v1 - v0)
    gate0 = pg * (1.0 / (1.0 + e1w))
    gate1 = pg * (e1w / (1.0 + e1w))
    e0 = gsel * EXPERTS_PER_GROUP + i0
    e1 = gsel * EXPERTS_PER_GROUP + i1

    eidx = lax.broadcasted_iota(jnp.int32, (N_EXPERTS, tm), 0)
    oh0 = jnp.where(eidx == e0, jnp.float32(1), jnp.float32(0))
    oh1 = jnp.where(eidx == e1, jnp.float32(1), jnp.float32(0))
    ta = lax.broadcasted_iota(jnp.int32, (tm, tm), 0)
    tb = lax.broadcasted_iota(jnp.int32, (tm, tm), 1)
    before = jnp.where(ta < tb, jnp.float32(1), jnp.float32(0)).astype(BF16)
    pre0 = _dot(oh0.astype(BF16), before)
    pre1 = _dot(oh1.astype(BF16), before)
    run = run_ref[:, 0:1]
    tot0 = jnp.sum(oh0, axis=1, keepdims=True)
    tot1 = jnp.sum(oh1, axis=1, keepdims=True)
    rank0 = jnp.sum(oh0 * (run + pre0), axis=0, keepdims=True)
    rank1 = jnp.sum(oh1 * (run + tot0 + pre1), axis=0, keepdims=True)
    new_run = jnp.broadcast_to(run + tot0 + tot1, run_ref.shape)
    run_ref[...] = new_run
    cnt_ref[...] = new_run

    route_ref[0:1, :] = e0.astype(F32)
    route_ref[1:2, :] = e1.astype(F32)
    route_ref[2:3, :] = gate0
    route_ref[3:4, :] = gate1
    route_ref[4:5, :] = rank0
    route_ref[5:6, :] = rank1
    route_ref[6:8, :] = jnp.zeros((2, tm), F32)


def _merge_router_call(oa, ob, zg, x2d, pa, pb, wo, g, wr_t):
    t, d = x2d.shape
    tm = ROW_TM
    const = dict(pipeline_mode=pl.Buffered(1))
    return pl.pallas_call(
        _merge_router_kernel,
        grid=(t // tm,),
        in_specs=[
            pl.BlockSpec((tm, oa.shape[1]), lambda i: (i, 0)),
            pl.BlockSpec((tm, ob.shape[1]), lambda i: (i, 0)),
            pl.BlockSpec((tm, GATE_COLS), lambda i: (i, 0)),
            pl.BlockSpec((tm, d), lambda i: (i, 0)),
            pl.BlockSpec(pa.shape, lambda i: (0, 0), **const),
            pl.BlockSpec(pb.shape, lambda i: (0, 0), **const),
            pl.BlockSpec(wo.shape, lambda i: (0, 0), **const),
            pl.BlockSpec((1, d), lambda i: (0, 0)),
            pl.BlockSpec(wr_t.shape, lambda i: (0, 0, 0), **const),
        ],
        out_specs=[
            pl.BlockSpec((tm, d), lambda i: (i, 0)),
            pl.BlockSpec((tm, d), lambda i: (i, 0)),
            pl.BlockSpec((8, tm), lambda i: (0, i)),
            pl.BlockSpec((N_EXPERTS, LANES), lambda i: (0, 0)),
        ],
        out_shape=[
            jax.ShapeDtypeStruct((t, d), F32),
            jax.ShapeDtypeStruct((t, d), F32),
            jax.ShapeDtypeStruct((8, t), F32),
            jax.ShapeDtypeStruct((N_EXPERTS, LANES), F32),
        ],
        scratch_shapes=[pltpu.VMEM((N_EXPERTS, LANES), F32)],
        compiler_params=_cparams(("arbitrary",)),
        name="merge_router",
    )(oa, ob, zg, x2d, pa, pb, wo, g, wr_t)


def _row_copy(src, s_row, dst, d_row, sem):
    return pltpu.make_async_copy(src.at[pl.ds(s_row, 1)], dst.at[pl.ds(d_row, 1)], sem)


def _dispatch_kernel(dest_ref, zstart_ref, zlen_ref, h_ref, x_hbm, zero_ref, sem, zsem):
    step = pl.program_id(0)

    def issue(t, carry):
        for k in range(TOP_K):
            _row_copy(h_ref, t, x_hbm, dest_ref[0, k, t], sem.at[k]).start()
        return carry

    lax.fori_loop(0, DISPATCH_TOK, issue, 0, unroll=8)

    @pl.when(step == 0)
    def _():
        zero_ref[...] = jnp.zeros_like(zero_ref)

        def group_copy(r8):
            return pltpu.make_async_copy(zero_ref, x_hbm.at[pl.ds(pl.multiple_of(r8, 8), 8)], zsem.at[1])

        def segment(e, wait):
            start = zstart_ref[e]
            n_head = jnp.minimum(lax.bitwise_and(-start, 7), zlen_ref[e])
            n_group = lax.shift_right_logical(zlen_ref[e] - n_head, 3)

            def head(r, carry):
                cp = _row_copy(zero_ref, 0, x_hbm, 0 if wait else start + r, zsem.at[0])
                cp.wait() if wait else cp.start()
                return carry

            def group(j, carry):
                cp = group_copy(0 if wait else start + n_head + 8 * j)
                cp.wait() if wait else cp.start()
                return carry

            lax.fori_loop(0, n_head, head, 0)
            lax.fori_loop(0, n_group, group, 0)

        lax.fori_loop(0, N_EXPERTS + 1, lambda e, c: (segment(e, False), c)[1], 0)
        lax.fori_loop(0, N_EXPERTS + 1, lambda e, c: (segment(e, True), c)[1], 0)

    for k in range(TOP_K):
        pltpu.make_async_copy(h_ref, x_hbm.at[pl.ds(0, DISPATCH_TOK)], sem.at[k]).wait()


def _dispatch_call(dest_blocks, zstart, zlen, h2, n_rows):
    t, d = h2.shape
    return pl.pallas_call(
        _dispatch_kernel,
        grid=(t // DISPATCH_TOK,),
        in_specs=[
            pl.BlockSpec((1, TOP_K, DISPATCH_TOK), lambda i: (i, 0, 0), memory_space=pltpu.SMEM),
            pl.BlockSpec(memory_space=pltpu.SMEM),
            pl.BlockSpec(memory_space=pltpu.SMEM),
            pl.BlockSpec((DISPATCH_TOK, d), lambda i: (i, 0)),
        ],
        out_specs=pl.BlockSpec(memory_space=pl.ANY),
        out_shape=jax.ShapeDtypeStruct((n_rows, d), F32),
        scratch_shapes=[pltpu.VMEM((8, d), F32), pltpu.SemaphoreType.DMA((TOP_K,)),
                        pltpu.SemaphoreType.DMA((2,))],
        compiler_params=_cparams(("arbitrary",)),
        name="dispatch",
    )(dest_blocks, zstart, zlen, h2)


def _expert_kernel(blk_e_ref, blk_valid_ref, x_ref, w1_ref, w3_ref, w2_ref, y_ref):
    i = pl.program_id(0)

    @pl.when(blk_valid_ref[i] == 1)
    def _():
        x = x_ref[...].astype(BF16)
        a = _dot(x, w1_ref[0])
        b = _dot(x, w3_ref[0])
        hdn = (a * _sigmoid(a)) * b
        y_ref[...] = _dot(hdn.astype(BF16), w2_ref[0])

    @pl.when(blk_valid_ref[i] == 0)
    def _():
        y_ref[...] = jnp.zeros_like(y_ref)


def _expert_call(blk_e, blk_valid, xin, w1, w3, w2):
    n_rows, d = xin.shape
    de = w1.shape[2]
    grid_spec = pltpu.PrefetchScalarGridSpec(
        num_scalar_prefetch=2,
        grid=(n_rows // EXP_TM,),
        in_specs=[
            pl.BlockSpec((EXP_TM, d), lambda i, be, bv: (i, 0)),
            pl.BlockSpec((1, d, de), lambda i, be, bv: (be[i], 0, 0)),
            pl.BlockSpec((1, d, de), lambda i, be, bv: (be[i], 0, 0)),
            pl.BlockSpec((1, de, d), lambda i, be, bv: (be[i], 0, 0)),
        ],
        out_specs=pl.BlockSpec((EXP_TM, d), lambda i, be, bv: (i, 0)),
    )
    return pl.pallas_call(
        _expert_kernel,
        grid_spec=grid_spec,
        out_shape=jax.ShapeDtypeStruct((n_rows, d), F32),
        compiler_params=_cparams(("arbitrary",)),
        name="experts",
    )(blk_e, blk_valid, xin, w1, w3, w2)


def _combine_kernel(dest_ref, dnext_ref, y_hbm, x1_ref, gt_ref, p_ref, gple_ref, wg_ref, wp_ref, gfin_ref,
                    o_ref, ybuf, sem, *, final_norm, n_steps):
    tm = x1_ref.shape[0]
    i = pl.program_id(0)
    slot = lax.bitwise_and(i, 1)

    def gather(d_ref, s):
        def issue(t, carry):
            for k in range(TOP_K):
                _row_copy(y_hbm, d_ref[0, k, t], ybuf.at[s, k], t, sem.at[s, k]).start()
            return carry

        lax.fori_loop(0, tm, issue, 0, unroll=8)

    @pl.when(i == 0)
    def _():
        gather(dest_ref, 0)

    @pl.when(i + 1 < n_steps)
    def _():
        gather(dnext_ref, 1 - slot)

    pp = _dot(p_ref[...].astype(BF16), wp_ref[...])

    for k in range(TOP_K):
        pltpu.make_async_copy(y_hbm.at[pl.ds(0, tm)], ybuf.at[slot, k], sem.at[slot, k]).wait()
    y = ybuf[slot, 0] * gt_ref[:, 2:3] + ybuf[slot, 1] * gt_ref[:, 3:4]
    x2 = x1_ref[...] + y
    hn = _rmsnorm_val(x2, gple_ref[...], RMS_EPS)
    gate = _sigmoid(_dot(hn.astype(BF16), wg_ref[...]))
    x3 = x2 + gate * pp
    if final_norm:
        o_ref[...] = _rmsnorm_val(x3, gfin_ref[...], RMS_EPS)
    else:
        o_ref[...] = x3


def _combine_call(dest_blocks, yb, x1, gates_t, p2d, g_ple, w_gate, w_proj, g_fin, final_norm):
    t, d = x1.shape
    tm = ROW_TM
    const = dict(pipeline_mode=pl.Buffered(1))
    return pl.pallas_call(
        functools.partial(_combine_kernel, final_norm=final_norm, n_steps=t // tm),
        grid=(t // tm,),
        in_specs=[
            pl.BlockSpec((1, TOP_K, tm), lambda i: (i, 0, 0), memory_space=pltpu.SMEM),
            pl.BlockSpec((1, TOP_K, tm), lambda i: (jnp.minimum(i + 1, t // tm - 1), 0, 0),
                         memory_space=pltpu.SMEM),
            pl.BlockSpec(memory_space=pl.ANY),
            pl.BlockSpec((tm, d), lambda i: (i, 0)),
            pl.BlockSpec((tm, 8), lambda i: (i, 0)),
            pl.BlockSpec((tm, p2d.shape[1]), lambda i: (i, 0)),
            pl.BlockSpec((1, d), lambda i: (0, 0)),
            pl.BlockSpec(w_gate.shape, lambda i: (0, 0), **const),
            pl.BlockSpec(w_proj.shape, lambda i: (0, 0), **const),
            pl.BlockSpec((1, d), lambda i: (0, 0)),
        ],
        out_specs=pl.BlockSpec((tm, d), lambda i: (i, 0)),
        out_shape=jax.ShapeDtypeStruct((t, d), F32),
        scratch_shapes=[pltpu.VMEM((2, TOP_K, tm, d), F32), pltpu.SemaphoreType.DMA((2, TOP_K))],
        compiler_params=_cparams(("arbitrary",)),
        name="combine",
    )(dest_blocks, dest_blocks, yb, x1, gates_t, p2d, g_ple, w_gate, w_proj, g_fin)


def _routing_plan(route, counts_f, t):
    counts = counts_f[:, 0].astype(jnp.int32)
    pcounts = ((counts + EXP_TM - 1) // EXP_TM) * EXP_TM
    pends = jnp.cumsum(pcounts)
    pstarts = pends - pcounts
    eid = route[0:2].astype(jnp.int32)
    rank = route[4:6].astype(jnp.int32)
    onehot = eid[..., None] == jnp.arange(N_EXPERTS, dtype=jnp.int32)
    dest = jnp.sum(jnp.where(onehot, pstarts, 0), axis=-1) + rank
    n_rows = TOP_K * t + N_EXPERTS * EXP_TM
    n_blocks = n_rows // EXP_TM
    blk_row = jnp.arange(n_blocks, dtype=jnp.int32) * EXP_TM
    blk_e = jnp.sum((pends[None, :] <= blk_row[:, None]).astype(jnp.int32), axis=1)
    blk_valid = (blk_row < pends[-1]).astype(jnp.int32)
    last_e = jnp.sum((pends <= pends[-1] - 1).astype(jnp.int32))
    blk_e = jnp.minimum(blk_e, last_e)
    zstart = jnp.concatenate([pstarts + counts, pends[-1:]]).astype(jnp.int32)
    zlen = jnp.concatenate([pcounts - counts, n_rows - pends[-1:]]).astype(jnp.int32)
    return dest, blk_e, blk_valid, zstart, zlen, n_rows


def _blocked(dest, tok_per_block):
    t = dest.shape[1]
    return dest.reshape(TOP_K, t // tok_per_block, tok_per_block).transpose(1, 0, 2)


def kernel(x, p, rel_bias, norm_mix_g, w_in, w_gate, lambda_q1, lambda_k1, lambda_q2, lambda_k2, subln_g,
           w_proj_a, w_proj_b, w_out, norm_ffn_g, w_coarse, w_fine, w1, w3, w2, norm_ple_g, w_ple_gate,
           w_ple_proj, final_norm_g):
    batch, seq, d = x.shape
    depth = w_in.shape[0]
    t = batch * seq
    assert seq % PERM_BLK == 0 and t % min(PROJ_TM, t) == 0
    assert all(seq // dil >= BLK and win // dil == BLK for win, dil in DILATED_GROUPS)

    nq = seq // ATT_BLK
    bias_a = _bias_tiles_call(rel_bias, n_heads=A_HEADS, n_off=nq, tq=ATT_BLK, tk=ATT_BLK, off_mult=ATT_BLK,
                              off_add=0, dil=1, max_rel=seq, head0=0, name="bias_diff",
                              key_major=True, mult=A_HEAD_DIM ** 0.5)
    bias_b = jnp.concatenate([
        _bias_tiles_call(rel_bias, n_heads=B_HEADS, n_off=1, tq=2 * BLK, tk=BLK, off_mult=0, off_add=BLK,
                         dil=dil, max_rel=win // dil, head0=A_HEADS + gi * B_HEADS, name="bias_dil%d" % gi,
                         key_major=True, mult=B_HEAD_DIM ** 0.5).reshape(1, B_HEADS, 2 * BLK, BLK)
        for gi, (win, dil) in enumerate(DILATED_GROUPS)], axis=0)

    x2d = x.reshape(t, d)
    for layer in range(depth):
        lam_init = 0.8 - 0.6 * math.exp(-0.3 * layer)
        lam = _lam_call(lambda_q1[layer:layer + 1], lambda_k1[layer:layer + 1],
                        lambda_q2[layer:layer + 1], lambda_k2[layer:layer + 1], lam_init)
        w_cat = jnp.concatenate([w_gate[layer], w_in[layer]], axis=1).astype(BF16)
        zg = _inproj_call(x2d, norm_mix_g[layer:layer + 1], w_cat)
        n_exp, _, d_exp = w1[layer].shape
        oa, (w1b, w3b, w2b, pa_b, pb_b, wo_b, wg_b, wp_b) = _diff_attn_call(
            zg, lam, bias_a, subln_g[layer:layer + 1],
            [w1[layer].reshape(n_exp * d, d_exp), w3[layer].reshape(n_exp * d, d_exp),
             w2[layer].reshape(n_exp * d_exp, d), w_proj_a[layer], w_proj_b[layer], w_out[layer],
             w_ple_gate[layer], w_ple_proj[layer]],
            batch=batch, seq=seq, out_scale=1.0 - lam_init)
        ob = _dilated_call(zg, bias_b, batch=batch, seq=seq)

        wr_t = jnp.concatenate([
            w_coarse[layer].T,
            w_fine[layer].transpose(0, 2, 1).reshape(N_EXPERTS, d),
            jnp.zeros((LANES - N_EXPERT_GROUPS - N_EXPERTS, d), F32)], axis=0)
        wr_t = wr_t.T
        wr_hi = wr_t.astype(BF16)
        wr_t = jnp.stack([wr_hi, (wr_t - wr_hi.astype(F32)).astype(BF16)], axis=0)
        x1, h2, route, counts = _merge_router_call(
            oa, ob, zg, x2d, pa_b, pb_b, wo_b, norm_ffn_g[layer:layer + 1], wr_t)

        dest, blk_e, blk_valid, zstart, zlen, n_rows = _routing_plan(route, counts, t)
        xin = _dispatch_call(_blocked(dest, DISPATCH_TOK), zstart, zlen, h2, n_rows)
        yb = _expert_call(blk_e, blk_valid, xin, w1b.reshape(n_exp, d, d_exp), w3b.reshape(n_exp, d, d_exp),
                          w2b.reshape(n_exp, d_exp, d))
        x2d = _combine_call(_blocked(dest, ROW_TM), yb, x1, route.T, p[layer].reshape(t, -1),
                            norm_ple_g[layer:layer + 1], wg_b, wp_b, final_norm_g.reshape(1, d),
                            final_norm=layer == depth - 1)
    return x2d.reshape(batch, seq, d)
```

```python
import functools
import math

import jax
import jax.numpy as jnp
from jax import lax
from jax.experimental import pallas as pl
from jax.experimental.pallas import tpu as pltpu

F32 = jnp.float32
BF16 = jnp.bfloat16

BLK = 128
NEG_INF = -1e30
RMS_EPS = 1e-6
SUBLN_EPS = 1e-5
N_BUCKETS = 32
MAX_DISTANCE = 2048
A_HEADS = 8
A_HEAD_DIM = 128
DILATED_GROUPS = ((128, 1), (512, 4), (2048, 16))
N_DGROUPS = 3
B_HEADS = 8
B_HEAD_DIM = 64
N_EXPERT_GROUPS = 4
EXPERTS_PER_GROUP = 8
N_EXPERTS = 32
TOP_K = 2

LANES = 128
VMEM_LIMIT = 56 * 1024 * 1024
ATT_BLK = 256
ATT_HEADS = 2
PERM_BLK = 256
GATE_COLS = 4096
PROJ_TN = 512
PROJ_TM = 2048
ROW_TM = 256
EXP_TM = 256
DISPATCH_TOK = 512
NORM_CHUNK = 128
CAST_SLAB_BYTES = 1 << 20


def _t5_thresholds():
    max_exact = N_BUCKETS // 2
    out = []
    for k in range(1, N_BUCKETS - max_exact):
        out.append(int(math.ceil(max_exact * (MAX_DISTANCE / max_exact) ** (k / (N_BUCKETS - max_exact)))))
    return tuple(out)


T5_THRESHOLDS = _t5_thresholds()


def _cparams(sem, vmem=VMEM_LIMIT):
    return pltpu.CompilerParams(dimension_semantics=sem, vmem_limit_bytes=vmem)


def _mult(x, m):
    return x if isinstance(x, int) else pl.multiple_of(x, m)


def _sigmoid(x):
    return 0.5 * jnp.tanh(0.5 * x) + 0.5


def _dot(a, b, **kw):
    return jnp.dot(a, b, preferred_element_type=F32, **kw)


def _dot_nt(a, b, **kw):
    return lax.dot_general(a, b, (((1,), (1,)), ((), ())), preferred_element_type=F32, **kw)


def _lam_kernel(q1_ref, k1_ref, q2_ref, k2_ref, o_ref, *, lam_init):
    s1 = jnp.sum(q1_ref[...] * k1_ref[...], axis=-1, keepdims=True)
    s2 = jnp.sum(q2_ref[...] * k2_ref[...], axis=-1, keepdims=True)
    o_ref[...] = jnp.exp(s1) - jnp.exp(s2) + lam_init


def _lam_call(lq1, lk1, lq2, lk2, lam_init):
    return pl.pallas_call(
        functools.partial(_lam_kernel, lam_init=lam_init),
        out_shape=jax.ShapeDtypeStruct((1, 1), F32),
        name="lam",
    )(lq1, lk1, lq2, lk2)


def _bias_tile_kernel(tab_ref, *rest, tq, tk, off_mult, off_add, dil, max_rel, head0, key_major, mult):
    n_pass = max(len(rest) - 2, 0)
    o_ref = rest[n_pass]
    if n_pass:
        col = 0
        for src in rest[:n_pass]:
            rest[-1][:, col:col + src.shape[1]] = src[...].astype(rest[-1].dtype)
            col += src.shape[1]
    h = pl.program_id(0)
    n = pl.program_id(1)
    i = lax.broadcasted_iota(jnp.int32, (tq, tk), 1 if key_major else 0)
    j = lax.broadcasted_iota(jnp.int32, (tq, tk), 0 if key_major else 1)
    rel = i - j + (n * off_mult + off_add)
    dist = rel * dil
    large = jnp.full((tq, tk), N_BUCKETS // 2, jnp.int32)
    for thr in T5_THRESHOLDS:
        large = large + jnp.where(dist >= thr, 1, 0)
    bucket = jnp.where(dist < N_BUCKETS // 2, dist, large)
    acc = jnp.zeros((tq, tk), F32)
    for b in range(N_BUCKETS):
        acc = jnp.where(bucket == b, tab_ref[b, head0 + h], acc)
    valid = jnp.where(rel >= 0, jnp.where(rel <= max_rel, 1, 0), 0)
    o_ref[0, 0] = jnp.where(valid == 1, acc * mult, NEG_INF)


def _bias_tiles_call(rel_bias, *, n_heads, n_off, tq, tk, off_mult, off_add, dil, max_rel, head0, name,
                     key_major=False, mult=1.0, concat_cast=()):
    kern = functools.partial(_bias_tile_kernel, tq=tq, tk=tk, off_mult=off_mult, off_add=off_add,
                             dil=dil, max_rel=max_rel, head0=head0, key_major=key_major, mult=mult)
    n_steps = n_heads * n_off
    if concat_cast and concat_cast[0].shape[0] % (n_steps * 16):
        concat_cast = ()
    in_specs = [pl.BlockSpec(memory_space=pltpu.SMEM)]
    out_specs = [pl.BlockSpec((1, 1, tq, tk), lambda h, n: (h, n, 0, 0))]
    out_shape = [jax.ShapeDtypeStruct((n_heads, n_off, tq, tk), F32)]
    if concat_cast:
        rows = concat_cast[0].shape[0]
        cols = sum(w.shape[1] for w in concat_cast)
        in_specs += [pl.BlockSpec((rows // n_steps, w.shape[1]), lambda h, n: (h * n_off + n, 0))
                     for w in concat_cast]
        out_specs.append(pl.BlockSpec((rows // n_steps, cols), lambda h, n: (h * n_off + n, 0)))
        out_shape.append(jax.ShapeDtypeStruct((rows, cols), BF16))
    outs = pl.pallas_call(
        kern,
        grid=(n_heads, n_off),
        in_specs=in_specs,
        out_specs=out_specs,
        out_shape=out_shape,
        compiler_params=_cparams(("parallel", "parallel")),
        name=name,
    )(rel_bias, *concat_cast)
    return outs[0], (outs[1] if concat_cast else None)


def _rmsnorm_rows(x_ref, g_ref, out_ref, eps):
    rows = x_ref.shape[0]
    g = g_ref[...]

    def body(c, carry):
        r0 = pl.multiple_of(c * NORM_CHUNK, NORM_CHUNK)
        x = x_ref[pl.ds(r0, NORM_CHUNK), :]
        ms = jnp.mean(x * x, axis=-1, keepdims=True)
        out_ref[pl.ds(r0, NORM_CHUNK), :] = ((x * lax.rsqrt(ms + eps)) * g).astype(out_ref.dtype)
        return carry

    lax.fori_loop(0, rows // NORM_CHUNK, body, 0)


def _rmsnorm_val(x, g, eps):
    ms = jnp.mean(x * x, axis=-1, keepdims=True)
    return (x * lax.rsqrt(ms + eps)) * g


def _inproj_kernel(x_ref, g_ref, w_ref, o_ref, h_ref, *, n_gate_blocks):
    j = pl.program_id(1)

    @pl.when(j == 0)
    def _():
        _rmsnorm_rows(x_ref, g_ref, h_ref, RMS_EPS)

    acc = _dot(h_ref[...], w_ref[...])

    @pl.when(j < n_gate_blocks)
    def _():
        o_ref[...] = _sigmoid(acc).astype(o_ref.dtype)

    @pl.when(j >= n_gate_blocks)
    def _():
        o_ref[...] = acc.astype(o_ref.dtype)


def _inproj_call(x2d, g, w_cat):
    t, d = x2d.shape
    n = w_cat.shape[1]
    tm = min(PROJ_TM, t)
    return pl.pallas_call(
        functools.partial(_inproj_kernel, n_gate_blocks=GATE_COLS // PROJ_TN),
        grid=(t // tm, n // PROJ_TN),
        in_specs=[
            pl.BlockSpec((tm, d), lambda i, j: (i, 0)),
            pl.BlockSpec((1, d), lambda i, j: (0, 0)),
            pl.BlockSpec((d, PROJ_TN), lambda i, j: (0, j)),
        ],
        out_specs=pl.BlockSpec((tm, PROJ_TN), lambda i, j: (i, j)),
        out_shape=jax.ShapeDtypeStruct((t, n), BF16),
        scratch_shapes=[pltpu.VMEM((tm, d), BF16)],
        compiler_params=_cparams(("parallel", "arbitrary")),
        name="inproj",
    )(x2d, g, w_cat)


def _diff_attn_kernel(lam_ref, q_ref, k_ref, v_ref, bias_ref, g_ref, *rest, out_scale, n_blk, n_cast):
    cast_in = rest[:n_cast]
    o_ref = rest[n_cast]
    cast_out = rest[n_cast + 1:2 * n_cast + 1]
    vt_ref, tbuf_a, tbuf_b, acc = rest[2 * n_cast + 1:]
    for src, dst in zip(cast_in, cast_out):
        dst[...] = src[...].astype(dst.dtype)
    qi = pl.program_id(2)
    width = 2 * A_HEAD_DIM
    n_map = 2 * ATT_HEADS
    c = (A_HEAD_DIM ** -0.5) * math.log2(math.e)

    @pl.when(qi == 0)
    def _():
        def transpose_block(b, carry):
            r0 = pl.multiple_of(b * ATT_BLK, ATT_BLK)
            for hp in range(ATT_HEADS):
                vb = v_ref[pl.ds(r0, ATT_BLK), hp * width:(hp + 1) * width]
                vt_ref[hp, b] = vb.astype(F32).T.astype(BF16)
            return carry

        lax.fori_loop(0, n_blk, transpose_block, 0)

    qs = [q_ref[:, mi * A_HEAD_DIM:(mi + 1) * A_HEAD_DIM] for mi in range(n_map)]
    acc[...] = jnp.zeros_like(acc)

    def scores(ki, dst):
        k0 = pl.multiple_of(ki * ATT_BLK, ATT_BLK)
        for mi in range(n_map):
            kb = k_ref[pl.ds(k0, ATT_BLK), mi * A_HEAD_DIM:(mi + 1) * A_HEAD_DIM]
            dst[mi] = _dot_nt(kb, qs[mi]) + bias_ref[mi // 2, qi - ki]

    def softmax(t, m, l):
        m_new = jnp.maximum(m, jnp.max(t, axis=0, keepdims=True))
        alpha = jnp.exp2((m - m_new) * c)
        p = jnp.exp2((t - m_new) * c)
        return m_new, alpha * l + jnp.sum(p, axis=0, keepdims=True), alpha, p.astype(BF16)

    scores(0, tbuf_a)

    def step(ki, carry, cur, nxt):
        ms, ls = carry
        stats = [softmax(cur[mi], ms[mi], ls[mi]) for mi in range(n_map)]
        scores(jnp.minimum(ki + 1, qi), nxt)
        for mi in range(n_map):
            acc[mi] = acc[mi] * stats[mi][2] + _dot(vt_ref[mi // 2, ki], stats[mi][3])
        return tuple(st[0] for st in stats), tuple(st[1] for st in stats)

    def body(ki, carry):
        return lax.cond(lax.bitwise_and(ki, 1) == 0,
                        lambda cr: step(ki, cr, tbuf_a, tbuf_b),
                        lambda cr: step(ki, cr, tbuf_b, tbuf_a), carry)

    minf = (jnp.full((1, ATT_BLK), -jnp.inf, F32),) * n_map
    zero = (jnp.zeros((1, ATT_BLK), F32),) * n_map
    _, ls = lax.fori_loop(0, qi + 1, body, (minf, zero))
    for hp in range(ATT_HEADS):
        w = acc[2 * hp] / ls[2 * hp] - lam_ref[0, 0] * (acc[2 * hp + 1] / ls[2 * hp + 1])
        ms = jnp.mean(w * w, axis=0, keepdims=True)
        y = ((w * lax.rsqrt(ms + SUBLN_EPS)) * g_ref[...]) * out_scale
        o_ref[:, hp * width:(hp + 1) * width] = y.T.astype(o_ref.dtype)


def _diff_attn_call(zg, lam, bias_tiles, subln_g, passengers, *, batch, seq, out_scale):
    t = batch * seq
    nq = seq // ATT_BLK
    width = 2 * A_HEAD_DIM
    blk_w = ATT_HEADS * width
    q_col = GATE_COLS // blk_w
    k_col = q_col + A_HEADS // ATT_HEADS
    v_col = k_col + A_HEADS // ATT_HEADS
    n_steps = (A_HEADS // ATT_HEADS) * batch * nq
    riders = [w for w in passengers
              if w.shape[0] % (n_steps * 16) == 0 and w.size * 4 // n_steps <= CAST_SLAB_BYTES]

    def slab_spec(w):
        return pl.BlockSpec((w.shape[0] // n_steps, w.shape[1]), lambda h, b, i: ((h * batch + b) * nq + i, 0))

    outs = pl.pallas_call(
        functools.partial(_diff_attn_kernel, out_scale=out_scale, n_blk=nq, n_cast=len(riders)),
        grid=(A_HEADS // ATT_HEADS, batch, nq),
        in_specs=[
            pl.BlockSpec(memory_space=pltpu.SMEM),
            pl.BlockSpec((ATT_BLK, blk_w), lambda h, b, i: (b * nq + i, q_col + h)),
            pl.BlockSpec((seq, blk_w), lambda h, b, i: (b, k_col + h)),
            pl.BlockSpec((seq, blk_w), lambda h, b, i: (b, v_col + h)),
            pl.BlockSpec((ATT_HEADS, nq, ATT_BLK, ATT_BLK), lambda h, b, i: (h, 0, 0, 0)),
            pl.BlockSpec((width, 1), lambda h, b, i: (0, 0)),
        ] + [slab_spec(w) for w in riders],
        out_specs=[pl.BlockSpec((ATT_BLK, blk_w), lambda h, b, i: (b * nq + i, h))]
        + [slab_spec(w) for w in riders],
        out_shape=[jax.ShapeDtypeStruct((t, A_HEADS * width), BF16)]
        + [jax.ShapeDtypeStruct(w.shape, BF16) for w in riders],
        scratch_shapes=[pltpu.VMEM((ATT_HEADS, nq, width, ATT_BLK), BF16),
                        pltpu.VMEM((2 * ATT_HEADS, ATT_BLK, ATT_BLK), F32),
                        pltpu.VMEM((2 * ATT_HEADS, ATT_BLK, ATT_BLK), F32),
                        pltpu.VMEM((2 * ATT_HEADS, width, ATT_BLK), F32)],
        compiler_params=_cparams(("parallel", "parallel", "arbitrary")),
        name="diff_attn",
    )(lam, zg, zg, zg, bias_tiles, subln_g.reshape(width, 1), *riders)
    cast = iter(outs[1:])
    return outs[0], [next(cast) if any(w is r for r in riders) else w.astype(BF16) for w in passengers]


def _perm_matrix(dil, inverse):
    w = PERM_BLK // dil
    shift = w.bit_length() - 1
    a = lax.broadcasted_iota(jnp.int32, (PERM_BLK, PERM_BLK), 0)
    b = lax.broadcasted_iota(jnp.int32, (PERM_BLK, PERM_BLK), 1)
    dst, src = (b, a) if inverse else (a, b)
    c = lax.shift_right_logical(dst, shift)
    ll = lax.bitwise_and(dst, w - 1)
    return jnp.where(src == ll * dil + c, jnp.float32(1), jnp.float32(0))


def _deinterleave(pairs, perm, dil, seq):
    w = PERM_BLK // dil
    stream_len = seq // dil

    def body(b8, carry):
        r0 = pl.multiple_of(b8 * PERM_BLK, PERM_BLK)
        ys = [_dot(perm, src[pl.ds(r0, PERM_BLK), :]).astype(dst.dtype) for src, dst in pairs]
        for y, (_, dst) in zip(ys, pairs):
            for c in range(dil):
                d0 = pl.multiple_of(c * stream_len + b8 * w, w)
                dst[pl.ds(d0, w), :] = y[c * w:(c + 1) * w, :]
        return carry

    lax.fori_loop(0, seq // PERM_BLK, body, 0)


def _interleave(items, dil, seq):
    w = PERM_BLK // dil
    stream_len = seq // dil

    def body(b8, carry):
        for src, _, stack, _, _ in items:
            for c in range(dil):
                s0 = pl.multiple_of(c * stream_len + b8 * w, w)
                stack[c * w:(c + 1) * w, :] = src[pl.ds(s0, w), :]
        outs = [_dot(pinv, stack[...], **kw) for _, _, stack, pinv, kw in items]
        r0 = pl.multiple_of(b8 * PERM_BLK, PERM_BLK)
        for out, (_, dst, _, _, _) in zip(outs, items):
            dst[pl.ds(r0, PERM_BLK), :] = out.astype(dst.dtype)
        return carry

    lax.fori_loop(0, seq // PERM_BLK, body, 0)


def _transpose_blocks(src_ref, vt_ref, seq):
    def body(blk, carry):
        r0 = pl.multiple_of(blk * BLK, BLK)
        vt_ref[blk] = src_ref[pl.ds(r0, BLK), :].astype(F32).T.astype(vt_ref.dtype)
        return carry

    lax.fori_loop(0, seq // BLK, body, 0)


def _dil_windows(q_src, k_src, vt_ref, o_dst, lse_dst, bias_ref, ot_ref, lt_ref, windows):
    scale = B_HEAD_DIM ** -0.5
    c = scale * math.log2(math.e)
    tiles = []
    for rq, nk in windows:
        rk = rq - (nk - BLK)
        for hh in range(B_HEADS):
            cols = slice(hh * B_HEAD_DIM, (hh + 1) * B_HEAD_DIM)
            qh = q_src[pl.ds(rq, BLK), cols]
            kh = k_src[pl.ds(rk, nk), cols]
            tiles.append(_dot_nt(kh, qh) + bias_ref[0, hh, 2 * BLK - nk:, :])
    for wi, (rq, nk) in enumerate(windows):
        qb = rq // BLK if isinstance(rq, int) else lax.shift_right_logical(rq, BLK.bit_length() - 1)
        for hh in range(B_HEADS):
            cols = slice(hh * B_HEAD_DIM, (hh + 1) * B_HEAD_DIM)
            t = tiles[wi * B_HEADS + hh]
            m = jnp.max(t, axis=0, keepdims=True)
            p = jnp.exp2((t - m) * c)
            den = jnp.sum(p, axis=0, keepdims=True)
            if nk == BLK:
                vth = vt_ref[qb, cols, :]
            else:
                vth = jnp.concatenate([vt_ref[qb - 1, cols, :], vt_ref[qb, cols, :]], axis=1)
            ot_ref[wi, cols, :] = _dot(vth, p.astype(BF16)) / den
            lt_ref[wi, hh:hh + 1, :] = m * scale + jnp.log(den)
        o_dst[pl.ds(rq, BLK), :] = ot_ref[wi].T.astype(o_dst.dtype)
        lse_dst[pl.ds(rq, BLK), :] = lt_ref[wi].T


def _dil_streams(q_src, k_src, vt_ref, o_dst, lse_dst, bias_ref, ot_ref, lt_ref, dil, seq):
    stream_len = seq // dil
    nq = stream_len // BLK

    def run(windows):
        _dil_windows(q_src, k_src, vt_ref, o_dst, lse_dst, bias_ref, ot_ref, lt_ref, windows)

    if nq == 1:
        def stream_pair(c2, carry):
            base = pl.multiple_of(c2 * (2 * stream_len), BLK)
            run([(base, BLK), (pl.multiple_of(base + stream_len, BLK), BLK)])
            return carry

        lax.fori_loop(0, dil // 2, stream_pair, 0)
        return

    def stream(c, carry):
        base = _mult(c * stream_len, BLK)
        n_full = nq - 1
        if n_full % 2 == 1:
            run([(base, BLK), (_mult(base + n_full * BLK, BLK), 2 * BLK)])
            n_full -= 1
        else:
            run([(base, BLK)])

        def qpair(j, carry2):
            rq = pl.multiple_of(base + (1 + 2 * j) * BLK, BLK)
            run([(rq, 2 * BLK), (pl.multiple_of(rq + BLK, BLK), 2 * BLK)])
            return carry2

        if n_full:
            lax.fori_loop(0, n_full // 2, qpair, 0)
        return carry

    if dil == 1:
        stream(0, 0)
    else:
        lax.fori_loop(0, dil, stream, 0)


def _head_expand(x, expand):
    hi = x.astype(BF16)
    lo = (x - hi.astype(F32)).astype(BF16)
    return _dot(jnp.concatenate([hi, lo], axis=1), expand)


def _dil_merge(o_src, lse_src, oacc, mrun, lrun, out_ref, first, last, seq):
    er = lax.broadcasted_iota(jnp.int32, (2 * LANES, B_HEADS * B_HEAD_DIM), 0)
    ec = lax.broadcasted_iota(jnp.int32, (2 * LANES, B_HEADS * B_HEAD_DIM), 1)
    expand = jnp.where(lax.shift_right_logical(ec, B_HEAD_DIM.bit_length() - 1) == lax.bitwise_and(er, LANES - 1),
                       jnp.float32(1), jnp.float32(0)).astype(BF16)

    def body(ch, carry):
        r0 = pl.multiple_of(ch * ROW_TM, ROW_TM)
        rows = pl.ds(r0, ROW_TM)
        lse = lse_src[rows, :]
        if first:
            mrun[rows, :] = lse
            lrun[rows, :] = jnp.ones_like(lse)
            oacc[rows, :] = o_src[rows, :].astype(F32)
            return carry
        m_old = mrun[rows, :]
        m_new = jnp.maximum(m_old, lse)
        a = jnp.exp(m_old - m_new)
        bw = jnp.exp(lse - m_new)
        l_new = lrun[rows, :] * a + bw
        if last:
            a = a / l_new
            bw = bw / l_new
        val = oacc[rows, :] * _head_expand(a, expand) + o_src[rows, :].astype(F32) * _head_expand(bw, expand)
        if last:
            out_ref[rows, :] = val.astype(out_ref.dtype)
        else:
            mrun[rows, :] = m_new
            lrun[rows, :] = l_new
            oacc[rows, :] = val
        return carry

    lax.fori_loop(0, seq // ROW_TM, body, 0)


def _dilated_kernel(q_ref, k_ref, v_ref, bias_ref, out_ref,
                    qs, ks, vs, vt, os_, lses, otok, lsetok, ostack, lstack, ot, lt, oacc, mrun, lrun,
                    *, seq):
    g = pl.program_id(1)

    @pl.when(g == 0)
    def _():
        lt[...] = jnp.zeros_like(lt)

    for gi, (_, dil) in enumerate(DILATED_GROUPS):

        @pl.when(g == gi)
        def _(gi=gi, dil=dil):
            first = gi == 0
            last = gi == N_DGROUPS - 1
            if dil == 1:
                _transpose_blocks(v_ref, vt, seq)
                _dil_streams(q_ref, k_ref, vt, otok, lsetok, bias_ref, ot, lt, 1, seq)
            else:
                perm = _perm_matrix(dil, inverse=False).astype(BF16)
                _deinterleave([(q_ref, qs), (k_ref, ks), (v_ref, vs)], perm, dil, seq)
                _transpose_blocks(vs, vt, seq)
                _dil_streams(qs, ks, vt, os_, lses, bias_ref, ot, lt, dil, seq)
                perm_inv = _perm_matrix(dil, inverse=True)
                _interleave([(os_, otok, ostack, perm_inv.astype(BF16), {}),
                             (lses, lsetok, lstack, perm_inv, dict(precision=lax.Precision.HIGHEST))],
                            dil, seq)
            _dil_merge(otok, lsetok, oacc, mrun, lrun, out_ref, first, last, seq)


def _dilated_call(zg, bias_tiles, *, batch, seq):
    width = B_HEADS * B_HEAD_DIM
    q_col = (GATE_COLS + 3 * A_HEADS * 2 * A_HEAD_DIM) // width
    k_col = q_col + N_DGROUPS
    v_col = k_col + N_DGROUPS
    return pl.pallas_call(
        functools.partial(_dilated_kernel, seq=seq),
        grid=(batch, N_DGROUPS),
        in_specs=[
            pl.BlockSpec((seq, width), lambda b, g: (b, q_col + g)),
            pl.BlockSpec((seq, width), lambda b, g: (b, k_col + g)),
            pl.BlockSpec((seq, width), lambda b, g: (b, v_col + g)),
            pl.BlockSpec((1, B_HEADS, 2 * BLK, BLK), lambda b, g: (g, 0, 0, 0)),
        ],
        out_specs=pl.BlockSpec((seq, width), lambda b, g: (b, 0)),
        out_shape=jax.ShapeDtypeStruct((batch * seq, width), BF16),
        scratch_shapes=[
            pltpu.VMEM((seq, width), BF16), pltpu.VMEM((seq, width), BF16), pltpu.VMEM((seq, width), BF16),
            pltpu.VMEM((seq // BLK, width, BLK), BF16),
            pltpu.VMEM((seq, width), BF16), pltpu.VMEM((seq, LANES), F32),
            pltpu.VMEM((seq, width), BF16), pltpu.VMEM((seq, LANES), F32),
            pltpu.VMEM((PERM_BLK, width), BF16), pltpu.VMEM((PERM_BLK, LANES), F32),
            pltpu.VMEM((2, width, BLK), F32), pltpu.VMEM((2, LANES, BLK), F32),
            pltpu.VMEM((seq, width), F32), pltpu.VMEM((seq, LANES), F32), pltpu.VMEM((seq, LANES), F32),
        ],
        compiler_params=_cparams(("parallel", "arbitrary")),
        name="dilated_attn",
    )(zg, zg, zg, bias_tiles)


def _merge_router_kernel(oa_ref, ob_ref, gate_ref, x_ref, pa_ref, pb_ref, wo_ref, g_ref, wr_ref,
                         x1_ref, h2_ref, route_ref, cnt_ref, run_ref):
    tm, d = x_ref.shape

    @pl.when(pl.program_id(0) == 0)
    def _():
        run_ref[...] = jnp.zeros_like(run_ref)

    a = _dot(oa_ref[...], pa_ref[...])
    bm = _dot(ob_ref[...], pb_ref[...])
    merged = gate_ref[:, :d].astype(F32) * a + gate_ref[:, d:].astype(F32) * bm
    x1 = x_ref[...] + _dot(merged.astype(BF16), wo_ref[...])
    x1_ref[...] = x1
    h2 = _rmsnorm_val(x1, g_ref[...], RMS_EPS)
    h2_ref[...] = h2

    h_hi = h2.astype(BF16)
    h_lo = (h2 - h_hi.astype(F32)).astype(BF16)
    lt = (_dot(h_hi, wr_ref[0]) + (_dot(h_hi, wr_ref[1]) + _dot(h_lo, wr_ref[0]))).T
    coarse = [lt[i:i + 1, :] for i in range(N_EXPERT_GROUPS)]
    best = coarse[0]
    gsel = jnp.zeros((1, tm), jnp.int32)
    for i in range(1, N_EXPERT_GROUPS):
        upd = coarse[i] > best
        gsel = jnp.where(upd, i, gsel)
        best = jnp.where(upd, coarse[i], best)
    den = jnp.exp(coarse[0] - best)
    for i in range(1, N_EXPERT_GROUPS):
        den = den + jnp.exp(coarse[i] - best)
    pg = 1.0 / den

    fine = []
    for k in range(EXPERTS_PER_GROUP):
        f = lt[N_EXPERT_GROUPS + k:N_EXPERT_GROUPS + k + 1, :]
        for gi in range(1, N_EXPERT_GROUPS):
            r = N_EXPERT_GROUPS + gi * EXPERTS_PER_GROUP + k
            f = jnp.where(gsel == gi, lt[r:r + 1, :], f)
        fine.append(f)
    v0 = fine[0]
    i0 = jnp.zeros((1, tm), jnp.int32)
    for k in range(1, EXPERTS_PER_GROUP):
        upd = fine[k] > v0
        i0 = jnp.where(upd, k, i0)
        v0 = jnp.where(upd, fine[k], v0)
    v1 = jnp.full((1, tm), -jnp.inf, F32)
    i1 = jnp.zeros((1, tm), jnp.int32)
    for k in range(EXPERTS_PER_GROUP):
        upd = jnp.where(i0 != k, jnp.where(fine[k] > v1, 1, 0), 0) == 1
        i1 = jnp.where(upd, k, i1)
        v1 = jnp.where(upd, fine[k], v1)
    e1w = jnp.exp(v1 - v0)
    gate0 = pg * (1.0 / (1.0 + e1w))
    gate1 = pg * (e1w / (1.0 + e1w))
    e0 = gsel * EXPERTS_PER_GROUP + i0
    e1 = gsel * EXPERTS_PER_GROUP + i1

    eidx = lax.broadcasted_iota(jnp.int32, (N_EXPERTS, tm), 0)
    oh0 = jnp.where(eidx == e0, jnp.float32(1), jnp.float32(0))
    oh1 = jnp.where(eidx == e1, jnp.float32(1), jnp.float32(0))
    ta = lax.broadcasted_iota(jnp.int32, (tm, tm), 0)
    tb = lax.broadcasted_iota(jnp.int32, (tm, tm), 1)
    before = jnp.where(ta < tb, jnp.float32(1), jnp.float32(0)).astype(BF16)
    pre0 = _dot(oh0.astype(BF16), before)
    pre1 = _dot(oh1.astype(BF16), before)
    run = run_ref[:, 0:1]
    tot0 = jnp.sum(oh0, axis=1, keepdims=True)
    tot1 = jnp.sum(oh1, axis=1, keepdims=True)
    rank0 = jnp.sum(oh0 * (run + pre0), axis=0, keepdims=True)
    rank1 = jnp.sum(oh1 * (run + tot0 + pre1), axis=0, keepdims=True)
    new_run = jnp.broadcast_to(run + tot0 + tot1, run_ref.shape)
    run_ref[...] = new_run
    cnt_ref[...] = new_run

    route_ref[0:1, :] = e0.astype(F32)
    route_ref[1:2, :] = e1.astype(F32)
    route_ref[2:3, :] = gate0
    route_ref[3:4, :] = gate1
    route_ref[4:5, :] = rank0
    route_ref[5:6, :] = rank1
    route_ref[6:8, :] = jnp.zeros((2, tm), F32)


def _merge_router_call(oa, ob, zg, x2d, pa, pb, wo, g, wr_t):
    t, d = x2d.shape
    tm = ROW_TM
    const = dict(pipeline_mode=pl.Buffered(1))
    return pl.pallas_call(
        _merge_router_kernel,
        grid=(t // tm,),
        in_specs=[
            pl.BlockSpec((tm, oa.shape[1]), lambda i: (i, 0)),
            pl.BlockSpec((tm, ob.shape[1]), lambda i: (i, 0)),
            pl.BlockSpec((tm, GATE_COLS), lambda i: (i, 0)),
            pl.BlockSpec((tm, d), lambda i: (i, 0)),
            pl.BlockSpec(pa.shape, lambda i: (0, 0), **const),
            pl.BlockSpec(pb.shape, lambda i: (0, 0), **const),
            pl.BlockSpec(wo.shape, lambda i: (0, 0), **const),
            pl.BlockSpec((1, d), lambda i: (0, 0)),
            pl.BlockSpec(wr_t.shape, lambda i: (0, 0, 0), **const),
        ],
        out_specs=[
            pl.BlockSpec((tm, d), lambda i: (i, 0)),
            pl.BlockSpec((tm, d), lambda i: (i, 0)),
            pl.BlockSpec((8, tm), lambda i: (0, i)),
            pl.BlockSpec((N_EXPERTS, LANES), lambda i: (0, 0)),
        ],
        out_shape=[
            jax.ShapeDtypeStruct((t, d), F32),
            jax.ShapeDtypeStruct((t, d), F32),
            jax.ShapeDtypeStruct((8, t), F32),
            jax.ShapeDtypeStruct((N_EXPERTS, LANES), F32),
        ],
        scratch_shapes=[pltpu.VMEM((N_EXPERTS, LANES), F32)],
        compiler_params=_cparams(("arbitrary",)),
        name="merge_router",
    )(oa, ob, zg, x2d, pa, pb, wo, g, wr_t)


def _row_copy(src, s_row, dst, d_row, sem):
    return pltpu.make_async_copy(src.at[pl.ds(s_row, 1)], dst.at[pl.ds(d_row, 1)], sem)


def _dispatch_kernel(dest_ref, zstart_ref, zlen_ref, h_ref, x_hbm, zero_ref, sem, zsem):
    step = pl.program_id(0)

    def issue(t, carry):
        for k in range(TOP_K):
            _row_copy(h_ref, t, x_hbm, dest_ref[0, k, t], sem.at[k]).start()
        return carry

    lax.fori_loop(0, DISPATCH_TOK, issue, 0, unroll=8)

    @pl.when(step == 0)
    def _():
        zero_ref[...] = jnp.zeros_like(zero_ref)

        def group_copy(r8):
            return pltpu.make_async_copy(zero_ref, x_hbm.at[pl.ds(pl.multiple_of(r8, 8), 8)], zsem.at[1])

        def segment(e, wait):
            start = zstart_ref[e]
            n_head = jnp.minimum(lax.bitwise_and(-start, 7), zlen_ref[e])
            n_group = lax.shift_right_logical(zlen_ref[e] - n_head, 3)

            def head(r, carry):
                cp = _row_copy(zero_ref, 0, x_hbm, 0 if wait else start + r, zsem.at[0])
                cp.wait() if wait else cp.start()
                return carry

            def group(j, carry):
                cp = group_copy(0 if wait else start + n_head + 8 * j)
                cp.wait() if wait else cp.start()
                return carry

            lax.fori_loop(0, n_head, head, 0)
            lax.fori_loop(0, n_group, group, 0)

        lax.fori_loop(0, N_EXPERTS + 1, lambda e, c: (segment(e, False), c)[1], 0)
        lax.fori_loop(0, N_EXPERTS + 1, lambda e, c: (segment(e, True), c)[1], 0)

    for k in range(TOP_K):
        pltpu.make_async_copy(h_ref, x_hbm.at[pl.ds(0, DISPATCH_TOK)], sem.at[k]).wait()


def _dispatch_call(dest_blocks, zstart, zlen, h2, n_rows):
    t, d = h2.shape
    return pl.pallas_call(
        _dispatch_kernel,
        grid=(t // DISPATCH_TOK,),
        in_specs=[
            pl.BlockSpec((1, TOP_K, DISPATCH_TOK), lambda i: (i, 0, 0), memory_space=pltpu.SMEM),
            pl.BlockSpec(memory_space=pltpu.SMEM),
            pl.BlockSpec(memory_space=pltpu.SMEM),
            pl.BlockSpec((DISPATCH_TOK, d), lambda i: (i, 0)),
        ],
        out_specs=pl.BlockSpec(memory_space=pl.ANY),
        out_shape=jax.ShapeDtypeStruct((n_rows, d), F32),
        scratch_shapes=[pltpu.VMEM((8, d), F32), pltpu.SemaphoreType.DMA((TOP_K,)),
                        pltpu.SemaphoreType.DMA((2,))],
        compiler_params=_cparams(("arbitrary",)),
        name="dispatch",
    )(dest_blocks, zstart, zlen, h2)


def _expert_kernel(blk_e_ref, blk_valid_ref, x_ref, w1_ref, w3_ref, w2_ref, y_ref):
    i = pl.program_id(0)

    @pl.when(blk_valid_ref[i] == 1)
    def _():
        x = x_ref[...].astype(BF16)
        a = _dot(x, w1_ref[0])
        b = _dot(x, w3_ref[0])
        hdn = (a * _sigmoid(a)) * b
        y_ref[...] = _dot(hdn.astype(BF16), w2_ref[0])

    @pl.when(blk_valid_ref[i] == 0)
    def _():
        y_ref[...] = jnp.zeros_like(y_ref)


def _expert_call(blk_e, blk_valid, xin, w1, w3, w2):
    n_rows, d = xin.shape
    de = w1.shape[2]
    grid_spec = pltpu.PrefetchScalarGridSpec(
        num_scalar_prefetch=2,
        grid=(n_rows // EXP_TM,),
        in_specs=[
            pl.BlockSpec((EXP_TM, d), lambda i, be, bv: (i, 0)),
            pl.BlockSpec((1, d, de), lambda i, be, bv: (be[i], 0, 0)),
            pl.BlockSpec((1, d, de), lambda i, be, bv: (be[i], 0, 0)),
            pl.BlockSpec((1, de, d), lambda i, be, bv: (be[i], 0, 0)),
        ],
        out_specs=pl.BlockSpec((EXP_TM, d), lambda i, be, bv: (i, 0)),
    )
    return pl.pallas_call(
        _expert_kernel,
        grid_spec=grid_spec,
        out_shape=jax.ShapeDtypeStruct((n_rows, d), F32),
        compiler_params=_cparams(("arbitrary",)),
        name="experts",
    )(blk_e, blk_valid, xin, w1, w3, w2)


def _combine_kernel(dest_ref, dnext_ref, y_hbm, x1_ref, gt_ref, p_ref, gple_ref, wg_ref, wp_ref, gfin_ref,
                    o_ref, ybuf, sem, *, final_norm, n_steps):
    tm = x1_ref.shape[0]
    i = pl.program_id(0)
    slot = lax.bitwise_and(i, 1)

    def gather(d_ref, s):
        def issue(t, carry):
            for k in range(TOP_K):
                _row_copy(y_hbm, d_ref[0, k, t], ybuf.at[s, k], t, sem.at[s, k]).start()
            return carry

        lax.fori_loop(0, tm, issue, 0, unroll=8)

    @pl.when(i == 0)
    def _():
        gather(dest_ref, 0)

    @pl.when(i + 1 < n_steps)
    def _():
        gather(dnext_ref, 1 - slot)

    pp = _dot(p_ref[...].astype(BF16), wp_ref[...])

    for k in range(TOP_K):
        pltpu.make_async_copy(y_hbm.at[pl.ds(0, tm)], ybuf.at[slot, k], sem.at[slot, k]).wait()
    y = ybuf[slot, 0] * gt_ref[:, 2:3] + ybuf[slot, 1] * gt_ref[:, 3:4]
    x2 = x1_ref[...] + y
    hn = _rmsnorm_val(x2, gple_ref[...], RMS_EPS)
    gate = _sigmoid(_dot(hn.astype(BF16), wg_ref[...]))
    x3 = x2 + gate * pp
    if final_norm:
        o_ref[...] = _rmsnorm_val(x3, gfin_ref[...], RMS_EPS)
    else:
        o_ref[...] = x3


def _combine_call(dest_blocks, yb, x1, gates_t, p2d, g_ple, w_gate, w_proj, g_fin, final_norm):
    t, d = x1.shape
    tm = ROW_TM
    const = dict(pipeline_mode=pl.Buffered(1))
    return pl.pallas_call(
        functools.partial(_combine_kernel, final_norm=final_norm, n_steps=t // tm),
        grid=(t // tm,),
        in_specs=[
            pl.BlockSpec((1, TOP_K, tm), lambda i: (i, 0, 0), memory_space=pltpu.SMEM),
            pl.BlockSpec((1, TOP_K, tm), lambda i: (jnp.minimum(i + 1, t // tm - 1), 0, 0),
                         memory_space=pltpu.SMEM),
            pl.BlockSpec(memory_space=pl.ANY),
            pl.BlockSpec((tm, d), lambda i: (i, 0)),
            pl.BlockSpec((tm, 8), lambda i: (i, 0)),
            pl.BlockSpec((tm, p2d.shape[1]), lambda i: (i, 0)),
            pl.BlockSpec((1, d), lambda i: (0, 0)),
            pl.BlockSpec(w_gate.shape, lambda i: (0, 0), **const),
            pl.BlockSpec(w_proj.shape, lambda i: (0, 0), **const),
            pl.BlockSpec((1, d), lambda i: (0, 0)),
        ],
        out_specs=pl.BlockSpec((tm, d), lambda i: (i, 0)),
        out_shape=jax.ShapeDtypeStruct((t, d), F32),
        scratch_shapes=[pltpu.VMEM((2, TOP_K, tm, d), F32), pltpu.SemaphoreType.DMA((2, TOP_K))],
        compiler_params=_cparams(("arbitrary",)),
        name="combine",
    )(dest_blocks, dest_blocks, yb, x1, gates_t, p2d, g_ple, w_gate, w_proj, g_fin)


def _routing_plan(route, counts_f, t):
    counts = counts_f[:, 0].astype(jnp.int32)
    pcounts = ((counts + EXP_TM - 1) // EXP_TM) * EXP_TM
    pends = jnp.cumsum(pcounts)
    pstarts = pends - pcounts
    eid = route[0:2].astype(jnp.int32)
    rank = route[4:6].astype(jnp.int32)
    onehot = eid[..., None] == jnp.arange(N_EXPERTS, dtype=jnp.int32)
    dest = jnp.sum(jnp.where(onehot, pstarts, 0), axis=-1) + rank
    n_rows = TOP_K * t + N_EXPERTS * EXP_TM
    n_blocks = n_rows // EXP_TM
    blk_row = jnp.arange(n_blocks, dtype=jnp.int32) * EXP_TM
    blk_e = jnp.sum((pends[None, :] <= blk_row[:, None]).astype(jnp.int32), axis=1)
    blk_valid = (blk_row < pends[-1]).astype(jnp.int32)
    last_e = jnp.sum((pends <= pends[-1] - 1).astype(jnp.int32))
    blk_e = jnp.minimum(blk_e, last_e)
    zstart = jnp.concatenate([pstarts + counts, pends[-1:]]).astype(jnp.int32)
    zlen = jnp.concatenate([pcounts - counts, n_rows - pends[-1:]]).astype(jnp.int32)
    return dest, blk_e, blk_valid, zstart, zlen, n_rows


def _blocked(dest, tok_per_block):
    t = dest.shape[1]
    return dest.reshape(TOP_K, t // tok_per_block, tok_per_block).transpose(1, 0, 2)


def kernel(x, p, rel_bias, norm_mix_g, w_in, w_gate, lambda_q1, lambda_k1, lambda_q2, lambda_k2, subln_g,
           w_proj_a, w_proj_b, w_out, norm_ffn_g, w_coarse, w_fine, w1, w3, w2, norm_ple_g, w_ple_gate,
           w_ple_proj, final_norm_g):
    batch, seq, d = x.shape
    depth = w_in.shape[0]
    t = batch * seq
    assert seq % PERM_BLK == 0 and t % min(PROJ_TM, t) == 0
    assert all(seq // dil >= BLK and win // dil == BLK for win, dil in DILATED_GROUPS)

    nq = seq // ATT_BLK
    bias_a, w_cat0 = _bias_tiles_call(rel_bias, n_heads=A_HEADS, n_off=nq, tq=ATT_BLK, tk=ATT_BLK,
                                      off_mult=ATT_BLK, off_add=0, dil=1, max_rel=seq, head0=0, name="bias_diff",
                                      key_major=True, mult=A_HEAD_DIM ** 0.5, concat_cast=(w_gate[0], w_in[0]))
    bias_b = jnp.concatenate([
        _bias_tiles_call(rel_bias, n_heads=B_HEADS, n_off=1, tq=2 * BLK, tk=BLK, off_mult=0, off_add=BLK,
                         dil=dil, max_rel=win // dil, head0=A_HEADS + gi * B_HEADS, name="bias_dil%d" % gi,
                         key_major=True, mult=B_HEAD_DIM ** 0.5)[0].reshape(1, B_HEADS, 2 * BLK, BLK)
        for gi, (win, dil) in enumerate(DILATED_GROUPS)], axis=0)

    x2d = x.reshape(t, d)
    for layer in range(depth):
        lam_init = 0.8 - 0.6 * math.exp(-0.3 * layer)
        lam = _lam_call(lambda_q1[layer:layer + 1], lambda_k1[layer:layer + 1],
                        lambda_q2[layer:layer + 1], lambda_k2[layer:layer + 1], lam_init)
        if layer == 0 and w_cat0 is not None:
            w_cat = w_cat0
        else:
            w_cat = jnp.concatenate([w_gate[layer], w_in[layer]], axis=1).astype(BF16)
        zg = _inproj_call(x2d, norm_mix_g[layer:layer + 1], w_cat)
        n_exp, _, d_exp = w1[layer].shape
        oa, (w1b, w3b, w2b) = _diff_attn_call(
            zg, lam, bias_a, subln_g[layer:layer + 1],
            [w1[layer].reshape(n_exp * d, d_exp), w3[layer].reshape(n_exp * d, d_exp),
             w2[layer].reshape(n_exp * d_exp, d)],
            batch=batch, seq=seq, out_scale=1.0 - lam_init)
        ob = _dilated_call(zg, bias_b, batch=batch, seq=seq)

        wr_t = jnp.concatenate([
            w_coarse[layer].T,
            w_fine[layer].transpose(0, 2, 1).reshape(N_EXPERTS, d),
            jnp.zeros((LANES - N_EXPERT_GROUPS - N_EXPERTS, d), F32)], axis=0)
        wr_t = wr_t.T
        wr_hi = wr_t.astype(BF16)
        wr_t = jnp.stack([wr_hi, (wr_t - wr_hi.astype(F32)).astype(BF16)], axis=0)
        x1, h2, route, counts = _merge_router_call(
            oa, ob, zg, x2d, w_proj_a[layer].astype(BF16), w_proj_b[layer].astype(BF16),
            w_out[layer].astype(BF16), norm_ffn_g[layer:layer + 1], wr_t)

        dest, blk_e, blk_valid, zstart, zlen, n_rows = _routing_plan(route, counts, t)
        xin = _dispatch_call(_blocked(dest, DISPATCH_TOK), zstart, zlen, h2, n_rows)
        yb = _expert_call(blk_e, blk_valid, xin, w1b.reshape(n_exp, d, d_exp), w3b.reshape(n_exp, d, d_exp),
                          w2b.reshape(n_exp, d_exp, d))
        x2d = _combine_call(_blocked(dest, ROW_TM), yb, x1, route.T, p[layer].reshape(t, -1),
                            norm_ple_g[layer:layer + 1], w_ple_gate[layer].astype(BF16),
                            w_ple_proj[layer].astype(BF16), final_norm_g.reshape(1, d),
                            final_norm=layer == depth - 1)
    return x2d.reshape(batch, seq, d)
```

```python
import functools
import math

import jax
import jax.numpy as jnp
from jax import lax
from jax.experimental import pallas as pl
from jax.experimental.pallas import tpu as pltpu

F32 = jnp.float32
BF16 = jnp.bfloat16

BLK = 128
NEG_INF = -1e30
RMS_EPS = 1e-6
SUBLN_EPS = 1e-5
N_BUCKETS = 32
MAX_DISTANCE = 2048
A_HEADS = 8
A_HEAD_DIM = 128
DILATED_GROUPS = ((128, 1), (512, 4), (2048, 16))
N_DGROUPS = 3
B_HEADS = 8
B_HEAD_DIM = 64
N_EXPERT_GROUPS = 4
EXPERTS_PER_GROUP = 8
N_EXPERTS = 32
TOP_K = 2

LANES = 128
VMEM_LIMIT = 56 * 1024 * 1024
ATT_BLK = 256
ATT_HEADS = 2
PERM_BLK = 256
GATE_COLS = 4096
PROJ_TN = 512
PROJ_TM = 2048
ROW_TM = 256
EXP_TM = 256
DISPATCH_TOK = 512
NORM_CHUNK = 128
CAST_SLAB_BYTES = 1 << 20
TOEPLITZ_LANES = 512


def _t5_thresholds():
    max_exact = N_BUCKETS // 2
    out = []
    for k in range(1, N_BUCKETS - max_exact):
        out.append(int(math.ceil(max_exact * (MAX_DISTANCE / max_exact) ** (k / (N_BUCKETS - max_exact)))))
    return tuple(out)


T5_THRESHOLDS = _t5_thresholds()


def _cparams(sem, vmem=VMEM_LIMIT):
    return pltpu.CompilerParams(dimension_semantics=sem, vmem_limit_bytes=vmem)


def _mult(x, m):
    return x if isinstance(x, int) else pl.multiple_of(x, m)


def _sigmoid(x):
    return 0.5 * jnp.tanh(0.5 * x) + 0.5


def _dot(a, b, **kw):
    return jnp.dot(a, b, preferred_element_type=F32, **kw)


def _dot_nt(a, b, **kw):
    return lax.dot_general(a, b, (((1,), (1,)), ((), ())), preferred_element_type=F32, **kw)


def _lam_kernel(q1_ref, k1_ref, q2_ref, k2_ref, o_ref, *, lam_init):
    s1 = jnp.sum(q1_ref[...] * k1_ref[...], axis=-1, keepdims=True)
    s2 = jnp.sum(q2_ref[...] * k2_ref[...], axis=-1, keepdims=True)
    o_ref[...] = jnp.exp(s1) - jnp.exp(s2) + lam_init


def _lam_call(lq1, lk1, lq2, lk2, lam_init):
    return pl.pallas_call(
        functools.partial(_lam_kernel, lam_init=lam_init),
        out_shape=jax.ShapeDtypeStruct((1, 1), F32),
        name="lam",
    )(lq1, lk1, lq2, lk2)


def _bias_tile_kernel(tab_ref, *rest, tq, tk, off_mult, off_add, dil, max_rel, head0, key_major, mult):
    n_pass = max(len(rest) - 2, 0)
    o_ref = rest[n_pass]
    if n_pass:
        col = 0
        for src in rest[:n_pass]:
            rest[-1][:, col:col + src.shape[1]] = src[...].astype(rest[-1].dtype)
            col += src.shape[1]
    h = pl.program_id(0)
    n = pl.program_id(1)
    assert key_major and tq + tk <= TOEPLITZ_LANES
    m = lax.broadcasted_iota(jnp.int32, (8, TOEPLITZ_LANES), 1)
    rel = jnp.where(m < tk, m, m - TOEPLITZ_LANES) + (n * off_mult + off_add)
    dist = rel * dil
    large = jnp.full(m.shape, N_BUCKETS // 2, jnp.int32)
    for thr in T5_THRESHOLDS:
        large = large + jnp.where(dist >= thr, 1, 0)
    bucket = jnp.where(dist < N_BUCKETS // 2, dist, large)
    acc = jnp.zeros(m.shape, F32)
    for b in range(N_BUCKETS):
        acc = jnp.where(bucket == b, tab_ref[b, head0 + h], acc)
    valid = jnp.where(rel >= 0, jnp.where(rel <= max_rel, 1, 0), 0)
    vec = jnp.where(valid == 1, acc * mult, NEG_INF)[0:1, :]
    rows = pltpu.roll(jnp.broadcast_to(vec, (tq, TOEPLITZ_LANES)), 0, 1, stride=1, stride_axis=0)
    o_ref[0, 0] = rows[:, :tk]


def _bias_tiles_call(rel_bias, *, n_heads, n_off, tq, tk, off_mult, off_add, dil, max_rel, head0, name,
                     key_major=False, mult=1.0, concat_cast=()):
    kern = functools.partial(_bias_tile_kernel, tq=tq, tk=tk, off_mult=off_mult, off_add=off_add,
                             dil=dil, max_rel=max_rel, head0=head0, key_major=key_major, mult=mult)
    n_steps = n_heads * n_off
    if concat_cast and concat_cast[0].shape[0] % (n_steps * 16):
        concat_cast = ()
    in_specs = [pl.BlockSpec(memory_space=pltpu.SMEM)]
    out_specs = [pl.BlockSpec((1, 1, tq, tk), lambda h, n: (h, n, 0, 0))]
    out_shape = [jax.ShapeDtypeStruct((n_heads, n_off, tq, tk), F32)]
    if concat_cast:
        rows = concat_cast[0].shape[0]
        cols = sum(w.shape[1] for w in concat_cast)
        in_specs += [pl.BlockSpec((rows // n_steps, w.shape[1]), lambda h, n: (h * n_off + n, 0))
                     for w in concat_cast]
        out_specs.append(pl.BlockSpec((rows // n_steps, cols), lambda h, n: (h * n_off + n, 0)))
        out_shape.append(jax.ShapeDtypeStruct((rows, cols), BF16))
    outs = pl.pallas_call(
        kern,
        grid=(n_heads, n_off),
        in_specs=in_specs,
        out_specs=out_specs,
        out_shape=out_shape,
        compiler_params=_cparams(("parallel", "parallel")),
        name=name,
    )(rel_bias, *concat_cast)
    return outs[0], (outs[1] if concat_cast else None)


def _rmsnorm_rows(x_ref, g_ref, out_ref, eps):
    rows = x_ref.shape[0]
    g = g_ref[...]

    def body(c, carry):
        r0 = pl.multiple_of(c * NORM_CHUNK, NORM_CHUNK)
        x = x_ref[pl.ds(r0, NORM_CHUNK), :]
        ms = jnp.mean(x * x, axis=-1, keepdims=True)
        out_ref[pl.ds(r0, NORM_CHUNK), :] = ((x * lax.rsqrt(ms + eps)) * g).astype(out_ref.dtype)
        return carry

    lax.fori_loop(0, rows // NORM_CHUNK, body, 0)


def _rmsnorm_val(x, g, eps):
    ms = jnp.mean(x * x, axis=-1, keepdims=True)
    return (x * lax.rsqrt(ms + eps)) * g


def _inproj_kernel(x_ref, g_ref, w_ref, o_ref, h_ref, *, n_gate_blocks):
    j = pl.program_id(1)

    @pl.when(j == 0)
    def _():
        _rmsnorm_rows(x_ref, g_ref, h_ref, RMS_EPS)

    acc = _dot(h_ref[...], w_ref[...])

    @pl.when(j < n_gate_blocks)
    def _():
        o_ref[...] = _sigmoid(acc).astype(o_ref.dtype)

    @pl.when(j >= n_gate_blocks)
    def _():
        o_ref[...] = acc.astype(o_ref.dtype)


def _inproj_call(x2d, g, w_cat):
    t, d = x2d.shape
    n = w_cat.shape[1]
    tm = min(PROJ_TM, t)
    return pl.pallas_call(
        functools.partial(_inproj_kernel, n_gate_blocks=GATE_COLS // PROJ_TN),
        grid=(t // tm, n // PROJ_TN),
        in_specs=[
            pl.BlockSpec((tm, d), lambda i, j: (i, 0)),
            pl.BlockSpec((1, d), lambda i, j: (0, 0)),
            pl.BlockSpec((d, PROJ_TN), lambda i, j: (0, j)),
        ],
        out_specs=pl.BlockSpec((tm, PROJ_TN), lambda i, j: (i, j)),
        out_shape=jax.ShapeDtypeStruct((t, n), BF16),
        scratch_shapes=[pltpu.VMEM((tm, d), BF16)],
        compiler_params=_cparams(("parallel", "arbitrary")),
        name="inproj",
    )(x2d, g, w_cat)


def _diff_attn_kernel(lam_ref, q_ref, k_ref, v_ref, bias_ref, g_ref, *rest, out_scale, n_blk, n_cast):
    cast_in = rest[:n_cast]
    o_ref = rest[n_cast]
    cast_out = rest[n_cast + 1:2 * n_cast + 1]
    vt_ref, tbuf_a, tbuf_b, acc = rest[2 * n_cast + 1:]
    for src, dst in zip(cast_in, cast_out):
        dst[...] = src[...].astype(dst.dtype)
    qi = pl.program_id(2)
    width = 2 * A_HEAD_DIM
    n_map = 2 * ATT_HEADS
    c = (A_HEAD_DIM ** -0.5) * math.log2(math.e)

    @pl.when(qi == 0)
    def _():
        def transpose_block(b, carry):
            r0 = pl.multiple_of(b * ATT_BLK, ATT_BLK)
            for hp in range(ATT_HEADS):
                vb = v_ref[pl.ds(r0, ATT_BLK), hp * width:(hp + 1) * width]
                vt_ref[hp, b] = vb.astype(F32).T.astype(BF16)
            return carry

        lax.fori_loop(0, n_blk, transpose_block, 0)

    qs = [q_ref[:, mi * A_HEAD_DIM:(mi + 1) * A_HEAD_DIM] for mi in range(n_map)]
    acc[...] = jnp.zeros_like(acc)

    def scores(ki, dst):
        k0 = pl.multiple_of(ki * ATT_BLK, ATT_BLK)
        for mi in range(n_map):
            kb = k_ref[pl.ds(k0, ATT_BLK), mi * A_HEAD_DIM:(mi + 1) * A_HEAD_DIM]
            dst[mi] = _dot_nt(kb, qs[mi]) + bias_ref[mi // 2, qi - ki]

    def softmax(t, m, l):
        m_new = jnp.maximum(m, jnp.max(t, axis=0, keepdims=True))
        alpha = jnp.exp2((m - m_new) * c)
        p = jnp.exp2((t - m_new) * c)
        return m_new, alpha * l + jnp.sum(p, axis=0, keepdims=True), alpha, p.astype(BF16)

    scores(0, tbuf_a)

    def step(ki, carry, cur, nxt):
        ms, ls = carry
        stats = [softmax(cur[mi], ms[mi], ls[mi]) for mi in range(n_map)]
        scores(jnp.minimum(ki + 1, qi), nxt)
        for mi in range(n_map):
            acc[mi] = acc[mi] * stats[mi][2] + _dot(vt_ref[mi // 2, ki], stats[mi][3])
        return tuple(st[0] for st in stats), tuple(st[1] for st in stats)

    def body(ki, carry):
        return lax.cond(lax.bitwise_and(ki, 1) == 0,
                        lambda cr: step(ki, cr, tbuf_a, tbuf_b),
                        lambda cr: step(ki, cr, tbuf_b, tbuf_a), carry)

    minf = (jnp.full((1, ATT_BLK), -jnp.inf, F32),) * n_map
    zero = (jnp.zeros((1, ATT_BLK), F32),) * n_map
    _, ls = lax.fori_loop(0, qi + 1, body, (minf, zero))
    for hp in range(ATT_HEADS):
        w = acc[2 * hp] / ls[2 * hp] - lam_ref[0, 0] * (acc[2 * hp + 1] / ls[2 * hp + 1])
        ms = jnp.mean(w * w, axis=0, keepdims=True)
        y = ((w * lax.rsqrt(ms + SUBLN_EPS)) * g_ref[...]) * out_scale
        o_ref[:, hp * width:(hp + 1) * width] = y.T.astype(o_ref.dtype)


def _diff_attn_call(zg, lam, bias_tiles, subln_g, passengers, *, batch, seq, out_scale):
    t = batch * seq
    nq = seq // ATT_BLK
    width = 2 * A_HEAD_DIM
    blk_w = ATT_HEADS * width
    q_col = GATE_COLS // blk_w
    k_col = q_col + A_HEADS // ATT_HEADS
    v_col = k_col + A_HEADS // ATT_HEADS
    n_steps = (A_HEADS // ATT_HEADS) * batch * nq
    riders = [w for w in passengers
              if w.shape[0] % (n_steps * 16) == 0 and w.size * 4 // n_steps <= CAST_SLAB_BYTES]

    def slab_spec(w):
        return pl.BlockSpec((w.shape[0] // n_steps, w.shape[1]), lambda h, b, i: ((h * batch + b) * nq + i, 0))

    outs = pl.pallas_call(
        functools.partial(_diff_attn_kernel, out_scale=out_scale, n_blk=nq, n_cast=len(riders)),
        grid=(A_HEADS // ATT_HEADS, batch, nq),
        in_specs=[
            pl.BlockSpec(memory_space=pltpu.SMEM),
            pl.BlockSpec((ATT_BLK, blk_w), lambda h, b, i: (b * nq + i, q_col + h)),
            pl.BlockSpec((seq, blk_w), lambda h, b, i: (b, k_col + h)),
            pl.BlockSpec((seq, blk_w), lambda h, b, i: (b, v_col + h)),
            pl.BlockSpec((ATT_HEADS, nq, ATT_BLK, ATT_BLK), lambda h, b, i: (h, 0, 0, 0)),
            pl.BlockSpec((width, 1), lambda h, b, i: (0, 0)),
        ] + [slab_spec(w) for w in riders],
        out_specs=[pl.BlockSpec((ATT_BLK, blk_w), lambda h, b, i: (b * nq + i, h))]
        + [slab_spec(w) for w in riders],
        out_shape=[jax.ShapeDtypeStruct((t, A_HEADS * width), BF16)]
        + [jax.ShapeDtypeStruct(w.shape, BF16) for w in riders],
        scratch_shapes=[pltpu.VMEM((ATT_HEADS, nq, width, ATT_BLK), BF16),
                        pltpu.VMEM((2 * ATT_HEADS, ATT_BLK, ATT_BLK), F32),
                        pltpu.VMEM((2 * ATT_HEADS, ATT_BLK, ATT_BLK), F32),
                        pltpu.VMEM((2 * ATT_HEADS, width, ATT_BLK), F32)],
        compiler_params=_cparams(("parallel", "parallel", "arbitrary")),
        name="diff_attn",
    )(lam, zg, zg, zg, bias_tiles, subln_g.reshape(width, 1), *riders)
    cast = iter(outs[1:])
    return outs[0], [next(cast) if any(w is r for r in riders) else w.astype(BF16) for w in passengers]


def _perm_matrix(dil, inverse):
    w = PERM_BLK // dil
    shift = w.bit_length() - 1
    a = lax.broadcasted_iota(jnp.int32, (PERM_BLK, PERM_BLK), 0)
    b = lax.broadcasted_iota(jnp.int32, (PERM_BLK, PERM_BLK), 1)
    dst, src = (b, a) if inverse else (a, b)
    c = lax.shift_right_logical(dst, shift)
    ll = lax.bitwise_and(dst, w - 1)
    return jnp.where(src == ll * dil + c, jnp.float32(1), jnp.float32(0))


def _deinterleave(pairs, perm, dil, seq):
    w = PERM_BLK // dil
    stream_len = seq // dil

    def body(b8, carry):
        r0 = pl.multiple_of(b8 * PERM_BLK, PERM_BLK)
        ys = [_dot(perm, src[pl.ds(r0, PERM_BLK), :]).astype(dst.dtype) for src, dst in pairs]
        for y, (_, dst) in zip(ys, pairs):
            for c in range(dil):
                d0 = pl.multiple_of(c * stream_len + b8 * w, w)
                dst[pl.ds(d0, w), :] = y[c * w:(c + 1) * w, :]
        return carry

    lax.fori_loop(0, seq // PERM_BLK, body, 0)


def _interleave(items, dil, seq):
    w = PERM_BLK // dil
    stream_len = seq // dil

    def body(b8, carry):
        for src, _, stack, _, _ in items:
            for c in range(dil):
                s0 = pl.multiple_of(c * stream_len + b8 * w, w)
                stack[c * w:(c + 1) * w, :] = src[pl.ds(s0, w), :]
        outs = [_dot(pinv, stack[...], **kw) for _, _, stack, pinv, kw in items]
        r0 = pl.multiple_of(b8 * PERM_BLK, PERM_BLK)
        for out, (_, dst, _, _, _) in zip(outs, items):
            dst[pl.ds(r0, PERM_BLK), :] = out.astype(dst.dtype)
        return carry

    lax.fori_loop(0, seq // PERM_BLK, body, 0)


def _transpose_blocks(src_ref, vt_ref, seq):
    def body(blk, carry):
        r0 = pl.multiple_of(blk * BLK, BLK)
        vt_ref[blk] = src_ref[pl.ds(r0, BLK), :].astype(F32).T.astype(vt_ref.dtype)
        return carry

    lax.fori_loop(0, seq // BLK, body, 0)


def _dil_windows(q_src, k_src, vt_ref, o_dst, lse_dst, bias_ref, ot_ref, lt_ref, windows):
    scale = B_HEAD_DIM ** -0.5
    c = scale * math.log2(math.e)
    tiles = []
    for rq, nk in windows:
        rk = rq - (nk - BLK)
        for hh in range(B_HEADS):
            cols = slice(hh * B_HEAD_DIM, (hh + 1) * B_HEAD_DIM)
            qh = q_src[pl.ds(rq, BLK), cols]
            kh = k_src[pl.ds(rk, nk), cols]
            tiles.append(_dot_nt(kh, qh) + bias_ref[0, hh, 2 * BLK - nk:, :])
    for wi, (rq, nk) in enumerate(windows):
        qb = rq // BLK if isinstance(rq, int) else lax.shift_right_logical(rq, BLK.bit_length() - 1)
        for hh in range(B_HEADS):
            cols = slice(hh * B_HEAD_DIM, (hh + 1) * B_HEAD_DIM)
            t = tiles[wi * B_HEADS + hh]
            m = jnp.max(t, axis=0, keepdims=True)
            p = jnp.exp2((t - m) * c)
            den = jnp.sum(p, axis=0, keepdims=True)
            if nk == BLK:
                vth = vt_ref[qb, cols, :]
            else:
                vth = jnp.concatenate([vt_ref[qb - 1, cols, :], vt_ref[qb, cols, :]], axis=1)
            ot_ref[wi, cols, :] = _dot(vth, p.astype(BF16)) / den
            lt_ref[wi, hh:hh + 1, :] = m * scale + jnp.log(den)
        o_dst[pl.ds(rq, BLK), :] = ot_ref[wi].T.astype(o_dst.dtype)
        lse_dst[pl.ds(rq, BLK), :] = lt_ref[wi].T


def _dil_streams(q_src, k_src, vt_ref, o_dst, lse_dst, bias_ref, ot_ref, lt_ref, dil, seq):
    stream_len = seq // dil
    nq = stream_len // BLK

    def run(windows):
        _dil_windows(q_src, k_src, vt_ref, o_dst, lse_dst, bias_ref, ot_ref, lt_ref, windows)

    if nq == 1:
        def stream_pair(c2, carry):
            base = pl.multiple_of(c2 * (2 * stream_len), BLK)
            run([(base, BLK), (pl.multiple_of(base + stream_len, BLK), BLK)])
            return carry

        lax.fori_loop(0, dil // 2, stream_pair, 0)
        return

    def stream(c, carry):
        base = _mult(c * stream_len, BLK)
        n_full = nq - 1
        if n_full % 2 == 1:
            run([(base, BLK), (_mult(base + n_full * BLK, BLK), 2 * BLK)])
            n_full -= 1
        else:
            run([(base, BLK)])

        def qpair(j, carry2):
            rq = pl.multiple_of(base + (1 + 2 * j) * BLK, BLK)
            run([(rq, 2 * BLK), (pl.multiple_of(rq + BLK, BLK), 2 * BLK)])
            return carry2

        if n_full:
            lax.fori_loop(0, n_full // 2, qpair, 0)
        return carry

    if dil == 1:
        stream(0, 0)
    else:
        lax.fori_loop(0, dil, stream, 0)


def _head_expand(x, expand):
    hi = x.astype(BF16)
    lo = (x - hi.astype(F32)).astype(BF16)
    return _dot(jnp.concatenate([hi, lo], axis=1), expand)


def _dil_merge(o_src, lse_src, oacc, mrun, lrun, out_ref, first, last, seq):
    er = lax.broadcasted_iota(jnp.int32, (2 * LANES, B_HEADS * B_HEAD_DIM), 0)
    ec = lax.broadcasted_iota(jnp.int32, (2 * LANES, B_HEADS * B_HEAD_DIM), 1)
    expand = jnp.where(lax.shift_right_logical(ec, B_HEAD_DIM.bit_length() - 1) == lax.bitwise_and(er, LANES - 1),
                       jnp.float32(1), jnp.float32(0)).astype(BF16)

    def body(ch, carry):
        r0 = pl.multiple_of(ch * ROW_TM, ROW_TM)
        rows = pl.ds(r0, ROW_TM)
        lse = lse_src[rows, :]
        if first:
            mrun[rows, :] = lse
            lrun[rows, :] = jnp.ones_like(lse)
            oacc[rows, :] = o_src[rows, :].astype(F32)
            return carry
        m_old = mrun[rows, :]
        m_new = jnp.maximum(m_old, lse)
        a = jnp.exp(m_old - m_new)
        bw = jnp.exp(lse - m_new)
        l_new = lrun[rows, :] * a + bw
        if last:
            a = a / l_new
            bw = bw / l_new
        val = oacc[rows, :] * _head_expand(a, expand) + o_src[rows, :].astype(F32) * _head_expand(bw, expand)
        if last:
            out_ref[rows, :] = val.astype(out_ref.dtype)
        else:
            mrun[rows, :] = m_new
            lrun[rows, :] = l_new
            oacc[rows, :] = val
        return carry

    lax.fori_loop(0, seq // ROW_TM, body, 0)


def _dilated_kernel(q_ref, k_ref, v_ref, bias_ref, out_ref,
                    qs, ks, vs, vt, os_, lses, otok, lsetok, ostack, lstack, ot, lt, oacc, mrun, lrun,
                    *, seq):
    g = pl.program_id(1)

    @pl.when(g == 0)
    def _():
        lt[...] = jnp.zeros_like(lt)

    for gi, (_, dil) in enumerate(DILATED_GROUPS):

        @pl.when(g == gi)
        def _(gi=gi, dil=dil):
            first = gi == 0
            last = gi == N_DGROUPS - 1
            if dil == 1:
                _transpose_blocks(v_ref, vt, seq)
                _dil_streams(q_ref, k_ref, vt, otok, lsetok, bias_ref, ot, lt, 1, seq)
            else:
                perm = _perm_matrix(dil, inverse=False).astype(BF16)
                _deinterleave([(q_ref, qs), (k_ref, ks), (v_ref, vs)], perm, dil, seq)
                _transpose_blocks(vs, vt, seq)
                _dil_streams(qs, ks, vt, os_, lses, bias_ref, ot, lt, dil, seq)
                perm_inv = _perm_matrix(dil, inverse=True)
                _interleave([(os_, otok, ostack, perm_inv.astype(BF16), {}),
                             (lses, lsetok, lstack, perm_inv, dict(precision=lax.Precision.HIGHEST))],
                            dil, seq)
            _dil_merge(otok, lsetok, oacc, mrun, lrun, out_ref, first, last, seq)


def _dilated_call(zg, bias_tiles, *, batch, seq):
    width = B_HEADS * B_HEAD_DIM
    q_col = (GATE_COLS + 3 * A_HEADS * 2 * A_HEAD_DIM) // width
    k_col = q_col + N_DGROUPS
    v_col = k_col + N_DGROUPS
    return pl.pallas_call(
        functools.partial(_dilated_kernel, seq=seq),
        grid=(batch, N_DGROUPS),
        in_specs=[
            pl.BlockSpec((seq, width), lambda b, g: (b, q_col + g)),
            pl.BlockSpec((seq, width), lambda b, g: (b, k_col + g)),
            pl.BlockSpec((seq, width), lambda b, g: (b, v_col + g)),
            pl.BlockSpec((1, B_HEADS, 2 * BLK, BLK), lambda b, g: (g, 0, 0, 0)),
        ],
        out_specs=pl.BlockSpec((seq, width), lambda b, g: (b, 0)),
        out_shape=jax.ShapeDtypeStruct((batch * seq, width), BF16),
        scratch_shapes=[
            pltpu.VMEM((seq, width), BF16), pltpu.VMEM((seq, width), BF16), pltpu.VMEM((seq, width), BF16),
            pltpu.VMEM((seq // BLK, width, BLK), BF16),
            pltpu.VMEM((seq, width), BF16), pltpu.VMEM((seq, LANES), F32),
            pltpu.VMEM((seq, width), BF16), pltpu.VMEM((seq, LANES), F32),
            pltpu.VMEM((PERM_BLK, width), BF16), pltpu.VMEM((PERM_BLK, LANES), F32),
            pltpu.VMEM((2, width, BLK), F32), pltpu.VMEM((2, LANES, BLK), F32),
            pltpu.VMEM((seq, width), F32), pltpu.VMEM((seq, LANES), F32), pltpu.VMEM((seq, LANES), F32),
        ],
        compiler_params=_cparams(("parallel", "arbitrary")),
        name="dilated_attn",
    )(zg, zg, zg, bias_tiles)


def _merge_router_kernel(oa_ref, ob_ref, gate_ref, x_ref, pa_ref, pb_ref, wo_ref, g_ref, wr_ref,
                         x1_ref, h2_ref, route_ref, cnt_ref, run_ref):
    tm, d = x_ref.shape

    @pl.when(pl.program_id(0) == 0)
    def _():
        run_ref[...] = jnp.zeros_like(run_ref)

    a = _dot(oa_ref[...], pa_ref[...])
    bm = _dot(ob_ref[...], pb_ref[...])
    merged = gate_ref[:, :d].astype(F32) * a + gate_ref[:, d:].astype(F32) * bm
    x1 = x_ref[...] + _dot(merged.astype(BF16), wo_ref[...])
    x1_ref[...] = x1
    h2 = _rmsnorm_val(x1, g_ref[...], RMS_EPS)
    h2_ref[...] = h2

    h_hi = h2.astype(BF16)
    h_lo = (h2 - h_hi.astype(F32)).astype(BF16)
    lt = (_dot(h_hi, wr_ref[0]) + (_dot(h_hi, wr_ref[1]) + _dot(h_lo, wr_ref[0]))).T
    coarse = [lt[i:i + 1, :] for i in range(N_EXPERT_GROUPS)]
    best = coarse[0]
    gsel = jnp.zeros((1, tm), jnp.int32)
    for i in range(1, N_EXPERT_GROUPS):
        upd = coarse[i] > best
        gsel = jnp.where(upd, i, gsel)
        best = jnp.where(upd, coarse[i], best)
    den = jnp.exp(coarse[0] - best)
    for i in range(1, N_EXPERT_GROUPS):
        den = den + jnp.exp(coarse[i] - best)
    pg = 1.0 / den

    fine = []
    for k in range(EXPERTS_PER_GROUP):
        f = lt[N_EXPERT_GROUPS + k:N_EXPERT_GROUPS + k + 1, :]
        for gi in range(1, N_EXPERT_GROUPS):
            r = N_EXPERT_GROUPS + gi * EXPERTS_PER_GROUP + k
            f = jnp.where(gsel == gi, lt[r:r + 1, :], f)
        fine.append(f)
    v0 = fine[0]
    i0 = jnp.zeros((1, tm), jnp.int32)
    for k in range(1, EXPERTS_PER_GROUP):
        upd = fine[k] > v0
        i0 = jnp.where(upd, k, i0)
        v0 = jnp.where(upd, fine[k], v0)
    v1 = jnp.full((1, tm), -jnp.inf, F32)
    i1 = jnp.zeros((1, tm), jnp.int32)
    for k in range(EXPERTS_PER_GROUP):
        upd = jnp.where(i0 != k, jnp.where(fine[k] > v1, 1, 0), 0) == 1
        i1 = jnp.where(upd, k, i1)
        v1 = jnp.where(upd, fine[k], v1)
    e1w = jnp.exp(v1 - v0)
    gate0 = pg * (1.0 / (1.0 + e1w))
    gate1 = pg * (e1w / (1.0 + e1w))
    e0 = gsel * EXPERTS_PER_GROUP + i0
    e1 = gsel * EXPERTS_PER_GROUP + i1

    eidx = lax.broadcasted_iota(jnp.int32, (N_EXPERTS, tm), 0)
    oh0 = jnp.where(eidx == e0, jnp.float32(1), jnp.float32(0))
    oh1 = jnp.where(eidx == e1, jnp.float32(1), jnp.float32(0))
    ta = lax.broadcasted_iota(jnp.int32, (tm, tm), 0)
    tb = lax.broadcasted_iota(jnp.int32, (tm, tm), 1)
    before = jnp.where(ta < tb, jnp.float32(1), jnp.float32(0)).astype(BF16)
    pre0 = _dot(oh0.astype(BF16), before)
    pre1 = _dot(oh1.astype(BF16), before)
    run = run_ref[:, 0:1]
    tot0 = jnp.sum(oh0, axis=1, keepdims=True)
    tot1 = jnp.sum(oh1, axis=1, keepdims=True)
    rank0 = jnp.sum(oh0 * (run + pre0), axis=0, keepdims=True)
    rank1 = jnp.sum(oh1 * (run + tot0 + pre1), axis=0, keepdims=True)
    new_run = jnp.broadcast_to(run + tot0 + tot1, run_ref.shape)
    run_ref[...] = new_run
    cnt_ref[...] = new_run

    route_ref[0:1, :] = e0.astype(F32)
    route_ref[1:2, :] = e1.astype(F32)
    route_ref[2:3, :] = gate0
    route_ref[3:4, :] = gate1
    route_ref[4:5, :] = rank0
    route_ref[5:6, :] = rank1
    route_ref[6:8, :] = jnp.zeros((2, tm), F32)


def _merge_router_call(oa, ob, zg, x2d, pa, pb, wo, g, wr_t):
    t, d = x2d.shape
    tm = ROW_TM
    const = dict(pipeline_mode=pl.Buffered(1))
    return pl.pallas_call(
        _merge_router_kernel,
        grid=(t // tm,),
        in_specs=[
            pl.BlockSpec((tm, oa.shape[1]), lambda i: (i, 0)),
            pl.BlockSpec((tm, ob.shape[1]), lambda i: (i, 0)),
            pl.BlockSpec((tm, GATE_COLS), lambda i: (i, 0)),
            pl.BlockSpec((tm, d), lambda i: (i, 0)),
            pl.BlockSpec(pa.shape, lambda i: (0, 0), **const),
            pl.BlockSpec(pb.shape, lambda i: (0, 0), **const),
            pl.BlockSpec(wo.shape, lambda i: (0, 0), **const),
            pl.BlockSpec((1, d), lambda i: (0, 0)),
            pl.BlockSpec(wr_t.shape, lambda i: (0, 0, 0), **const),
        ],
        out_specs=[
            pl.BlockSpec((tm, d), lambda i: (i, 0)),
            pl.BlockSpec((tm, d), lambda i: (i, 0)),
            pl.BlockSpec((8, tm), lambda i: (0, i)),
            pl.BlockSpec((N_EXPERTS, LANES), lambda i: (0, 0)),
        ],
        out_shape=[
            jax.ShapeDtypeStruct((t, d), F32),
            jax.ShapeDtypeStruct((t, d), F32),
            jax.ShapeDtypeStruct((8, t), F32),
            jax.ShapeDtypeStruct((N_EXPERTS, LANES), F32),
        ],
        scratch_shapes=[pltpu.VMEM((N_EXPERTS, LANES), F32)],
        compiler_params=_cparams(("arbitrary",)),
        name="merge_router",
    )(oa, ob, zg, x2d, pa, pb, wo, g, wr_t)


def _row_copy(src, s_row, dst, d_row, sem):
    return pltpu.make_async_copy(src.at[pl.ds(s_row, 1)], dst.at[pl.ds(d_row, 1)], sem)


def _dispatch_kernel(dest_ref, zstart_ref, zlen_ref, h_ref, x_hbm, zero_ref, sem, zsem):
    step = pl.program_id(0)

    def issue(t, carry):
        for k in range(TOP_K):
            _row_copy(h_ref, t, x_hbm, dest_ref[0, k, t], sem.at[k]).start()
        return carry

    lax.fori_loop(0, DISPATCH_TOK, issue, 0, unroll=8)

    @pl.when(step == 0)
    def _():
        zero_ref[...] = jnp.zeros_like(zero_ref)

        def group_copy(r8):
            return pltpu.make_async_copy(zero_ref, x_hbm.at[pl.ds(pl.multiple_of(r8, 8), 8)], zsem.at[1])

        def segment(e, wait):
            start = zstart_ref[e]
            n_head = jnp.minimum(lax.bitwise_and(-start, 7), zlen_ref[e])
            n_group = lax.shift_right_logical(zlen_ref[e] - n_head, 3)

            def head(r, carry):
                cp = _row_copy(zero_ref, 0, x_hbm, 0 if wait else start + r, zsem.at[0])
                cp.wait() if wait else cp.start()
                return carry

            def group(j, carry):
                cp = group_copy(0 if wait else start + n_head + 8 * j)
                cp.wait() if wait else cp.start()
                return carry

            lax.fori_loop(0, n_head, head, 0)
            lax.fori_loop(0, n_group, group, 0)

        lax.fori_loop(0, N_EXPERTS + 1, lambda e, c: (segment(e, False), c)[1], 0)
        lax.fori_loop(0, N_EXPERTS + 1, lambda e, c: (segment(e, True), c)[1], 0)

    for k in range(TOP_K):
        pltpu.make_async_copy(h_ref, x_hbm.at[pl.ds(0, DISPATCH_TOK)], sem.at[k]).wait()


def _dispatch_call(dest_blocks, zstart, zlen, h2, n_rows):
    t, d = h2.shape
    return pl.pallas_call(
        _dispatch_kernel,
        grid=(t // DISPATCH_TOK,),
        in_specs=[
            pl.BlockSpec((1, TOP_K, DISPATCH_TOK), lambda i: (i, 0, 0), memory_space=pltpu.SMEM),
            pl.BlockSpec(memory_space=pltpu.SMEM),
            pl.BlockSpec(memory_space=pltpu.SMEM),
            pl.BlockSpec((DISPATCH_TOK, d), lambda i: (i, 0)),
        ],
        out_specs=pl.BlockSpec(memory_space=pl.ANY),
        out_shape=jax.ShapeDtypeStruct((n_rows, d), F32),
        scratch_shapes=[pltpu.VMEM((8, d), F32), pltpu.SemaphoreType.DMA((TOP_K,)),
                        pltpu.SemaphoreType.DMA((2,))],
        compiler_params=_cparams(("arbitrary",)),
        name="dispatch",
    )(dest_blocks, zstart, zlen, h2)


def _expert_kernel(blk_e_ref, blk_valid_ref, x_ref, w1_ref, w3_ref, w2_ref, y_ref):
    i = pl.program_id(0)

    @pl.when(blk_valid_ref[i] == 1)
    def _():
        x = x_ref[...].astype(BF16)
        a = _dot(x, w1_ref[0])
        b = _dot(x, w3_ref[0])
        hdn = (a * _sigmoid(a)) * b
        y_ref[...] = _dot(hdn.astype(BF16), w2_ref[0])

    @pl.when(blk_valid_ref[i] == 0)
    def _():
        y_ref[...] = jnp.zeros_like(y_ref)


def _expert_call(blk_e, blk_valid, xin, w1, w3, w2):
    n_rows, d = xin.shape
    de = w1.shape[2]
    grid_spec = pltpu.PrefetchScalarGridSpec(
        num_scalar_prefetch=2,
        grid=(n_rows // EXP_TM,),
        in_specs=[
            pl.BlockSpec((EXP_TM, d), lambda i, be, bv: (i, 0)),
            pl.BlockSpec((1, d, de), lambda i, be, bv: (be[i], 0, 0)),
            pl.BlockSpec((1, d, de), lambda i, be, bv: (be[i], 0, 0)),
            pl.BlockSpec((1, de, d), lambda i, be, bv: (be[i], 0, 0)),
        ],
        out_specs=pl.BlockSpec((EXP_TM, d), lambda i, be, bv: (i, 0)),
    )
    return pl.pallas_call(
        _expert_kernel,
        grid_spec=grid_spec,
        out_shape=jax.ShapeDtypeStruct((n_rows, d), F32),
        compiler_params=_cparams(("arbitrary",)),
        name="experts",
    )(blk_e, blk_valid, xin, w1, w3, w2)


def _combine_kernel(dest_ref, dnext_ref, y_hbm, x1_ref, gt_ref, p_ref, gple_ref, wg_ref, wp_ref, gfin_ref,
                    o_ref, ybuf, sem, *, final_norm, n_steps):
    tm = x1_ref.shape[0]
    i = pl.program_id(0)
    slot = lax.bitwise_and(i, 1)

    def gather(d_ref, s):
        def issue(t, carry):
            for k in range(TOP_K):
                _row_copy(y_hbm, d_ref[0, k, t], ybuf.at[s, k], t, sem.at[s, k]).start()
            return carry

        lax.fori_loop(0, tm, issue, 0, unroll=8)

    @pl.when(i == 0)
    def _():
        gather(dest_ref, 0)

    @pl.when(i + 1 < n_steps)
    def _():
        gather(dnext_ref, 1 - slot)

    pp = _dot(p_ref[...].astype(BF16), wp_ref[...])

    for k in range(TOP_K):
        pltpu.make_async_copy(y_hbm.at[pl.ds(0, tm)], ybuf.at[slot, k], sem.at[slot, k]).wait()
    y = ybuf[slot, 0] * gt_ref[:, 2:3] + ybuf[slot, 1] * gt_ref[:, 3:4]
    x2 = x1_ref[...] + y
    hn = _rmsnorm_val(x2, gple_ref[...], RMS_EPS)
    gate = _sigmoid(_dot(hn.astype(BF16), wg_ref[...]))
    x3 = x2 + gate * pp
    if final_norm:
        o_ref[...] = _rmsnorm_val(x3, gfin_ref[...], RMS_EPS)
    else:
        o_ref[...] = x3


def _combine_call(dest_blocks, yb, x1, gates_t, p2d, g_ple, w_gate, w_proj, g_fin, final_norm):
    t, d = x1.shape
    tm = ROW_TM
    const = dict(pipeline_mode=pl.Buffered(1))
    return pl.pallas_call(
        functools.partial(_combine_kernel, final_norm=final_norm, n_steps=t // tm),
        grid=(t // tm,),
        in_specs=[
            pl.BlockSpec((1, TOP_K, tm), lambda i: (i, 0, 0), memory_space=pltpu.SMEM),
            pl.BlockSpec((1, TOP_K, tm), lambda i: (jnp.minimum(i + 1, t // tm - 1), 0, 0),
                         memory_space=pltpu.SMEM),
            pl.BlockSpec(memory_space=pl.ANY),
            pl.BlockSpec((tm, d), lambda i: (i, 0)),
            pl.BlockSpec((tm, 8), lambda i: (i, 0)),
            pl.BlockSpec((tm, p2d.shape[1]), lambda i: (i, 0)),
            pl.BlockSpec((1, d), lambda i: (0, 0)),
            pl.BlockSpec(w_gate.shape, lambda i: (0, 0), **const),
            pl.BlockSpec(w_proj.shape, lambda i: (0, 0), **const),
            pl.BlockSpec((1, d), lambda i: (0, 0)),
        ],
        out_specs=pl.BlockSpec((tm, d), lambda i: (i, 0)),
        out_shape=jax.ShapeDtypeStruct((t, d), F32),
        scratch_shapes=[pltpu.VMEM((2, TOP_K, tm, d), F32), pltpu.SemaphoreType.DMA((2, TOP_K))],
        compiler_params=_cparams(("arbitrary",)),
        name="combine",
    )(dest_blocks, dest_blocks, yb, x1, gates_t, p2d, g_ple, w_gate, w_proj, g_fin)


def _routing_plan(route, counts_f, t):
    counts = counts_f[:, 0].astype(jnp.int32)
    pcounts = ((counts + EXP_TM - 1) // EXP_TM) * EXP_TM
    pends = jnp.cumsum(pcounts)
    pstarts = pends - pcounts
    eid = route[0:2].astype(jnp.int32)
    rank = route[4:6].astype(jnp.int32)
    onehot = eid[..., None] == jnp.arange(N_EXPERTS, dtype=jnp.int32)
    dest = jnp.sum(jnp.where(onehot, pstarts, 0), axis=-1) + rank
    n_rows = TOP_K * t + N_EXPERTS * EXP_TM
    n_blocks = n_rows // EXP_TM
    blk_row = jnp.arange(n_blocks, dtype=jnp.int32) * EXP_TM
    blk_e = jnp.sum((pends[None, :] <= blk_row[:, None]).astype(jnp.int32), axis=1)
    blk_valid = (blk_row < pends[-1]).astype(jnp.int32)
    last_e = jnp.sum((pends <= pends[-1] - 1).astype(jnp.int32))
    blk_e = jnp.minimum(blk_e, last_e)
    zstart = jnp.concatenate([pstarts + counts, pends[-1:]]).astype(jnp.int32)
    zlen = jnp.concatenate([pcounts - counts, n_rows - pends[-1:]]).astype(jnp.int32)
    return dest, blk_e, blk_valid, zstart, zlen, n_rows


def _blocked(dest, tok_per_block):
    t = dest.shape[1]
    return dest.reshape(TOP_K, t // tok_per_block, tok_per_block).transpose(1, 0, 2)


def kernel(x, p, rel_bias, norm_mix_g, w_in, w_gate, lambda_q1, lambda_k1, lambda_q2, lambda_k2, subln_g,
           w_proj_a, w_proj_b, w_out, norm_ffn_g, w_coarse, w_fine, w1, w3, w2, norm_ple_g, w_ple_gate,
           w_ple_proj, final_norm_g):
    batch, seq, d = x.shape
    depth = w_in.shape[0]
    t = batch * seq
    assert seq % PERM_BLK == 0 and t % min(PROJ_TM, t) == 0
    assert all(seq // dil >= BLK and win // dil == BLK for win, dil in DILATED_GROUPS)

    nq = seq // ATT_BLK
    bias_a, w_cat0 = _bias_tiles_call(rel_bias, n_heads=A_HEADS, n_off=nq, tq=ATT_BLK, tk=ATT_BLK,
                                      off_mult=ATT_BLK, off_add=0, dil=1, max_rel=seq, head0=0, name="bias_diff",
                                      key_major=True, mult=A_HEAD_DIM ** 0.5, concat_cast=(w_gate[0], w_in[0]))
    bias_b = jnp.concatenate([
        _bias_tiles_call(rel_bias, n_heads=B_HEADS, n_off=1, tq=2 * BLK, tk=BLK, off_mult=0, off_add=BLK,
                         dil=dil, max_rel=win // dil, head0=A_HEADS + gi * B_HEADS, name="bias_dil%d" % gi,
                         key_major=True, mult=B_HEAD_DIM ** 0.5)[0].reshape(1, B_HEADS, 2 * BLK, BLK)
        for gi, (win, dil) in enumerate(DILATED_GROUPS)], axis=0)

    x2d = x.reshape(t, d)
    for layer in range(depth):
        lam_init = 0.8 - 0.6 * math.exp(-0.3 * layer)
        lam = _lam_call(lambda_q1[layer:layer + 1], lambda_k1[layer:layer + 1],
                        lambda_q2[layer:layer + 1], lambda_k2[layer:layer + 1], lam_init)
        if layer == 0 and w_cat0 is not None:
            w_cat = w_cat0
        else:
            w_cat = jnp.concatenate([w_gate[layer], w_in[layer]], axis=1).astype(BF16)
        zg = _inproj_call(x2d, norm_mix_g[layer:layer + 1], w_cat)
        n_exp, _, d_exp = w1[layer].shape
        oa, (w1b, w3b, w2b) = _diff_attn_call(
            zg, lam, bias_a, subln_g[layer:layer + 1],
            [w1[layer].reshape(n_exp * d, d_exp), w3[layer].reshape(n_exp * d, d_exp),
             w2[layer].reshape(n_exp * d_exp, d)],
            batch=batch, seq=seq, out_scale=1.0 - lam_init)
        ob = _dilated_call(zg, bias_b, batch=batch, seq=seq)

        wr_t = jnp.concatenate([
            w_coarse[layer].T,
            w_fine[layer].transpose(0, 2, 1).reshape(N_EXPERTS, d),
            jnp.zeros((LANES - N_EXPERT_GROUPS - N_EXPERTS, d), F32)], axis=0)
        wr_t = wr_t.T
        wr_hi = wr_t.astype(BF16)
        wr_t = jnp.stack([wr_hi, (wr_t - wr_hi.astype(F32)).astype(BF16)], axis=0)
        x1, h2, route, counts = _merge_router_call(
            oa, ob, zg, x2d, w_proj_a[layer].astype(BF16), w_proj_b[layer].astype(BF16),
            w_out[layer].astype(BF16), norm_ffn_g[layer:layer + 1], wr_t)

        dest, blk_e, blk_valid, zstart, zlen, n_rows = _routing_plan(route, counts, t)
        xin = _dispatch_call(_blocked(dest, DISPATCH_TOK), zstart, zlen, h2, n_rows)
        yb = _expert_call(blk_e, blk_valid, xin, w1b.reshape(n_exp, d, d_exp), w3b.reshape(n_exp, d, d_exp),
                          w2b.reshape(n_exp, d_exp, d))
        x2d = _combine_call(_blocked(dest, ROW_TM), yb, x1, route.T, p[layer].reshape(t, -1),
                            norm_ple_g[layer:layer + 1], w_ple_gate[layer].astype(BF16),
                            w_ple_proj[layer].astype(BF16), final_norm_g.reshape(1, d),
                            final_norm=layer == depth - 1)
    return x2d.reshape(batch, seq, d)
```

```python
import functools
import math

import jax
import jax.numpy as jnp
from jax import lax
from jax.experimental import pallas as pl
from jax.experimental.pallas import tpu as pltpu

F32 = jnp.float32
BF16 = jnp.bfloat16

BLK = 128
NEG_INF = -1e30
RMS_EPS = 1e-6
SUBLN_EPS = 1e-5
N_BUCKETS = 32
MAX_DISTANCE = 2048
A_HEADS = 8
A_HEAD_DIM = 128
DILATED_GROUPS = ((128, 1), (512, 4), (2048, 16))
N_DGROUPS = 3
B_HEADS = 8
B_HEAD_DIM = 64
N_EXPERT_GROUPS = 4
EXPERTS_PER_GROUP = 8
N_EXPERTS = 32
TOP_K = 2

LANES = 128
VMEM_LIMIT = 56 * 1024 * 1024
ATT_BLK = 256
ATT_HEADS = 2
PERM_BLK = 256
GATE_COLS = 4096
PROJ_TN = 512
PROJ_TM = 2048
ROW_TM = 256
EXP_TM = 256
DISPATCH_TOK = 512
NORM_CHUNK = 128
CAST_SLAB_BYTES = 1 << 20
DIL_GROUP = 4
TOEPLITZ_LANES = 512


def _t5_thresholds():
    max_exact = N_BUCKETS // 2
    out = []
    for k in range(1, N_BUCKETS - max_exact):
        out.append(int(math.ceil(max_exact * (MAX_DISTANCE / max_exact) ** (k / (N_BUCKETS - max_exact)))))
    return tuple(out)


T5_THRESHOLDS = _t5_thresholds()


def _cparams(sem, vmem=VMEM_LIMIT):
    return pltpu.CompilerParams(dimension_semantics=sem, vmem_limit_bytes=vmem)


def _mult(x, m):
    return x if isinstance(x, int) else pl.multiple_of(x, m)


def _sigmoid(x):
    return 0.5 * jnp.tanh(0.5 * x) + 0.5


def _dot(a, b, **kw):
    return jnp.dot(a, b, preferred_element_type=F32, **kw)


def _dot_nt(a, b, **kw):
    return lax.dot_general(a, b, (((1,), (1,)), ((), ())), preferred_element_type=F32, **kw)


def _lam_kernel(q1_ref, k1_ref, q2_ref, k2_ref, o_ref, *, lam_init):
    s1 = jnp.sum(q1_ref[...] * k1_ref[...], axis=-1, keepdims=True)
    s2 = jnp.sum(q2_ref[...] * k2_ref[...], axis=-1, keepdims=True)
    o_ref[...] = jnp.exp(s1) - jnp.exp(s2) + lam_init


def _lam_call(lq1, lk1, lq2, lk2, lam_init):
    return pl.pallas_call(
        functools.partial(_lam_kernel, lam_init=lam_init),
        out_shape=jax.ShapeDtypeStruct((1, 1), F32),
        name="lam",
    )(lq1, lk1, lq2, lk2)


def _bias_tile_kernel(tab_ref, *rest, tq, tk, off_mult, off_add, dil, max_rel, head0, key_major, mult):
    n_pass = max(len(rest) - 2, 0)
    o_ref = rest[n_pass]
    if n_pass:
        col = 0
        for src in rest[:n_pass]:
            rest[-1][:, col:col + src.shape[1]] = src[...].astype(rest[-1].dtype)
            col += src.shape[1]
    h = pl.program_id(0)
    n = pl.program_id(1)
    assert key_major and tq + tk <= TOEPLITZ_LANES
    m = lax.broadcasted_iota(jnp.int32, (8, TOEPLITZ_LANES), 1)
    rel = jnp.where(m < tk, m, m - TOEPLITZ_LANES) + (n * off_mult + off_add)
    dist = rel * dil
    large = jnp.full(m.shape, N_BUCKETS // 2, jnp.int32)
    for thr in T5_THRESHOLDS:
        large = large + jnp.where(dist >= thr, 1, 0)
    bucket = jnp.where(dist < N_BUCKETS // 2, dist, large)
    acc = jnp.zeros(m.shape, F32)
    for b in range(N_BUCKETS):
        acc = jnp.where(bucket == b, tab_ref[b, head0 + h], acc)
    valid = jnp.where(rel >= 0, jnp.where(rel <= max_rel, 1, 0), 0)
    vec = jnp.where(valid == 1, acc * mult, NEG_INF)[0:1, :]
    rows = pltpu.roll(jnp.broadcast_to(vec, (tq, TOEPLITZ_LANES)), 0, 1, stride=1, stride_axis=0)
    o_ref[0, 0] = rows[:, :tk]


def _bias_tiles_call(rel_bias, *, n_heads, n_off, tq, tk, off_mult, off_add, dil, max_rel, head0, name,
                     key_major=False, mult=1.0, concat_cast=()):
    kern = functools.partial(_bias_tile_kernel, tq=tq, tk=tk, off_mult=off_mult, off_add=off_add,
                             dil=dil, max_rel=max_rel, head0=head0, key_major=key_major, mult=mult)
    n_steps = n_heads * n_off
    if concat_cast and concat_cast[0].shape[0] % (n_steps * 16):
        concat_cast = ()
    in_specs = [pl.BlockSpec(memory_space=pltpu.SMEM)]
    out_specs = [pl.BlockSpec((1, 1, tq, tk), lambda h, n: (h, n, 0, 0))]
    out_shape = [jax.ShapeDtypeStruct((n_heads, n_off, tq, tk), F32)]
    if concat_cast:
        rows = concat_cast[0].shape[0]
        cols = sum(w.shape[1] for w in concat_cast)
        in_specs += [pl.BlockSpec((rows // n_steps, w.shape[1]), lambda h, n: (h * n_off + n, 0))
                     for w in concat_cast]
        out_specs.append(pl.BlockSpec((rows // n_steps, cols), lambda h, n: (h * n_off + n, 0)))
        out_shape.append(jax.ShapeDtypeStruct((rows, cols), BF16))
    outs = pl.pallas_call(
        kern,
        grid=(n_heads, n_off),
        in_specs=in_specs,
        out_specs=out_specs,
        out_shape=out_shape,
        compiler_params=_cparams(("parallel", "parallel")),
        name=name,
    )(rel_bias, *concat_cast)
    return outs[0], (outs[1] if concat_cast else None)


def _rmsnorm_rows(x_ref, g_ref, out_ref, eps):
    rows = x_ref.shape[0]
    g = g_ref[...]

    def body(c, carry):
        r0 = pl.multiple_of(c * NORM_CHUNK, NORM_CHUNK)
        x = x_ref[pl.ds(r0, NORM_CHUNK), :]
        ms = jnp.mean(x * x, axis=-1, keepdims=True)
        out_ref[pl.ds(r0, NORM_CHUNK), :] = ((x * lax.rsqrt(ms + eps)) * g).astype(out_ref.dtype)
        return carry

    lax.fori_loop(0, rows // NORM_CHUNK, body, 0)


def _rmsnorm_val(x, g, eps):
    ms = jnp.mean(x * x, axis=-1, keepdims=True)
    return (x * lax.rsqrt(ms + eps)) * g


def _inproj_kernel(x_ref, g_ref, w_ref, o_ref, h_ref, *, n_gate_blocks):
    j = pl.program_id(1)

    @pl.when(j == 0)
    def _():
        _rmsnorm_rows(x_ref, g_ref, h_ref, RMS_EPS)

    acc = _dot(h_ref[...], w_ref[...])

    @pl.when(j < n_gate_blocks)
    def _():
        o_ref[...] = _sigmoid(acc).astype(o_ref.dtype)

    @pl.when(j >= n_gate_blocks)
    def _():
        o_ref[...] = acc.astype(o_ref.dtype)


def _inproj_call(x2d, g, w_cat):
    t, d = x2d.shape
    n = w_cat.shape[1]
    tm = min(PROJ_TM, t)
    return pl.pallas_call(
        functools.partial(_inproj_kernel, n_gate_blocks=GATE_COLS // PROJ_TN),
        grid=(t // tm, n // PROJ_TN),
        in_specs=[
            pl.BlockSpec((tm, d), lambda i, j: (i, 0)),
            pl.BlockSpec((1, d), lambda i, j: (0, 0)),
            pl.BlockSpec((d, PROJ_TN), lambda i, j: (0, j)),
        ],
        out_specs=pl.BlockSpec((tm, PROJ_TN), lambda i, j: (i, j)),
        out_shape=jax.ShapeDtypeStruct((t, n), BF16),
        scratch_shapes=[pltpu.VMEM((tm, d), BF16)],
        compiler_params=_cparams(("parallel", "arbitrary")),
        name="inproj",
    )(x2d, g, w_cat)


def _diff_attn_kernel(lam_ref, q_ref, k_ref, v_ref, bias_ref, g_ref, *rest, out_scale, n_blk, n_cast):
    cast_in = rest[:n_cast]
    o_ref = rest[n_cast]
    cast_out = rest[n_cast + 1:2 * n_cast + 1]
    vt_ref, tbuf_a, tbuf_b, acc = rest[2 * n_cast + 1:]
    for src, dst in zip(cast_in, cast_out):
        dst[...] = src[...].astype(dst.dtype)
    qi = pl.program_id(2)
    width = 2 * A_HEAD_DIM
    n_map = 2 * ATT_HEADS
    c = (A_HEAD_DIM ** -0.5) * math.log2(math.e)

    @pl.when(qi == 0)
    def _():
        def transpose_block(b, carry):
            r0 = pl.multiple_of(b * ATT_BLK, ATT_BLK)
            for hp in range(ATT_HEADS):
                vb = v_ref[pl.ds(r0, ATT_BLK), hp * width:(hp + 1) * width]
                vt_ref[hp, b] = vb.astype(F32).T.astype(BF16)
            return carry

        lax.fori_loop(0, n_blk, transpose_block, 0)

    qs = [q_ref[:, mi * A_HEAD_DIM:(mi + 1) * A_HEAD_DIM] for mi in range(n_map)]
    acc[...] = jnp.zeros_like(acc)

    def scores(ki, dst):
        k0 = pl.multiple_of(ki * ATT_BLK, ATT_BLK)
        for mi in range(n_map):
            kb = k_ref[pl.ds(k0, ATT_BLK), mi * A_HEAD_DIM:(mi + 1) * A_HEAD_DIM]
            dst[mi] = _dot_nt(kb, qs[mi]) + bias_ref[mi // 2, qi - ki]

    def softmax(t, m, l):
        m_new = jnp.maximum(m, jnp.max(t, axis=0, keepdims=True))
        alpha = jnp.exp2((m - m_new) * c)
        p = jnp.exp2((t - m_new) * c)
        return m_new, alpha * l + jnp.sum(p, axis=0, keepdims=True), alpha, p.astype(BF16)

    scores(0, tbuf_a)

    def step(ki, carry, cur, nxt):
        ms, ls = carry
        stats = [softmax(cur[mi], ms[mi], ls[mi]) for mi in range(n_map)]
        scores(jnp.minimum(ki + 1, qi), nxt)
        for mi in range(n_map):
            acc[mi] = acc[mi] * stats[mi][2] + _dot(vt_ref[mi // 2, ki], stats[mi][3])
        return tuple(st[0] for st in stats), tuple(st[1] for st in stats)

    def body(ki, carry):
        return lax.cond(lax.bitwise_and(ki, 1) == 0,
                        lambda cr: step(ki, cr, tbuf_a, tbuf_b),
                        lambda cr: step(ki, cr, tbuf_b, tbuf_a), carry)

    minf = (jnp.full((1, ATT_BLK), -jnp.inf, F32),) * n_map
    zero = (jnp.zeros((1, ATT_BLK), F32),) * n_map
    _, ls = lax.fori_loop(0, qi + 1, body, (minf, zero))
    for hp in range(ATT_HEADS):
        w = acc[2 * hp] / ls[2 * hp] - lam_ref[0, 0] * (acc[2 * hp + 1] / ls[2 * hp + 1])
        ms = jnp.mean(w * w, axis=0, keepdims=True)
        y = ((w * lax.rsqrt(ms + SUBLN_EPS)) * g_ref[...]) * out_scale
        o_ref[:, hp * width:(hp + 1) * width] = y.T.astype(o_ref.dtype)


def _diff_attn_call(zg, lam, bias_tiles, subln_g, passengers, *, batch, seq, out_scale):
    t = batch * seq
    nq = seq // ATT_BLK
    width = 2 * A_HEAD_DIM
    blk_w = ATT_HEADS * width
    q_col = GATE_COLS // blk_w
    k_col = q_col + A_HEADS // ATT_HEADS
    v_col = k_col + A_HEADS // ATT_HEADS
    n_steps = (A_HEADS // ATT_HEADS) * batch * nq
    riders = [w for w in passengers
              if w.shape[0] % (n_steps * 16) == 0 and w.size * 4 // n_steps <= CAST_SLAB_BYTES]

    def slab_spec(w):
        return pl.BlockSpec((w.shape[0] // n_steps, w.shape[1]), lambda h, b, i: ((h * batch + b) * nq + i, 0))

    outs = pl.pallas_call(
        functools.partial(_diff_attn_kernel, out_scale=out_scale, n_blk=nq, n_cast=len(riders)),
        grid=(A_HEADS // ATT_HEADS, batch, nq),
        in_specs=[
            pl.BlockSpec(memory_space=pltpu.SMEM),
            pl.BlockSpec((ATT_BLK, blk_w), lambda h, b, i: (b * nq + i, q_col + h)),
            pl.BlockSpec((seq, blk_w), lambda h, b, i: (b, k_col + h)),
            pl.BlockSpec((seq, blk_w), lambda h, b, i: (b, v_col + h)),
            pl.BlockSpec((ATT_HEADS, nq, ATT_BLK, ATT_BLK), lambda h, b, i: (h, 0, 0, 0)),
            pl.BlockSpec((width, 1), lambda h, b, i: (0, 0)),
        ] + [slab_spec(w) for w in riders],
        out_specs=[pl.BlockSpec((ATT_BLK, blk_w), lambda h, b, i: (b * nq + i, h))]
        + [slab_spec(w) for w in riders],
        out_shape=[jax.ShapeDtypeStruct((t, A_HEADS * width), BF16)]
        + [jax.ShapeDtypeStruct(w.shape, BF16) for w in riders],
        scratch_shapes=[pltpu.VMEM((ATT_HEADS, nq, width, ATT_BLK), BF16),
                        pltpu.VMEM((2 * ATT_HEADS, ATT_BLK, ATT_BLK), F32),
                        pltpu.VMEM((2 * ATT_HEADS, ATT_BLK, ATT_BLK), F32),
                        pltpu.VMEM((2 * ATT_HEADS, width, ATT_BLK), F32)],
        compiler_params=_cparams(("parallel", "parallel", "arbitrary")),
        name="diff_attn",
    )(lam, zg, zg, zg, bias_tiles, subln_g.reshape(width, 1), *riders)
    cast = iter(outs[1:])
    return outs[0], [next(cast) if any(w is r for r in riders) else w.astype(BF16) for w in passengers]


def _perm_matrix(dil, inverse):
    w = PERM_BLK // dil
    shift = w.bit_length() - 1
    a = lax.broadcasted_iota(jnp.int32, (PERM_BLK, PERM_BLK), 0)
    b = lax.broadcasted_iota(jnp.int32, (PERM_BLK, PERM_BLK), 1)
    dst, src = (b, a) if inverse else (a, b)
    c = lax.shift_right_logical(dst, shift)
    ll = lax.bitwise_and(dst, w - 1)
    return jnp.where(src == ll * dil + c, jnp.float32(1), jnp.float32(0))


def _deinterleave(pairs, perm, dil, seq):
    w = PERM_BLK // dil
    stream_len = seq // dil

    def body(b8, carry):
        r0 = pl.multiple_of(b8 * PERM_BLK, PERM_BLK)
        ys = [_dot(perm, src[pl.ds(r0, PERM_BLK), :]).astype(dst.dtype) for src, dst in pairs]
        for y, (_, dst) in zip(ys, pairs):
            for c in range(dil):
                d0 = pl.multiple_of(c * stream_len + b8 * w, w)
                dst[pl.ds(d0, w), :] = y[c * w:(c + 1) * w, :]
        return carry

    lax.fori_loop(0, seq // PERM_BLK, body, 0)


def _permute_rows(perm, x):
    if x.dtype == BF16:
        return _dot(perm, x)
    hi = x.astype(BF16)
    r1 = x - hi.astype(F32)
    mid = r1.astype(BF16)
    lo = (r1 - mid.astype(F32)).astype(BF16)
    y = _dot(perm, jnp.concatenate([hi, mid, lo], axis=1))
    n = x.shape[1]
    return y[:, :n] + (y[:, n:2 * n] + y[:, 2 * n:])


def _interleave(items, perm_inv, dil, seq):
    w = PERM_BLK // dil
    stream_len = seq // dil

    def body(b8, carry):
        for src, _, stack in items:
            for c in range(dil):
                s0 = pl.multiple_of(c * stream_len + b8 * w, w)
                stack[c * w:(c + 1) * w, :] = src[pl.ds(s0, w), :]
        outs = [_permute_rows(perm_inv, stack[...]) for _, _, stack in items]
        r0 = pl.multiple_of(b8 * PERM_BLK, PERM_BLK)
        for out, (_, dst, _) in zip(outs, items):
            dst[pl.ds(r0, PERM_BLK), :] = out.astype(dst.dtype)
        return carry

    lax.fori_loop(0, seq // PERM_BLK, body, 0)


def _transpose_blocks(src_ref, vt_ref, seq):
    def body(blk, carry):
        r0 = pl.multiple_of(blk * BLK, BLK)
        vt_ref[blk] = src_ref[pl.ds(r0, BLK), :].astype(F32).T.astype(vt_ref.dtype)
        return carry

    lax.fori_loop(0, seq // BLK, body, 0)


def _dil_windows(q_src, k_src, vt_ref, o_dst, lse_dst, bias_ref, ot_ref, lt_ref, windows):
    scale = B_HEAD_DIM ** -0.5
    c = scale * math.log2(math.e)
    tiles = []
    for rq, nk in windows:
        rk = rq - (nk - BLK)
        for hh in range(B_HEADS):
            cols = slice(hh * B_HEAD_DIM, (hh + 1) * B_HEAD_DIM)
            qh = q_src[pl.ds(rq, BLK), cols]
            kh = k_src[pl.ds(rk, nk), cols]
            tiles.append(_dot_nt(kh, qh) + bias_ref[0, hh, 2 * BLK - nk:, :])
    for wi, (rq, nk) in enumerate(windows):
        qb = rq // BLK if isinstance(rq, int) else lax.shift_right_logical(rq, BLK.bit_length() - 1)
        for hh in range(B_HEADS):
            cols = slice(hh * B_HEAD_DIM, (hh + 1) * B_HEAD_DIM)
            t = tiles[wi * B_HEADS + hh]
            m = jnp.max(t, axis=0, keepdims=True)
            p = jnp.exp2((t - m) * c)
            den = jnp.sum(p, axis=0, keepdims=True)
            if nk == BLK:
                vth = vt_ref[qb, cols, :]
            else:
                vth = jnp.concatenate([vt_ref[qb - 1, cols, :], vt_ref[qb, cols, :]], axis=1)
            ot_ref[wi, cols, :] = _dot(vth, p.astype(BF16)) / den
            lt_ref[wi, hh:hh + 1, :] = m * scale + jnp.log(den)
        o_dst[pl.ds(rq, BLK), :] = ot_ref[wi].T.astype(o_dst.dtype)
        lse_dst[pl.ds(rq, BLK), :] = lt_ref[wi].T


def _dil_streams(q_src, k_src, vt_ref, o_dst, lse_dst, bias_ref, ot_ref, lt_ref, dil, seq):
    stream_len = seq // dil
    nq = stream_len // BLK

    def run(windows):
        _dil_windows(q_src, k_src, vt_ref, o_dst, lse_dst, bias_ref, ot_ref, lt_ref, windows)

    if nq == 1:
        group = math.gcd(DIL_GROUP, dil)

        def stream_group(cg, carry):
            base = pl.multiple_of(cg * (group * stream_len), BLK)
            run([(pl.multiple_of(base + s * stream_len, BLK), BLK) for s in range(group)])
            return carry

        lax.fori_loop(0, dil // group, stream_group, 0)
        return

    def stream(c, carry):
        base = _mult(c * stream_len, BLK)
        n_tail = (nq - 1) % DIL_GROUP
        run([(base, BLK)] + [(_mult(base + (nq - 1 - s) * BLK, BLK), 2 * BLK) for s in range(n_tail)])

        def qgroup(j, carry2):
            rq = pl.multiple_of(base + (1 + DIL_GROUP * j) * BLK, BLK)
            run([(pl.multiple_of(rq + s * BLK, BLK), 2 * BLK) for s in range(DIL_GROUP)])
            return carry2

        n_groups = (nq - 1 - n_tail) // DIL_GROUP
        if n_groups:
            lax.fori_loop(0, n_groups, qgroup, 0)
        return carry

    if dil == 1:
        stream(0, 0)
    else:
        lax.fori_loop(0, dil, stream, 0)


def _head_expand(x, expand):
    hi = x.astype(BF16)
    lo = (x - hi.astype(F32)).astype(BF16)
    return _dot(jnp.concatenate([hi, lo], axis=1), expand)


def _dil_merge(o_src, lse_src, oacc, mrun, lrun, out_ref, first, last, seq):
    er = lax.broadcasted_iota(jnp.int32, (2 * LANES, B_HEADS * B_HEAD_DIM), 0)
    ec = lax.broadcasted_iota(jnp.int32, (2 * LANES, B_HEADS * B_HEAD_DIM), 1)
    expand = jnp.where(lax.shift_right_logical(ec, B_HEAD_DIM.bit_length() - 1) == lax.bitwise_and(er, LANES - 1),
                       jnp.float32(1), jnp.float32(0)).astype(BF16)

    def body(ch, carry):
        r0 = pl.multiple_of(ch * ROW_TM, ROW_TM)
        rows = pl.ds(r0, ROW_TM)
        lse = lse_src[rows, :]
        if first:
            mrun[rows, :] = lse
            lrun[rows, :] = jnp.ones_like(lse)
            oacc[rows, :] = o_src[rows, :].astype(F32)
            return carry
        m_old = mrun[rows, :]
        m_new = jnp.maximum(m_old, lse)
        a = jnp.exp(m_old - m_new)
        bw = jnp.exp(lse - m_new)
        l_new = lrun[rows, :] * a + bw
        if last:
            a = a / l_new
            bw = bw / l_new
        val = oacc[rows, :] * _head_expand(a, expand) + o_src[rows, :].astype(F32) * _head_expand(bw, expand)
        if last:
            out_ref[rows, :] = val.astype(out_ref.dtype)
        else:
            mrun[rows, :] = m_new
            lrun[rows, :] = l_new
            oacc[rows, :] = val
        return carry

    lax.fori_loop(0, seq // ROW_TM, body, 0)


def _dilated_kernel(q_ref, k_ref, v_ref, bias_ref, out_ref,
                    qs, ks, vs, vt, os_, lses, otok, lsetok, ostack, lstack, ot, lt, oacc, mrun, lrun,
                    *, seq):
    g = pl.program_id(1)

    @pl.when(g == 0)
    def _():
        lt[...] = jnp.zeros_like(lt)

    for gi, (_, dil) in enumerate(DILATED_GROUPS):

        @pl.when(g == gi)
        def _(gi=gi, dil=dil):
            first = gi == 0
            last = gi == N_DGROUPS - 1
            if dil == 1:
                _transpose_blocks(v_ref, vt, seq)
                _dil_streams(q_ref, k_ref, vt, otok, lsetok, bias_ref, ot, lt, 1, seq)
            else:
                perm = _perm_matrix(dil, inverse=False).astype(BF16)
                _deinterleave([(q_ref, qs), (k_ref, ks), (v_ref, vs)], perm, dil, seq)
                _transpose_blocks(vs, vt, seq)
                _dil_streams(qs, ks, vt, os_, lses, bias_ref, ot, lt, dil, seq)
                perm_inv = _perm_matrix(dil, inverse=True).astype(BF16)
                _interleave([(os_, otok, ostack), (lses, lsetok, lstack)], perm_inv, dil, seq)
            _dil_merge(otok, lsetok, oacc, mrun, lrun, out_ref, first, last, seq)


def _dilated_call(zg, bias_tiles, *, batch, seq):
    width = B_HEADS * B_HEAD_DIM
    q_col = (GATE_COLS + 3 * A_HEADS * 2 * A_HEAD_DIM) // width
    k_col = q_col + N_DGROUPS
    v_col = k_col + N_DGROUPS
    return pl.pallas_call(
        functools.partial(_dilated_kernel, seq=seq),
        grid=(batch, N_DGROUPS),
        in_specs=[
            pl.BlockSpec((seq, width), lambda b, g: (b, q_col + g)),
            pl.BlockSpec((seq, width), lambda b, g: (b, k_col + g)),
            pl.BlockSpec((seq, width), lambda b, g: (b, v_col + g)),
            pl.BlockSpec((1, B_HEADS, 2 * BLK, BLK), lambda b, g: (g, 0, 0, 0)),
        ],
        out_specs=pl.BlockSpec((seq, width), lambda b, g: (b, 0)),
        out_shape=jax.ShapeDtypeStruct((batch * seq, width), BF16),
        scratch_shapes=[
            pltpu.VMEM((seq, width), BF16), pltpu.VMEM((seq, width), BF16), pltpu.VMEM((seq, width), BF16),
            pltpu.VMEM((seq // BLK, width, BLK), BF16),
            pltpu.VMEM((seq, width), BF16), pltpu.VMEM((seq, LANES), F32),
            pltpu.VMEM((seq, width), BF16), pltpu.VMEM((seq, LANES), F32),
            pltpu.VMEM((PERM_BLK, width), BF16), pltpu.VMEM((PERM_BLK, LANES), F32),
            pltpu.VMEM((DIL_GROUP, width, BLK), F32), pltpu.VMEM((DIL_GROUP, LANES, BLK), F32),
            pltpu.VMEM((seq, width), F32), pltpu.VMEM((seq, LANES), F32), pltpu.VMEM((seq, LANES), F32),
        ],
        compiler_params=_cparams(("parallel", "arbitrary")),
        name="dilated_attn",
    )(zg, zg, zg, bias_tiles)


def _merge_router_kernel(oa_ref, ob_ref, gate_ref, x_ref, pa_ref, pb_ref, wo_ref, g_ref, wr_ref,
                         x1_ref, h2_ref, route_ref, cnt_ref, run_ref):
    tm, d = x_ref.shape

    @pl.when(pl.program_id(0) == 0)
    def _():
        run_ref[...] = jnp.zeros_like(run_ref)

    a = _dot(oa_ref[...], pa_ref[...])
    bm = _dot(ob_ref[...], pb_ref[...])
    merged = gate_ref[:, :d].astype(F32) * a + gate_ref[:, d:].astype(F32) * bm
    x1 = x_ref[...] + _dot(merged.astype(BF16), wo_ref[...])
    x1_ref[...] = x1
    h2 = _rmsnorm_val(x1, g_ref[...], RMS_EPS)
    h2_ref[...] = h2

    h_hi = h2.astype(BF16)
    h_lo = (h2 - h_hi.astype(F32)).astype(BF16)
    lt = (_dot(h_hi, wr_ref[0]) + (_dot(h_hi, wr_ref[1]) + _dot(h_lo, wr_ref[0]))).T
    coarse = [lt[i:i + 1, :] for i in range(N_EXPERT_GROUPS)]
    best = coarse[0]
    gsel = jnp.zeros((1, tm), jnp.int32)
    for i in range(1, N_EXPERT_GROUPS):
        upd = coarse[i] > best
        gsel = jnp.where(upd, i, gsel)
        best = jnp.where(upd, coarse[i], best)
    den = jnp.exp(coarse[0] - best)
    for i in range(1, N_EXPERT_GROUPS):
        den = den + jnp.exp(coarse[i] - best)
    pg = 1.0 / den

    fine = []
    for k in range(EXPERTS_PER_GROUP):
        f = lt[N_EXPERT_GROUPS + k:N_EXPERT_GROUPS + k + 1, :]
        for gi in range(1, N_EXPERT_GROUPS):
            r = N_EXPERT_GROUPS + gi * EXPERTS_PER_GROUP + k
            f = jnp.where(gsel == gi, lt[r:r + 1, :], f)
        fine.append(f)
    v0 = fine[0]
    i0 = jnp.zeros((1, tm), jnp.int32)
    for k in range(1, EXPERTS_PER_GROUP):
        upd = fine[k] > v0
        i0 = jnp.where(upd, k, i0)
        v0 = jnp.where(upd, fine[k], v0)
    v1 = jnp.full((1, tm), -jnp.inf, F32)
    i1 = jnp.zeros((1, tm), jnp.int32)
    for k in range(EXPERTS_PER_GROUP):
        upd = jnp.where(i0 != k, jnp.where(fine[k] > v1, 1, 0), 0) == 1
        i1 = jnp.where(upd, k, i1)
        v1 = jnp.where(upd, fine[k], v1)
    e1w = jnp.exp(v1 - v0)
    gate0 = pg * (1.0 / (1.0 + e1w))
    gate1 = pg * (e1w / (1.0 + e1w))
    e0 = gsel * EXPERTS_PER_GROUP + i0
    e1 = gsel * EXPERTS_PER_GROUP + i1

    eidx = lax.broadcasted_iota(jnp.int32, (N_EXPERTS, tm), 0)
    oh0 = jnp.where(eidx == e0, jnp.float32(1), jnp.float32(0))
    oh1 = jnp.where(eidx == e1, jnp.float32(1), jnp.float32(0))
    ta = lax.broadcasted_iota(jnp.int32, (tm, tm), 0)
    tb = lax.broadcasted_iota(jnp.int32, (tm, tm), 1)
    before = jnp.where(ta < tb, jnp.float32(1), jnp.float32(0)).astype(BF16)
    pre0 = _dot(oh0.astype(BF16), before)
    pre1 = _dot(oh1.astype(BF16), before)
    run = run_ref[:, 0:1]
    tot0 = jnp.sum(oh0, axis=1, keepdims=True)
    tot1 = jnp.sum(oh1, axis=1, keepdims=True)
    rank0 = jnp.sum(oh0 * (run + pre0), axis=0, keepdims=True)
    rank1 = jnp.sum(oh1 * (run + tot0 + pre1), axis=0, keepdims=True)
    new_run = jnp.broadcast_to(run + tot0 + tot1, run_ref.shape)
    run_ref[...] = new_run
    cnt_ref[...] = new_run

    route_ref[0:1, :] = e0.astype(F32)
    route_ref[1:2, :] = e1.astype(F32)
    route_ref[2:3, :] = gate0
    route_ref[3:4, :] = gate1
    route_ref[4:5, :] = rank0
    route_ref[5:6, :] = rank1
    route_ref[6:8, :] = jnp.zeros((2, tm), F32)


def _merge_router_call(oa, ob, zg, x2d, pa, pb, wo, g, wr_t):
    t, d = x2d.shape
    tm = ROW_TM
    const = dict(pipeline_mode=pl.Buffered(1))
    return pl.pallas_call(
        _merge_router_kernel,
        grid=(t // tm,),
        in_specs=[
            pl.BlockSpec((tm, oa.shape[1]), lambda i: (i, 0)),
            pl.BlockSpec((tm, ob.shape[1]), lambda i: (i, 0)),
            pl.BlockSpec((tm, GATE_COLS), lambda i: (i, 0)),
            pl.BlockSpec((tm, d), lambda i: (i, 0)),
            pl.BlockSpec(pa.shape, lambda i: (0, 0), **const),
            pl.BlockSpec(pb.shape, lambda i: (0, 0), **const),
            pl.BlockSpec(wo.shape, lambda i: (0, 0), **const),
            pl.BlockSpec((1, d), lambda i: (0, 0)),
            pl.BlockSpec(wr_t.shape, lambda i: (0, 0, 0), **const),
        ],
        out_specs=[
            pl.BlockSpec((tm, d), lambda i: (i, 0)),
            pl.BlockSpec((tm, d), lambda i: (i, 0)),
            pl.BlockSpec((8, tm), lambda i: (0, i)),
            pl.BlockSpec((N_EXPERTS, LANES), lambda i: (0, 0)),
        ],
        out_shape=[
            jax.ShapeDtypeStruct((t, d), F32),
            jax.ShapeDtypeStruct((t, d), F32),
            jax.ShapeDtypeStruct((8, t), F32),
            jax.ShapeDtypeStruct((N_EXPERTS, LANES), F32),
        ],
        scratch_shapes=[pltpu.VMEM((N_EXPERTS, LANES), F32)],
        compiler_params=_cparams(("arbitrary",)),
        name="merge_router",
    )(oa, ob, zg, x2d, pa, pb, wo, g, wr_t)


def _row_copy(src, s_row, dst, d_row, sem):
    return pltpu.make_async_copy(src.at[pl.ds(s_row, 1)], dst.at[pl.ds(d_row, 1)], sem)


def _dispatch_kernel(dest_ref, zstart_ref, zlen_ref, h_ref, x_hbm, zero_ref, sem, zsem):
    step = pl.program_id(0)

    def issue(t, carry):
        for k in range(TOP_K):
            _row_copy(h_ref, t, x_hbm, dest_ref[0, k, t], sem.at[k]).start()
        return carry

    lax.fori_loop(0, DISPATCH_TOK, issue, 0, unroll=8)

    @pl.when(step == 0)
    def _():
        zero_ref[...] = jnp.zeros_like(zero_ref)

        def group_copy(r8):
            return pltpu.make_async_copy(zero_ref, x_hbm.at[pl.ds(pl.multiple_of(r8, 8), 8)], zsem.at[1])

        def segment(e, wait):
            start = zstart_ref[e]
            n_head = jnp.minimum(lax.bitwise_and(-start, 7), zlen_ref[e])
            n_group = lax.shift_right_logical(zlen_ref[e] - n_head, 3)

            def head(r, carry):
                cp = _row_copy(zero_ref, 0, x_hbm, 0 if wait else start + r, zsem.at[0])
                cp.wait() if wait else cp.start()
                return carry

            def group(j, carry):
                cp = group_copy(0 if wait else start + n_head + 8 * j)
                cp.wait() if wait else cp.start()
                return carry

            lax.fori_loop(0, n_head, head, 0)
            lax.fori_loop(0, n_group, group, 0)

        lax.fori_loop(0, N_EXPERTS + 1, lambda e, c: (segment(e, False), c)[1], 0)
        lax.fori_loop(0, N_EXPERTS + 1, lambda e, c: (segment(e, True), c)[1], 0)

    for k in range(TOP_K):
        pltpu.make_async_copy(h_ref, x_hbm.at[pl.ds(0, DISPATCH_TOK)], sem.at[k]).wait()


def _dispatch_call(dest_blocks, zstart, zlen, h2, n_rows):
    t, d = h2.shape
    return pl.pallas_call(
        _dispatch_kernel,
        grid=(t // DISPATCH_TOK,),
        in_specs=[
            pl.BlockSpec((1, TOP_K, DISPATCH_TOK), lambda i: (i, 0, 0), memory_space=pltpu.SMEM),
            pl.BlockSpec(memory_space=pltpu.SMEM),
            pl.BlockSpec(memory_space=pltpu.SMEM),
            pl.BlockSpec((DISPATCH_TOK, d), lambda i: (i, 0)),
        ],
        out_specs=pl.BlockSpec(memory_space=pl.ANY),
        out_shape=jax.ShapeDtypeStruct((n_rows, d), F32),
        scratch_shapes=[pltpu.VMEM((8, d), F32), pltpu.SemaphoreType.DMA((TOP_K,)),
                        pltpu.SemaphoreType.DMA((2,))],
        compiler_params=_cparams(("arbitrary",)),
        name="dispatch",
    )(dest_blocks, zstart, zlen, h2)


def _expert_kernel(blk_e_ref, blk_valid_ref, x_ref, w1_ref, w3_ref, w2_ref, y_ref):
    i = pl.program_id(0)

    @pl.when(blk_valid_ref[i] == 1)
    def _():
        x = x_ref[...].astype(BF16)
        a = _dot(x, w1_ref[0])
        b = _dot(x, w3_ref[0])
        hdn = (a * _sigmoid(a)) * b
        y_ref[...] = _dot(hdn.astype(BF16), w2_ref[0])

    @pl.when(blk_valid_ref[i] == 0)
    def _():
        y_ref[...] = jnp.zeros_like(y_ref)


def _expert_call(blk_e, blk_valid, xin, w1, w3, w2):
    n_rows, d = xin.shape
    de = w1.shape[2]
    grid_spec = pltpu.PrefetchScalarGridSpec(
        num_scalar_prefetch=2,
        grid=(n_rows // EXP_TM,),
        in_specs=[
            pl.BlockSpec((EXP_TM, d), lambda i, be, bv: (i, 0)),
            pl.BlockSpec((1, d, de), lambda i, be, bv: (be[i], 0, 0)),
            pl.BlockSpec((1, d, de), lambda i, be, bv: (be[i], 0, 0)),
            pl.BlockSpec((1, de, d), lambda i, be, bv: (be[i], 0, 0)),
        ],
        out_specs=pl.BlockSpec((EXP_TM, d), lambda i, be, bv: (i, 0)),
    )
    return pl.pallas_call(
        _expert_kernel,
        grid_spec=grid_spec,
        out_shape=jax.ShapeDtypeStruct((n_rows, d), F32),
        compiler_params=_cparams(("arbitrary",)),
        name="experts",
    )(blk_e, blk_valid, xin, w1, w3, w2)


def _combine_kernel(dest_ref, dnext_ref, y_hbm, x1_ref, gt_ref, p_ref, gple_ref, wg_ref, wp_ref, gfin_ref,
                    o_ref, ybuf, sem, *, final_norm, n_steps):
    tm = x1_ref.shape[0]
    i = pl.program_id(0)
    slot = lax.bitwise_and(i, 1)

    def gather(d_ref, s):
        def issue(t, carry):
            for k in range(TOP_K):
                _row_copy(y_hbm, d_ref[0, k, t], ybuf.at[s, k], t, sem.at[s, k]).start()
            return carry

        lax.fori_loop(0, tm, issue, 0, unroll=8)

    @pl.when(i == 0)
    def _():
        gather(dest_ref, 0)

    @pl.when(i + 1 < n_steps)
    def _():
        gather(dnext_ref, 1 - slot)

    pp = _dot(p_ref[...].astype(BF16), wp_ref[...])

    for k in range(TOP_K):
        pltpu.make_async_copy(y_hbm.at[pl.ds(0, tm)], ybuf.at[slot, k], sem.at[slot, k]).wait()
    y = ybuf[slot, 0] * gt_ref[:, 2:3] + ybuf[slot, 1] * gt_ref[:, 3:4]
    x2 = x1_ref[...] + y
    hn = _rmsnorm_val(x2, gple_ref[...], RMS_EPS)
    gate = _sigmoid(_dot(hn.astype(BF16), wg_ref[...]))
    x3 = x2 + gate * pp
    if final_norm:
        o_ref[...] = _rmsnorm_val(x3, gfin_ref[...], RMS_EPS)
    else:
        o_ref[...] = x3


def _combine_call(dest_blocks, yb, x1, gates_t, p2d, g_ple, w_gate, w_proj, g_fin, final_norm):
    t, d = x1.shape
    tm = ROW_TM
    const = dict(pipeline_mode=pl.Buffered(1))
    return pl.pallas_call(
        functools.partial(_combine_kernel, final_norm=final_norm, n_steps=t // tm),
        grid=(t // tm,),
        in_specs=[
            pl.BlockSpec((1, TOP_K, tm), lambda i: (i, 0, 0), memory_space=pltpu.SMEM),
            pl.BlockSpec((1, TOP_K, tm), lambda i: (jnp.minimum(i + 1, t // tm - 1), 0, 0),
                         memory_space=pltpu.SMEM),
            pl.BlockSpec(memory_space=pl.ANY),
            pl.BlockSpec((tm, d), lambda i: (i, 0)),
            pl.BlockSpec((tm, 8), lambda i: (i, 0)),
            pl.BlockSpec((tm, p2d.shape[1]), lambda i: (i, 0)),
            pl.BlockSpec((1, d), lambda i: (0, 0)),
            pl.BlockSpec(w_gate.shape, lambda i: (0, 0), **const),
            pl.BlockSpec(w_proj.shape, lambda i: (0, 0), **const),
            pl.BlockSpec((1, d), lambda i: (0, 0)),
        ],
        out_specs=pl.BlockSpec((tm, d), lambda i: (i, 0)),
        out_shape=jax.ShapeDtypeStruct((t, d), F32),
        scratch_shapes=[pltpu.VMEM((2, TOP_K, tm, d), F32), pltpu.SemaphoreType.DMA((2, TOP_K))],
        compiler_params=_cparams(("arbitrary",)),
        name="combine",
    )(dest_blocks, dest_blocks, yb, x1, gates_t, p2d, g_ple, w_gate, w_proj, g_fin)


def _routing_plan(route, counts_f, t):
    counts = counts_f[:, 0].astype(jnp.int32)
    pcounts = ((counts + EXP_TM - 1) // EXP_TM) * EXP_TM
    pends = jnp.cumsum(pcounts)
    pstarts = pends - pcounts
    eid = route[0:2].astype(jnp.int32)
    rank = route[4:6].astype(jnp.int32)
    onehot = eid[..., None] == jnp.arange(N_EXPERTS, dtype=jnp.int32)
    dest = jnp.sum(jnp.where(onehot, pstarts, 0), axis=-1) + rank
    n_rows = TOP_K * t + N_EXPERTS * EXP_TM
    n_blocks = n_rows // EXP_TM
    blk_row = jnp.arange(n_blocks, dtype=jnp.int32) * EXP_TM
    blk_e = jnp.sum((pends[None, :] <= blk_row[:, None]).astype(jnp.int32), axis=1)
    blk_valid = (blk_row < pends[-1]).astype(jnp.int32)
    last_e = jnp.sum((pends <= pends[-1] - 1).astype(jnp.int32))
    blk_e = jnp.minimum(blk_e, last_e)
    zstart = jnp.concatenate([pstarts + counts, pends[-1:]]).astype(jnp.int32)
    zlen = jnp.concatenate([pcounts - counts, n_rows - pends[-1:]]).astype(jnp.int32)
    return dest, blk_e, blk_valid, zstart, zlen, n_rows


def _blocked(dest, tok_per_block):
    t = dest.shape[1]
    return dest.reshape(TOP_K, t // tok_per_block, tok_per_block).transpose(1, 0, 2)


def kernel(x, p, rel_bias, norm_mix_g, w_in, w_gate, lambda_q1, lambda_k1, lambda_q2, lambda_k2, subln_g,
           w_proj_a, w_proj_b, w_out, norm_ffn_g, w_coarse, w_fine, w1, w3, w2, norm_ple_g, w_ple_gate,
           w_ple_proj, final_norm_g):
    batch, seq, d = x.shape
    depth = w_in.shape[0]
    t = batch * seq
    assert seq % PERM_BLK == 0 and t % min(PROJ_TM, t) == 0
    assert all(seq // dil >= BLK and win // dil == BLK for win, dil in DILATED_GROUPS)

    nq = seq // ATT_BLK
    bias_a, w_cat0 = _bias_tiles_call(rel_bias, n_heads=A_HEADS, n_off=nq, tq=ATT_BLK, tk=ATT_BLK,
                                      off_mult=ATT_BLK, off_add=0, dil=1, max_rel=seq, head0=0, name="bias_diff",
                                      key_major=True, mult=A_HEAD_DIM ** 0.5, concat_cast=(w_gate[0], w_in[0]))
    bias_b = jnp.concatenate([
        _bias_tiles_call(rel_bias, n_heads=B_HEADS, n_off=1, tq=2 * BLK, tk=BLK, off_mult=0, off_add=BLK,
                         dil=dil, max_rel=win // dil, head0=A_HEADS + gi * B_HEADS, name="bias_dil%d" % gi,
                         key_major=True, mult=B_HEAD_DIM ** 0.5)[0].reshape(1, B_HEADS, 2 * BLK, BLK)
        for gi, (win, dil) in enumerate(DILATED_GROUPS)], axis=0)

    x2d = x.reshape(t, d)
    for layer in range(depth):
        lam_init = 0.8 - 0.6 * math.exp(-0.3 * layer)
        lam = _lam_call(lambda_q1[layer:layer + 1], lambda_k1[layer:layer + 1],
                        lambda_q2[layer:layer + 1], lambda_k2[layer:layer + 1], lam_init)
        if layer == 0 and w_cat0 is not None:
            w_cat = w_cat0
        else:
            w_cat = jnp.concatenate([w_gate[layer], w_in[layer]], axis=1).astype(BF16)
        zg = _inproj_call(x2d, norm_mix_g[layer:layer + 1], w_cat)
        n_exp, _, d_exp = w1[layer].shape
        oa, (w1b, w3b, w2b) = _diff_attn_call(
            zg, lam, bias_a, subln_g[layer:layer + 1],
            [w1[layer].reshape(n_exp * d, d_exp), w3[layer].reshape(n_exp * d, d_exp),
             w2[layer].reshape(n_exp * d_exp, d)],
            batch=batch, seq=seq, out_scale=1.0 - lam_init)
        ob = _dilated_call(zg, bias_b, batch=batch, seq=seq)

        wr_t = jnp.concatenate([
            w_coarse[layer].T,
            w_fine[layer].transpose(0, 2, 1).reshape(N_EXPERTS, d),
            jnp.zeros((LANES - N_EXPERT_GROUPS - N_EXPERTS, d), F32)], axis=0)
        wr_t = wr_t.T
        wr_hi = wr_t.astype(BF16)
        wr_t = jnp.stack([wr_hi, (wr_t - wr_hi.astype(F32)).astype(BF16)], axis=0)
        x1, h2, route, counts = _merge_router_call(
            oa, ob, zg, x2d, w_proj_a[layer].astype(BF16), w_proj_b[layer].astype(BF16),
            w_out[layer].astype(BF16), norm_ffn_g[layer:layer + 1], wr_t)

        dest, blk_e, blk_valid, zstart, zlen, n_rows = _routing_plan(route, counts, t)
        xin = _dispatch_call(_blocked(dest, DISPATCH_TOK), zstart, zlen, h2, n_rows)
        yb = _expert_call(blk_e, blk_valid, xin, w1b.reshape(n_exp, d, d_exp), w3b.reshape(n_exp, d, d_exp),
                          w2b.reshape(n_exp, d_exp, d))
        x2d = _combine_call(_blocked(dest, ROW_TM), yb, x1, route.T, p[layer].reshape(t, -1),
                            norm_ple_g[layer:layer + 1], w_ple_gate[layer].astype(BF16),
                            w_ple_proj[layer].astype(BF16), final_norm_g.reshape(1, d),
                            final_norm=layer == depth - 1)
    return x2d.reshape(batch, seq, d)
```

```python
import functools
import math

import jax
import jax.numpy as jnp
from jax import lax
from jax.experimental import pallas as pl
from jax.experimental.pallas import tpu as pltpu

F32 = jnp.float32
BF16 = jnp.bfloat16

BLK = 128
NEG_INF = -1e30
RMS_EPS = 1e-6
SUBLN_EPS = 1e-5
N_BUCKETS = 32
MAX_DISTANCE = 2048
A_HEADS = 8
A_HEAD_DIM = 128
DILATED_GROUPS = ((128, 1), (512, 4), (2048, 16))
N_DGROUPS = 3
B_HEADS = 8
B_HEAD_DIM = 64
N_EXPERT_GROUPS = 4
EXPERTS_PER_GROUP = 8
N_EXPERTS = 32
TOP_K = 2

LANES = 128
VMEM_LIMIT = 56 * 1024 * 1024
ATT_BLK = 256
ATT_HEADS = 2
PERM_BLK = 256
GATE_COLS = 4096
PROJ_TN = 512
PROJ_TM = 2048
ROW_TM = 256
COMBINE_TM = 512
EXP_TM = 256
DISPATCH_TOK = 512
NORM_CHUNK = 128
CAST_SLAB_BYTES = 1 << 20
DIL_GROUP = 4
TOEPLITZ_LANES = 512


def _t5_thresholds():
    max_exact = N_BUCKETS // 2
    out = []
    for k in range(1, N_BUCKETS - max_exact):
        out.append(int(math.ceil(max_exact * (MAX_DISTANCE / max_exact) ** (k / (N_BUCKETS - max_exact)))))
    return tuple(out)


T5_THRESHOLDS = _t5_thresholds()


def _cparams(sem, vmem=VMEM_LIMIT):
    return pltpu.CompilerParams(dimension_semantics=sem, vmem_limit_bytes=vmem)


def _mult(x, m):
    return x if isinstance(x, int) else pl.multiple_of(x, m)


def _sigmoid(x):
    return 0.5 * jnp.tanh(0.5 * x) + 0.5


def _dot(a, b, **kw):
    return jnp.dot(a, b, preferred_element_type=F32, **kw)


def _dot_nt(a, b, **kw):
    return lax.dot_general(a, b, (((1,), (1,)), ((), ())), preferred_element_type=F32, **kw)


def _lam_kernel(q1_ref, k1_ref, q2_ref, k2_ref, o_ref, *, lam_init):
    s1 = jnp.sum(q1_ref[...] * k1_ref[...], axis=-1, keepdims=True)
    s2 = jnp.sum(q2_ref[...] * k2_ref[...], axis=-1, keepdims=True)
    o_ref[...] = jnp.exp(s1) - jnp.exp(s2) + lam_init


def _lam_call(lq1, lk1, lq2, lk2, lam_init):
    return pl.pallas_call(
        functools.partial(_lam_kernel, lam_init=lam_init),
        out_shape=jax.ShapeDtypeStruct((1, 1), F32),
        name="lam",
    )(lq1, lk1, lq2, lk2)


def _bias_tile_kernel(tab_ref, *rest, tq, tk, off_mult, off_add, dil, max_rel, head0, key_major, mult):
    n_pass = max(len(rest) - 2, 0)
    o_ref = rest[n_pass]
    if n_pass:
        col = 0
        for src in rest[:n_pass]:
            rest[-1][:, col:col + src.shape[1]] = src[...].astype(rest[-1].dtype)
            col += src.shape[1]
    h = pl.program_id(0)
    n = pl.program_id(1)
    assert key_major and tq + tk <= TOEPLITZ_LANES
    m = lax.broadcasted_iota(jnp.int32, (8, TOEPLITZ_LANES), 1)
    rel = jnp.where(m < tk, m, m - TOEPLITZ_LANES) + (n * off_mult + off_add)
    dist = rel * dil
    large = jnp.full(m.shape, N_BUCKETS // 2, jnp.int32)
    for thr in T5_THRESHOLDS:
        large = large + jnp.where(dist >= thr, 1, 0)
    bucket = jnp.where(dist < N_BUCKETS // 2, dist, large)
    acc = jnp.zeros(m.shape, F32)
    for b in range(N_BUCKETS):
        acc = jnp.where(bucket == b, tab_ref[b, head0 + h], acc)
    valid = jnp.where(rel >= 0, jnp.where(rel <= max_rel, 1, 0), 0)
    vec = jnp.where(valid == 1, acc * mult, NEG_INF)[0:1, :]
    rows = pltpu.roll(jnp.broadcast_to(vec, (tq, TOEPLITZ_LANES)), 0, 1, stride=1, stride_axis=0)
    o_ref[0, 0] = rows[:, :tk]


def _bias_tiles_call(rel_bias, *, n_heads, n_off, tq, tk, off_mult, off_add, dil, max_rel, head0, name,
                     key_major=False, mult=1.0, concat_cast=()):
    kern = functools.partial(_bias_tile_kernel, tq=tq, tk=tk, off_mult=off_mult, off_add=off_add,
                             dil=dil, max_rel=max_rel, head0=head0, key_major=key_major, mult=mult)
    n_steps = n_heads * n_off
    if concat_cast and concat_cast[0].shape[0] % (n_steps * 16):
        concat_cast = ()
    in_specs = [pl.BlockSpec(memory_space=pltpu.SMEM)]
    out_specs = [pl.BlockSpec((1, 1, tq, tk), lambda h, n: (h, n, 0, 0))]
    out_shape = [jax.ShapeDtypeStruct((n_heads, n_off, tq, tk), F32)]
    if concat_cast:
        rows = concat_cast[0].shape[0]
        cols = sum(w.shape[1] for w in concat_cast)
        in_specs += [pl.BlockSpec((rows // n_steps, w.shape[1]), lambda h, n: (h * n_off + n, 0))
                     for w in concat_cast]
        out_specs.append(pl.BlockSpec((rows // n_steps, cols), lambda h, n: (h * n_off + n, 0)))
        out_shape.append(jax.ShapeDtypeStruct((rows, cols), BF16))
    outs = pl.pallas_call(
        kern,
        grid=(n_heads, n_off),
        in_specs=in_specs,
        out_specs=out_specs,
        out_shape=out_shape,
        compiler_params=_cparams(("parallel", "parallel")),
        name=name,
    )(rel_bias, *concat_cast)
    return outs[0], (outs[1] if concat_cast else None)


def _rmsnorm_rows(x_ref, g_ref, out_ref, eps):
    rows = x_ref.shape[0]
    g = g_ref[...]

    def body(c, carry):
        r0 = pl.multiple_of(c * NORM_CHUNK, NORM_CHUNK)
        x = x_ref[pl.ds(r0, NORM_CHUNK), :]
        ms = jnp.mean(x * x, axis=-1, keepdims=True)
        out_ref[pl.ds(r0, NORM_CHUNK), :] = ((x * lax.rsqrt(ms + eps)) * g).astype(out_ref.dtype)
        return carry

    lax.fori_loop(0, rows // NORM_CHUNK, body, 0)


def _rmsnorm_val(x, g, eps):
    ms = jnp.mean(x * x, axis=-1, keepdims=True)
    return (x * lax.rsqrt(ms + eps)) * g


def _inproj_kernel(x_ref, g_ref, w_ref, o_ref, h_ref, *, n_gate_blocks):
    j = pl.program_id(1)

    @pl.when(j == 0)
    def _():
        _rmsnorm_rows(x_ref, g_ref, h_ref, RMS_EPS)

    acc = _dot(h_ref[...], w_ref[...])

    @pl.when(j < n_gate_blocks)
    def _():
        o_ref[...] = _sigmoid(acc).astype(o_ref.dtype)

    @pl.when(j >= n_gate_blocks)
    def _():
        o_ref[...] = acc.astype(o_ref.dtype)


def _inproj_call(x2d, g, w_cat):
    t, d = x2d.shape
    n = w_cat.shape[1]
    tm = min(PROJ_TM, t)
    return pl.pallas_call(
        functools.partial(_inproj_kernel, n_gate_blocks=GATE_COLS // PROJ_TN),
        grid=(t // tm, n // PROJ_TN),
        in_specs=[
            pl.BlockSpec((tm, d), lambda i, j: (i, 0)),
            pl.BlockSpec((1, d), lambda i, j: (0, 0)),
            pl.BlockSpec((d, PROJ_TN), lambda i, j: (0, j)),
        ],
        out_specs=pl.BlockSpec((tm, PROJ_TN), lambda i, j: (i, j)),
        out_shape=jax.ShapeDtypeStruct((t, n), BF16),
        scratch_shapes=[pltpu.VMEM((tm, d), BF16)],
        compiler_params=_cparams(("parallel", "arbitrary")),
        name="inproj",
    )(x2d, g, w_cat)


def _diff_attn_kernel(lam_ref, q_ref, k_ref, v_ref, bias_ref, g_ref, *rest, out_scale, n_blk, n_cast):
    cast_in = rest[:n_cast]
    o_ref = rest[n_cast]
    cast_out = rest[n_cast + 1:2 * n_cast + 1]
    vt_ref, tbuf_a, tbuf_b, acc = rest[2 * n_cast + 1:]
    for src, dst in zip(cast_in, cast_out):
        dst[...] = src[...].astype(dst.dtype)
    qi = pl.program_id(2)
    width = 2 * A_HEAD_DIM
    n_map = 2 * ATT_HEADS
    c = (A_HEAD_DIM ** -0.5) * math.log2(math.e)

    @pl.when(qi == 0)
    def _():
        def transpose_block(b, carry):
            r0 = pl.multiple_of(b * ATT_BLK, ATT_BLK)
            for hp in range(ATT_HEADS):
                vb = v_ref[pl.ds(r0, ATT_BLK), hp * width:(hp + 1) * width]
                vt_ref[hp, b] = vb.astype(F32).T.astype(BF16)
            return carry

        lax.fori_loop(0, n_blk, transpose_block, 0)

    qs = [q_ref[:, mi * A_HEAD_DIM:(mi + 1) * A_HEAD_DIM] for mi in range(n_map)]
    acc[...] = jnp.zeros_like(acc)

    def scores(ki, dst):
        k0 = pl.multiple_of(ki * ATT_BLK, ATT_BLK)
        for mi in range(n_map):
            kb = k_ref[pl.ds(k0, ATT_BLK), mi * A_HEAD_DIM:(mi + 1) * A_HEAD_DIM]
            dst[mi] = _dot_nt(kb, qs[mi]) + bias_ref[mi // 2, qi - ki]

    def softmax(t, m, l):
        m_new = jnp.maximum(m, jnp.max(t, axis=0, keepdims=True))
        alpha = jnp.exp2((m - m_new) * c)
        p = jnp.exp2((t - m_new) * c)
        return m_new, alpha * l + jnp.sum(p, axis=0, keepdims=True), alpha, p.astype(BF16)

    scores(0, tbuf_a)

    def step(ki, carry, cur, nxt):
        ms, ls = carry
        stats = [softmax(cur[mi], ms[mi], ls[mi]) for mi in range(n_map)]
        scores(jnp.minimum(ki + 1, qi), nxt)
        for mi in range(n_map):
            acc[mi] = acc[mi] * stats[mi][2] + _dot(vt_ref[mi // 2, ki], stats[mi][3])
        return tuple(st[0] for st in stats), tuple(st[1] for st in stats)

    def body(ki, carry):
        return lax.cond(lax.bitwise_and(ki, 1) == 0,
                        lambda cr: step(ki, cr, tbuf_a, tbuf_b),
                        lambda cr: step(ki, cr, tbuf_b, tbuf_a), carry)

    minf = (jnp.full((1, ATT_BLK), -jnp.inf, F32),) * n_map
    zero = (jnp.zeros((1, ATT_BLK), F32),) * n_map
    _, ls = lax.fori_loop(0, qi + 1, body, (minf, zero))
    for hp in range(ATT_HEADS):
        w = acc[2 * hp] / ls[2 * hp] - lam_ref[0, 0] * (acc[2 * hp + 1] / ls[2 * hp + 1])
        ms = jnp.mean(w * w, axis=0, keepdims=True)
        y = ((w * lax.rsqrt(ms + SUBLN_EPS)) * g_ref[...]) * out_scale
        o_ref[:, hp * width:(hp + 1) * width] = y.T.astype(o_ref.dtype)


def _diff_attn_call(zg, lam, bias_tiles, subln_g, passengers, *, batch, seq, out_scale):
    t = batch * seq
    nq = seq // ATT_BLK
    width = 2 * A_HEAD_DIM
    blk_w = ATT_HEADS * width
    q_col = GATE_COLS // blk_w
    k_col = q_col + A_HEADS // ATT_HEADS
    v_col = k_col + A_HEADS // ATT_HEADS
    n_steps = (A_HEADS // ATT_HEADS) * batch * nq
    riders = [w for w in passengers
              if w.shape[0] % (n_steps * 16) == 0 and w.size * 4 // n_steps <= CAST_SLAB_BYTES]

    def slab_spec(w):
        return pl.BlockSpec((w.shape[0] // n_steps, w.shape[1]), lambda h, b, i: ((h * batch + b) * nq + i, 0))

    outs = pl.pallas_call(
        functools.partial(_diff_attn_kernel, out_scale=out_scale, n_blk=nq, n_cast=len(riders)),
        grid=(A_HEADS // ATT_HEADS, batch, nq),
        in_specs=[
            pl.BlockSpec(memory_space=pltpu.SMEM),
            pl.BlockSpec((ATT_BLK, blk_w), lambda h, b, i: (b * nq + i, q_col + h)),
            pl.BlockSpec((seq, blk_w), lambda h, b, i: (b, k_col + h)),
            pl.BlockSpec((seq, blk_w), lambda h, b, i: (b, v_col + h)),
            pl.BlockSpec((ATT_HEADS, nq, ATT_BLK, ATT_BLK), lambda h, b, i: (h, 0, 0, 0)),
            pl.BlockSpec((width, 1), lambda h, b, i: (0, 0)),
        ] + [slab_spec(w) for w in riders],
        out_specs=[pl.BlockSpec((ATT_BLK, blk_w), lambda h, b, i: (b * nq + i, h))]
        + [slab_spec(w) for w in riders],
        out_shape=[jax.ShapeDtypeStruct((t, A_HEADS * width), BF16)]
        + [jax.ShapeDtypeStruct(w.shape, BF16) for w in riders],
        scratch_shapes=[pltpu.VMEM((ATT_HEADS, nq, width, ATT_BLK), BF16),
                        pltpu.VMEM((2 * ATT_HEADS, ATT_BLK, ATT_BLK), F32),
                        pltpu.VMEM((2 * ATT_HEADS, ATT_BLK, ATT_BLK), F32),
                        pltpu.VMEM((2 * ATT_HEADS, width, ATT_BLK), F32)],
        compiler_params=_cparams(("parallel", "parallel", "arbitrary")),
        name="diff_attn",
    )(lam, zg, zg, zg, bias_tiles, subln_g.reshape(width, 1), *riders)
    cast = iter(outs[1:])
    return outs[0], [next(cast) if any(w is r for r in riders) else w.astype(BF16) for w in passengers]


def _perm_matrix(dil, inverse):
    w = PERM_BLK // dil
    shift = w.bit_length() - 1
    a = lax.broadcasted_iota(jnp.int32, (PERM_BLK, PERM_BLK), 0)
    b = lax.broadcasted_iota(jnp.int32, (PERM_BLK, PERM_BLK), 1)
    dst, src = (b, a) if inverse else (a, b)
    c = lax.shift_right_logical(dst, shift)
    ll = lax.bitwise_and(dst, w - 1)
    return jnp.where(src == ll * dil + c, jnp.float32(1), jnp.float32(0))


def _deinterleave(pairs, perm, dil, seq):
    w = PERM_BLK // dil
    stream_len = seq // dil

    def body(b8, carry):
        r0 = pl.multiple_of(b8 * PERM_BLK, PERM_BLK)
        ys = [_dot(perm, src[pl.ds(r0, PERM_BLK), :]).astype(dst.dtype) for src, dst in pairs]
        for y, (_, dst) in zip(ys, pairs):
            for c in range(dil):
                d0 = pl.multiple_of(c * stream_len + b8 * w, w)
                dst[pl.ds(d0, w), :] = y[c * w:(c + 1) * w, :]
        return carry

    lax.fori_loop(0, seq // PERM_BLK, body, 0)


def _permute_rows(perm, x):
    if x.dtype == BF16:
        return _dot(perm, x)
    hi = x.astype(BF16)
    r1 = x - hi.astype(F32)
    mid = r1.astype(BF16)
    lo = (r1 - mid.astype(F32)).astype(BF16)
    y = _dot(perm, jnp.concatenate([hi, mid, lo], axis=1))
    n = x.shape[1]
    return y[:, :n] + (y[:, n:2 * n] + y[:, 2 * n:])


def _interleave(items, perm_inv, dil, seq):
    w = PERM_BLK // dil
    stream_len = seq // dil

    def body(b8, carry):
        for src, _, stack in items:
            for c in range(dil):
                s0 = pl.multiple_of(c * stream_len + b8 * w, w)
                stack[c * w:(c + 1) * w, :] = src[pl.ds(s0, w), :]
        outs = [_permute_rows(perm_inv, stack[...]) for _, _, stack in items]
        r0 = pl.multiple_of(b8 * PERM_BLK, PERM_BLK)
        for out, (_, dst, _) in zip(outs, items):
            dst[pl.ds(r0, PERM_BLK), :] = out.astype(dst.dtype)
        return carry

    lax.fori_loop(0, seq // PERM_BLK, body, 0)


def _transpose_blocks(src_ref, vt_ref, seq):
    def body(blk, carry):
        r0 = pl.multiple_of(blk * BLK, BLK)
        vt_ref[blk] = src_ref[pl.ds(r0, BLK), :].astype(F32).T.astype(vt_ref.dtype)
        return carry

    lax.fori_loop(0, seq // BLK, body, 0)


def _dil_windows(q_src, k_src, vt_ref, o_dst, lse_dst, bias_ref, ot_ref, lt_ref, windows):
    scale = B_HEAD_DIM ** -0.5
    c = scale * math.log2(math.e)
    tiles = []
    for rq, nk in windows:
        rk = rq - (nk - BLK)
        for hh in range(B_HEADS):
            cols = slice(hh * B_HEAD_DIM, (hh + 1) * B_HEAD_DIM)
            qh = q_src[pl.ds(rq, BLK), cols]
            kh = k_src[pl.ds(rk, nk), cols]
            tiles.append(_dot_nt(kh, qh) + bias_ref[0, hh, 2 * BLK - nk:, :])
    for wi, (rq, nk) in enumerate(windows):
        qb = rq // BLK if isinstance(rq, int) else lax.shift_right_logical(rq, BLK.bit_length() - 1)
        for hh in range(B_HEADS):
            cols = slice(hh * B_HEAD_DIM, (hh + 1) * B_HEAD_DIM)
            t = tiles[wi * B_HEADS + hh]
            m = jnp.max(t, axis=0, keepdims=True)
            p = jnp.exp2((t - m) * c)
            den = jnp.sum(p, axis=0, keepdims=True)
            if nk == BLK:
                vth = vt_ref[qb, cols, :]
            else:
                vth = jnp.concatenate([vt_ref[qb - 1, cols, :], vt_ref[qb, cols, :]], axis=1)
            ot_ref[wi, cols, :] = _dot(vth, p.astype(BF16)) / den
            lt_ref[wi, hh:hh + 1, :] = m * scale + jnp.log(den)
        o_dst[pl.ds(rq, BLK), :] = ot_ref[wi].T.astype(o_dst.dtype)
        lse_dst[pl.ds(rq, BLK), :] = lt_ref[wi].T


def _dil_streams(q_src, k_src, vt_ref, o_dst, lse_dst, bias_ref, ot_ref, lt_ref, dil, seq):
    stream_len = seq // dil
    nq = stream_len // BLK

    def run(windows):
        _dil_windows(q_src, k_src, vt_ref, o_dst, lse_dst, bias_ref, ot_ref, lt_ref, windows)

    if nq == 1:
        group = math.gcd(DIL_GROUP, dil)

        def stream_group(cg, carry):
            base = pl.multiple_of(cg * (group * stream_len), BLK)
            run([(pl.multiple_of(base + s * stream_len, BLK), BLK) for s in range(group)])
            return carry

        lax.fori_loop(0, dil // group, stream_group, 0)
        return

    def stream(c, carry):
        base = _mult(c * stream_len, BLK)
        n_tail = (nq - 1) % DIL_GROUP
        run([(base, BLK)] + [(_mult(base + (nq - 1 - s) * BLK, BLK), 2 * BLK) for s in range(n_tail)])

        def qgroup(j, carry2):
            rq = pl.multiple_of(base + (1 + DIL_GROUP * j) * BLK, BLK)
            run([(pl.multiple_of(rq + s * BLK, BLK), 2 * BLK) for s in range(DIL_GROUP)])
            return carry2

        n_groups = (nq - 1 - n_tail) // DIL_GROUP
        if n_groups:
            lax.fori_loop(0, n_groups, qgroup, 0)
        return carry

    if dil == 1:
        stream(0, 0)
    else:
        lax.fori_loop(0, dil, stream, 0)


def _head_expand(x, expand):
    hi = x.astype(BF16)
    lo = (x - hi.astype(F32)).astype(BF16)
    return _dot(jnp.concatenate([hi, lo], axis=1), expand)


def _dil_merge(o_src, lse_src, oacc, mrun, lrun, out_ref, first, last, seq):
    er = lax.broadcasted_iota(jnp.int32, (2 * LANES, B_HEADS * B_HEAD_DIM), 0)
    ec = lax.broadcasted_iota(jnp.int32, (2 * LANES, B_HEADS * B_HEAD_DIM), 1)
    expand = jnp.where(lax.shift_right_logical(ec, B_HEAD_DIM.bit_length() - 1) == lax.bitwise_and(er, LANES - 1),
                       jnp.float32(1), jnp.float32(0)).astype(BF16)

    def body(ch, carry):
        r0 = pl.multiple_of(ch * ROW_TM, ROW_TM)
        rows = pl.ds(r0, ROW_TM)
        lse = lse_src[rows, :]
        if first:
            mrun[rows, :] = lse
            lrun[rows, :] = jnp.ones_like(lse)
            oacc[rows, :] = o_src[rows, :].astype(F32)
            return carry
        m_old = mrun[rows, :]
        m_new = jnp.maximum(m_old, lse)
        a = jnp.exp(m_old - m_new)
        bw = jnp.exp(lse - m_new)
        l_new = lrun[rows, :] * a + bw
        if last:
            a = a / l_new
            bw = bw / l_new
        val = oacc[rows, :] * _head_expand(a, expand) + o_src[rows, :].astype(F32) * _head_expand(bw, expand)
        if last:
            out_ref[rows, :] = val.astype(out_ref.dtype)
        else:
            mrun[rows, :] = m_new
            lrun[rows, :] = l_new
            oacc[rows, :] = val
        return carry

    lax.fori_loop(0, seq // ROW_TM, body, 0)


def _dilated_kernel(q_ref, k_ref, v_ref, bias_ref, out_ref,
                    qs, ks, vs, vt, os_, lses, otok, lsetok, ostack, lstack, ot, lt, oacc, mrun, lrun,
                    *, seq):
    g = pl.program_id(1)

    @pl.when(g == 0)
    def _():
        lt[...] = jnp.zeros_like(lt)

    for gi, (_, dil) in enumerate(DILATED_GROUPS):

        @pl.when(g == gi)
        def _(gi=gi, dil=dil):
            first = gi == 0
            last = gi == N_DGROUPS - 1
            if dil == 1:
                _transpose_blocks(v_ref, vt, seq)
                _dil_streams(q_ref, k_ref, vt, otok, lsetok, bias_ref, ot, lt, 1, seq)
            else:
                perm = _perm_matrix(dil, inverse=False).astype(BF16)
                _deinterleave([(q_ref, qs), (k_ref, ks), (v_ref, vs)], perm, dil, seq)
                _transpose_blocks(vs, vt, seq)
                _dil_streams(qs, ks, vt, os_, lses, bias_ref, ot, lt, dil, seq)
                perm_inv = _perm_matrix(dil, inverse=True).astype(BF16)
                _interleave([(os_, otok, ostack), (lses, lsetok, lstack)], perm_inv, dil, seq)
            _dil_merge(otok, lsetok, oacc, mrun, lrun, out_ref, first, last, seq)


def _dilated_call(zg, bias_tiles, *, batch, seq):
    width = B_HEADS * B_HEAD_DIM
    q_col = (GATE_COLS + 3 * A_HEADS * 2 * A_HEAD_DIM) // width
    k_col = q_col + N_DGROUPS
    v_col = k_col + N_DGROUPS
    return pl.pallas_call(
        functools.partial(_dilated_kernel, seq=seq),
        grid=(batch, N_DGROUPS),
        in_specs=[
            pl.BlockSpec((seq, width), lambda b, g: (b, q_col + g)),
            pl.BlockSpec((seq, width), lambda b, g: (b, k_col + g)),
            pl.BlockSpec((seq, width), lambda b, g: (b, v_col + g)),
            pl.BlockSpec((1, B_HEADS, 2 * BLK, BLK), lambda b, g: (g, 0, 0, 0)),
        ],
        out_specs=pl.BlockSpec((seq, width), lambda b, g: (b, 0)),
        out_shape=jax.ShapeDtypeStruct((batch * seq, width), BF16),
        scratch_shapes=[
            pltpu.VMEM((seq, width), BF16), pltpu.VMEM((seq, width), BF16), pltpu.VMEM((seq, width), BF16),
            pltpu.VMEM((seq // BLK, width, BLK), BF16),
            pltpu.VMEM((seq, width), BF16), pltpu.VMEM((seq, LANES), F32),
            pltpu.VMEM((seq, width), BF16), pltpu.VMEM((seq, LANES), F32),
            pltpu.VMEM((PERM_BLK, width), BF16), pltpu.VMEM((PERM_BLK, LANES), F32),
            pltpu.VMEM((DIL_GROUP, width, BLK), F32), pltpu.VMEM((DIL_GROUP, LANES, BLK), F32),
            pltpu.VMEM((seq, width), F32), pltpu.VMEM((seq, LANES), F32), pltpu.VMEM((seq, LANES), F32),
        ],
        compiler_params=_cparams(("parallel", "arbitrary")),
        name="dilated_attn",
    )(zg, zg, zg, bias_tiles)


def _merge_router_kernel(oa_ref, ob_ref, gate_ref, x_ref, pa_ref, pb_ref, wo_ref, g_ref, wr_ref,
                         x1_ref, h2_ref, route_ref, cnt_ref, run_ref):
    tm, d = x_ref.shape

    @pl.when(pl.program_id(0) == 0)
    def _():
        run_ref[...] = jnp.zeros_like(run_ref)

    a = _dot(oa_ref[...], pa_ref[...])
    bm = _dot(ob_ref[...], pb_ref[...])
    merged = gate_ref[:, :d].astype(F32) * a + gate_ref[:, d:].astype(F32) * bm
    x1 = x_ref[...] + _dot(merged.astype(BF16), wo_ref[...])
    x1_ref[...] = x1
    h2 = _rmsnorm_val(x1, g_ref[...], RMS_EPS)
    h2_ref[...] = h2

    h_hi = h2.astype(BF16)
    h_lo = (h2 - h_hi.astype(F32)).astype(BF16)
    lt = (_dot(h_hi, wr_ref[0]) + (_dot(h_hi, wr_ref[1]) + _dot(h_lo, wr_ref[0]))).T
    coarse = [lt[i:i + 1, :] for i in range(N_EXPERT_GROUPS)]
    best = coarse[0]
    gsel = jnp.zeros((1, tm), jnp.int32)
    for i in range(1, N_EXPERT_GROUPS):
        upd = coarse[i] > best
        gsel = jnp.where(upd, i, gsel)
        best = jnp.where(upd, coarse[i], best)
    den = jnp.exp(coarse[0] - best)
    for i in range(1, N_EXPERT_GROUPS):
        den = den + jnp.exp(coarse[i] - best)
    pg = 1.0 / den

    fine = []
    for k in range(EXPERTS_PER_GROUP):
        f = lt[N_EXPERT_GROUPS + k:N_EXPERT_GROUPS + k + 1, :]
        for gi in range(1, N_EXPERT_GROUPS):
            r = N_EXPERT_GROUPS + gi * EXPERTS_PER_GROUP + k
            f = jnp.where(gsel == gi, lt[r:r + 1, :], f)
        fine.append(f)
    v0 = fine[0]
    i0 = jnp.zeros((1, tm), jnp.int32)
    for k in range(1, EXPERTS_PER_GROUP):
        upd = fine[k] > v0
        i0 = jnp.where(upd, k, i0)
        v0 = jnp.where(upd, fine[k], v0)
    v1 = jnp.full((1, tm), -jnp.inf, F32)
    i1 = jnp.zeros((1, tm), jnp.int32)
    for k in range(EXPERTS_PER_GROUP):
        upd = jnp.where(i0 != k, jnp.where(fine[k] > v1, 1, 0), 0) == 1
        i1 = jnp.where(upd, k, i1)
        v1 = jnp.where(upd, fine[k], v1)
    e1w = jnp.exp(v1 - v0)
    gate0 = pg * (1.0 / (1.0 + e1w))
    gate1 = pg * (e1w / (1.0 + e1w))
    e0 = gsel * EXPERTS_PER_GROUP + i0
    e1 = gsel * EXPERTS_PER_GROUP + i1

    eidx = lax.broadcasted_iota(jnp.int32, (N_EXPERTS, tm), 0)
    oh0 = jnp.where(eidx == e0, jnp.float32(1), jnp.float32(0))
    oh1 = jnp.where(eidx == e1, jnp.float32(1), jnp.float32(0))
    ta = lax.broadcasted_iota(jnp.int32, (tm, tm), 0)
    tb = lax.broadcasted_iota(jnp.int32, (tm, tm), 1)
    before = jnp.where(ta < tb, jnp.float32(1), jnp.float32(0)).astype(BF16)
    pre0 = _dot(oh0.astype(BF16), before)
    pre1 = _dot(oh1.astype(BF16), before)
    run = run_ref[:, 0:1]
    tot0 = jnp.sum(oh0, axis=1, keepdims=True)
    tot1 = jnp.sum(oh1, axis=1, keepdims=True)
    rank0 = jnp.sum(oh0 * (run + pre0), axis=0, keepdims=True)
    rank1 = jnp.sum(oh1 * (run + tot0 + pre1), axis=0, keepdims=True)
    new_run = jnp.broadcast_to(run + tot0 + tot1, run_ref.shape)
    run_ref[...] = new_run
    cnt_ref[...] = new_run

    route_ref[0:1, :] = e0.astype(F32)
    route_ref[1:2, :] = e1.astype(F32)
    route_ref[2:3, :] = gate0
    route_ref[3:4, :] = gate1
    route_ref[4:5, :] = rank0
    route_ref[5:6, :] = rank1
    route_ref[6:8, :] = jnp.zeros((2, tm), F32)


def _merge_router_call(oa, ob, zg, x2d, pa, pb, wo, g, wr_t):
    t, d = x2d.shape
    tm = ROW_TM
    const = dict(pipeline_mode=pl.Buffered(1))
    return pl.pallas_call(
        _merge_router_kernel,
        grid=(t // tm,),
        in_specs=[
            pl.BlockSpec((tm, oa.shape[1]), lambda i: (i, 0)),
            pl.BlockSpec((tm, ob.shape[1]), lambda i: (i, 0)),
            pl.BlockSpec((tm, GATE_COLS), lambda i: (i, 0)),
            pl.BlockSpec((tm, d), lambda i: (i, 0)),
            pl.BlockSpec(pa.shape, lambda i: (0, 0), **const),
            pl.BlockSpec(pb.shape, lambda i: (0, 0), **const),
            pl.BlockSpec(wo.shape, lambda i: (0, 0), **const),
            pl.BlockSpec((1, d), lambda i: (0, 0)),
            pl.BlockSpec(wr_t.shape, lambda i: (0, 0, 0), **const),
        ],
        out_specs=[
            pl.BlockSpec((tm, d), lambda i: (i, 0)),
            pl.BlockSpec((tm, d), lambda i: (i, 0)),
            pl.BlockSpec((8, tm), lambda i: (0, i)),
            pl.BlockSpec((N_EXPERTS, LANES), lambda i: (0, 0)),
        ],
        out_shape=[
            jax.ShapeDtypeStruct((t, d), F32),
            jax.ShapeDtypeStruct((t, d), F32),
            jax.ShapeDtypeStruct((8, t), F32),
            jax.ShapeDtypeStruct((N_EXPERTS, LANES), F32),
        ],
        scratch_shapes=[pltpu.VMEM((N_EXPERTS, LANES), F32)],
        compiler_params=_cparams(("arbitrary",)),
        name="merge_router",
    )(oa, ob, zg, x2d, pa, pb, wo, g, wr_t)


def _row_copy(src, s_row, dst, d_row, sem):
    return pltpu.make_async_copy(src.at[pl.ds(s_row, 1)], dst.at[pl.ds(d_row, 1)], sem)


def _dispatch_kernel(dest_ref, zstart_ref, zlen_ref, h_ref, x_hbm, zero_ref, sem, zsem):
    step = pl.program_id(0)

    def issue(t, carry):
        for k in range(TOP_K):
            _row_copy(h_ref, t, x_hbm, dest_ref[0, k, t], sem.at[k]).start()
        return carry

    lax.fori_loop(0, DISPATCH_TOK, issue, 0, unroll=8)

    @pl.when(step == 0)
    def _():
        zero_ref[...] = jnp.zeros_like(zero_ref)

        def group_copy(r8):
            return pltpu.make_async_copy(zero_ref, x_hbm.at[pl.ds(pl.multiple_of(r8, 8), 8)], zsem.at[1])

        def segment(e, wait):
            start = zstart_ref[e]
            n_head = jnp.minimum(lax.bitwise_and(-start, 7), zlen_ref[e])
            n_group = lax.shift_right_logical(zlen_ref[e] - n_head, 3)

            def head(r, carry):
                cp = _row_copy(zero_ref, 0, x_hbm, 0 if wait else start + r, zsem.at[0])
                cp.wait() if wait else cp.start()
                return carry

            def group(j, carry):
                cp = group_copy(0 if wait else start + n_head + 8 * j)
                cp.wait() if wait else cp.start()
                return carry

            lax.fori_loop(0, n_head, head, 0)
            lax.fori_loop(0, n_group, group, 0)

        lax.fori_loop(0, N_EXPERTS + 1, lambda e, c: (segment(e, False), c)[1], 0)
        lax.fori_loop(0, N_EXPERTS + 1, lambda e, c: (segment(e, True), c)[1], 0)

    for k in range(TOP_K):
        pltpu.make_async_copy(h_ref, x_hbm.at[pl.ds(0, DISPATCH_TOK)], sem.at[k]).wait()


def _dispatch_call(dest_blocks, zstart, zlen, h2, n_rows):
    t, d = h2.shape
    return pl.pallas_call(
        _dispatch_kernel,
        grid=(t // DISPATCH_TOK,),
        in_specs=[
            pl.BlockSpec((1, TOP_K, DISPATCH_TOK), lambda i: (i, 0, 0), memory_space=pltpu.SMEM),
            pl.BlockSpec(memory_space=pltpu.SMEM),
            pl.BlockSpec(memory_space=pltpu.SMEM),
            pl.BlockSpec((DISPATCH_TOK, d), lambda i: (i, 0)),
        ],
        out_specs=pl.BlockSpec(memory_space=pl.ANY),
        out_shape=jax.ShapeDtypeStruct((n_rows, d), F32),
        scratch_shapes=[pltpu.VMEM((8, d), F32), pltpu.SemaphoreType.DMA((TOP_K,)),
                        pltpu.SemaphoreType.DMA((2,))],
        compiler_params=_cparams(("arbitrary",)),
        name="dispatch",
    )(dest_blocks, zstart, zlen, h2)


def _expert_kernel(blk_e_ref, blk_valid_ref, x_ref, w1_ref, w3_ref, w2_ref, y_ref):
    i = pl.program_id(0)

    @pl.when(blk_valid_ref[i] == 1)
    def _():
        x = x_ref[...].astype(BF16)
        a = _dot(x, w1_ref[0])
        b = _dot(x, w3_ref[0])
        hdn = (a * _sigmoid(a)) * b
        y_ref[...] = _dot(hdn.astype(BF16), w2_ref[0])

    @pl.when(blk_valid_ref[i] == 0)
    def _():
        y_ref[...] = jnp.zeros_like(y_ref)


def _expert_call(blk_e, blk_valid, xin, w1, w3, w2):
    n_rows, d = xin.shape
    de = w1.shape[2]
    grid_spec = pltpu.PrefetchScalarGridSpec(
        num_scalar_prefetch=2,
        grid=(n_rows // EXP_TM,),
        in_specs=[
            pl.BlockSpec((EXP_TM, d), lambda i, be, bv: (i, 0)),
            pl.BlockSpec((1, d, de), lambda i, be, bv: (be[i], 0, 0)),
            pl.BlockSpec((1, d, de), lambda i, be, bv: (be[i], 0, 0)),
            pl.BlockSpec((1, de, d), lambda i, be, bv: (be[i], 0, 0)),
        ],
        out_specs=pl.BlockSpec((EXP_TM, d), lambda i, be, bv: (i, 0)),
    )
    return pl.pallas_call(
        _expert_kernel,
        grid_spec=grid_spec,
        out_shape=jax.ShapeDtypeStruct((n_rows, d), F32),
        compiler_params=_cparams(("arbitrary",)),
        name="experts",
    )(blk_e, blk_valid, xin, w1, w3, w2)


def _combine_kernel(dest_ref, dnext_ref, y_hbm, x1_ref, gt_ref, p_ref, gple_ref, wg_ref, wp_ref, gfin_ref,
                    o_ref, ybuf, sem, *, final_norm, n_steps):
    tm = x1_ref.shape[0]
    i = pl.program_id(0)
    slot = lax.bitwise_and(i, 1)

    def gather(d_ref, s):
        def issue(t, carry):
            for k in range(TOP_K):
                _row_copy(y_hbm, d_ref[0, k, t], ybuf.at[s, k], t, sem.at[s, k]).start()
            return carry

        lax.fori_loop(0, tm, issue, 0, unroll=8)

    @pl.when(i == 0)
    def _():
        gather(dest_ref, 0)

    @pl.when(i + 1 < n_steps)
    def _():
        gather(dnext_ref, 1 - slot)

    for k in range(TOP_K):
        pltpu.make_async_copy(y_hbm.at[pl.ds(0, tm)], ybuf.at[slot, k], sem.at[slot, k]).wait()

    for r0 in range(0, tm, ROW_TM):
        rows = slice(r0, r0 + ROW_TM)
        pp = _dot(p_ref[rows, :].astype(BF16), wp_ref[...])
        y = ybuf[slot, 0, rows, :] * gt_ref[rows, 2:3] + ybuf[slot, 1, rows, :] * gt_ref[rows, 3:4]
        x2 = x1_ref[rows, :] + y
        hn = _rmsnorm_val(x2, gple_ref[...], RMS_EPS)
        gate = _sigmoid(_dot(hn.astype(BF16), wg_ref[...]))
        x3 = x2 + gate * pp
        if final_norm:
            o_ref[rows, :] = _rmsnorm_val(x3, gfin_ref[...], RMS_EPS)
        else:
            o_ref[rows, :] = x3


def _combine_call(dest_blocks, yb, x1, gates_t, p2d, g_ple, w_gate, w_proj, g_fin, final_norm):
    t, d = x1.shape
    tm = COMBINE_TM
    const = dict(pipeline_mode=pl.Buffered(1))
    return pl.pallas_call(
        functools.partial(_combine_kernel, final_norm=final_norm, n_steps=t // tm),
        grid=(t // tm,),
        in_specs=[
            pl.BlockSpec((1, TOP_K, tm), lambda i: (i, 0, 0), memory_space=pltpu.SMEM),
            pl.BlockSpec((1, TOP_K, tm), lambda i: (jnp.minimum(i + 1, t // tm - 1), 0, 0),
                         memory_space=pltpu.SMEM),
            pl.BlockSpec(memory_space=pl.ANY),
            pl.BlockSpec((tm, d), lambda i: (i, 0)),
            pl.BlockSpec((tm, 8), lambda i: (i, 0)),
            pl.BlockSpec((tm, p2d.shape[1]), lambda i: (i, 0)),
            pl.BlockSpec((1, d), lambda i: (0, 0)),
            pl.BlockSpec(w_gate.shape, lambda i: (0, 0), **const),
            pl.BlockSpec(w_proj.shape, lambda i: (0, 0), **const),
            pl.BlockSpec((1, d), lambda i: (0, 0)),
        ],
        out_specs=pl.BlockSpec((tm, d), lambda i: (i, 0)),
        out_shape=jax.ShapeDtypeStruct((t, d), F32),
        scratch_shapes=[pltpu.VMEM((2, TOP_K, tm, d), F32), pltpu.SemaphoreType.DMA((2, TOP_K))],
        compiler_params=_cparams(("arbitrary",)),
        name="combine",
    )(dest_blocks, dest_blocks, yb, x1, gates_t, p2d, g_ple, w_gate, w_proj, g_fin)


def _routing_plan(route, counts_f, t):
    counts = counts_f[:, 0].astype(jnp.int32)
    pcounts = ((counts + EXP_TM - 1) // EXP_TM) * EXP_TM
    pends = jnp.cumsum(pcounts)
    pstarts = pends - pcounts
    eid = route[0:2].astype(jnp.int32)
    rank = route[4:6].astype(jnp.int32)
    onehot = eid[..., None] == jnp.arange(N_EXPERTS, dtype=jnp.int32)
    dest = jnp.sum(jnp.where(onehot, pstarts, 0), axis=-1) + rank
    n_rows = TOP_K * t + N_EXPERTS * EXP_TM
    n_blocks = n_rows // EXP_TM
    blk_row = jnp.arange(n_blocks, dtype=jnp.int32) * EXP_TM
    blk_e = jnp.sum((pends[None, :] <= blk_row[:, None]).astype(jnp.int32), axis=1)
    blk_valid = (blk_row < pends[-1]).astype(jnp.int32)
    last_e = jnp.sum((pends <= pends[-1] - 1).astype(jnp.int32))
    blk_e = jnp.minimum(blk_e, last_e)
    zstart = jnp.concatenate([pstarts + counts, pends[-1:]]).astype(jnp.int32)
    zlen = jnp.concatenate([pcounts - counts, n_rows - pends[-1:]]).astype(jnp.int32)
    return dest, blk_e, blk_valid, zstart, zlen, n_rows


def _blocked(dest, tok_per_block):
    t = dest.shape[1]
    return dest.reshape(TOP_K, t // tok_per_block, tok_per_block).transpose(1, 0, 2)


def kernel(x, p, rel_bias, norm_mix_g, w_in, w_gate, lambda_q1, lambda_k1, lambda_q2, lambda_k2, subln_g,
           w_proj_a, w_proj_b, w_out, norm_ffn_g, w_coarse, w_fine, w1, w3, w2, norm_ple_g, w_ple_gate,
           w_ple_proj, final_norm_g):
    batch, seq, d = x.shape
    depth = w_in.shape[0]
    t = batch * seq
    assert seq % PERM_BLK == 0 and t % min(PROJ_TM, t) == 0
    assert all(seq // dil >= BLK and win // dil == BLK for win, dil in DILATED_GROUPS)

    nq = seq // ATT_BLK
    bias_a, w_cat0 = _bias_tiles_call(rel_bias, n_heads=A_HEADS, n_off=nq, tq=ATT_BLK, tk=ATT_BLK,
                                      off_mult=ATT_BLK, off_add=0, dil=1, max_rel=seq, head0=0, name="bias_diff",
                                      key_major=True, mult=A_HEAD_DIM ** 0.5, concat_cast=(w_gate[0], w_in[0]))
    bias_b = jnp.concatenate([
        _bias_tiles_call(rel_bias, n_heads=B_HEADS, n_off=1, tq=2 * BLK, tk=BLK, off_mult=0, off_add=BLK,
                         dil=dil, max_rel=win // dil, head0=A_HEADS + gi * B_HEADS, name="bias_dil%d" % gi,
                         key_major=True, mult=B_HEAD_DIM ** 0.5)[0].reshape(1, B_HEADS, 2 * BLK, BLK)
        for gi, (win, dil) in enumerate(DILATED_GROUPS)], axis=0)

    x2d = x.reshape(t, d)
    for layer in range(depth):
        lam_init = 0.8 - 0.6 * math.exp(-0.3 * layer)
        lam = _lam_call(lambda_q1[layer:layer + 1], lambda_k1[layer:layer + 1],
                        lambda_q2[layer:layer + 1], lambda_k2[layer:layer + 1], lam_init)
        if layer == 0 and w_cat0 is not None:
            w_cat = w_cat0
        else:
            w_cat = jnp.concatenate([w_gate[layer], w_in[layer]], axis=1).astype(BF16)
        zg = _inproj_call(x2d, norm_mix_g[layer:layer + 1], w_cat)
        n_exp, _, d_exp = w1[layer].shape
        oa, (w1b, w3b, w2b) = _diff_attn_call(
            zg, lam, bias_a, subln_g[layer:layer + 1],
            [w1[layer].reshape(n_exp * d, d_exp), w3[layer].reshape(n_exp * d, d_exp),
             w2[layer].reshape(n_exp * d_exp, d)],
            batch=batch, seq=seq, out_scale=1.0 - lam_init)
        ob = _dilated_call(zg, bias_b, batch=batch, seq=seq)

        wr_t = jnp.concatenate([
            w_coarse[layer].T,
            w_fine[layer].transpose(0, 2, 1).reshape(N_EXPERTS, d),
            jnp.zeros((LANES - N_EXPERT_GROUPS - N_EXPERTS, d), F32)], axis=0)
        wr_t = wr_t.T
        wr_hi = wr_t.astype(BF16)
        wr_t = jnp.stack([wr_hi, (wr_t - wr_hi.astype(F32)).astype(BF16)], axis=0)
        x1, h2, route, counts = _merge_router_call(
            oa, ob, zg, x2d, w_proj_a[layer].astype(BF16), w_proj_b[layer].astype(BF16),
            w_out[layer].astype(BF16), norm_ffn_g[layer:layer + 1], wr_t)

        dest, blk_e, blk_valid, zstart, zlen, n_rows = _routing_plan(route, counts, t)
        xin = _dispatch_call(_blocked(dest, DISPATCH_TOK), zstart, zlen, h2, n_rows)
        yb = _expert_call(blk_e, blk_valid, xin, w1b.reshape(n_exp, d, d_exp), w3b.reshape(n_exp, d, d_exp),
                          w2b.reshape(n_exp, d_exp, d))
        x2d = _combine_call(_blocked(dest, COMBINE_TM), yb, x1, route.T, p[layer].reshape(t, -1),
                            norm_ple_g[layer:layer + 1], w_ple_gate[layer].astype(BF16),
                            w_ple_proj[layer].astype(BF16), final_norm_g.reshape(1, d),
                            final_norm=layer == depth - 1)
    return x2d.reshape(batch, seq, d)
```

```python
import functools
import math

import jax
import jax.numpy as jnp
from jax import lax
from jax.experimental import pallas as pl
from jax.experimental.pallas import tpu as pltpu

F32 = jnp.float32
BF16 = jnp.bfloat16

BLK = 128
NEG_INF = -1e30
RMS_EPS = 1e-6
SUBLN_EPS = 1e-5
N_BUCKETS = 32
MAX_DISTANCE = 2048
A_HEADS = 8
A_HEAD_DIM = 128
DILATED_GROUPS = ((128, 1), (512, 4), (2048, 16))
N_DGROUPS = 3
B_HEADS = 8
B_HEAD_DIM = 64
N_EXPERT_GROUPS = 4
EXPERTS_PER_GROUP = 8
N_EXPERTS = 32
TOP_K = 2

LANES = 128
SUBLANES = 8
VMEM_LIMIT = 56 * 1024 * 1024
ATT_BLK = 256
ATT_HEADS = 2
PERM_BLK = 256
GATE_COLS = 4096
PROJ_TN = 512
PROJ_TM = 2048
ROW_TM = 256
COMBINE_TM = 512
EXP_TM = 256
DISPATCH_TOK = 2048
NORM_CHUNK = 128
CAST_SLAB_BYTES = 1 << 20
DIL_GROUP = 4
TOEPLITZ_LANES = 512


def _t5_thresholds():
    max_exact = N_BUCKETS // 2
    out = []
    for k in range(1, N_BUCKETS - max_exact):
        out.append(int(math.ceil(max_exact * (MAX_DISTANCE / max_exact) ** (k / (N_BUCKETS - max_exact)))))
    return tuple(out)


T5_THRESHOLDS = _t5_thresholds()


def _cparams(sem, vmem=VMEM_LIMIT):
    return pltpu.CompilerParams(dimension_semantics=sem, vmem_limit_bytes=vmem)


def _mult(x, m):
    return x if isinstance(x, int) else pl.multiple_of(x, m)


def _sigmoid(x):
    return 0.5 * jnp.tanh(0.5 * x) + 0.5


def _dot(a, b, **kw):
    return jnp.dot(a, b, preferred_element_type=F32, **kw)


def _dot_nt(a, b, **kw):
    return lax.dot_general(a, b, (((1,), (1,)), ((), ())), preferred_element_type=F32, **kw)


def _lam_kernel(q1_ref, k1_ref, q2_ref, k2_ref, o_ref, *, lam_init):
    s1 = jnp.sum(q1_ref[...] * k1_ref[...], axis=-1, keepdims=True)
    s2 = jnp.sum(q2_ref[...] * k2_ref[...], axis=-1, keepdims=True)
    o_ref[...] = jnp.exp(s1) - jnp.exp(s2) + lam_init


def _lam_call(lq1, lk1, lq2, lk2, lam_init):
    return pl.pallas_call(
        functools.partial(_lam_kernel, lam_init=lam_init),
        out_shape=jax.ShapeDtypeStruct((1, 1), F32),
        name="lam",
    )(lq1, lk1, lq2, lk2)


def _bias_tile_kernel(tab_ref, *rest, tq, tk, off_mult, off_add, dil, max_rel, head0, key_major, mult):
    n_pass = max(len(rest) - 2, 0)
    o_ref = rest[n_pass]
    if n_pass:
        col = 0
        for src in rest[:n_pass]:
            rest[-1][:, col:col + src.shape[1]] = src[...].astype(rest[-1].dtype)
            col += src.shape[1]
    h = pl.program_id(0)
    n = pl.program_id(1)
    assert key_major and tq + tk <= TOEPLITZ_LANES
    m = lax.broadcasted_iota(jnp.int32, (8, TOEPLITZ_LANES), 1)
    rel = jnp.where(m < tk, m, m - TOEPLITZ_LANES) + (n * off_mult + off_add)
    dist = rel * dil
    large = jnp.full(m.shape, N_BUCKETS // 2, jnp.int32)
    for thr in T5_THRESHOLDS:
        large = large + jnp.where(dist >= thr, 1, 0)
    bucket = jnp.where(dist < N_BUCKETS // 2, dist, large)
    acc = jnp.zeros(m.shape, F32)
    for b in range(N_BUCKETS):
        acc = jnp.where(bucket == b, tab_ref[b, head0 + h], acc)
    valid = jnp.where(rel >= 0, jnp.where(rel <= max_rel, 1, 0), 0)
    vec = jnp.where(valid == 1, acc * mult, NEG_INF)[0:1, :]
    rows = pltpu.roll(jnp.broadcast_to(vec, (tq, TOEPLITZ_LANES)), 0, 1, stride=1, stride_axis=0)
    o_ref[0, 0] = rows[:, :tk]


def _bias_tiles_call(rel_bias, *, n_heads, n_off, tq, tk, off_mult, off_add, dil, max_rel, head0, name,
                     key_major=False, mult=1.0, concat_cast=()):
    kern = functools.partial(_bias_tile_kernel, tq=tq, tk=tk, off_mult=off_mult, off_add=off_add,
                             dil=dil, max_rel=max_rel, head0=head0, key_major=key_major, mult=mult)
    n_steps = n_heads * n_off
    if concat_cast and concat_cast[0].shape[0] % (n_steps * 16):
        concat_cast = ()
    in_specs = [pl.BlockSpec(memory_space=pltpu.SMEM)]
    out_specs = [pl.BlockSpec((1, 1, tq, tk), lambda h, n: (h, n, 0, 0))]
    out_shape = [jax.ShapeDtypeStruct((n_heads, n_off, tq, tk), F32)]
    if concat_cast:
        rows = concat_cast[0].shape[0]
        cols = sum(w.shape[1] for w in concat_cast)
        in_specs += [pl.BlockSpec((rows // n_steps, w.shape[1]), lambda h, n: (h * n_off + n, 0))
                     for w in concat_cast]
        out_specs.append(pl.BlockSpec((rows // n_steps, cols), lambda h, n: (h * n_off + n, 0)))
        out_shape.append(jax.ShapeDtypeStruct((rows, cols), BF16))
    outs = pl.pallas_call(
        kern,
        grid=(n_heads, n_off),
        in_specs=in_specs,
        out_specs=out_specs,
        out_shape=out_shape,
        compiler_params=_cparams(("parallel", "parallel")),
        name=name,
    )(rel_bias, *concat_cast)
    return outs[0], (outs[1] if concat_cast else None)


def _rmsnorm_rows(x_ref, g_ref, out_ref, eps):
    rows = x_ref.shape[0]
    g = g_ref[...]

    def body(c, carry):
        r0 = pl.multiple_of(c * NORM_CHUNK, NORM_CHUNK)
        x = x_ref[pl.ds(r0, NORM_CHUNK), :]
        ms = jnp.mean(x * x, axis=-1, keepdims=True)
        out_ref[pl.ds(r0, NORM_CHUNK), :] = ((x * lax.rsqrt(ms + eps)) * g).astype(out_ref.dtype)
        return carry

    lax.fori_loop(0, rows // NORM_CHUNK, body, 0)


def _rmsnorm_val(x, g, eps):
    ms = jnp.mean(x * x, axis=-1, keepdims=True)
    return (x * lax.rsqrt(ms + eps)) * g


def _inproj_kernel(x_ref, g_ref, w_ref, o_ref, h_ref, *, n_gate_blocks):
    j = pl.program_id(1)

    @pl.when(j == 0)
    def _():
        _rmsnorm_rows(x_ref, g_ref, h_ref, RMS_EPS)

    acc = _dot(h_ref[...], w_ref[...])

    @pl.when(j < n_gate_blocks)
    def _():
        o_ref[...] = _sigmoid(acc).astype(o_ref.dtype)

    @pl.when(j >= n_gate_blocks)
    def _():
        o_ref[...] = acc.astype(o_ref.dtype)


def _inproj_call(x2d, g, w_cat):
    t, d = x2d.shape
    n = w_cat.shape[1]
    tm = min(PROJ_TM, t)
    return pl.pallas_call(
        functools.partial(_inproj_kernel, n_gate_blocks=GATE_COLS // PROJ_TN),
        grid=(t // tm, n // PROJ_TN),
        in_specs=[
            pl.BlockSpec((tm, d), lambda i, j: (i, 0)),
            pl.BlockSpec((1, d), lambda i, j: (0, 0)),
            pl.BlockSpec((d, PROJ_TN), lambda i, j: (0, j)),
        ],
        out_specs=pl.BlockSpec((tm, PROJ_TN), lambda i, j: (i, j)),
        out_shape=jax.ShapeDtypeStruct((t, n), BF16),
        scratch_shapes=[pltpu.VMEM((tm, d), BF16)],
        compiler_params=_cparams(("parallel", "arbitrary")),
        name="inproj",
    )(x2d, g, w_cat)


def _diff_attn_kernel(lam_ref, q_ref, k_ref, v_ref, bias_ref, g_ref, *rest, out_scale, n_blk, n_cast):
    cast_in = rest[:n_cast]
    o_ref = rest[n_cast]
    cast_out = rest[n_cast + 1:2 * n_cast + 1]
    vt_ref, tbuf_a, tbuf_b, acc = rest[2 * n_cast + 1:]
    for src, dst in zip(cast_in, cast_out):
        dst[...] = src[...].astype(dst.dtype)
    qi = pl.program_id(2)
    width = 2 * A_HEAD_DIM
    n_map = 2 * ATT_HEADS
    c = (A_HEAD_DIM ** -0.5) * math.log2(math.e)

    @pl.when(qi == 0)
    def _():
        def transpose_block(b, carry):
            r0 = pl.multiple_of(b * ATT_BLK, ATT_BLK)
            for hp in range(ATT_HEADS):
                vb = v_ref[pl.ds(r0, ATT_BLK), hp * width:(hp + 1) * width]
                vt_ref[hp, b] = vb.astype(F32).T.astype(BF16)
            return carry

        lax.fori_loop(0, n_blk, transpose_block, 0)

    qs = [q_ref[:, mi * A_HEAD_DIM:(mi + 1) * A_HEAD_DIM] for mi in range(n_map)]
    acc[...] = jnp.zeros_like(acc)

    def scores(ki, dst):
        k0 = pl.multiple_of(ki * ATT_BLK, ATT_BLK)
        for mi in range(n_map):
            kb = k_ref[pl.ds(k0, ATT_BLK), mi * A_HEAD_DIM:(mi + 1) * A_HEAD_DIM]
            dst[mi] = _dot_nt(kb, qs[mi]) + bias_ref[mi // 2, qi - ki]

    def softmax(t, m, l):
        m_new = jnp.maximum(m, jnp.max(t, axis=0, keepdims=True))
        alpha = jnp.exp2((m - m_new) * c)
        p = jnp.exp2((t - m_new) * c)
        return m_new, alpha * l + jnp.sum(p, axis=0, keepdims=True), alpha, p.astype(BF16)

    scores(0, tbuf_a)

    def step(ki, carry, cur, nxt):
        ms, ls = carry
        stats = [softmax(cur[mi], ms[mi], ls[mi]) for mi in range(n_map)]
        scores(jnp.minimum(ki + 1, qi), nxt)
        for mi in range(n_map):
            acc[mi] = acc[mi] * stats[mi][2] + _dot(vt_ref[mi // 2, ki], stats[mi][3])
        return tuple(st[0] for st in stats), tuple(st[1] for st in stats)

    def body(ki, carry):
        return lax.cond(lax.bitwise_and(ki, 1) == 0,
                        lambda cr: step(ki, cr, tbuf_a, tbuf_b),
                        lambda cr: step(ki, cr, tbuf_b, tbuf_a), carry)

    minf = (jnp.full((1, ATT_BLK), -jnp.inf, F32),) * n_map
    zero = (jnp.zeros((1, ATT_BLK), F32),) * n_map
    _, ls = lax.fori_loop(0, qi + 1, body, (minf, zero))
    for hp in range(ATT_HEADS):
        w = acc[2 * hp] / ls[2 * hp] - lam_ref[0, 0] * (acc[2 * hp + 1] / ls[2 * hp + 1])
        ms = jnp.mean(w * w, axis=0, keepdims=True)
        y = ((w * lax.rsqrt(ms + SUBLN_EPS)) * g_ref[...]) * out_scale
        o_ref[:, hp * width:(hp + 1) * width] = y.T.astype(o_ref.dtype)


def _diff_attn_call(zg, lam, bias_tiles, subln_g, passengers, *, batch, seq, out_scale):
    t = batch * seq
    nq = seq // ATT_BLK
    width = 2 * A_HEAD_DIM
    blk_w = ATT_HEADS * width
    q_col = GATE_COLS // blk_w
    k_col = q_col + A_HEADS // ATT_HEADS
    v_col = k_col + A_HEADS // ATT_HEADS
    n_steps = (A_HEADS // ATT_HEADS) * batch * nq
    riders = [w for w in passengers
              if w.shape[0] % (n_steps * 16) == 0 and w.size * 4 // n_steps <= CAST_SLAB_BYTES]

    def slab_spec(w):
        return pl.BlockSpec((w.shape[0] // n_steps, w.shape[1]), lambda h, b, i: ((h * batch + b) * nq + i, 0))

    outs = pl.pallas_call(
        functools.partial(_diff_attn_kernel, out_scale=out_scale, n_blk=nq, n_cast=len(riders)),
        grid=(A_HEADS // ATT_HEADS, batch, nq),
        in_specs=[
            pl.BlockSpec(memory_space=pltpu.SMEM),
            pl.BlockSpec((ATT_BLK, blk_w), lambda h, b, i: (b * nq + i, q_col + h)),
            pl.BlockSpec((seq, blk_w), lambda h, b, i: (b, k_col + h)),
            pl.BlockSpec((seq, blk_w), lambda h, b, i: (b, v_col + h)),
            pl.BlockSpec((ATT_HEADS, nq, ATT_BLK, ATT_BLK), lambda h, b, i: (h, 0, 0, 0)),
            pl.BlockSpec((width, 1), lambda h, b, i: (0, 0)),
        ] + [slab_spec(w) for w in riders],
        out_specs=[pl.BlockSpec((ATT_BLK, blk_w), lambda h, b, i: (b * nq + i, h))]
        + [slab_spec(w) for w in riders],
        out_shape=[jax.ShapeDtypeStruct((t, A_HEADS * width), BF16)]
        + [jax.ShapeDtypeStruct(w.shape, BF16) for w in riders],
        scratch_shapes=[pltpu.VMEM((ATT_HEADS, nq, width, ATT_BLK), BF16),
                        pltpu.VMEM((2 * ATT_HEADS, ATT_BLK, ATT_BLK), F32),
                        pltpu.VMEM((2 * ATT_HEADS, ATT_BLK, ATT_BLK), F32),
                        pltpu.VMEM((2 * ATT_HEADS, width, ATT_BLK), F32)],
        compiler_params=_cparams(("parallel", "parallel", "arbitrary")),
        name="diff_attn",
    )(lam, zg, zg, zg, bias_tiles, subln_g.reshape(width, 1), *riders)
    cast = iter(outs[1:])
    return outs[0], [next(cast) if any(w is r for r in riders) else w.astype(BF16) for w in passengers]


def _perm_matrix(dil, inverse):
    w = PERM_BLK // dil
    shift = w.bit_length() - 1
    a = lax.broadcasted_iota(jnp.int32, (PERM_BLK, PERM_BLK), 0)
    b = lax.broadcasted_iota(jnp.int32, (PERM_BLK, PERM_BLK), 1)
    dst, src = (b, a) if inverse else (a, b)
    c = lax.shift_right_logical(dst, shift)
    ll = lax.bitwise_and(dst, w - 1)
    return jnp.where(src == ll * dil + c, jnp.float32(1), jnp.float32(0))


def _deinterleave(pairs, perm, dil, seq):
    w = PERM_BLK // dil
    stream_len = seq // dil

    def body(b8, carry):
        r0 = pl.multiple_of(b8 * PERM_BLK, PERM_BLK)
        ys = [_dot(perm, src[pl.ds(r0, PERM_BLK), :]).astype(dst.dtype) for src, dst in pairs]
        for y, (_, dst) in zip(ys, pairs):
            for c in range(dil):
                d0 = pl.multiple_of(c * stream_len + b8 * w, w)
                dst[pl.ds(d0, w), :] = y[c * w:(c + 1) * w, :]
        return carry

    lax.fori_loop(0, seq // PERM_BLK, body, 0)


def _permute_rows(perm, x):
    if x.dtype == BF16:
        return _dot(perm, x)
    hi = x.astype(BF16)
    r1 = x - hi.astype(F32)
    mid = r1.astype(BF16)
    lo = (r1 - mid.astype(F32)).astype(BF16)
    y = _dot(perm, jnp.concatenate([hi, mid, lo], axis=1))
    n = x.shape[1]
    return y[:, :n] + (y[:, n:2 * n] + y[:, 2 * n:])


def _interleave(items, perm_inv, dil, seq):
    w = PERM_BLK // dil
    stream_len = seq // dil

    def body(b8, carry):
        for src, _, stack in items:
            for c in range(dil):
                s0 = pl.multiple_of(c * stream_len + b8 * w, w)
                stack[c * w:(c + 1) * w, :] = src[pl.ds(s0, w), :]
        outs = [_permute_rows(perm_inv, stack[...]) for _, _, stack in items]
        r0 = pl.multiple_of(b8 * PERM_BLK, PERM_BLK)
        for out, (_, dst, _) in zip(outs, items):
            dst[pl.ds(r0, PERM_BLK), :] = out.astype(dst.dtype)
        return carry

    lax.fori_loop(0, seq // PERM_BLK, body, 0)


def _transpose_blocks(src_ref, vt_ref, seq):
    def body(blk, carry):
        r0 = pl.multiple_of(blk * BLK, BLK)
        vt_ref[blk] = src_ref[pl.ds(r0, BLK), :].astype(F32).T.astype(vt_ref.dtype)
        return carry

    lax.fori_loop(0, seq // BLK, body, 0)


def _dil_windows(q_src, k_src, vt_ref, o_dst, lse_dst, bias_ref, ot_ref, lt_ref, windows):
    scale = B_HEAD_DIM ** -0.5
    c = scale * math.log2(math.e)
    tiles = []
    for rq, nk in windows:
        rk = rq - (nk - BLK)
        for hh in range(B_HEADS):
            cols = slice(hh * B_HEAD_DIM, (hh + 1) * B_HEAD_DIM)
            qh = q_src[pl.ds(rq, BLK), cols]
            kh = k_src[pl.ds(rk, nk), cols]
            tiles.append(_dot_nt(kh, qh) + bias_ref[0, hh, 2 * BLK - nk:, :])
    for wi, (rq, nk) in enumerate(windows):
        qb = rq // BLK if isinstance(rq, int) else lax.shift_right_logical(rq, BLK.bit_length() - 1)
        for hh in range(B_HEADS):
            cols = slice(hh * B_HEAD_DIM, (hh + 1) * B_HEAD_DIM)
            t = tiles[wi * B_HEADS + hh]
            m = jnp.max(t, axis=0, keepdims=True)
            p = jnp.exp2((t - m) * c)
            den = jnp.sum(p, axis=0, keepdims=True)
            if nk == BLK:
                vth = vt_ref[qb, cols, :]
            else:
                vth = jnp.concatenate([vt_ref[qb - 1, cols, :], vt_ref[qb, cols, :]], axis=1)
            ot_ref[wi, cols, :] = _dot(vth, p.astype(BF16)) / den
            lt_ref[wi, hh:hh + 1, :] = m * scale + jnp.log(den)
        o_dst[pl.ds(rq, BLK), :] = ot_ref[wi].T.astype(o_dst.dtype)
        lse_dst[pl.ds(rq, BLK), :] = lt_ref[wi].T


def _dil_streams(q_src, k_src, vt_ref, o_dst, lse_dst, bias_ref, ot_ref, lt_ref, dil, seq):
    stream_len = seq // dil
    nq = stream_len // BLK

    def run(windows):
        _dil_windows(q_src, k_src, vt_ref, o_dst, lse_dst, bias_ref, ot_ref, lt_ref, windows)

    if nq == 1:
        group = math.gcd(DIL_GROUP, dil)

        def stream_group(cg, carry):
            base = pl.multiple_of(cg * (group * stream_len), BLK)
            run([(pl.multiple_of(base + s * stream_len, BLK), BLK) for s in range(group)])
            return carry

        lax.fori_loop(0, dil // group, stream_group, 0)
        return

    def stream(c, carry):
        base = _mult(c * stream_len, BLK)
        n_tail = (nq - 1) % DIL_GROUP
        run([(base, BLK)] + [(_mult(base + (nq - 1 - s) * BLK, BLK), 2 * BLK) for s in range(n_tail)])

        def qgroup(j, carry2):
            rq = pl.multiple_of(base + (1 + DIL_GROUP * j) * BLK, BLK)
            run([(pl.multiple_of(rq + s * BLK, BLK), 2 * BLK) for s in range(DIL_GROUP)])
            return carry2

        n_groups = (nq - 1 - n_tail) // DIL_GROUP
        if n_groups:
            lax.fori_loop(0, n_groups, qgroup, 0)
        return carry

    if dil == 1:
        stream(0, 0)
    else:
        lax.fori_loop(0, dil, stream, 0)


def _head_expand(x, expand):
    hi = x.astype(BF16)
    lo = (x - hi.astype(F32)).astype(BF16)
    return _dot(jnp.concatenate([hi, lo], axis=1), expand)


def _dil_merge(o_src, lse_src, oacc, mrun, lrun, out_ref, first, last, seq):
    er = lax.broadcasted_iota(jnp.int32, (2 * LANES, B_HEADS * B_HEAD_DIM), 0)
    ec = lax.broadcasted_iota(jnp.int32, (2 * LANES, B_HEADS * B_HEAD_DIM), 1)
    expand = jnp.where(lax.shift_right_logical(ec, B_HEAD_DIM.bit_length() - 1) == lax.bitwise_and(er, LANES - 1),
                       jnp.float32(1), jnp.float32(0)).astype(BF16)

    def body(ch, carry):
        r0 = pl.multiple_of(ch * ROW_TM, ROW_TM)
        rows = pl.ds(r0, ROW_TM)
        lse = lse_src[rows, :]
        if first:
            mrun[rows, :] = lse
            lrun[rows, :] = jnp.ones_like(lse)
            oacc[rows, :] = o_src[rows, :].astype(F32)
            return carry
        m_old = mrun[rows, :]
        m_new = jnp.maximum(m_old, lse)
        a = jnp.exp(m_old - m_new)
        bw = jnp.exp(lse - m_new)
        l_new = lrun[rows, :] * a + bw
        if last:
            a = a / l_new
            bw = bw / l_new
        val = oacc[rows, :] * _head_expand(a, expand) + o_src[rows, :].astype(F32) * _head_expand(bw, expand)
        if last:
            out_ref[rows, :] = val.astype(out_ref.dtype)
        else:
            mrun[rows, :] = m_new
            lrun[rows, :] = l_new
            oacc[rows, :] = val
        return carry

    lax.fori_loop(0, seq // ROW_TM, body, 0)


def _dilated_kernel(q_ref, k_ref, v_ref, bias_ref, out_ref,
                    qs, ks, vs, vt, os_, lses, otok, lsetok, ostack, lstack, ot, lt, oacc, mrun, lrun,
                    *, seq):
    g = pl.program_id(1)

    @pl.when(g == 0)
    def _():
        lt[...] = jnp.zeros_like(lt)

    for gi, (_, dil) in enumerate(DILATED_GROUPS):

        @pl.when(g == gi)
        def _(gi=gi, dil=dil):
            first = gi == 0
            last = gi == N_DGROUPS - 1
            if dil == 1:
                _transpose_blocks(v_ref, vt, seq)
                _dil_streams(q_ref, k_ref, vt, otok, lsetok, bias_ref, ot, lt, 1, seq)
            else:
                perm = _perm_matrix(dil, inverse=False).astype(BF16)
                _deinterleave([(q_ref, qs), (k_ref, ks), (v_ref, vs)], perm, dil, seq)
                _transpose_blocks(vs, vt, seq)
                _dil_streams(qs, ks, vt, os_, lses, bias_ref, ot, lt, dil, seq)
                perm_inv = _perm_matrix(dil, inverse=True).astype(BF16)
                _interleave([(os_, otok, ostack), (lses, lsetok, lstack)], perm_inv, dil, seq)
            _dil_merge(otok, lsetok, oacc, mrun, lrun, out_ref, first, last, seq)


def _dilated_call(zg, bias_tiles, *, batch, seq):
    width = B_HEADS * B_HEAD_DIM
    q_col = (GATE_COLS + 3 * A_HEADS * 2 * A_HEAD_DIM) // width
    k_col = q_col + N_DGROUPS
    v_col = k_col + N_DGROUPS
    return pl.pallas_call(
        functools.partial(_dilated_kernel, seq=seq),
        grid=(batch, N_DGROUPS),
        in_specs=[
            pl.BlockSpec((seq, width), lambda b, g: (b, q_col + g)),
            pl.BlockSpec((seq, width), lambda b, g: (b, k_col + g)),
            pl.BlockSpec((seq, width), lambda b, g: (b, v_col + g)),
            pl.BlockSpec((1, B_HEADS, 2 * BLK, BLK), lambda b, g: (g, 0, 0, 0)),
        ],
        out_specs=pl.BlockSpec((seq, width), lambda b, g: (b, 0)),
        out_shape=jax.ShapeDtypeStruct((batch * seq, width), BF16),
        scratch_shapes=[
            pltpu.VMEM((seq, width), BF16), pltpu.VMEM((seq, width), BF16), pltpu.VMEM((seq, width), BF16),
            pltpu.VMEM((seq // BLK, width, BLK), BF16),
            pltpu.VMEM((seq, width), BF16), pltpu.VMEM((seq, LANES), F32),
            pltpu.VMEM((seq, width), BF16), pltpu.VMEM((seq, LANES), F32),
            pltpu.VMEM((PERM_BLK, width), BF16), pltpu.VMEM((PERM_BLK, LANES), F32),
            pltpu.VMEM((DIL_GROUP, width, BLK), F32), pltpu.VMEM((DIL_GROUP, LANES, BLK), F32),
            pltpu.VMEM((seq, width), F32), pltpu.VMEM((seq, LANES), F32), pltpu.VMEM((seq, LANES), F32),
        ],
        compiler_params=_cparams(("parallel", "arbitrary")),
        name="dilated_attn",
    )(zg, zg, zg, bias_tiles)


def _merge_router_kernel(oa_ref, ob_ref, gate_ref, x_ref, pa_ref, pb_ref, wo_ref, g_ref, wr_ref,
                         x1_ref, h2_ref, route_ref, cnt_ref, run_ref):
    tm, d = x_ref.shape

    @pl.when(pl.program_id(0) == 0)
    def _():
        run_ref[...] = jnp.zeros_like(run_ref)

    a = _dot(oa_ref[...], pa_ref[...])
    bm = _dot(ob_ref[...], pb_ref[...])
    merged = gate_ref[:, :d].astype(F32) * a + gate_ref[:, d:].astype(F32) * bm
    x1 = x_ref[...] + _dot(merged.astype(BF16), wo_ref[...])
    x1_ref[...] = x1
    h2 = _rmsnorm_val(x1, g_ref[...], RMS_EPS)
    h2_ref[...] = h2

    h_hi = h2.astype(BF16)
    h_lo = (h2 - h_hi.astype(F32)).astype(BF16)
    lt = (_dot(h_hi, wr_ref[0]) + (_dot(h_hi, wr_ref[1]) + _dot(h_lo, wr_ref[0]))).T
    coarse = [lt[i:i + 1, :] for i in range(N_EXPERT_GROUPS)]
    best = coarse[0]
    gsel = jnp.zeros((1, tm), jnp.int32)
    for i in range(1, N_EXPERT_GROUPS):
        upd = coarse[i] > best
        gsel = jnp.where(upd, i, gsel)
        best = jnp.where(upd, coarse[i], best)
    den = jnp.exp(coarse[0] - best)
    for i in range(1, N_EXPERT_GROUPS):
        den = den + jnp.exp(coarse[i] - best)
    pg = 1.0 / den

    fine = []
    for k in range(EXPERTS_PER_GROUP):
        f = lt[N_EXPERT_GROUPS + k:N_EXPERT_GROUPS + k + 1, :]
        for gi in range(1, N_EXPERT_GROUPS):
            r = N_EXPERT_GROUPS + gi * EXPERTS_PER_GROUP + k
            f = jnp.where(gsel == gi, lt[r:r + 1, :], f)
        fine.append(f)
    v0 = fine[0]
    i0 = jnp.zeros((1, tm), jnp.int32)
    for k in range(1, EXPERTS_PER_GROUP):
        upd = fine[k] > v0
        i0 = jnp.where(upd, k, i0)
        v0 = jnp.where(upd, fine[k], v0)
    v1 = jnp.full((1, tm), -jnp.inf, F32)
    i1 = jnp.zeros((1, tm), jnp.int32)
    for k in range(EXPERTS_PER_GROUP):
        upd = jnp.where(i0 != k, jnp.where(fine[k] > v1, 1, 0), 0) == 1
        i1 = jnp.where(upd, k, i1)
        v1 = jnp.where(upd, fine[k], v1)
    e1w = jnp.exp(v1 - v0)
    gate0 = pg * (1.0 / (1.0 + e1w))
    gate1 = pg * (e1w / (1.0 + e1w))
    e0 = gsel * EXPERTS_PER_GROUP + i0
    e1 = gsel * EXPERTS_PER_GROUP + i1

    eidx = lax.broadcasted_iota(jnp.int32, (N_EXPERTS, tm), 0)
    oh0 = jnp.where(eidx == e0, jnp.float32(1), jnp.float32(0))
    oh1 = jnp.where(eidx == e1, jnp.float32(1), jnp.float32(0))
    ta = lax.broadcasted_iota(jnp.int32, (tm, tm), 0)
    tb = lax.broadcasted_iota(jnp.int32, (tm, tm), 1)
    before = jnp.where(ta < tb, jnp.float32(1), jnp.float32(0)).astype(BF16)
    pre0 = _dot(oh0.astype(BF16), before)
    pre1 = _dot(oh1.astype(BF16), before)
    run = run_ref[:, 0:1]
    tot0 = jnp.sum(oh0, axis=1, keepdims=True)
    tot1 = jnp.sum(oh1, axis=1, keepdims=True)
    rank0 = jnp.sum(oh0 * (run + pre0), axis=0, keepdims=True)
    rank1 = jnp.sum(oh1 * (run + tot0 + pre1), axis=0, keepdims=True)
    new_run = jnp.broadcast_to(run + tot0 + tot1, run_ref.shape)
    run_ref[...] = new_run
    cnt_ref[...] = new_run

    route_ref[0:1, :] = e0.astype(F32)
    route_ref[1:2, :] = e1.astype(F32)
    route_ref[2:3, :] = gate0
    route_ref[3:4, :] = gate1
    route_ref[4:5, :] = rank0
    route_ref[5:6, :] = rank1
    route_ref[6:8, :] = jnp.zeros((2, tm), F32)


def _merge_router_call(oa, ob, zg, x2d, pa, pb, wo, g, wr_t):
    t, d = x2d.shape
    tm = ROW_TM
    const = dict(pipeline_mode=pl.Buffered(1))
    return pl.pallas_call(
        _merge_router_kernel,
        grid=(t // tm,),
        in_specs=[
            pl.BlockSpec((tm, oa.shape[1]), lambda i: (i, 0)),
            pl.BlockSpec((tm, ob.shape[1]), lambda i: (i, 0)),
            pl.BlockSpec((tm, GATE_COLS), lambda i: (i, 0)),
            pl.BlockSpec((tm, d), lambda i: (i, 0)),
            pl.BlockSpec(pa.shape, lambda i: (0, 0), **const),
            pl.BlockSpec(pb.shape, lambda i: (0, 0), **const),
            pl.BlockSpec(wo.shape, lambda i: (0, 0), **const),
            pl.BlockSpec((1, d), lambda i: (0, 0)),
            pl.BlockSpec(wr_t.shape, lambda i: (0, 0, 0), **const),
        ],
        out_specs=[
            pl.BlockSpec((tm, d), lambda i: (i, 0)),
            pl.BlockSpec((tm, d), lambda i: (i, 0)),
            pl.BlockSpec((SUBLANES, tm), lambda i: (0, i)),
            pl.BlockSpec((N_EXPERTS, LANES), lambda i: (0, 0)),
        ],
        out_shape=[
            jax.ShapeDtypeStruct((t, d), F32),
            jax.ShapeDtypeStruct((t, d), F32),
            jax.ShapeDtypeStruct((SUBLANES, t), F32),
            jax.ShapeDtypeStruct((N_EXPERTS, LANES), F32),
        ],
        scratch_shapes=[pltpu.VMEM((N_EXPERTS, LANES), F32)],
        compiler_params=_cparams(("arbitrary",)),
        name="merge_router",
    )(oa, ob, zg, x2d, pa, pb, wo, g, wr_t)


def _row_copy(src, s_row, dst, d_row, sem):
    return pltpu.make_async_copy(src.at[pl.ds(s_row, 1)], dst.at[pl.ds(d_row, 1)], sem)


def _dispatch_kernel(dest_ref, zstart_ref, zlen_ref, h_ref, x_hbm, zero_ref, sem, zsem):
    step = pl.program_id(0)

    def issue(t, carry):
        for k in range(TOP_K):
            _row_copy(h_ref, t, x_hbm, dest_ref[0, k, t], sem.at[k]).start()
        return carry

    lax.fori_loop(0, DISPATCH_TOK, issue, 0, unroll=8)

    @pl.when(step == 0)
    def _():
        zero_ref[...] = jnp.zeros_like(zero_ref)

        def group_copy(r8):
            return pltpu.make_async_copy(
                zero_ref, x_hbm.at[pl.ds(pl.multiple_of(r8, SUBLANES), SUBLANES)], zsem.at[1])

        def segment(e, wait):
            start = zstart_ref[e]
            n_head = jnp.minimum(lax.bitwise_and(-start, SUBLANES - 1), zlen_ref[e])
            n_group = lax.shift_right_logical(zlen_ref[e] - n_head, SUBLANES.bit_length() - 1)

            def head(r, carry):
                cp = _row_copy(zero_ref, 0, x_hbm, 0 if wait else start + r, zsem.at[0])
                cp.wait() if wait else cp.start()
                return carry

            def group(j, carry):
                cp = group_copy(0 if wait else start + n_head + SUBLANES * j)
                cp.wait() if wait else cp.start()
                return carry

            lax.fori_loop(0, n_head, head, 0)
            lax.fori_loop(0, n_group, group, 0)

        lax.fori_loop(0, N_EXPERTS + 1, lambda e, c: (segment(e, False), c)[1], 0)
        lax.fori_loop(0, N_EXPERTS + 1, lambda e, c: (segment(e, True), c)[1], 0)

    for k in range(TOP_K):
        pltpu.make_async_copy(h_ref, x_hbm.at[pl.ds(0, DISPATCH_TOK)], sem.at[k]).wait()


def _dispatch_call(dest_blocks, zstart, zlen, h2, n_rows):
    t, d = h2.shape
    return pl.pallas_call(
        _dispatch_kernel,
        grid=(t // DISPATCH_TOK,),
        in_specs=[
            pl.BlockSpec((1, TOP_K, DISPATCH_TOK), lambda i: (i, 0, 0), memory_space=pltpu.SMEM),
            pl.BlockSpec(memory_space=pltpu.SMEM),
            pl.BlockSpec(memory_space=pltpu.SMEM),
            pl.BlockSpec((DISPATCH_TOK, d), lambda i: (i, 0)),
        ],
        out_specs=pl.BlockSpec(memory_space=pl.ANY),
        out_shape=jax.ShapeDtypeStruct((n_rows, d), F32),
        scratch_shapes=[pltpu.VMEM((SUBLANES, d), F32), pltpu.SemaphoreType.DMA((TOP_K,)),
                        pltpu.SemaphoreType.DMA((2,))],
        compiler_params=_cparams(("arbitrary",)),
        name="dispatch",
    )(dest_blocks, zstart, zlen, h2)


def _expert_kernel(blk_e_ref, blk_valid_ref, x_ref, w1_ref, w3_ref, w2_ref, y_ref):
    i = pl.program_id(0)

    @pl.when(blk_valid_ref[i] == 1)
    def _():
        x = x_ref[...].astype(BF16)
        a = _dot(x, w1_ref[0])
        b = _dot(x, w3_ref[0])
        hdn = (a * _sigmoid(a)) * b
        y_ref[...] = _dot(hdn.astype(BF16), w2_ref[0])

    @pl.when(blk_valid_ref[i] == 0)
    def _():
        y_ref[...] = jnp.zeros_like(y_ref)


def _expert_call(blk_e, blk_valid, xin, w1, w3, w2):
    n_rows, d = xin.shape
    de = w1.shape[2]
    grid_spec = pltpu.PrefetchScalarGridSpec(
        num_scalar_prefetch=2,
        grid=(n_rows // EXP_TM,),
        in_specs=[
            pl.BlockSpec((EXP_TM, d), lambda i, be, bv: (i, 0)),
            pl.BlockSpec((1, d, de), lambda i, be, bv: (be[i], 0, 0)),
            pl.BlockSpec((1, d, de), lambda i, be, bv: (be[i], 0, 0)),
            pl.BlockSpec((1, de, d), lambda i, be, bv: (be[i], 0, 0)),
        ],
        out_specs=pl.BlockSpec((EXP_TM, d), lambda i, be, bv: (i, 0)),
    )
    return pl.pallas_call(
        _expert_kernel,
        grid_spec=grid_spec,
        out_shape=jax.ShapeDtypeStruct((n_rows, d), F32),
        compiler_params=_cparams(("arbitrary",)),
        name="experts",
    )(blk_e, blk_valid, xin, w1, w3, w2)


def _combine_kernel(dest_ref, dnext_ref, y_hbm, x1_ref, gt_ref, p_ref, gple_ref, wg_ref, wp_ref, gfin_ref,
                    o_ref, ybuf, sem, *, final_norm, n_steps):
    tm = x1_ref.shape[0]
    i = pl.program_id(0)
    slot = lax.bitwise_and(i, 1)

    def gather(d_ref, s):
        def issue(t, carry):
            for k in range(TOP_K):
                _row_copy(y_hbm, d_ref[0, k, t], ybuf.at[s, k], t, sem.at[s, k]).start()
            return carry

        lax.fori_loop(0, tm, issue, 0, unroll=8)

    @pl.when(i == 0)
    def _():
        gather(dest_ref, 0)

    @pl.when(i + 1 < n_steps)
    def _():
        gather(dnext_ref, 1 - slot)

    for k in range(TOP_K):
        pltpu.make_async_copy(y_hbm.at[pl.ds(0, tm)], ybuf.at[slot, k], sem.at[slot, k]).wait()

    for r0 in range(0, tm, ROW_TM):
        rows = slice(r0, r0 + ROW_TM)
        pp = _dot(p_ref[rows, :].astype(BF16), wp_ref[...])
        y = ybuf[slot, 0, rows, :] * gt_ref[rows, 2:3] + ybuf[slot, 1, rows, :] * gt_ref[rows, 3:4]
        x2 = x1_ref[rows, :] + y
        hn = _rmsnorm_val(x2, gple_ref[...], RMS_EPS)
        gate = _sigmoid(_dot(hn.astype(BF16), wg_ref[...]))
        x3 = x2 + gate * pp
        if final_norm:
            o_ref[rows, :] = _rmsnorm_val(x3, gfin_ref[...], RMS_EPS)
        else:
            o_ref[rows, :] = x3


def _combine_call(dest_blocks, yb, x1, gates_t, p2d, g_ple, w_gate, w_proj, g_fin, final_norm):
    t, d = x1.shape
    tm = COMBINE_TM
    const = dict(pipeline_mode=pl.Buffered(1))
    return pl.pallas_call(
        functools.partial(_combine_kernel, final_norm=final_norm, n_steps=t // tm),
        grid=(t // tm,),
        in_specs=[
            pl.BlockSpec((1, TOP_K, tm), lambda i: (i, 0, 0), memory_space=pltpu.SMEM),
            pl.BlockSpec((1, TOP_K, tm), lambda i: (jnp.minimum(i + 1, t // tm - 1), 0, 0),
                         memory_space=pltpu.SMEM),
            pl.BlockSpec(memory_space=pl.ANY),
            pl.BlockSpec((tm, d), lambda i: (i, 0)),
            pl.BlockSpec((tm, SUBLANES), lambda i: (i, 0)),
            pl.BlockSpec((tm, p2d.shape[1]), lambda i: (i, 0)),
            pl.BlockSpec((1, d), lambda i: (0, 0)),
            pl.BlockSpec(w_gate.shape, lambda i: (0, 0), **const),
            pl.BlockSpec(w_proj.shape, lambda i: (0, 0), **const),
            pl.BlockSpec((1, d), lambda i: (0, 0)),
        ],
        out_specs=pl.BlockSpec((tm, d), lambda i: (i, 0)),
        out_shape=jax.ShapeDtypeStruct((t, d), F32),
        scratch_shapes=[pltpu.VMEM((2, TOP_K, tm, d), F32), pltpu.SemaphoreType.DMA((2, TOP_K))],
        compiler_params=_cparams(("arbitrary",)),
        name="combine",
    )(dest_blocks, dest_blocks, yb, x1, gates_t, p2d, g_ple, w_gate, w_proj, g_fin)


def _routing_plan(route, counts_f, t):
    counts = counts_f[:, 0].astype(jnp.int32)
    pcounts = ((counts + EXP_TM - 1) // EXP_TM) * EXP_TM
    pends = jnp.cumsum(pcounts)
    pstarts = pends - pcounts
    eid = route[0:2].astype(jnp.int32)
    rank = route[4:6].astype(jnp.int32)
    onehot = eid[..., None] == jnp.arange(N_EXPERTS, dtype=jnp.int32)
    dest = jnp.sum(jnp.where(onehot, pstarts, 0), axis=-1) + rank
    n_rows = TOP_K * t + N_EXPERTS * EXP_TM
    n_blocks = n_rows // EXP_TM
    blk_row = jnp.arange(n_blocks, dtype=jnp.int32) * EXP_TM
    blk_e = jnp.sum((pends[None, :] <= blk_row[:, None]).astype(jnp.int32), axis=1)
    blk_valid = (blk_row < pends[-1]).astype(jnp.int32)
    last_e = jnp.sum((pends <= pends[-1] - 1).astype(jnp.int32))
    blk_e = jnp.minimum(blk_e, last_e)
    zstart = jnp.concatenate([pstarts + counts, pends[-1:]]).astype(jnp.int32)
    zlen = jnp.concatenate([pcounts - counts, n_rows - pends[-1:]]).astype(jnp.int32)
    return dest, blk_e, blk_valid, zstart, zlen, n_rows


def _blocked(dest, tok_per_block):
    t = dest.shape[1]
    return dest.reshape(TOP_K, t // tok_per_block, tok_per_block).transpose(1, 0, 2)


def kernel(x, p, rel_bias, norm_mix_g, w_in, w_gate, lambda_q1, lambda_k1, lambda_q2, lambda_k2, subln_g,
           w_proj_a, w_proj_b, w_out, norm_ffn_g, w_coarse, w_fine, w1, w3, w2, norm_ple_g, w_ple_gate,
           w_ple_proj, final_norm_g):
    batch, seq, d = x.shape
    depth = w_in.shape[0]
    t = batch * seq
    assert seq % PERM_BLK == 0 and t % min(PROJ_TM, t) == 0
    assert all(seq // dil >= BLK and win // dil == BLK for win, dil in DILATED_GROUPS)

    nq = seq // ATT_BLK
    bias_a, w_cat0 = _bias_tiles_call(rel_bias, n_heads=A_HEADS, n_off=nq, tq=ATT_BLK, tk=ATT_BLK,
                                      off_mult=ATT_BLK, off_add=0, dil=1, max_rel=seq, head0=0, name="bias_diff",
                                      key_major=True, mult=A_HEAD_DIM ** 0.5, concat_cast=(w_gate[0], w_in[0]))
    bias_b = jnp.concatenate([
        _bias_tiles_call(rel_bias, n_heads=B_HEADS, n_off=1, tq=2 * BLK, tk=BLK, off_mult=0, off_add=BLK,
                         dil=dil, max_rel=win // dil, head0=A_HEADS + gi * B_HEADS, name="bias_dil%d" % gi,
                         key_major=True, mult=B_HEAD_DIM ** 0.5)[0].reshape(1, B_HEADS, 2 * BLK, BLK)
        for gi, (win, dil) in enumerate(DILATED_GROUPS)], axis=0)

    x2d = x.reshape(t, d)
    for layer in range(depth):
        lam_init = 0.8 - 0.6 * math.exp(-0.3 * layer)
        lam = _lam_call(lambda_q1[layer:layer + 1], lambda_k1[layer:layer + 1],
                        lambda_q2[layer:layer + 1], lambda_k2[layer:layer + 1], lam_init)
        if layer == 0 and w_cat0 is not None:
            w_cat = w_cat0
        else:
            w_cat = jnp.concatenate([w_gate[layer], w_in[layer]], axis=1).astype(BF16)
        zg = _inproj_call(x2d, norm_mix_g[layer:layer + 1], w_cat)
        n_exp, _, d_exp = w1[layer].shape
        oa, (w1b, w3b, w2b) = _diff_attn_call(
            zg, lam, bias_a, subln_g[layer:layer + 1],
            [w1[layer].reshape(n_exp * d, d_exp), w3[layer].reshape(n_exp * d, d_exp),
             w2[layer].reshape(n_exp * d_exp, d)],
            batch=batch, seq=seq, out_scale=1.0 - lam_init)
        ob = _dilated_call(zg, bias_b, batch=batch, seq=seq)

        wr_t = jnp.concatenate([
            w_coarse[layer].T,
            w_fine[layer].transpose(0, 2, 1).reshape(N_EXPERTS, d),
            jnp.zeros((LANES - N_EXPERT_GROUPS - N_EXPERTS, d), F32)], axis=0)
        wr_t = wr_t.T
        wr_hi = wr_t.astype(BF16)
        wr_t = jnp.stack([wr_hi, (wr_t - wr_hi.astype(F32)).astype(BF16)], axis=0)
        x1, h2, route, counts = _merge_router_call(
            oa, ob, zg, x2d, w_proj_a[layer].astype(BF16), w_proj_b[layer].astype(BF16),
            w_out[layer].astype(BF16), norm_ffn_g[layer:layer + 1], wr_t)

        dest, blk_e, blk_valid, zstart, zlen, n_rows = _routing_plan(route, counts, t)
        xin = _dispatch_call(_blocked(dest, DISPATCH_TOK), zstart, zlen, h2, n_rows)
        yb = _expert_call(blk_e, blk_valid, xin, w1b.reshape(n_exp, d, d_exp), w3b.reshape(n_exp, d, d_exp),
                          w2b.reshape(n_exp, d_exp, d))
        x2d = _combine_call(_blocked(dest, COMBINE_TM), yb, x1, route.T, p[layer].reshape(t, -1),
                            norm_ple_g[layer:layer + 1], w_ple_gate[layer].astype(BF16),
                            w_ple_proj[layer].astype(BF16), final_norm_g.reshape(1, d),
                            final_norm=layer == depth - 1)
    return x2d.reshape(batch, seq, d)
```

```python
import functools
import math

import jax
import jax.numpy as jnp
from jax import lax
from jax.experimental import pallas as pl
from jax.experimental.pallas import tpu as pltpu

F32 = jnp.float32
BF16 = jnp.bfloat16

BLK = 128
NEG_INF = -1e30
RMS_EPS = 1e-6
SUBLN_EPS = 1e-5
N_BUCKETS = 32
MAX_DISTANCE = 2048
A_HEADS = 8
A_HEAD_DIM = 128
DILATED_GROUPS = ((128, 1), (512, 4), (2048, 16))
N_DGROUPS = 3
B_HEADS = 8
B_HEAD_DIM = 64
N_EXPERT_GROUPS = 4
EXPERTS_PER_GROUP = 8
N_EXPERTS = 32
TOP_K = 2

LANES = 128
SUBLANES = 8
VMEM_LIMIT = 56 * 1024 * 1024
ATT_BLK = 256
ATT_HEADS = 2
PERM_BLK = 256
GATE_COLS = 4096
PROJ_TN = 512
PROJ_TM = 2048
ROW_TM = 256
COMBINE_TM = 512
EXP_TM = 256
DISPATCH_TOK = 2048
NORM_CHUNK = 128
CAST_SLAB_BYTES = 1 << 20
DIL_GROUP = 4
TOEPLITZ_LANES = 512


def _t5_thresholds():
    max_exact = N_BUCKETS // 2
    out = []
    for k in range(1, N_BUCKETS - max_exact):
        out.append(int(math.ceil(max_exact * (MAX_DISTANCE / max_exact) ** (k / (N_BUCKETS - max_exact)))))
    return tuple(out)


T5_THRESHOLDS = _t5_thresholds()


def _cparams(sem, vmem=VMEM_LIMIT):
    return pltpu.CompilerParams(dimension_semantics=sem, vmem_limit_bytes=vmem)


def _mult(x, m):
    return x if isinstance(x, int) else pl.multiple_of(x, m)


def _sigmoid(x):
    return 0.5 * jnp.tanh(0.5 * x) + 0.5


def _dot(a, b, **kw):
    return jnp.dot(a, b, preferred_element_type=F32, **kw)


def _dot_nt(a, b, **kw):
    return lax.dot_general(a, b, (((1,), (1,)), ((), ())), preferred_element_type=F32, **kw)


def _lam_kernel(q1_ref, k1_ref, q2_ref, k2_ref, o_ref, *, lam_init):
    s1 = jnp.sum(q1_ref[...] * k1_ref[...], axis=-1, keepdims=True)
    s2 = jnp.sum(q2_ref[...] * k2_ref[...], axis=-1, keepdims=True)
    o_ref[...] = jnp.exp(s1) - jnp.exp(s2) + lam_init


def _lam_call(lq1, lk1, lq2, lk2, lam_init):
    return pl.pallas_call(
        functools.partial(_lam_kernel, lam_init=lam_init),
        out_shape=jax.ShapeDtypeStruct((1, 1), F32),
        name="lam",
    )(lq1, lk1, lq2, lk2)


def _bias_tile_kernel(tab_ref, *rest, tq, tk, off_mult, off_add, dil, max_rel, head0, key_major, mult):
    n_pass = max(len(rest) - 2, 0)
    o_ref = rest[n_pass]
    if n_pass:
        col = 0
        for src in rest[:n_pass]:
            rest[-1][:, col:col + src.shape[1]] = src[...].astype(rest[-1].dtype)
            col += src.shape[1]
    h = pl.program_id(0)
    n = pl.program_id(1)
    assert key_major and tq + tk <= TOEPLITZ_LANES
    m = lax.broadcasted_iota(jnp.int32, (8, TOEPLITZ_LANES), 1)
    rel = jnp.where(m < tk, m, m - TOEPLITZ_LANES) + (n * off_mult + off_add)
    dist = rel * dil
    large = jnp.full(m.shape, N_BUCKETS // 2, jnp.int32)
    for thr in T5_THRESHOLDS:
        large = large + jnp.where(dist >= thr, 1, 0)
    bucket = jnp.where(dist < N_BUCKETS // 2, dist, large)
    acc = jnp.zeros(m.shape, F32)
    for b in range(N_BUCKETS):
        acc = jnp.where(bucket == b, tab_ref[b, head0 + h], acc)
    valid = jnp.where(rel >= 0, jnp.where(rel <= max_rel, 1, 0), 0)
    vec = jnp.where(valid == 1, acc * mult, NEG_INF)[0:1, :]
    rows = pltpu.roll(jnp.broadcast_to(vec, (tq, TOEPLITZ_LANES)), 0, 1, stride=1, stride_axis=0)
    o_ref[0, 0] = rows[:, :tk]


def _bias_tiles_call(rel_bias, *, n_heads, n_off, tq, tk, off_mult, off_add, dil, max_rel, head0, name,
                     key_major=False, mult=1.0, concat_cast=()):
    kern = functools.partial(_bias_tile_kernel, tq=tq, tk=tk, off_mult=off_mult, off_add=off_add,
                             dil=dil, max_rel=max_rel, head0=head0, key_major=key_major, mult=mult)
    n_steps = n_heads * n_off
    if concat_cast and concat_cast[0].shape[0] % (n_steps * 16):
        concat_cast = ()
    in_specs = [pl.BlockSpec(memory_space=pltpu.SMEM)]
    out_specs = [pl.BlockSpec((1, 1, tq, tk), lambda h, n: (h, n, 0, 0))]
    out_shape = [jax.ShapeDtypeStruct((n_heads, n_off, tq, tk), F32)]
    if concat_cast:
        rows = concat_cast[0].shape[0]
        cols = sum(w.shape[1] for w in concat_cast)
        in_specs += [pl.BlockSpec((rows // n_steps, w.shape[1]), lambda h, n: (h * n_off + n, 0))
                     for w in concat_cast]
        out_specs.append(pl.BlockSpec((rows // n_steps, cols), lambda h, n: (h * n_off + n, 0)))
        out_shape.append(jax.ShapeDtypeStruct((rows, cols), BF16))
    outs = pl.pallas_call(
        kern,
        grid=(n_heads, n_off),
        in_specs=in_specs,
        out_specs=out_specs,
        out_shape=out_shape,
        compiler_params=_cparams(("parallel", "parallel")),
        name=name,
    )(rel_bias, *concat_cast)
    return outs[0], (outs[1] if concat_cast else None)


def _rmsnorm_rows(x_ref, g_ref, out_ref, eps):
    rows = x_ref.shape[0]
    g = g_ref[...]

    def body(c, carry):
        r0 = pl.multiple_of(c * NORM_CHUNK, NORM_CHUNK)
        x = x_ref[pl.ds(r0, NORM_CHUNK), :]
        ms = jnp.mean(x * x, axis=-1, keepdims=True)
        out_ref[pl.ds(r0, NORM_CHUNK), :] = ((x * lax.rsqrt(ms + eps)) * g).astype(out_ref.dtype)
        return carry

    lax.fori_loop(0, rows // NORM_CHUNK, body, 0)


def _rmsnorm_val(x, g, eps):
    ms = jnp.mean(x * x, axis=-1, keepdims=True)
    return (x * lax.rsqrt(ms + eps)) * g


def _inproj_kernel(x_ref, g_ref, w_ref, o_ref, h_ref, *, n_gate_blocks):
    j = pl.program_id(1)

    @pl.when(j == 0)
    def _():
        _rmsnorm_rows(x_ref, g_ref, h_ref, RMS_EPS)

    acc = _dot(h_ref[...], w_ref[...])

    @pl.when(j < n_gate_blocks)
    def _():
        o_ref[...] = _sigmoid(acc).astype(o_ref.dtype)

    @pl.when(j >= n_gate_blocks)
    def _():
        o_ref[...] = acc.astype(o_ref.dtype)


def _inproj_call(x2d, g, w_cat):
    t, d = x2d.shape
    n = w_cat.shape[1]
    tm = min(PROJ_TM, t)
    return pl.pallas_call(
        functools.partial(_inproj_kernel, n_gate_blocks=GATE_COLS // PROJ_TN),
        grid=(t // tm, n // PROJ_TN),
        in_specs=[
            pl.BlockSpec((tm, d), lambda i, j: (i, 0)),
            pl.BlockSpec((1, d), lambda i, j: (0, 0)),
            pl.BlockSpec((d, PROJ_TN), lambda i, j: (0, j)),
        ],
        out_specs=pl.BlockSpec((tm, PROJ_TN), lambda i, j: (i, j)),
        out_shape=jax.ShapeDtypeStruct((t, n), BF16),
        scratch_shapes=[pltpu.VMEM((tm, d), BF16)],
        compiler_params=_cparams(("parallel", "arbitrary")),
        name="inproj",
    )(x2d, g, w_cat)


def _diff_attn_kernel(lam_ref, q_ref, k_ref, v_ref, bias_ref, g_ref, *rest, out_scale, n_blk, n_cast):
    cast_in = rest[:n_cast]
    o_ref = rest[n_cast]
    cast_out = rest[n_cast + 1:2 * n_cast + 1]
    vt_ref, tbuf_a, tbuf_b, acc = rest[2 * n_cast + 1:]
    for src, dst in zip(cast_in, cast_out):
        dst[...] = src[...].astype(dst.dtype)
    qi = pl.program_id(2)
    width = 2 * A_HEAD_DIM
    n_map = 2 * ATT_HEADS
    c = (A_HEAD_DIM ** -0.5) * math.log2(math.e)

    @pl.when(qi == 0)
    def _():
        def transpose_block(b, carry):
            r0 = pl.multiple_of(b * ATT_BLK, ATT_BLK)
            for hp in range(ATT_HEADS):
                vb = v_ref[pl.ds(r0, ATT_BLK), hp * width:(hp + 1) * width]
                vt_ref[hp, b] = vb.astype(F32).T.astype(BF16)
            return carry

        lax.fori_loop(0, n_blk, transpose_block, 0)

    acc[...] = jnp.zeros_like(acc)

    def scores(ki, qt, dst):
        k0 = pl.multiple_of(ki * ATT_BLK, ATT_BLK)
        q0 = pl.multiple_of(qt * ATT_BLK, ATT_BLK)
        for mi in range(n_map):
            cols = slice(mi * A_HEAD_DIM, (mi + 1) * A_HEAD_DIM)
            dst[mi] = _dot_nt(k_ref[pl.ds(k0, ATT_BLK), cols], q_ref[pl.ds(q0, ATT_BLK), cols]) \
                + bias_ref[mi // 2, qt - ki]

    def softmax(t, m, l):
        m_new = jnp.maximum(m, jnp.max(t, axis=0, keepdims=True))
        alpha = jnp.exp2((m - m_new) * c)
        p = jnp.exp2((t - m_new) * c)
        return m_new, alpha * l + jnp.sum(p, axis=0, keepdims=True), alpha, p.astype(BF16)

    first_buf = lax.bitwise_and(lax.shift_right_logical(qi * (qi + 1), 1), 1)

    @pl.when(qi == 0)
    def _():
        scores(0, 0, tbuf_a)

    def step(ki, carry, cur, nxt):
        ms, ls = carry
        stats = [softmax(cur[mi], ms[mi], ls[mi]) for mi in range(n_map)]
        last = ki == qi
        scores(jnp.where(last, 0, ki + 1), jnp.where(last, jnp.minimum(qi + 1, n_blk - 1), qi), nxt)
        for mi in range(n_map):
            acc[mi] = acc[mi] * stats[mi][2] + _dot(vt_ref[mi // 2, ki], stats[mi][3])
        return tuple(st[0] for st in stats), tuple(st[1] for st in stats)

    def body(ki, carry):
        return lax.cond(lax.bitwise_and(first_buf + ki, 1) == 0,
                        lambda cr: step(ki, cr, tbuf_a, tbuf_b),
                        lambda cr: step(ki, cr, tbuf_b, tbuf_a), carry)

    minf = (jnp.full((1, ATT_BLK), -jnp.inf, F32),) * n_map
    zero = (jnp.zeros((1, ATT_BLK), F32),) * n_map
    _, ls = lax.fori_loop(0, qi + 1, body, (minf, zero))
    for hp in range(ATT_HEADS):
        w = acc[2 * hp] / ls[2 * hp] - lam_ref[0, 0] * (acc[2 * hp + 1] / ls[2 * hp + 1])
        ms = jnp.mean(w * w, axis=0, keepdims=True)
        y = ((w * lax.rsqrt(ms + SUBLN_EPS)) * g_ref[...]) * out_scale
        o_ref[:, hp * width:(hp + 1) * width] = y.T.astype(o_ref.dtype)


def _diff_attn_call(zg, lam, bias_tiles, subln_g, passengers, *, batch, seq, out_scale):
    t = batch * seq
    nq = seq // ATT_BLK
    width = 2 * A_HEAD_DIM
    blk_w = ATT_HEADS * width
    q_col = GATE_COLS // blk_w
    k_col = q_col + A_HEADS // ATT_HEADS
    v_col = k_col + A_HEADS // ATT_HEADS
    n_steps = (A_HEADS // ATT_HEADS) * batch * nq
    riders = [w for w in passengers
              if w.shape[0] % (n_steps * 16) == 0 and w.size * 4 // n_steps <= CAST_SLAB_BYTES]

    def slab_spec(w):
        return pl.BlockSpec((w.shape[0] // n_steps, w.shape[1]), lambda h, b, i: ((h * batch + b) * nq + i, 0))

    outs = pl.pallas_call(
        functools.partial(_diff_attn_kernel, out_scale=out_scale, n_blk=nq, n_cast=len(riders)),
        grid=(A_HEADS // ATT_HEADS, batch, nq),
        in_specs=[
            pl.BlockSpec(memory_space=pltpu.SMEM),
            pl.BlockSpec((seq, blk_w), lambda h, b, i: (b, q_col + h)),
            pl.BlockSpec((seq, blk_w), lambda h, b, i: (b, k_col + h)),
            pl.BlockSpec((seq, blk_w), lambda h, b, i: (b, v_col + h)),
            pl.BlockSpec((ATT_HEADS, nq, ATT_BLK, ATT_BLK), lambda h, b, i: (h, 0, 0, 0)),
            pl.BlockSpec((width, 1), lambda h, b, i: (0, 0)),
        ] + [slab_spec(w) for w in riders],
        out_specs=[pl.BlockSpec((ATT_BLK, blk_w), lambda h, b, i: (b * nq + i, h))]
        + [slab_spec(w) for w in riders],
        out_shape=[jax.ShapeDtypeStruct((t, A_HEADS * width), BF16)]
        + [jax.ShapeDtypeStruct(w.shape, BF16) for w in riders],
        scratch_shapes=[pltpu.VMEM((ATT_HEADS, nq, width, ATT_BLK), BF16),
                        pltpu.VMEM((2 * ATT_HEADS, ATT_BLK, ATT_BLK), F32),
                        pltpu.VMEM((2 * ATT_HEADS, ATT_BLK, ATT_BLK), F32),
                        pltpu.VMEM((2 * ATT_HEADS, width, ATT_BLK), F32)],
        compiler_params=_cparams(("parallel", "parallel", "arbitrary")),
        name="diff_attn",
    )(lam, zg, zg, zg, bias_tiles, subln_g.reshape(width, 1), *riders)
    cast = iter(outs[1:])
    return outs[0], [next(cast) if any(w is r for r in riders) else w.astype(BF16) for w in passengers]


def _perm_matrix(dil, inverse):
    w = PERM_BLK // dil
    shift = w.bit_length() - 1
    a = lax.broadcasted_iota(jnp.int32, (PERM_BLK, PERM_BLK), 0)
    b = lax.broadcasted_iota(jnp.int32, (PERM_BLK, PERM_BLK), 1)
    dst, src = (b, a) if inverse else (a, b)
    c = lax.shift_right_logical(dst, shift)
    ll = lax.bitwise_and(dst, w - 1)
    return jnp.where(src == ll * dil + c, jnp.float32(1), jnp.float32(0))


def _deinterleave(pairs, perm, dil, seq):
    w = PERM_BLK // dil
    stream_len = seq // dil

    def body(b8, carry):
        r0 = pl.multiple_of(b8 * PERM_BLK, PERM_BLK)
        ys = [_dot(perm, src[pl.ds(r0, PERM_BLK), :]).astype(dst.dtype) for src, dst in pairs]
        for y, (_, dst) in zip(ys, pairs):
            for c in range(dil):
                d0 = pl.multiple_of(c * stream_len + b8 * w, w)
                dst[pl.ds(d0, w), :] = y[c * w:(c + 1) * w, :]
        return carry

    lax.fori_loop(0, seq // PERM_BLK, body, 0)


def _permute_rows(perm, x):
    if x.dtype == BF16:
        return _dot(perm, x)
    hi = x.astype(BF16)
    r1 = x - hi.astype(F32)
    mid = r1.astype(BF16)
    lo = (r1 - mid.astype(F32)).astype(BF16)
    y = _dot(perm, jnp.concatenate([hi, mid, lo], axis=1))
    n = x.shape[1]
    return y[:, :n] + (y[:, n:2 * n] + y[:, 2 * n:])


def _interleave(items, perm_inv, dil, seq):
    w = PERM_BLK // dil
    stream_len = seq // dil

    def body(b8, carry):
        for src, _, stack in items:
            for c in range(dil):
                s0 = pl.multiple_of(c * stream_len + b8 * w, w)
                stack[c * w:(c + 1) * w, :] = src[pl.ds(s0, w), :]
        outs = [_permute_rows(perm_inv, stack[...]) for _, _, stack in items]
        r0 = pl.multiple_of(b8 * PERM_BLK, PERM_BLK)
        for out, (_, dst, _) in zip(outs, items):
            dst[pl.ds(r0, PERM_BLK), :] = out.astype(dst.dtype)
        return carry

    lax.fori_loop(0, seq // PERM_BLK, body, 0)


def _transpose_blocks(src_ref, vt_ref, seq):
    def body(blk, carry):
        r0 = pl.multiple_of(blk * BLK, BLK)
        vt_ref[blk] = src_ref[pl.ds(r0, BLK), :].astype(F32).T.astype(vt_ref.dtype)
        return carry

    lax.fori_loop(0, seq // BLK, body, 0)


def _dil_windows(q_src, k_src, vt_ref, o_dst, lse_dst, bias_ref, ot_ref, lt_ref, windows):
    scale = B_HEAD_DIM ** -0.5
    c = scale * math.log2(math.e)
    tiles = []
    for rq, nk in windows:
        rk = rq - (nk - BLK)
        for hh in range(B_HEADS):
            cols = slice(hh * B_HEAD_DIM, (hh + 1) * B_HEAD_DIM)
            qh = q_src[pl.ds(rq, BLK), cols]
            kh = k_src[pl.ds(rk, nk), cols]
            tiles.append(_dot_nt(kh, qh) + bias_ref[0, hh, 2 * BLK - nk:, :])
    for wi, (rq, nk) in enumerate(windows):
        qb = rq // BLK if isinstance(rq, int) else lax.shift_right_logical(rq, BLK.bit_length() - 1)
        for hh in range(B_HEADS):
            cols = slice(hh * B_HEAD_DIM, (hh + 1) * B_HEAD_DIM)
            t = tiles[wi * B_HEADS + hh]
            m = jnp.max(t, axis=0, keepdims=True)
            p = jnp.exp2((t - m) * c)
            den = jnp.sum(p, axis=0, keepdims=True)
            if nk == BLK:
                vth = vt_ref[qb, cols, :]
            else:
                vth = jnp.concatenate([vt_ref[qb - 1, cols, :], vt_ref[qb, cols, :]], axis=1)
            ot_ref[wi, cols, :] = _dot(vth, p.astype(BF16)) / den
            lt_ref[wi, hh:hh + 1, :] = m * scale + jnp.log(den)
        o_dst[pl.ds(rq, BLK), :] = ot_ref[wi].T.astype(o_dst.dtype)
        lse_dst[pl.ds(rq, BLK), :] = lt_ref[wi].T


def _dil_streams(q_src, k_src, vt_ref, o_dst, lse_dst, bias_ref, ot_ref, lt_ref, dil, seq):
    stream_len = seq // dil
    nq = stream_len // BLK

    def run(windows):
        _dil_windows(q_src, k_src, vt_ref, o_dst, lse_dst, bias_ref, ot_ref, lt_ref, windows)

    if nq == 1:
        group = math.gcd(DIL_GROUP, dil)

        def stream_group(cg, carry):
            base = pl.multiple_of(cg * (group * stream_len), BLK)
            run([(pl.multiple_of(base + s * stream_len, BLK), BLK) for s in range(group)])
            return carry

        lax.fori_loop(0, dil // group, stream_group, 0)
        return

    def stream(c, carry):
        base = _mult(c * stream_len, BLK)
        n_tail = (nq - 1) % DIL_GROUP
        run([(base, BLK)] + [(_mult(base + (nq - 1 - s) * BLK, BLK), 2 * BLK) for s in range(n_tail)])

        def qgroup(j, carry2):
            rq = pl.multiple_of(base + (1 + DIL_GROUP * j) * BLK, BLK)
            run([(pl.multiple_of(rq + s * BLK, BLK), 2 * BLK) for s in range(DIL_GROUP)])
            return carry2

        n_groups = (nq - 1 - n_tail) // DIL_GROUP
        if n_groups:
            lax.fori_loop(0, n_groups, qgroup, 0)
        return carry

    if dil == 1:
        stream(0, 0)
    else:
        lax.fori_loop(0, dil, stream, 0)


def _head_expand(x, expand):
    hi = x.astype(BF16)
    lo = (x - hi.astype(F32)).astype(BF16)
    return _dot(jnp.concatenate([hi, lo], axis=1), expand)


def _dil_merge(o_src, lse_src, oacc, mrun, lrun, out_ref, first, last, seq):
    er = lax.broadcasted_iota(jnp.int32, (2 * LANES, B_HEADS * B_HEAD_DIM), 0)
    ec = lax.broadcasted_iota(jnp.int32, (2 * LANES, B_HEADS * B_HEAD_DIM), 1)
    expand = jnp.where(lax.shift_right_logical(ec, B_HEAD_DIM.bit_length() - 1) == lax.bitwise_and(er, LANES - 1),
                       jnp.float32(1), jnp.float32(0)).astype(BF16)

    def body(ch, carry):
        r0 = pl.multiple_of(ch * ROW_TM, ROW_TM)
        rows = pl.ds(r0, ROW_TM)
        lse = lse_src[rows, :]
        if first:
            mrun[rows, :] = lse
            lrun[rows, :] = jnp.ones_like(lse)
            oacc[rows, :] = o_src[rows, :].astype(F32)
            return carry
        m_old = mrun[rows, :]
        m_new = jnp.maximum(m_old, lse)
        a = jnp.exp(m_old - m_new)
        bw = jnp.exp(lse - m_new)
        l_new = lrun[rows, :] * a + bw
        if last:
            a = a / l_new
            bw = bw / l_new
        val = oacc[rows, :] * _head_expand(a, expand) + o_src[rows, :].astype(F32) * _head_expand(bw, expand)
        if last:
            out_ref[rows, :] = val.astype(out_ref.dtype)
        else:
            mrun[rows, :] = m_new
            lrun[rows, :] = l_new
            oacc[rows, :] = val
        return carry

    lax.fori_loop(0, seq // ROW_TM, body, 0)


def _dilated_kernel(q_ref, k_ref, v_ref, bias_ref, out_ref,
                    qs, ks, vs, vt, os_, lses, otok, lsetok, ostack, lstack, ot, lt, oacc, mrun, lrun,
                    *, seq):
    g = pl.program_id(1)

    @pl.when(g == 0)
    def _():
        lt[...] = jnp.zeros_like(lt)

    for gi, (_, dil) in enumerate(DILATED_GROUPS):

        @pl.when(g == gi)
        def _(gi=gi, dil=dil):
            first = gi == 0
            last = gi == N_DGROUPS - 1
            if dil == 1:
                _transpose_blocks(v_ref, vt, seq)
                _dil_streams(q_ref, k_ref, vt, otok, lsetok, bias_ref, ot, lt, 1, seq)
            else:
                perm = _perm_matrix(dil, inverse=False).astype(BF16)
                _deinterleave([(q_ref, qs), (k_ref, ks), (v_ref, vs)], perm, dil, seq)
                _transpose_blocks(vs, vt, seq)
                _dil_streams(qs, ks, vt, os_, lses, bias_ref, ot, lt, dil, seq)
                perm_inv = _perm_matrix(dil, inverse=True).astype(BF16)
                _interleave([(os_, otok, ostack), (lses, lsetok, lstack)], perm_inv, dil, seq)
            _dil_merge(otok, lsetok, oacc, mrun, lrun, out_ref, first, last, seq)


def _dilated_call(zg, bias_tiles, *, batch, seq):
    width = B_HEADS * B_HEAD_DIM
    q_col = (GATE_COLS + 3 * A_HEADS * 2 * A_HEAD_DIM) // width
    k_col = q_col + N_DGROUPS
    v_col = k_col + N_DGROUPS
    return pl.pallas_call(
        functools.partial(_dilated_kernel, seq=seq),
        grid=(batch, N_DGROUPS),
        in_specs=[
            pl.BlockSpec((seq, width), lambda b, g: (b, q_col + g)),
            pl.BlockSpec((seq, width), lambda b, g: (b, k_col + g)),
            pl.BlockSpec((seq, width), lambda b, g: (b, v_col + g)),
            pl.BlockSpec((1, B_HEADS, 2 * BLK, BLK), lambda b, g: (g, 0, 0, 0)),
        ],
        out_specs=pl.BlockSpec((seq, width), lambda b, g: (b, 0)),
        out_shape=jax.ShapeDtypeStruct((batch * seq, width), BF16),
        scratch_shapes=[
            pltpu.VMEM((seq, width), BF16), pltpu.VMEM((seq, width), BF16), pltpu.VMEM((seq, width), BF16),
            pltpu.VMEM((seq // BLK, width, BLK), BF16),
            pltpu.VMEM((seq, width), BF16), pltpu.VMEM((seq, LANES), F32),
            pltpu.VMEM((seq, width), BF16), pltpu.VMEM((seq, LANES), F32),
            pltpu.VMEM((PERM_BLK, width), BF16), pltpu.VMEM((PERM_BLK, LANES), F32),
            pltpu.VMEM((DIL_GROUP, width, BLK), F32), pltpu.VMEM((DIL_GROUP, LANES, BLK), F32),
            pltpu.VMEM((seq, width), F32), pltpu.VMEM((seq, LANES), F32), pltpu.VMEM((seq, LANES), F32),
        ],
        compiler_params=_cparams(("parallel", "arbitrary")),
        name="dilated_attn",
    )(zg, zg, zg, bias_tiles)


def _merge_router_kernel(oa_ref, ob_ref, gate_ref, x_ref, pa_ref, pb_ref, wo_ref, g_ref, wr_ref,
                         x1_ref, h2_ref, route_ref, cnt_ref, run_ref):
    tm, d = x_ref.shape

    @pl.when(pl.program_id(0) == 0)
    def _():
        run_ref[...] = jnp.zeros_like(run_ref)

    a = _dot(oa_ref[...], pa_ref[...])
    bm = _dot(ob_ref[...], pb_ref[...])
    merged = gate_ref[:, :d].astype(F32) * a + gate_ref[:, d:].astype(F32) * bm
    x1 = x_ref[...] + _dot(merged.astype(BF16), wo_ref[...])
    x1_ref[...] = x1
    h2 = _rmsnorm_val(x1, g_ref[...], RMS_EPS)
    h2_ref[...] = h2

    h_hi = h2.astype(BF16)
    h_lo = (h2 - h_hi.astype(F32)).astype(BF16)
    lt = (_dot(h_hi, wr_ref[0]) + (_dot(h_hi, wr_ref[1]) + _dot(h_lo, wr_ref[0]))).T
    coarse = [lt[i:i + 1, :] for i in range(N_EXPERT_GROUPS)]
    best = coarse[0]
    gsel = jnp.zeros((1, tm), jnp.int32)
    for i in range(1, N_EXPERT_GROUPS):
        upd = coarse[i] > best
        gsel = jnp.where(upd, i, gsel)
        best = jnp.where(upd, coarse[i], best)
    den = jnp.exp(coarse[0] - best)
    for i in range(1, N_EXPERT_GROUPS):
        den = den + jnp.exp(coarse[i] - best)
    pg = 1.0 / den

    fine = []
    for k in range(EXPERTS_PER_GROUP):
        f = lt[N_EXPERT_GROUPS + k:N_EXPERT_GROUPS + k + 1, :]
        for gi in range(1, N_EXPERT_GROUPS):
            r = N_EXPERT_GROUPS + gi * EXPERTS_PER_GROUP + k
            f = jnp.where(gsel == gi, lt[r:r + 1, :], f)
        fine.append(f)
    v0 = fine[0]
    i0 = jnp.zeros((1, tm), jnp.int32)
    for k in range(1, EXPERTS_PER_GROUP):
        upd = fine[k] > v0
        i0 = jnp.where(upd, k, i0)
        v0 = jnp.where(upd, fine[k], v0)
    v1 = jnp.full((1, tm), -jnp.inf, F32)
    i1 = jnp.zeros((1, tm), jnp.int32)
    for k in range(EXPERTS_PER_GROUP):
        upd = jnp.where(i0 != k, jnp.where(fine[k] > v1, 1, 0), 0) == 1
        i1 = jnp.where(upd, k, i1)
        v1 = jnp.where(upd, fine[k], v1)
    e1w = jnp.exp(v1 - v0)
    gate0 = pg * (1.0 / (1.0 + e1w))
    gate1 = pg * (e1w / (1.0 + e1w))
    e0 = gsel * EXPERTS_PER_GROUP + i0
    e1 = gsel * EXPERTS_PER_GROUP + i1

    eidx = lax.broadcasted_iota(jnp.int32, (N_EXPERTS, tm), 0)
    oh0 = jnp.where(eidx == e0, jnp.float32(1), jnp.float32(0))
    oh1 = jnp.where(eidx == e1, jnp.float32(1), jnp.float32(0))
    ta = lax.broadcasted_iota(jnp.int32, (tm, tm), 0)
    tb = lax.broadcasted_iota(jnp.int32, (tm, tm), 1)
    before = jnp.where(ta < tb, jnp.float32(1), jnp.float32(0)).astype(BF16)
    pre0 = _dot(oh0.astype(BF16), before)
    pre1 = _dot(oh1.astype(BF16), before)
    run = run_ref[:, 0:1]
    tot0 = jnp.sum(oh0, axis=1, keepdims=True)
    tot1 = jnp.sum(oh1, axis=1, keepdims=True)
    rank0 = jnp.sum(oh0 * (run + pre0), axis=0, keepdims=True)
    rank1 = jnp.sum(oh1 * (run + tot0 + pre1), axis=0, keepdims=True)
    new_run = jnp.broadcast_to(run + tot0 + tot1, run_ref.shape)
    run_ref[...] = new_run
    cnt_ref[...] = new_run

    route_ref[0:1, :] = e0.astype(F32)
    route_ref[1:2, :] = e1.astype(F32)
    route_ref[2:3, :] = gate0
    route_ref[3:4, :] = gate1
    route_ref[4:5, :] = rank0
    route_ref[5:6, :] = rank1
    route_ref[6:8, :] = jnp.zeros((2, tm), F32)


def _merge_router_call(oa, ob, zg, x2d, pa, pb, wo, g, wr_t):
    t, d = x2d.shape
    tm = ROW_TM
    const = dict(pipeline_mode=pl.Buffered(1))
    return pl.pallas_call(
        _merge_router_kernel,
        grid=(t // tm,),
        in_specs=[
            pl.BlockSpec((tm, oa.shape[1]), lambda i: (i, 0)),
            pl.BlockSpec((tm, ob.shape[1]), lambda i: (i, 0)),
            pl.BlockSpec((tm, GATE_COLS), lambda i: (i, 0)),
            pl.BlockSpec((tm, d), lambda i: (i, 0)),
            pl.BlockSpec(pa.shape, lambda i: (0, 0), **const),
            pl.BlockSpec(pb.shape, lambda i: (0, 0), **const),
            pl.BlockSpec(wo.shape, lambda i: (0, 0), **const),
            pl.BlockSpec((1, d), lambda i: (0, 0)),
            pl.BlockSpec(wr_t.shape, lambda i: (0, 0, 0), **const),
        ],
        out_specs=[
            pl.BlockSpec((tm, d), lambda i: (i, 0)),
            pl.BlockSpec((tm, d), lambda i: (i, 0)),
            pl.BlockSpec((SUBLANES, tm), lambda i: (0, i)),
            pl.BlockSpec((N_EXPERTS, LANES), lambda i: (0, 0)),
        ],
        out_shape=[
            jax.ShapeDtypeStruct((t, d), F32),
            jax.ShapeDtypeStruct((t, d), F32),
            jax.ShapeDtypeStruct((SUBLANES, t), F32),
            jax.ShapeDtypeStruct((N_EXPERTS, LANES), F32),
        ],
        scratch_shapes=[pltpu.VMEM((N_EXPERTS, LANES), F32)],
        compiler_params=_cparams(("arbitrary",)),
        name="merge_router",
    )(oa, ob, zg, x2d, pa, pb, wo, g, wr_t)


def _row_copy(src, s_row, dst, d_row, sem):
    return pltpu.make_async_copy(src.at[pl.ds(s_row, 1)], dst.at[pl.ds(d_row, 1)], sem)


def _dispatch_kernel(dest_ref, zstart_ref, zlen_ref, h_ref, x_hbm, zero_ref, sem, zsem):
    step = pl.program_id(0)

    def issue(t, carry):
        for k in range(TOP_K):
            _row_copy(h_ref, t, x_hbm, dest_ref[0, k, t], sem.at[k]).start()
        return carry

    lax.fori_loop(0, DISPATCH_TOK, issue, 0, unroll=8)

    @pl.when(step == 0)
    def _():
        zero_ref[...] = jnp.zeros_like(zero_ref)

        def group_copy(r8):
            return pltpu.make_async_copy(
                zero_ref, x_hbm.at[pl.ds(pl.multiple_of(r8, SUBLANES), SUBLANES)], zsem.at[1])

        def segment(e, wait):
            start = zstart_ref[e]
            n_head = jnp.minimum(lax.bitwise_and(-start, SUBLANES - 1), zlen_ref[e])
            n_group = lax.shift_right_logical(zlen_ref[e] - n_head, SUBLANES.bit_length() - 1)

            def head(r, carry):
                cp = _row_copy(zero_ref, 0, x_hbm, 0 if wait else start + r, zsem.at[0])
                cp.wait() if wait else cp.start()
                return carry

            def group(j, carry):
                cp = group_copy(0 if wait else start + n_head + SUBLANES * j)
                cp.wait() if wait else cp.start()
                return carry

            lax.fori_loop(0, n_head, head, 0)
            lax.fori_loop(0, n_group, group, 0)

        lax.fori_loop(0, N_EXPERTS + 1, lambda e, c: (segment(e, False), c)[1], 0)
        lax.fori_loop(0, N_EXPERTS + 1, lambda e, c: (segment(e, True), c)[1], 0)

    for k in range(TOP_K):
        pltpu.make_async_copy(h_ref, x_hbm.at[pl.ds(0, DISPATCH_TOK)], sem.at[k]).wait()


def _dispatch_call(dest_blocks, zstart, zlen, h2, n_rows):
    t, d = h2.shape
    return pl.pallas_call(
        _dispatch_kernel,
        grid=(t // DISPATCH_TOK,),
        in_specs=[
            pl.BlockSpec((1, TOP_K, DISPATCH_TOK), lambda i: (i, 0, 0), memory_space=pltpu.SMEM),
            pl.BlockSpec(memory_space=pltpu.SMEM),
            pl.BlockSpec(memory_space=pltpu.SMEM),
            pl.BlockSpec((DISPATCH_TOK, d), lambda i: (i, 0)),
        ],
        out_specs=pl.BlockSpec(memory_space=pl.ANY),
        out_shape=jax.ShapeDtypeStruct((n_rows, d), F32),
        scratch_shapes=[pltpu.VMEM((SUBLANES, d), F32), pltpu.SemaphoreType.DMA((TOP_K,)),
                        pltpu.SemaphoreType.DMA((2,))],
        compiler_params=_cparams(("arbitrary",)),
        name="dispatch",
    )(dest_blocks, zstart, zlen, h2)


def _expert_kernel(blk_e_ref, blk_valid_ref, x_ref, w1_ref, w3_ref, w2_ref, y_ref):
    i = pl.program_id(0)

    @pl.when(blk_valid_ref[i] == 1)
    def _():
        x = x_ref[...].astype(BF16)
        a = _dot(x, w1_ref[0])
        b = _dot(x, w3_ref[0])
        hdn = (a * _sigmoid(a)) * b
        y_ref[...] = _dot(hdn.astype(BF16), w2_ref[0])

    @pl.when(blk_valid_ref[i] == 0)
    def _():
        y_ref[...] = jnp.zeros_like(y_ref)


def _expert_call(blk_e, blk_valid, xin, w1, w3, w2):
    n_rows, d = xin.shape
    de = w1.shape[2]
    grid_spec = pltpu.PrefetchScalarGridSpec(
        num_scalar_prefetch=2,
        grid=(n_rows // EXP_TM,),
        in_specs=[
            pl.BlockSpec((EXP_TM, d), lambda i, be, bv: (i, 0)),
            pl.BlockSpec((1, d, de), lambda i, be, bv: (be[i], 0, 0)),
            pl.BlockSpec((1, d, de), lambda i, be, bv: (be[i], 0, 0)),
            pl.BlockSpec((1, de, d), lambda i, be, bv: (be[i], 0, 0)),
        ],
        out_specs=pl.BlockSpec((EXP_TM, d), lambda i, be, bv: (i, 0)),
    )
    return pl.pallas_call(
        _expert_kernel,
        grid_spec=grid_spec,
        out_shape=jax.ShapeDtypeStruct((n_rows, d), F32),
        compiler_params=_cparams(("arbitrary",)),
        name="experts",
    )(blk_e, blk_valid, xin, w1, w3, w2)


def _combine_kernel(dest_ref, dnext_ref, y_hbm, x1_ref, gt_ref, p_ref, gple_ref, wg_ref, wp_ref, gfin_ref,
                    o_ref, ybuf, sem, *, final_norm, n_steps):
    tm = x1_ref.shape[0]
    i = pl.program_id(0)
    slot = lax.bitwise_and(i, 1)

    def gather(d_ref, s):
        def issue(t, carry):
            for k in range(TOP_K):
                _row_copy(y_hbm, d_ref[0, k, t], ybuf.at[s, k], t, sem.at[s, k]).start()
            return carry

        lax.fori_loop(0, tm, issue, 0, unroll=8)

    @pl.when(i == 0)
    def _():
        gather(dest_ref, 0)

    @pl.when(i + 1 < n_steps)
    def _():
        gather(dnext_ref, 1 - slot)

    for k in range(TOP_K):
        pltpu.make_async_copy(y_hbm.at[pl.ds(0, tm)], ybuf.at[slot, k], sem.at[slot, k]).wait()

    for r0 in range(0, tm, ROW_TM):
        rows = slice(r0, r0 + ROW_TM)
        pp = _dot(p_ref[rows, :].astype(BF16), wp_ref[...])
        y = ybuf[slot, 0, rows, :] * gt_ref[rows, 2:3] + ybuf[slot, 1, rows, :] * gt_ref[rows, 3:4]
        x2 = x1_ref[rows, :] + y
        hn = _rmsnorm_val(x2, gple_ref[...], RMS_EPS)
        gate = _sigmoid(_dot(hn.astype(BF16), wg_ref[...]))
        x3 = x2 + gate * pp
        if final_norm:
            o_ref[rows, :] = _rmsnorm_val(x3, gfin_ref[...], RMS_EPS)
        else:
            o_ref[rows, :] = x3


def _combine_call(dest_blocks, yb, x1, gates_t, p2d, g_ple, w_gate, w_proj, g_fin, final_norm):
    t, d = x1.shape
    tm = COMBINE_TM
    const = dict(pipeline_mode=pl.Buffered(1))
    return pl.pallas_call(
        functools.partial(_combine_kernel, final_norm=final_norm, n_steps=t // tm),
        grid=(t // tm,),
        in_specs=[
            pl.BlockSpec((1, TOP_K, tm), lambda i: (i, 0, 0), memory_space=pltpu.SMEM),
            pl.BlockSpec((1, TOP_K, tm), lambda i: (jnp.minimum(i + 1, t // tm - 1), 0, 0),
                         memory_space=pltpu.SMEM),
            pl.BlockSpec(memory_space=pl.ANY),
            pl.BlockSpec((tm, d), lambda i: (i, 0)),
            pl.BlockSpec((tm, SUBLANES), lambda i: (i, 0)),
            pl.BlockSpec((tm, p2d.shape[1]), lambda i: (i, 0)),
            pl.BlockSpec((1, d), lambda i: (0, 0)),
            pl.BlockSpec(w_gate.shape, lambda i: (0, 0), **const),
            pl.BlockSpec(w_proj.shape, lambda i: (0, 0), **const),
            pl.BlockSpec((1, d), lambda i: (0, 0)),
        ],
        out_specs=pl.BlockSpec((tm, d), lambda i: (i, 0)),
        out_shape=jax.ShapeDtypeStruct((t, d), F32),
        scratch_shapes=[pltpu.VMEM((2, TOP_K, tm, d), F32), pltpu.SemaphoreType.DMA((2, TOP_K))],
        compiler_params=_cparams(("arbitrary",)),
        name="combine",
    )(dest_blocks, dest_blocks, yb, x1, gates_t, p2d, g_ple, w_gate, w_proj, g_fin)


def _routing_plan(route, counts_f, t):
    counts = counts_f[:, 0].astype(jnp.int32)
    pcounts = ((counts + EXP_TM - 1) // EXP_TM) * EXP_TM
    pends = jnp.cumsum(pcounts)
    pstarts = pends - pcounts
    eid = route[0:2].astype(jnp.int32)
    rank = route[4:6].astype(jnp.int32)
    onehot = eid[..., None] == jnp.arange(N_EXPERTS, dtype=jnp.int32)
    dest = jnp.sum(jnp.where(onehot, pstarts, 0), axis=-1) + rank
    n_rows = TOP_K * t + N_EXPERTS * EXP_TM
    n_blocks = n_rows // EXP_TM
    blk_row = jnp.arange(n_blocks, dtype=jnp.int32) * EXP_TM
    blk_e = jnp.sum((pends[None, :] <= blk_row[:, None]).astype(jnp.int32), axis=1)
    blk_valid = (blk_row < pends[-1]).astype(jnp.int32)
    last_e = jnp.sum((pends <= pends[-1] - 1).astype(jnp.int32))
    blk_e = jnp.minimum(blk_e, last_e)
    zstart = jnp.concatenate([pstarts + counts, pends[-1:]]).astype(jnp.int32)
    zlen = jnp.concatenate([pcounts - counts, n_rows - pends[-1:]]).astype(jnp.int32)
    return dest, blk_e, blk_valid, zstart, zlen, n_rows


def _blocked(dest, tok_per_block):
    t = dest.shape[1]
    return dest.reshape(TOP_K, t // tok_per_block, tok_per_block).transpose(1, 0, 2)


def kernel(x, p, rel_bias, norm_mix_g, w_in, w_gate, lambda_q1, lambda_k1, lambda_q2, lambda_k2, subln_g,
           w_proj_a, w_proj_b, w_out, norm_ffn_g, w_coarse, w_fine, w1, w3, w2, norm_ple_g, w_ple_gate,
           w_ple_proj, final_norm_g):
    batch, seq, d = x.shape
    depth = w_in.shape[0]
    t = batch * seq
    assert seq % PERM_BLK == 0 and t % min(PROJ_TM, t) == 0
    assert all(seq // dil >= BLK and win // dil == BLK for win, dil in DILATED_GROUPS)

    nq = seq // ATT_BLK
    bias_a, w_cat0 = _bias_tiles_call(rel_bias, n_heads=A_HEADS, n_off=nq, tq=ATT_BLK, tk=ATT_BLK,
                                      off_mult=ATT_BLK, off_add=0, dil=1, max_rel=seq, head0=0, name="bias_diff",
                                      key_major=True, mult=A_HEAD_DIM ** 0.5, concat_cast=(w_gate[0], w_in[0]))
    bias_b = jnp.concatenate([
        _bias_tiles_call(rel_bias, n_heads=B_HEADS, n_off=1, tq=2 * BLK, tk=BLK, off_mult=0, off_add=BLK,
                         dil=dil, max_rel=win // dil, head0=A_HEADS + gi * B_HEADS, name="bias_dil%d" % gi,
                         key_major=True, mult=B_HEAD_DIM ** 0.5)[0].reshape(1, B_HEADS, 2 * BLK, BLK)
        for gi, (win, dil) in enumerate(DILATED_GROUPS)], axis=0)

    x2d = x.reshape(t, d)
    for layer in range(depth):
        lam_init = 0.8 - 0.6 * math.exp(-0.3 * layer)
        lam = _lam_call(lambda_q1[layer:layer + 1], lambda_k1[layer:layer + 1],
                        lambda_q2[layer:layer + 1], lambda_k2[layer:layer + 1], lam_init)
        if layer == 0 and w_cat0 is not None:
            w_cat = w_cat0
        else:
            w_cat = jnp.concatenate([w_gate[layer], w_in[layer]], axis=1).astype(BF16)
        zg = _inproj_call(x2d, norm_mix_g[layer:layer + 1], w_cat)
        n_exp, _, d_exp = w1[layer].shape
        oa, (w1b, w3b, w2b) = _diff_attn_call(
            zg, lam, bias_a, subln_g[layer:layer + 1],
            [w1[layer].reshape(n_exp * d, d_exp), w3[layer].reshape(n_exp * d, d_exp),
             w2[layer].reshape(n_exp * d_exp, d)],
            batch=batch, seq=seq, out_scale=1.0 - lam_init)
        ob = _dilated_call(zg, bias_b, batch=batch, seq=seq)

        wr_t = jnp.concatenate([
            w_coarse[layer].T,
            w_fine[layer].transpose(0, 2, 1).reshape(N_EXPERTS, d),
            jnp.zeros((LANES - N_EXPERT_GROUPS - N_EXPERTS, d), F32)], axis=0)
        wr_t = wr_t.T
        wr_hi = wr_t.astype(BF16)
        wr_t = jnp.stack([wr_hi, (wr_t - wr_hi.astype(F32)).astype(BF16)], axis=0)
        x1, h2, route, counts = _merge_router_call(
            oa, ob, zg, x2d, w_proj_a[layer].astype(BF16), w_proj_b[layer].astype(BF16),
            w_out[layer].astype(BF16), norm_ffn_g[layer:layer + 1], wr_t)

        dest, blk_e, blk_valid, zstart, zlen, n_rows = _routing_plan(route, counts, t)
        xin = _dispatch_call(_blocked(dest, DISPATCH_TOK), zstart, zlen, h2, n_rows)
        yb = _expert_call(blk_e, blk_valid, xin, w1b.reshape(n_exp, d, d_exp), w3b.reshape(n_exp, d, d_exp),
                          w2b.reshape(n_exp, d_exp, d))
        x2d = _combine_call(_blocked(dest, COMBINE_TM), yb, x1, route.T, p[layer].reshape(t, -1),
                            norm_ple_g[layer:layer + 1], w_ple_gate[layer].astype(BF16),
                            w_ple_proj[layer].astype(BF16), final_norm_g.reshape(1, d),
                            final_norm=layer == depth - 1)
    return x2d.reshape(batch, seq, d)
```

```python
import functools
import math

import jax
import jax.numpy as jnp
from jax import lax
from jax.experimental import pallas as pl
from jax.experimental.pallas import tpu as pltpu

F32 = jnp.float32
BF16 = jnp.bfloat16

BLK = 128
NEG_INF = -1e30
RMS_EPS = 1e-6
SUBLN_EPS = 1e-5
N_BUCKETS = 32
MAX_DISTANCE = 2048
A_HEADS = 8
A_HEAD_DIM = 128
DILATED_GROUPS = ((128, 1), (512, 4), (2048, 16))
N_DGROUPS = 3
B_HEADS = 8
B_HEAD_DIM = 64
N_EXPERT_GROUPS = 4
EXPERTS_PER_GROUP = 8
N_EXPERTS = 32
TOP_K = 2

LANES = 128
SUBLANES = 8
VMEM_LIMIT = 56 * 1024 * 1024
ATT_BLK = 256
ATT_HEADS = 2
PERM_BLK = 256
GATE_COLS = 4096
PROJ_TN = 512
PROJ_TM = 2048
ROW_TM = 256
COMBINE_TM = 512
EXP_TM = 256
DISPATCH_TOK = 2048
NORM_CHUNK = 128
CAST_SLAB_BYTES = 1 << 20
DIL_GROUP = 4
TOEPLITZ_LANES = 512


def _t5_thresholds():
    max_exact = N_BUCKETS // 2
    out = []
    for k in range(1, N_BUCKETS - max_exact):
        out.append(int(math.ceil(max_exact * (MAX_DISTANCE / max_exact) ** (k / (N_BUCKETS - max_exact)))))
    return tuple(out)


T5_THRESHOLDS = _t5_thresholds()


def _cparams(sem, vmem=VMEM_LIMIT):
    return pltpu.CompilerParams(dimension_semantics=sem, vmem_limit_bytes=vmem)


def _mult(x, m):
    return x if isinstance(x, int) else pl.multiple_of(x, m)


def _sigmoid(x):
    return 0.5 * jnp.tanh(0.5 * x) + 0.5


def _dot(a, b, **kw):
    return jnp.dot(a, b, preferred_element_type=F32, **kw)


def _dot_nt(a, b, **kw):
    return lax.dot_general(a, b, (((1,), (1,)), ((), ())), preferred_element_type=F32, **kw)


def _lam_kernel(q1_ref, k1_ref, q2_ref, k2_ref, o_ref, *, lam_init):
    s1 = jnp.sum(q1_ref[...] * k1_ref[...], axis=-1, keepdims=True)
    s2 = jnp.sum(q2_ref[...] * k2_ref[...], axis=-1, keepdims=True)
    o_ref[...] = jnp.exp(s1) - jnp.exp(s2) + lam_init


def _lam_call(lq1, lk1, lq2, lk2, lam_init):
    return pl.pallas_call(
        functools.partial(_lam_kernel, lam_init=lam_init),
        out_shape=jax.ShapeDtypeStruct((1, 1), F32),
        name="lam",
    )(lq1, lk1, lq2, lk2)


def _bias_tile_kernel(tab_ref, *rest, tq, tk, off_mult, off_add, dil, max_rel, head0, key_major, mult):
    n_pass = max(len(rest) - 2, 0)
    o_ref = rest[n_pass]
    if n_pass:
        col = 0
        for src in rest[:n_pass]:
            rest[-1][:, col:col + src.shape[1]] = src[...].astype(rest[-1].dtype)
            col += src.shape[1]
    h = pl.program_id(0)
    n = pl.program_id(1)
    assert key_major and tq + tk <= TOEPLITZ_LANES
    m = lax.broadcasted_iota(jnp.int32, (8, TOEPLITZ_LANES), 1)
    rel = jnp.where(m < tk, m, m - TOEPLITZ_LANES) + (n * off_mult + off_add)
    dist = rel * dil
    large = jnp.full(m.shape, N_BUCKETS // 2, jnp.int32)
    for thr in T5_THRESHOLDS:
        large = large + jnp.where(dist >= thr, 1, 0)
    bucket = jnp.where(dist < N_BUCKETS // 2, dist, large)
    acc = jnp.zeros(m.shape, F32)
    for b in range(N_BUCKETS):
        acc = jnp.where(bucket == b, tab_ref[b, head0 + h], acc)
    valid = jnp.where(rel >= 0, jnp.where(rel <= max_rel, 1, 0), 0)
    vec = jnp.where(valid == 1, acc * mult, NEG_INF)[0:1, :]
    rows = pltpu.roll(jnp.broadcast_to(vec, (tq, TOEPLITZ_LANES)), 0, 1, stride=1, stride_axis=0)
    o_ref[0, 0] = rows[:, :tk]


def _bias_tiles_call(rel_bias, *, n_heads, n_off, tq, tk, off_mult, off_add, dil, max_rel, head0, name,
                     key_major=False, mult=1.0, concat_cast=()):
    kern = functools.partial(_bias_tile_kernel, tq=tq, tk=tk, off_mult=off_mult, off_add=off_add,
                             dil=dil, max_rel=max_rel, head0=head0, key_major=key_major, mult=mult)
    n_steps = n_heads * n_off
    if concat_cast and concat_cast[0].shape[0] % (n_steps * 16):
        concat_cast = ()
    in_specs = [pl.BlockSpec(memory_space=pltpu.SMEM)]
    out_specs = [pl.BlockSpec((1, 1, tq, tk), lambda h, n: (h, n, 0, 0))]
    out_shape = [jax.ShapeDtypeStruct((n_heads, n_off, tq, tk), F32)]
    if concat_cast:
        rows = concat_cast[0].shape[0]
        cols = sum(w.shape[1] for w in concat_cast)
        in_specs += [pl.BlockSpec((rows // n_steps, w.shape[1]), lambda h, n: (h * n_off + n, 0))
                     for w in concat_cast]
        out_specs.append(pl.BlockSpec((rows // n_steps, cols), lambda h, n: (h * n_off + n, 0)))
        out_shape.append(jax.ShapeDtypeStruct((rows, cols), BF16))
    outs = pl.pallas_call(
        kern,
        grid=(n_heads, n_off),
        in_specs=in_specs,
        out_specs=out_specs,
        out_shape=out_shape,
        compiler_params=_cparams(("parallel", "parallel")),
        name=name,
    )(rel_bias, *concat_cast)
    return outs[0], (outs[1] if concat_cast else None)


def _rmsnorm_rows(x_ref, g_ref, out_ref, eps):
    rows = x_ref.shape[0]
    g = g_ref[...]

    def body(c, carry):
        r0 = pl.multiple_of(c * NORM_CHUNK, NORM_CHUNK)
        x = x_ref[pl.ds(r0, NORM_CHUNK), :]
        ms = jnp.mean(x * x, axis=-1, keepdims=True)
        out_ref[pl.ds(r0, NORM_CHUNK), :] = ((x * lax.rsqrt(ms + eps)) * g).astype(out_ref.dtype)
        return carry

    lax.fori_loop(0, rows // NORM_CHUNK, body, 0)


def _rmsnorm_val(x, g, eps):
    ms = jnp.mean(x * x, axis=-1, keepdims=True)
    return (x * lax.rsqrt(ms + eps)) * g


def _inproj_kernel(x_ref, g_ref, w_ref, o_ref, h_ref, *, n_gate_blocks):
    j = pl.program_id(1)

    @pl.when(j == 0)
    def _():
        _rmsnorm_rows(x_ref, g_ref, h_ref, RMS_EPS)

    acc = _dot(h_ref[...], w_ref[...])

    @pl.when(j < n_gate_blocks)
    def _():
        o_ref[...] = _sigmoid(acc).astype(o_ref.dtype)

    @pl.when(j >= n_gate_blocks)
    def _():
        o_ref[...] = acc.astype(o_ref.dtype)


def _inproj_call(x2d, g, w_cat):
    t, d = x2d.shape
    n = w_cat.shape[1]
    tm = min(PROJ_TM, t)
    return pl.pallas_call(
        functools.partial(_inproj_kernel, n_gate_blocks=GATE_COLS // PROJ_TN),
        grid=(t // tm, n // PROJ_TN),
        in_specs=[
            pl.BlockSpec((tm, d), lambda i, j: (i, 0)),
            pl.BlockSpec((1, d), lambda i, j: (0, 0)),
            pl.BlockSpec((d, PROJ_TN), lambda i, j: (0, j)),
        ],
        out_specs=pl.BlockSpec((tm, PROJ_TN), lambda i, j: (i, j)),
        out_shape=jax.ShapeDtypeStruct((t, n), BF16),
        scratch_shapes=[pltpu.VMEM((tm, d), BF16)],
        compiler_params=_cparams(("parallel", "arbitrary")),
        name="inproj",
    )(x2d, g, w_cat)


def _diff_attn_kernel(lam_ref, q_ref, k_ref, v_ref, bias_ref, g_ref, *rest, out_scale, n_blk, n_cast):
    cast_in = rest[:n_cast]
    o_ref = rest[n_cast]
    cast_out = rest[n_cast + 1:2 * n_cast + 1]
    vt_ref, tbuf_a, tbuf_b, acc = rest[2 * n_cast + 1:]
    for src, dst in zip(cast_in, cast_out):
        dst[...] = src[...].astype(dst.dtype)
    qi = pl.program_id(2)
    width = 2 * A_HEAD_DIM
    n_map = 2 * ATT_HEADS
    c = (A_HEAD_DIM ** -0.5) * math.log2(math.e)

    @pl.when(qi == 0)
    def _():
        def transpose_block(b, carry):
            r0 = pl.multiple_of(b * ATT_BLK, ATT_BLK)
            for hp in range(ATT_HEADS):
                vb = v_ref[pl.ds(r0, ATT_BLK), hp * width:(hp + 1) * width]
                vt_ref[hp, b] = vb.astype(F32).T.astype(BF16)
            return carry

        lax.fori_loop(0, n_blk, transpose_block, 0)

    acc[...] = jnp.zeros_like(acc)

    q0 = pl.multiple_of(qi * ATT_BLK, ATT_BLK)
    map_cols = [slice(mi * A_HEAD_DIM, (mi + 1) * A_HEAD_DIM) for mi in range(n_map)]
    qs = [q_ref[pl.ds(q0, ATT_BLK), cols] for cols in map_cols]

    def scores(ki, dst, qt=None):
        k0 = pl.multiple_of(ki * ATT_BLK, ATT_BLK)
        for mi, cols in enumerate(map_cols):
            if qt is None:
                q, tile = qs[mi], qi - ki
            else:
                q, tile = q_ref[pl.ds(pl.multiple_of(qt * ATT_BLK, ATT_BLK), ATT_BLK), cols], qt - ki
            dst[mi] = _dot_nt(k_ref[pl.ds(k0, ATT_BLK), cols], q) + bias_ref[mi // 2, tile]

    def softmax(t, m, l):
        m_new = jnp.maximum(m, jnp.max(t, axis=0, keepdims=True))
        alpha = jnp.exp2((m - m_new) * c)
        p = jnp.exp2((t - m_new) * c)
        return m_new, alpha * l + jnp.sum(p, axis=0, keepdims=True), alpha, p.astype(BF16)

    first_buf = lax.bitwise_and(lax.shift_right_logical(qi * (qi + 1), 1), 1)

    @pl.when(qi == 0)
    def _():
        scores(0, tbuf_a)

    def step(ki, carry, cur, nxt, last):
        ms, ls = carry
        stats = [softmax(cur[mi], ms[mi], ls[mi]) for mi in range(n_map)]
        if last:
            scores(0, nxt, qt=jnp.minimum(qi + 1, n_blk - 1))
        else:
            scores(ki + 1, nxt)
        for mi in range(n_map):
            acc[mi] = acc[mi] * stats[mi][2] + _dot(vt_ref[mi // 2, ki], stats[mi][3])
        return tuple(st[0] for st in stats), tuple(st[1] for st in stats)

    def body(ki, carry, last=False):
        return lax.cond(lax.bitwise_and(first_buf + ki, 1) == 0,
                        lambda cr: step(ki, cr, tbuf_a, tbuf_b, last),
                        lambda cr: step(ki, cr, tbuf_b, tbuf_a, last), carry)

    minf = (jnp.full((1, ATT_BLK), -jnp.inf, F32),) * n_map
    zero = (jnp.zeros((1, ATT_BLK), F32),) * n_map
    _, ls = body(qi, lax.fori_loop(0, qi, body, (minf, zero)), last=True)
    for hp in range(ATT_HEADS):
        w = acc[2 * hp] / ls[2 * hp] - lam_ref[0, 0] * (acc[2 * hp + 1] / ls[2 * hp + 1])
        ms = jnp.mean(w * w, axis=0, keepdims=True)
        y = ((w * lax.rsqrt(ms + SUBLN_EPS)) * g_ref[...]) * out_scale
        o_ref[:, hp * width:(hp + 1) * width] = y.T.astype(o_ref.dtype)


def _diff_attn_call(zg, lam, bias_tiles, subln_g, passengers, *, batch, seq, out_scale):
    t = batch * seq
    nq = seq // ATT_BLK
    width = 2 * A_HEAD_DIM
    blk_w = ATT_HEADS * width
    q_col = GATE_COLS // blk_w
    k_col = q_col + A_HEADS // ATT_HEADS
    v_col = k_col + A_HEADS // ATT_HEADS
    n_steps = (A_HEADS // ATT_HEADS) * batch * nq
    riders = [w for w in passengers
              if w.shape[0] % (n_steps * 16) == 0 and w.size * 4 // n_steps <= CAST_SLAB_BYTES]

    def slab_spec(w):
        return pl.BlockSpec((w.shape[0] // n_steps, w.shape[1]), lambda h, b, i: ((h * batch + b) * nq + i, 0))

    outs = pl.pallas_call(
        functools.partial(_diff_attn_kernel, out_scale=out_scale, n_blk=nq, n_cast=len(riders)),
        grid=(A_HEADS // ATT_HEADS, batch, nq),
        in_specs=[
            pl.BlockSpec(memory_space=pltpu.SMEM),
            pl.BlockSpec((seq, blk_w), lambda h, b, i: (b, q_col + h)),
            pl.BlockSpec((seq, blk_w), lambda h, b, i: (b, k_col + h)),
            pl.BlockSpec((seq, blk_w), lambda h, b, i: (b, v_col + h)),
            pl.BlockSpec((ATT_HEADS, nq, ATT_BLK, ATT_BLK), lambda h, b, i: (h, 0, 0, 0)),
            pl.BlockSpec((width, 1), lambda h, b, i: (0, 0)),
        ] + [slab_spec(w) for w in riders],
        out_specs=[pl.BlockSpec((ATT_BLK, blk_w), lambda h, b, i: (b * nq + i, h))]
        + [slab_spec(w) for w in riders],
        out_shape=[jax.ShapeDtypeStruct((t, A_HEADS * width), BF16)]
        + [jax.ShapeDtypeStruct(w.shape, BF16) for w in riders],
        scratch_shapes=[pltpu.VMEM((ATT_HEADS, nq, width, ATT_BLK), BF16),
                        pltpu.VMEM((2 * ATT_HEADS, ATT_BLK, ATT_BLK), F32),
                        pltpu.VMEM((2 * ATT_HEADS, ATT_BLK, ATT_BLK), F32),
                        pltpu.VMEM((2 * ATT_HEADS, width, ATT_BLK), F32)],
        compiler_params=_cparams(("parallel", "parallel", "arbitrary")),
        name="diff_attn",
    )(lam, zg, zg, zg, bias_tiles, subln_g.reshape(width, 1), *riders)
    cast = iter(outs[1:])
    return outs[0], [next(cast) if any(w is r for r in riders) else w.astype(BF16) for w in passengers]


def _perm_matrix(dil, inverse):
    w = PERM_BLK // dil
    shift = w.bit_length() - 1
    a = lax.broadcasted_iota(jnp.int32, (PERM_BLK, PERM_BLK), 0)
    b = lax.broadcasted_iota(jnp.int32, (PERM_BLK, PERM_BLK), 1)
    dst, src = (b, a) if inverse else (a, b)
    c = lax.shift_right_logical(dst, shift)
    ll = lax.bitwise_and(dst, w - 1)
    return jnp.where(src == ll * dil + c, jnp.float32(1), jnp.float32(0))


def _deinterleave(pairs, perm, dil, seq):
    w = PERM_BLK // dil
    stream_len = seq // dil

    def body(b8, carry):
        r0 = pl.multiple_of(b8 * PERM_BLK, PERM_BLK)
        ys = [_dot(perm, src[pl.ds(r0, PERM_BLK), :]).astype(dst.dtype) for src, dst in pairs]
        for y, (_, dst) in zip(ys, pairs):
            for c in range(dil):
                d0 = pl.multiple_of(c * stream_len + b8 * w, w)
                dst[pl.ds(d0, w), :] = y[c * w:(c + 1) * w, :]
        return carry

    lax.fori_loop(0, seq // PERM_BLK, body, 0)


def _permute_rows(perm, x):
    if x.dtype == BF16:
        return _dot(perm, x)
    hi = x.astype(BF16)
    r1 = x - hi.astype(F32)
    mid = r1.astype(BF16)
    lo = (r1 - mid.astype(F32)).astype(BF16)
    y = _dot(perm, jnp.concatenate([hi, mid, lo], axis=1))
    n = x.shape[1]
    return y[:, :n] + (y[:, n:2 * n] + y[:, 2 * n:])


def _interleave(items, perm_inv, dil, seq):
    w = PERM_BLK // dil
    stream_len = seq // dil

    def body(b8, carry):
        for src, _, stack in items:
            for c in range(dil):
                s0 = pl.multiple_of(c * stream_len + b8 * w, w)
                stack[c * w:(c + 1) * w, :] = src[pl.ds(s0, w), :]
        outs = [_permute_rows(perm_inv, stack[...]) for _, _, stack in items]
        r0 = pl.multiple_of(b8 * PERM_BLK, PERM_BLK)
        for out, (_, dst, _) in zip(outs, items):
            dst[pl.ds(r0, PERM_BLK), :] = out.astype(dst.dtype)
        return carry

    lax.fori_loop(0, seq // PERM_BLK, body, 0)


def _transpose_blocks(src_ref, vt_ref, seq):
    def body(blk, carry):
        r0 = pl.multiple_of(blk * BLK, BLK)
        vt_ref[blk] = src_ref[pl.ds(r0, BLK), :].astype(F32).T.astype(vt_ref.dtype)
        return carry

    lax.fori_loop(0, seq // BLK, body, 0)


def _dil_windows(q_src, k_src, vt_ref, o_dst, lse_dst, bias_ref, ot_ref, lt_ref, windows):
    scale = B_HEAD_DIM ** -0.5
    c = scale * math.log2(math.e)
    tiles = []
    for rq, nk in windows:
        rk = rq - (nk - BLK)
        for hh in range(B_HEADS):
            cols = slice(hh * B_HEAD_DIM, (hh + 1) * B_HEAD_DIM)
            qh = q_src[pl.ds(rq, BLK), cols]
            kh = k_src[pl.ds(rk, nk), cols]
            tiles.append(_dot_nt(kh, qh) + bias_ref[0, hh, 2 * BLK - nk:, :])
    for wi, (rq, nk) in enumerate(windows):
        qb = rq // BLK if isinstance(rq, int) else lax.shift_right_logical(rq, BLK.bit_length() - 1)
        for hh in range(B_HEADS):
            cols = slice(hh * B_HEAD_DIM, (hh + 1) * B_HEAD_DIM)
            t = tiles[wi * B_HEADS + hh]
            m = jnp.max(t, axis=0, keepdims=True)
            p = jnp.exp2((t - m) * c)
            den = jnp.sum(p, axis=0, keepdims=True)
            if nk == BLK:
                vth = vt_ref[qb, cols, :]
            else:
                vth = jnp.concatenate([vt_ref[qb - 1, cols, :], vt_ref[qb, cols, :]], axis=1)
            ot_ref[wi, cols, :] = _dot(vth, p.astype(BF16)) / den
            lt_ref[wi, hh:hh + 1, :] = m * scale + jnp.log(den)
        o_dst[pl.ds(rq, BLK), :] = ot_ref[wi].T.astype(o_dst.dtype)
        lse_dst[pl.ds(rq, BLK), :] = lt_ref[wi].T


def _dil_streams(q_src, k_src, vt_ref, o_dst, lse_dst, bias_ref, ot_ref, lt_ref, dil, seq):
    stream_len = seq // dil
    nq = stream_len // BLK

    def run(windows):
        _dil_windows(q_src, k_src, vt_ref, o_dst, lse_dst, bias_ref, ot_ref, lt_ref, windows)

    if nq == 1:
        group = math.gcd(DIL_GROUP, dil)

        def stream_group(cg, carry):
            base = pl.multiple_of(cg * (group * stream_len), BLK)
            run([(pl.multiple_of(base + s * stream_len, BLK), BLK) for s in range(group)])
            return carry

        lax.fori_loop(0, dil // group, stream_group, 0)
        return

    def stream(c, carry):
        base = _mult(c * stream_len, BLK)
        n_tail = (nq - 1) % DIL_GROUP
        run([(base, BLK)] + [(_mult(base + (nq - 1 - s) * BLK, BLK), 2 * BLK) for s in range(n_tail)])

        def qgroup(j, carry2):
            rq = pl.multiple_of(base + (1 + DIL_GROUP * j) * BLK, BLK)
            run([(pl.multiple_of(rq + s * BLK, BLK), 2 * BLK) for s in range(DIL_GROUP)])
            return carry2

        n_groups = (nq - 1 - n_tail) // DIL_GROUP
        if n_groups:
            lax.fori_loop(0, n_groups, qgroup, 0)
        return carry

    if dil == 1:
        stream(0, 0)
    else:
        lax.fori_loop(0, dil, stream, 0)


def _head_expand(x, expand):
    hi = x.astype(BF16)
    lo = (x - hi.astype(F32)).astype(BF16)
    return _dot(jnp.concatenate([hi, lo], axis=1), expand)


def _dil_merge(o_src, lse_src, oacc, mrun, lrun, out_ref, first, last, seq):
    er = lax.broadcasted_iota(jnp.int32, (2 * LANES, B_HEADS * B_HEAD_DIM), 0)
    ec = lax.broadcasted_iota(jnp.int32, (2 * LANES, B_HEADS * B_HEAD_DIM), 1)
    expand = jnp.where(lax.shift_right_logical(ec, B_HEAD_DIM.bit_length() - 1) == lax.bitwise_and(er, LANES - 1),
                       jnp.float32(1), jnp.float32(0)).astype(BF16)

    def body(ch, carry):
        r0 = pl.multiple_of(ch * ROW_TM, ROW_TM)
        rows = pl.ds(r0, ROW_TM)
        lse = lse_src[rows, :]
        if first:
            mrun[rows, :] = lse
            lrun[rows, :] = jnp.ones_like(lse)
            oacc[rows, :] = o_src[rows, :].astype(F32)
            return carry
        m_old = mrun[rows, :]
        m_new = jnp.maximum(m_old, lse)
        a = jnp.exp(m_old - m_new)
        bw = jnp.exp(lse - m_new)
        l_new = lrun[rows, :] * a + bw
        if last:
            a = a / l_new
            bw = bw / l_new
        val = oacc[rows, :] * _head_expand(a, expand) + o_src[rows, :].astype(F32) * _head_expand(bw, expand)
        if last:
            out_ref[rows, :] = val.astype(out_ref.dtype)
        else:
            mrun[rows, :] = m_new
            lrun[rows, :] = l_new
            oacc[rows, :] = val
        return carry

    lax.fori_loop(0, seq // ROW_TM, body, 0)


def _dilated_kernel(q_ref, k_ref, v_ref, bias_ref, out_ref,
                    qs, ks, vs, vt, os_, lses, otok, lsetok, ostack, lstack, ot, lt, oacc, mrun, lrun,
                    *, seq):
    g = pl.program_id(1)

    @pl.when(g == 0)
    def _():
        lt[...] = jnp.zeros_like(lt)

    for gi, (_, dil) in enumerate(DILATED_GROUPS):

        @pl.when(g == gi)
        def _(gi=gi, dil=dil):
            first = gi == 0
            last = gi == N_DGROUPS - 1
            if dil == 1:
                _transpose_blocks(v_ref, vt, seq)
                _dil_streams(q_ref, k_ref, vt, otok, lsetok, bias_ref, ot, lt, 1, seq)
            else:
                perm = _perm_matrix(dil, inverse=False).astype(BF16)
                _deinterleave([(q_ref, qs), (k_ref, ks), (v_ref, vs)], perm, dil, seq)
                _transpose_blocks(vs, vt, seq)
                _dil_streams(qs, ks, vt, os_, lses, bias_ref, ot, lt, dil, seq)
                perm_inv = _perm_matrix(dil, inverse=True).astype(BF16)
                _interleave([(os_, otok, ostack), (lses, lsetok, lstack)], perm_inv, dil, seq)
            _dil_merge(otok, lsetok, oacc, mrun, lrun, out_ref, first, last, seq)


def _dilated_call(zg, bias_tiles, *, batch, seq):
    width = B_HEADS * B_HEAD_DIM
    q_col = (GATE_COLS + 3 * A_HEADS * 2 * A_HEAD_DIM) // width
    k_col = q_col + N_DGROUPS
    v_col = k_col + N_DGROUPS
    return pl.pallas_call(
        functools.partial(_dilated_kernel, seq=seq),
        grid=(batch, N_DGROUPS),
        in_specs=[
            pl.BlockSpec((seq, width), lambda b, g: (b, q_col + g)),
            pl.BlockSpec((seq, width), lambda b, g: (b, k_col + g)),
            pl.BlockSpec((seq, width), lambda b, g: (b, v_col + g)),
            pl.BlockSpec((1, B_HEADS, 2 * BLK, BLK), lambda b, g: (g, 0, 0, 0)),
        ],
        out_specs=pl.BlockSpec((seq, width), lambda b, g: (b, 0)),
        out_shape=jax.ShapeDtypeStruct((batch * seq, width), BF16),
        scratch_shapes=[
            pltpu.VMEM((seq, width), BF16), pltpu.VMEM((seq, width), BF16), pltpu.VMEM((seq, width), BF16),
            pltpu.VMEM((seq // BLK, width, BLK), BF16),
            pltpu.VMEM((seq, width), BF16), pltpu.VMEM((seq, LANES), F32),
            pltpu.VMEM((seq, width), BF16), pltpu.VMEM((seq, LANES), F32),
            pltpu.VMEM((PERM_BLK, width), BF16), pltpu.VMEM((PERM_BLK, LANES), F32),
            pltpu.VMEM((DIL_GROUP, width, BLK), F32), pltpu.VMEM((DIL_GROUP, LANES, BLK), F32),
            pltpu.VMEM((seq, width), F32), pltpu.VMEM((seq, LANES), F32), pltpu.VMEM((seq, LANES), F32),
        ],
        compiler_params=_cparams(("parallel", "arbitrary")),
        name="dilated_attn",
    )(zg, zg, zg, bias_tiles)


def _merge_router_kernel(oa_ref, ob_ref, gate_ref, x_ref, pa_ref, pb_ref, wo_ref, g_ref, wr_ref,
                         x1_ref, h2_ref, route_ref, cnt_ref, run_ref):
    tm, d = x_ref.shape

    @pl.when(pl.program_id(0) == 0)
    def _():
        run_ref[...] = jnp.zeros_like(run_ref)

    a = _dot(oa_ref[...], pa_ref[...])
    bm = _dot(ob_ref[...], pb_ref[...])
    merged = gate_ref[:, :d].astype(F32) * a + gate_ref[:, d:].astype(F32) * bm
    x1 = x_ref[...] + _dot(merged.astype(BF16), wo_ref[...])
    x1_ref[...] = x1
    h2 = _rmsnorm_val(x1, g_ref[...], RMS_EPS)
    h2_ref[...] = h2

    h_hi = h2.astype(BF16)
    h_lo = (h2 - h_hi.astype(F32)).astype(BF16)
    lt = (_dot(h_hi, wr_ref[0]) + (_dot(h_hi, wr_ref[1]) + _dot(h_lo, wr_ref[0]))).T
    coarse = [lt[i:i + 1, :] for i in range(N_EXPERT_GROUPS)]
    best = coarse[0]
    gsel = jnp.zeros((1, tm), jnp.int32)
    for i in range(1, N_EXPERT_GROUPS):
        upd = coarse[i] > best
        gsel = jnp.where(upd, i, gsel)
        best = jnp.where(upd, coarse[i], best)
    den = jnp.exp(coarse[0] - best)
    for i in range(1, N_EXPERT_GROUPS):
        den = den + jnp.exp(coarse[i] - best)
    pg = 1.0 / den

    fine = []
    for k in range(EXPERTS_PER_GROUP):
        f = lt[N_EXPERT_GROUPS + k:N_EXPERT_GROUPS + k + 1, :]
        for gi in range(1, N_EXPERT_GROUPS):
            r = N_EXPERT_GROUPS + gi * EXPERTS_PER_GROUP + k
            f = jnp.where(gsel == gi, lt[r:r + 1, :], f)
        fine.append(f)
    v0 = fine[0]
    i0 = jnp.zeros((1, tm), jnp.int32)
    for k in range(1, EXPERTS_PER_GROUP):
        upd = fine[k] > v0
        i0 = jnp.where(upd, k, i0)
        v0 = jnp.where(upd, fine[k], v0)
    v1 = jnp.full((1, tm), -jnp.inf, F32)
    i1 = jnp.zeros((1, tm), jnp.int32)
    for k in range(EXPERTS_PER_GROUP):
        upd = jnp.where(i0 != k, jnp.where(fine[k] > v1, 1, 0), 0) == 1
        i1 = jnp.where(upd, k, i1)
        v1 = jnp.where(upd, fine[k], v1)
    e1w = jnp.exp(v1 - v0)
    gate0 = pg * (1.0 / (1.0 + e1w))
    gate1 = pg * (e1w / (1.0 + e1w))
    e0 = gsel * EXPERTS_PER_GROUP + i0
    e1 = gsel * EXPERTS_PER_GROUP + i1

    eidx = lax.broadcasted_iota(jnp.int32, (N_EXPERTS, tm), 0)
    oh0 = jnp.where(eidx == e0, jnp.float32(1), jnp.float32(0))
    oh1 = jnp.where(eidx == e1, jnp.float32(1), jnp.float32(0))
    ta = lax.broadcasted_iota(jnp.int32, (tm, tm), 0)
    tb = lax.broadcasted_iota(jnp.int32, (tm, tm), 1)
    before = jnp.where(ta < tb, jnp.float32(1), jnp.float32(0)).astype(BF16)
    pre0 = _dot(oh0.astype(BF16), before)
    pre1 = _dot(oh1.astype(BF16), before)
    run = run_ref[:, 0:1]
    tot0 = jnp.sum(oh0, axis=1, keepdims=True)
    tot1 = jnp.sum(oh1, axis=1, keepdims=True)
    rank0 = jnp.sum(oh0 * (run + pre0), axis=0, keepdims=True)
    rank1 = jnp.sum(oh1 * (run + tot0 + pre1), axis=0, keepdims=True)
    new_run = jnp.broadcast_to(run + tot0 + tot1, run_ref.shape)
    run_ref[...] = new_run
    cnt_ref[...] = new_run

    route_ref[0:1, :] = e0.astype(F32)
    route_ref[1:2, :] = e1.astype(F32)
    route_ref[2:3, :] = gate0
    route_ref[3:4, :] = gate1
    route_ref[4:5, :] = rank0
    route_ref[5:6, :] = rank1
    route_ref[6:8, :] = jnp.zeros((2, tm), F32)


def _merge_router_call(oa, ob, zg, x2d, pa, pb, wo, g, wr_t):
    t, d = x2d.shape
    tm = ROW_TM
    const = dict(pipeline_mode=pl.Buffered(1))
    return pl.pallas_call(
        _merge_router_kernel,
        grid=(t // tm,),
        in_specs=[
            pl.BlockSpec((tm, oa.shape[1]), lambda i: (i, 0)),
            pl.BlockSpec((tm, ob.shape[1]), lambda i: (i, 0)),
            pl.BlockSpec((tm, GATE_COLS), lambda i: (i, 0)),
            pl.BlockSpec((tm, d), lambda i: (i, 0)),
            pl.BlockSpec(pa.shape, lambda i: (0, 0), **const),
            pl.BlockSpec(pb.shape, lambda i: (0, 0), **const),
            pl.BlockSpec(wo.shape, lambda i: (0, 0), **const),
            pl.BlockSpec((1, d), lambda i: (0, 0)),
            pl.BlockSpec(wr_t.shape, lambda i: (0, 0, 0), **const),
        ],
        out_specs=[
            pl.BlockSpec((tm, d), lambda i: (i, 0)),
            pl.BlockSpec((tm, d), lambda i: (i, 0)),
            pl.BlockSpec((SUBLANES, tm), lambda i: (0, i)),
            pl.BlockSpec((N_EXPERTS, LANES), lambda i: (0, 0)),
        ],
        out_shape=[
            jax.ShapeDtypeStruct((t, d), F32),
            jax.ShapeDtypeStruct((t, d), F32),
            jax.ShapeDtypeStruct((SUBLANES, t), F32),
            jax.ShapeDtypeStruct((N_EXPERTS, LANES), F32),
        ],
        scratch_shapes=[pltpu.VMEM((N_EXPERTS, LANES), F32)],
        compiler_params=_cparams(("arbitrary",)),
        name="merge_router",
    )(oa, ob, zg, x2d, pa, pb, wo, g, wr_t)


def _row_copy(src, s_row, dst, d_row, sem):
    return pltpu.make_async_copy(src.at[pl.ds(s_row, 1)], dst.at[pl.ds(d_row, 1)], sem)


def _dispatch_kernel(dest_ref, zstart_ref, zlen_ref, h_ref, x_hbm, zero_ref, sem, zsem):
    step = pl.program_id(0)

    def issue(t, carry):
        for k in range(TOP_K):
            _row_copy(h_ref, t, x_hbm, dest_ref[0, k, t], sem.at[k]).start()
        return carry

    lax.fori_loop(0, DISPATCH_TOK, issue, 0, unroll=8)

    @pl.when(step == 0)
    def _():
        zero_ref[...] = jnp.zeros_like(zero_ref)

        def group_copy(r8):
            return pltpu.make_async_copy(
                zero_ref, x_hbm.at[pl.ds(pl.multiple_of(r8, SUBLANES), SUBLANES)], zsem.at[1])

        def segment(e, wait):
            start = zstart_ref[e]
            n_head = jnp.minimum(lax.bitwise_and(-start, SUBLANES - 1), zlen_ref[e])
            n_group = lax.shift_right_logical(zlen_ref[e] - n_head, SUBLANES.bit_length() - 1)

            def head(r, carry):
                cp = _row_copy(zero_ref, 0, x_hbm, 0 if wait else start + r, zsem.at[0])
                cp.wait() if wait else cp.start()
                return carry

            def group(j, carry):
                cp = group_copy(0 if wait else start + n_head + SUBLANES * j)
                cp.wait() if wait else cp.start()
                return carry

            lax.fori_loop(0, n_head, head, 0)
            lax.fori_loop(0, n_group, group, 0)

        lax.fori_loop(0, N_EXPERTS + 1, lambda e, c: (segment(e, False), c)[1], 0)
        lax.fori_loop(0, N_EXPERTS + 1, lambda e, c: (segment(e, True), c)[1], 0)

    for k in range(TOP_K):
        pltpu.make_async_copy(h_ref, x_hbm.at[pl.ds(0, DISPATCH_TOK)], sem.at[k]).wait()


def _dispatch_call(dest_blocks, zstart, zlen, h2, n_rows):
    t, d = h2.shape
    return pl.pallas_call(
        _dispatch_kernel,
        grid=(t // DISPATCH_TOK,),
        in_specs=[
            pl.BlockSpec((1, TOP_K, DISPATCH_TOK), lambda i: (i, 0, 0), memory_space=pltpu.SMEM),
            pl.BlockSpec(memory_space=pltpu.SMEM),
            pl.BlockSpec(memory_space=pltpu.SMEM),
            pl.BlockSpec((DISPATCH_TOK, d), lambda i: (i, 0)),
        ],
        out_specs=pl.BlockSpec(memory_space=pl.ANY),
        out_shape=jax.ShapeDtypeStruct((n_rows, d), F32),
        scratch_shapes=[pltpu.VMEM((SUBLANES, d), F32), pltpu.SemaphoreType.DMA((TOP_K,)),
                        pltpu.SemaphoreType.DMA((2,))],
        compiler_params=_cparams(("arbitrary",)),
        name="dispatch",
    )(dest_blocks, zstart, zlen, h2)


def _expert_kernel(blk_e_ref, blk_valid_ref, x_ref, w1_ref, w3_ref, w2_ref, y_ref):
    i = pl.program_id(0)

    @pl.when(blk_valid_ref[i] == 1)
    def _():
        x = x_ref[...].astype(BF16)
        a = _dot(x, w1_ref[0])
        b = _dot(x, w3_ref[0])
        hdn = (a * _sigmoid(a)) * b
        y_ref[...] = _dot(hdn.astype(BF16), w2_ref[0])

    @pl.when(blk_valid_ref[i] == 0)
    def _():
        y_ref[...] = jnp.zeros_like(y_ref)


def _expert_call(blk_e, blk_valid, xin, w1, w3, w2):
    n_rows, d = xin.shape
    de = w1.shape[2]
    grid_spec = pltpu.PrefetchScalarGridSpec(
        num_scalar_prefetch=2,
        grid=(n_rows // EXP_TM,),
        in_specs=[
            pl.BlockSpec((EXP_TM, d), lambda i, be, bv: (i, 0)),
            pl.BlockSpec((1, d, de), lambda i, be, bv: (be[i], 0, 0)),
            pl.BlockSpec((1, d, de), lambda i, be, bv: (be[i], 0, 0)),
            pl.BlockSpec((1, de, d), lambda i, be, bv: (be[i], 0, 0)),
        ],
        out_specs=pl.BlockSpec((EXP_TM, d), lambda i, be, bv: (i, 0)),
    )
    return pl.pallas_call(
        _expert_kernel,
        grid_spec=grid_spec,
        out_shape=jax.ShapeDtypeStruct((n_rows, d), F32),
        compiler_params=_cparams(("arbitrary",)),
        name="experts",
    )(blk_e, blk_valid, xin, w1, w3, w2)


def _combine_kernel(dest_ref, dnext_ref, y_hbm, x1_ref, gt_ref, p_ref, gple_ref, wg_ref, wp_ref, gfin_ref,
                    o_ref, ybuf, sem, *, final_norm, n_steps):
    tm = x1_ref.shape[0]
    i = pl.program_id(0)
    slot = lax.bitwise_and(i, 1)

    def gather(d_ref, s):
        def issue(t, carry):
            for k in range(TOP_K):
                _row_copy(y_hbm, d_ref[0, k, t], ybuf.at[s, k], t, sem.at[s, k]).start()
            return carry

        lax.fori_loop(0, tm, issue, 0, unroll=8)

    @pl.when(i == 0)
    def _():
        gather(dest_ref, 0)

    @pl.when(i + 1 < n_steps)
    def _():
        gather(dnext_ref, 1 - slot)

    for k in range(TOP_K):
        pltpu.make_async_copy(y_hbm.at[pl.ds(0, tm)], ybuf.at[slot, k], sem.at[slot, k]).wait()

    for r0 in range(0, tm, ROW_TM):
        rows = slice(r0, r0 + ROW_TM)
        pp = _dot(p_ref[rows, :].astype(BF16), wp_ref[...])
        y = ybuf[slot, 0, rows, :] * gt_ref[rows, 2:3] + ybuf[slot, 1, rows, :] * gt_ref[rows, 3:4]
        x2 = x1_ref[rows, :] + y
        hn = _rmsnorm_val(x2, gple_ref[...], RMS_EPS)
        gate = _sigmoid(_dot(hn.astype(BF16), wg_ref[...]))
        x3 = x2 + gate * pp
        if final_norm:
            o_ref[rows, :] = _rmsnorm_val(x3, gfin_ref[...], RMS_EPS)
        else:
            o_ref[rows, :] = x3


def _combine_call(dest_blocks, yb, x1, gates_t, p2d, g_ple, w_gate, w_proj, g_fin, final_norm):
    t, d = x1.shape
    tm = COMBINE_TM
    const = dict(pipeline_mode=pl.Buffered(1))
    return pl.pallas_call(
        functools.partial(_combine_kernel, final_norm=final_norm, n_steps=t // tm),
        grid=(t // tm,),
        in_specs=[
            pl.BlockSpec((1, TOP_K, tm), lambda i: (i, 0, 0), memory_space=pltpu.SMEM),
            pl.BlockSpec((1, TOP_K, tm), lambda i: (jnp.minimum(i + 1, t // tm - 1), 0, 0),
                         memory_space=pltpu.SMEM),
            pl.BlockSpec(memory_space=pl.ANY),
            pl.BlockSpec((tm, d), lambda i: (i, 0)),
            pl.BlockSpec((tm, SUBLANES), lambda i: (i, 0)),
            pl.BlockSpec((tm, p2d.shape[1]), lambda i: (i, 0)),
            pl.BlockSpec((1, d), lambda i: (0, 0)),
            pl.BlockSpec(w_gate.shape, lambda i: (0, 0), **const),
            pl.BlockSpec(w_proj.shape, lambda i: (0, 0), **const),
            pl.BlockSpec((1, d), lambda i: (0, 0)),
        ],
        out_specs=pl.BlockSpec((tm, d), lambda i: (i, 0)),
        out_shape=jax.ShapeDtypeStruct((t, d), F32),
        scratch_shapes=[pltpu.VMEM((2, TOP_K, tm, d), F32), pltpu.SemaphoreType.DMA((2, TOP_K))],
        compiler_params=_cparams(("arbitrary",)),
        name="combine",
    )(dest_blocks, dest_blocks, yb, x1, gates_t, p2d, g_ple, w_gate, w_proj, g_fin)


def _routing_plan(route, counts_f, t):
    counts = counts_f[:, 0].astype(jnp.int32)
    pcounts = ((counts + EXP_TM - 1) // EXP_TM) * EXP_TM
    pends = jnp.cumsum(pcounts)
    pstarts = pends - pcounts
    eid = route[0:2].astype(jnp.int32)
    rank = route[4:6].astype(jnp.int32)
    onehot = eid[..., None] == jnp.arange(N_EXPERTS, dtype=jnp.int32)
    dest = jnp.sum(jnp.where(onehot, pstarts, 0), axis=-1) + rank
    n_rows = TOP_K * t + N_EXPERTS * EXP_TM
    n_blocks = n_rows // EXP_TM
    blk_row = jnp.arange(n_blocks, dtype=jnp.int32) * EXP_TM
    blk_e = jnp.sum((pends[None, :] <= blk_row[:, None]).astype(jnp.int32), axis=1)
    blk_valid = (blk_row < pends[-1]).astype(jnp.int32)
    last_e = jnp.sum((pends <= pends[-1] - 1).astype(jnp.int32))
    blk_e = jnp.minimum(blk_e, last_e)
    zstart = jnp.concatenate([pstarts + counts, pends[-1:]]).astype(jnp.int32)
    zlen = jnp.concatenate([pcounts - counts, n_rows - pends[-1:]]).astype(jnp.int32)
    return dest, blk_e, blk_valid, zstart, zlen, n_rows


def _blocked(dest, tok_per_block):
    t = dest.shape[1]
    return dest.reshape(TOP_K, t // tok_per_block, tok_per_block).transpose(1, 0, 2)


def kernel(x, p, rel_bias, norm_mix_g, w_in, w_gate, lambda_q1, lambda_k1, lambda_q2, lambda_k2, subln_g,
           w_proj_a, w_proj_b, w_out, norm_ffn_g, w_coarse, w_fine, w1, w3, w2, norm_ple_g, w_ple_gate,
           w_ple_proj, final_norm_g):
    batch, seq, d = x.shape
    depth = w_in.shape[0]
    t = batch * seq
    assert seq % PERM_BLK == 0 and t % min(PROJ_TM, t) == 0
    assert all(seq // dil >= BLK and win // dil == BLK for win, dil in DILATED_GROUPS)

    nq = seq // ATT_BLK
    bias_a, w_cat0 = _bias_tiles_call(rel_bias, n_heads=A_HEADS, n_off=nq, tq=ATT_BLK, tk=ATT_BLK,
                                      off_mult=ATT_BLK, off_add=0, dil=1, max_rel=seq, head0=0, name="bias_diff",
                                      key_major=True, mult=A_HEAD_DIM ** 0.5, concat_cast=(w_gate[0], w_in[0]))
    bias_b = jnp.concatenate([
        _bias_tiles_call(rel_bias, n_heads=B_HEADS, n_off=1, tq=2 * BLK, tk=BLK, off_mult=0, off_add=BLK,
                         dil=dil, max_rel=win // dil, head0=A_HEADS + gi * B_HEADS, name="bias_dil%d" % gi,
                         key_major=True, mult=B_HEAD_DIM ** 0.5)[0].reshape(1, B_HEADS, 2 * BLK, BLK)
        for gi, (win, dil) in enumerate(DILATED_GROUPS)], axis=0)

    x2d = x.reshape(t, d)
    for layer in range(depth):
        lam_init = 0.8 - 0.6 * math.exp(-0.3 * layer)
        lam = _lam_call(lambda_q1[layer:layer + 1], lambda_k1[layer:layer + 1],
                        lambda_q2[layer:layer + 1], lambda_k2[layer:layer + 1], lam_init)
        if layer == 0 and w_cat0 is not None:
            w_cat = w_cat0
        else:
            w_cat = jnp.concatenate([w_gate[layer], w_in[layer]], axis=1).astype(BF16)
        zg = _inproj_call(x2d, norm_mix_g[layer:layer + 1], w_cat)
        n_exp, _, d_exp = w1[layer].shape
        oa, (w1b, w3b, w2b) = _diff_attn_call(
            zg, lam, bias_a, subln_g[layer:layer + 1],
            [w1[layer].reshape(n_exp * d, d_exp), w3[layer].reshape(n_exp * d, d_exp),
             w2[layer].reshape(n_exp * d_exp, d)],
            batch=batch, seq=seq, out_scale=1.0 - lam_init)
        ob = _dilated_call(zg, bias_b, batch=batch, seq=seq)

        wr_t = jnp.concatenate([
            w_coarse[layer].T,
            w_fine[layer].transpose(0, 2, 1).reshape(N_EXPERTS, d),
            jnp.zeros((LANES - N_EXPERT_GROUPS - N_EXPERTS, d), F32)], axis=0)
        wr_t = wr_t.T
        wr_hi = wr_t.astype(BF16)
        wr_t = jnp.stack([wr_hi, (wr_t - wr_hi.astype(F32)).astype(BF16)], axis=0)
        x1, h2, route, counts = _merge_router_call(
            oa, ob, zg, x2d, w_proj_a[layer].astype(BF16), w_proj_b[layer].astype(BF16),
            w_out[layer].astype(BF16), norm_ffn_g[layer:layer + 1], wr_t)

        dest, blk_e, blk_valid, zstart, zlen, n_rows = _routing_plan(route, counts, t)
        xin = _dispatch_call(_blocked(dest, DISPATCH_TOK), zstart, zlen, h2, n_rows)
        yb = _expert_call(blk_e, blk_valid, xin, w1b.reshape(n_exp, d, d_exp), w3b.reshape(n_exp, d, d_exp),
                          w2b.reshape(n_exp, d_exp, d))
        x2d = _combine_call(_blocked(dest, COMBINE_TM), yb, x1, route.T, p[layer].reshape(t, -1),
                            norm_ple_g[layer:layer + 1], w_ple_gate[layer].astype(BF16),
                            w_ple_proj[layer].astype(BF16), final_norm_g.reshape(1, d),
                            final_norm=layer == depth - 1)
    return x2d.reshape(batch, seq, d)
```

```python
import functools
import math

import jax
import jax.numpy as jnp
from jax import lax
from jax.experimental import pallas as pl
from jax.experimental.pallas import tpu as pltpu

F32 = jnp.float32
BF16 = jnp.bfloat16

BLK = 128
NEG_INF = -1e30
RMS_EPS = 1e-6
SUBLN_EPS = 1e-5
N_BUCKETS = 32
MAX_DISTANCE = 2048
A_HEADS = 8
A_HEAD_DIM = 128
DILATED_GROUPS = ((128, 1), (512, 4), (2048, 16))
N_DGROUPS = 3
B_HEADS = 8
B_HEAD_DIM = 64
N_EXPERT_GROUPS = 4
EXPERTS_PER_GROUP = 8
N_EXPERTS = 32
TOP_K = 2

LANES = 128
SUBLANES = 8
VMEM_LIMIT = 56 * 1024 * 1024
ATT_BLK = 256
ATT_HEADS = 2
PERM_BLK = 256
GATE_COLS = 4096
PROJ_TN = 512
PROJ_TM = 2048
ROW_TM = 256
COMBINE_TM = 512
EXP_TM = 256
DISPATCH_TOK = 2048
NORM_CHUNK = 128
CAST_SLAB_BYTES = 1 << 20
DIL_GROUP = 8
TOEPLITZ_LANES = 512


def _t5_thresholds():
    max_exact = N_BUCKETS // 2
    out = []
    for k in range(1, N_BUCKETS - max_exact):
        out.append(int(math.ceil(max_exact * (MAX_DISTANCE / max_exact) ** (k / (N_BUCKETS - max_exact)))))
    return tuple(out)


T5_THRESHOLDS = _t5_thresholds()


def _cparams(sem, vmem=VMEM_LIMIT):
    return pltpu.CompilerParams(dimension_semantics=sem, vmem_limit_bytes=vmem)


def _mult(x, m):
    return x if isinstance(x, int) else pl.multiple_of(x, m)


def _sigmoid(x):
    return 0.5 * jnp.tanh(0.5 * x) + 0.5


def _dot(a, b, **kw):
    return jnp.dot(a, b, preferred_element_type=F32, **kw)


def _dot_nt(a, b, **kw):
    return lax.dot_general(a, b, (((1,), (1,)), ((), ())), preferred_element_type=F32, **kw)


def _lam_kernel(q1_ref, k1_ref, q2_ref, k2_ref, o_ref, *, lam_init):
    s1 = jnp.sum(q1_ref[...] * k1_ref[...], axis=-1, keepdims=True)
    s2 = jnp.sum(q2_ref[...] * k2_ref[...], axis=-1, keepdims=True)
    o_ref[...] = jnp.exp(s1) - jnp.exp(s2) + lam_init


def _lam_call(lq1, lk1, lq2, lk2, lam_init):
    return pl.pallas_call(
        functools.partial(_lam_kernel, lam_init=lam_init),
        out_shape=jax.ShapeDtypeStruct((1, 1), F32),
        name="lam",
    )(lq1, lk1, lq2, lk2)


def _bias_tile_kernel(tab_ref, *rest, tq, tk, off_mult, off_add, dil, max_rel, head0, key_major, mult):
    n_pass = max(len(rest) - 2, 0)
    o_ref = rest[n_pass]
    if n_pass:
        col = 0
        for src in rest[:n_pass]:
            rest[-1][:, col:col + src.shape[1]] = src[...].astype(rest[-1].dtype)
            col += src.shape[1]
    h = pl.program_id(0)
    n = pl.program_id(1)
    assert key_major and tq + tk <= TOEPLITZ_LANES
    m = lax.broadcasted_iota(jnp.int32, (8, TOEPLITZ_LANES), 1)
    rel = jnp.where(m < tk, m, m - TOEPLITZ_LANES) + (n * off_mult + off_add)
    dist = rel * dil
    large = jnp.full(m.shape, N_BUCKETS // 2, jnp.int32)
    for thr in T5_THRESHOLDS:
        large = large + jnp.where(dist >= thr, 1, 0)
    bucket = jnp.where(dist < N_BUCKETS // 2, dist, large)
    acc = jnp.zeros(m.shape, F32)
    for b in range(N_BUCKETS):
        acc = jnp.where(bucket == b, tab_ref[b, head0 + h], acc)
    valid = jnp.where(rel >= 0, jnp.where(rel <= max_rel, 1, 0), 0)
    vec = jnp.where(valid == 1, acc * mult, NEG_INF)[0:1, :]
    rows = pltpu.roll(jnp.broadcast_to(vec, (tq, TOEPLITZ_LANES)), 0, 1, stride=1, stride_axis=0)
    o_ref[0, 0] = rows[:, :tk]


def _bias_tiles_call(rel_bias, *, n_heads, n_off, tq, tk, off_mult, off_add, dil, max_rel, head0, name,
                     key_major=False, mult=1.0, concat_cast=()):
    kern = functools.partial(_bias_tile_kernel, tq=tq, tk=tk, off_mult=off_mult, off_add=off_add,
                             dil=dil, max_rel=max_rel, head0=head0, key_major=key_major, mult=mult)
    n_steps = n_heads * n_off
    if concat_cast and concat_cast[0].shape[0] % (n_steps * 16):
        concat_cast = ()
    in_specs = [pl.BlockSpec(memory_space=pltpu.SMEM)]
    out_specs = [pl.BlockSpec((1, 1, tq, tk), lambda h, n: (h, n, 0, 0))]
    out_shape = [jax.ShapeDtypeStruct((n_heads, n_off, tq, tk), F32)]
    if concat_cast:
        rows = concat_cast[0].shape[0]
        cols = sum(w.shape[1] for w in concat_cast)
        in_specs += [pl.BlockSpec((rows // n_steps, w.shape[1]), lambda h, n: (h * n_off + n, 0))
                     for w in concat_cast]
        out_specs.append(pl.BlockSpec((rows // n_steps, cols), lambda h, n: (h * n_off + n, 0)))
        out_shape.append(jax.ShapeDtypeStruct((rows, cols), BF16))
    outs = pl.pallas_call(
        kern,
        grid=(n_heads, n_off),
        in_specs=in_specs,
        out_specs=out_specs,
        out_shape=out_shape,
        compiler_params=_cparams(("parallel", "parallel")),
        name=name,
    )(rel_bias, *concat_cast)
    return outs[0], (outs[1] if concat_cast else None)


def _rmsnorm_rows(x_ref, g_ref, out_ref, eps):
    rows = x_ref.shape[0]
    g = g_ref[...]

    def body(c, carry):
        r0 = pl.multiple_of(c * NORM_CHUNK, NORM_CHUNK)
        x = x_ref[pl.ds(r0, NORM_CHUNK), :]
        ms = jnp.mean(x * x, axis=-1, keepdims=True)
        out_ref[pl.ds(r0, NORM_CHUNK), :] = ((x * lax.rsqrt(ms + eps)) * g).astype(out_ref.dtype)
        return carry

    lax.fori_loop(0, rows // NORM_CHUNK, body, 0)


def _rmsnorm_val(x, g, eps):
    ms = jnp.mean(x * x, axis=-1, keepdims=True)
    return (x * lax.rsqrt(ms + eps)) * g


def _inproj_kernel(x_ref, g_ref, w_ref, o_ref, h_ref, *, n_gate_blocks):
    j = pl.program_id(1)

    @pl.when(j == 0)
    def _():
        _rmsnorm_rows(x_ref, g_ref, h_ref, RMS_EPS)

    acc = _dot(h_ref[...], w_ref[...])

    @pl.when(j < n_gate_blocks)
    def _():
        o_ref[...] = _sigmoid(acc).astype(o_ref.dtype)

    @pl.when(j >= n_gate_blocks)
    def _():
        o_ref[...] = acc.astype(o_ref.dtype)


def _inproj_call(x2d, g, w_cat):
    t, d = x2d.shape
    n = w_cat.shape[1]
    tm = min(PROJ_TM, t)
    return pl.pallas_call(
        functools.partial(_inproj_kernel, n_gate_blocks=GATE_COLS // PROJ_TN),
        grid=(t // tm, n // PROJ_TN),
        in_specs=[
            pl.BlockSpec((tm, d), lambda i, j: (i, 0)),
            pl.BlockSpec((1, d), lambda i, j: (0, 0)),
            pl.BlockSpec((d, PROJ_TN), lambda i, j: (0, j)),
        ],
        out_specs=pl.BlockSpec((tm, PROJ_TN), lambda i, j: (i, j)),
        out_shape=jax.ShapeDtypeStruct((t, n), BF16),
        scratch_shapes=[pltpu.VMEM((tm, d), BF16)],
        compiler_params=_cparams(("parallel", "arbitrary")),
        name="inproj",
    )(x2d, g, w_cat)


def _diff_attn_kernel(lam_ref, q_ref, k_ref, v_ref, bias_ref, g_ref, *rest, out_scale, n_blk, n_cast):
    cast_in = rest[:n_cast]
    o_ref = rest[n_cast]
    cast_out = rest[n_cast + 1:2 * n_cast + 1]
    vt_ref, tbuf_a, tbuf_b, acc = rest[2 * n_cast + 1:]
    for src, dst in zip(cast_in, cast_out):
        dst[...] = src[...].astype(dst.dtype)
    qi = pl.program_id(2)
    width = 2 * A_HEAD_DIM
    n_map = 2 * ATT_HEADS
    c = (A_HEAD_DIM ** -0.5) * math.log2(math.e)

    @pl.when(qi == 0)
    def _():
        def transpose_block(b, carry):
            r0 = pl.multiple_of(b * ATT_BLK, ATT_BLK)
            for hp in range(ATT_HEADS):
                vb = v_ref[pl.ds(r0, ATT_BLK), hp * width:(hp + 1) * width]
                vt_ref[hp, b] = vb.astype(F32).T.astype(BF16)
            return carry

        lax.fori_loop(0, n_blk, transpose_block, 0)

    acc[...] = jnp.zeros_like(acc)

    q0 = pl.multiple_of(qi * ATT_BLK, ATT_BLK)
    map_cols = [slice(mi * A_HEAD_DIM, (mi + 1) * A_HEAD_DIM) for mi in range(n_map)]
    qs = [q_ref[pl.ds(q0, ATT_BLK), cols] for cols in map_cols]

    def scores(ki, dst, qt=None):
        k0 = pl.multiple_of(ki * ATT_BLK, ATT_BLK)
        for mi, cols in enumerate(map_cols):
            if qt is None:
                q, tile = qs[mi], qi - ki
            else:
                q, tile = q_ref[pl.ds(pl.multiple_of(qt * ATT_BLK, ATT_BLK), ATT_BLK), cols], qt - ki
            dst[mi] = _dot_nt(k_ref[pl.ds(k0, ATT_BLK), cols], q) + bias_ref[mi // 2, tile]

    def softmax(t, m, l):
        m_new = jnp.maximum(m, jnp.max(t, axis=0, keepdims=True))
        alpha = jnp.exp2((m - m_new) * c)
        p = jnp.exp2((t - m_new) * c)
        return m_new, alpha * l + jnp.sum(p, axis=0, keepdims=True), alpha, p.astype(BF16)

    first_buf = lax.bitwise_and(lax.shift_right_logical(qi * (qi + 1), 1), 1)

    @pl.when(qi == 0)
    def _():
        scores(0, tbuf_a)

    def step(ki, carry, cur, nxt, last):
        ms, ls = carry
        stats = [softmax(cur[mi], ms[mi], ls[mi]) for mi in range(n_map)]
        if last:
            scores(0, nxt, qt=jnp.minimum(qi + 1, n_blk - 1))
        else:
            scores(ki + 1, nxt)
        for mi in range(n_map):
            acc[mi] = acc[mi] * stats[mi][2] + _dot(vt_ref[mi // 2, ki], stats[mi][3])
        return tuple(st[0] for st in stats), tuple(st[1] for st in stats)

    def body(ki, carry, last=False):
        return lax.cond(lax.bitwise_and(first_buf + ki, 1) == 0,
                        lambda cr: step(ki, cr, tbuf_a, tbuf_b, last),
                        lambda cr: step(ki, cr, tbuf_b, tbuf_a, last), carry)

    minf = (jnp.full((1, ATT_BLK), -jnp.inf, F32),) * n_map
    zero = (jnp.zeros((1, ATT_BLK), F32),) * n_map
    _, ls = body(qi, lax.fori_loop(0, qi, body, (minf, zero)), last=True)
    for hp in range(ATT_HEADS):
        w = acc[2 * hp] / ls[2 * hp] - lam_ref[0, 0] * (acc[2 * hp + 1] / ls[2 * hp + 1])
        ms = jnp.mean(w * w, axis=0, keepdims=True)
        y = ((w * lax.rsqrt(ms + SUBLN_EPS)) * g_ref[...]) * out_scale
        o_ref[:, hp * width:(hp + 1) * width] = y.T.astype(o_ref.dtype)


def _diff_attn_call(zg, lam, bias_tiles, subln_g, passengers, *, batch, seq, out_scale):
    t = batch * seq
    nq = seq // ATT_BLK
    width = 2 * A_HEAD_DIM
    blk_w = ATT_HEADS * width
    q_col = GATE_COLS // blk_w
    k_col = q_col + A_HEADS // ATT_HEADS
    v_col = k_col + A_HEADS // ATT_HEADS
    n_steps = (A_HEADS // ATT_HEADS) * batch * nq
    riders = [w for w in passengers
              if w.shape[0] % (n_steps * 16) == 0 and w.size * 4 // n_steps <= CAST_SLAB_BYTES]

    def slab_spec(w):
        return pl.BlockSpec((w.shape[0] // n_steps, w.shape[1]), lambda h, b, i: ((h * batch + b) * nq + i, 0))

    outs = pl.pallas_call(
        functools.partial(_diff_attn_kernel, out_scale=out_scale, n_blk=nq, n_cast=len(riders)),
        grid=(A_HEADS // ATT_HEADS, batch, nq),
        in_specs=[
            pl.BlockSpec(memory_space=pltpu.SMEM),
            pl.BlockSpec((seq, blk_w), lambda h, b, i: (b, q_col + h)),
            pl.BlockSpec((seq, blk_w), lambda h, b, i: (b, k_col + h)),
            pl.BlockSpec((seq, blk_w), lambda h, b, i: (b, v_col + h)),
            pl.BlockSpec((ATT_HEADS, nq, ATT_BLK, ATT_BLK), lambda h, b, i: (h, 0, 0, 0)),
            pl.BlockSpec((width, 1), lambda h, b, i: (0, 0)),
        ] + [slab_spec(w) for w in riders],
        out_specs=[pl.BlockSpec((ATT_BLK, blk_w), lambda h, b, i: (b * nq + i, h))]
        + [slab_spec(w) for w in riders],
        out_shape=[jax.ShapeDtypeStruct((t, A_HEADS * width), BF16)]
        + [jax.ShapeDtypeStruct(w.shape, BF16) for w in riders],
        scratch_shapes=[pltpu.VMEM((ATT_HEADS, nq, width, ATT_BLK), BF16),
                        pltpu.VMEM((2 * ATT_HEADS, ATT_BLK, ATT_BLK), F32),
                        pltpu.VMEM((2 * ATT_HEADS, ATT_BLK, ATT_BLK), F32),
                        pltpu.VMEM((2 * ATT_HEADS, width, ATT_BLK), F32)],
        compiler_params=_cparams(("parallel", "parallel", "arbitrary")),
        name="diff_attn",
    )(lam, zg, zg, zg, bias_tiles, subln_g.reshape(width, 1), *riders)
    cast = iter(outs[1:])
    return outs[0], [next(cast) if any(w is r for r in riders) else w.astype(BF16) for w in passengers]


def _perm_matrix(dil, inverse):
    w = PERM_BLK // dil
    shift = w.bit_length() - 1
    a = lax.broadcasted_iota(jnp.int32, (PERM_BLK, PERM_BLK), 0)
    b = lax.broadcasted_iota(jnp.int32, (PERM_BLK, PERM_BLK), 1)
    dst, src = (b, a) if inverse else (a, b)
    c = lax.shift_right_logical(dst, shift)
    ll = lax.bitwise_and(dst, w - 1)
    return jnp.where(src == ll * dil + c, jnp.float32(1), jnp.float32(0))


def _deinterleave(pairs, perm, dil, seq):
    w = PERM_BLK // dil
    stream_len = seq // dil

    def body(b8, carry):
        r0 = pl.multiple_of(b8 * PERM_BLK, PERM_BLK)
        ys = [_dot(perm, src[pl.ds(r0, PERM_BLK), :]).astype(dst.dtype) for src, dst in pairs]
        for y, (_, dst) in zip(ys, pairs):
            for c in range(dil):
                d0 = pl.multiple_of(c * stream_len + b8 * w, w)
                dst[pl.ds(d0, w), :] = y[c * w:(c + 1) * w, :]
        return carry

    lax.fori_loop(0, seq // PERM_BLK, body, 0)


def _permute_rows(perm, x):
    if x.dtype == BF16:
        return _dot(perm, x)
    hi = x.astype(BF16)
    r1 = x - hi.astype(F32)
    mid = r1.astype(BF16)
    lo = (r1 - mid.astype(F32)).astype(BF16)
    y = _dot(perm, jnp.concatenate([hi, mid, lo], axis=1))
    n = x.shape[1]
    return y[:, :n] + (y[:, n:2 * n] + y[:, 2 * n:])


def _interleave(items, perm_inv, dil, seq):
    w = PERM_BLK // dil
    stream_len = seq // dil

    def body(b8, carry):
        for src, _, stack in items:
            for c in range(dil):
                s0 = pl.multiple_of(c * stream_len + b8 * w, w)
                stack[c * w:(c + 1) * w, :] = src[pl.ds(s0, w), :]
        outs = [_permute_rows(perm_inv, stack[...]) for _, _, stack in items]
        r0 = pl.multiple_of(b8 * PERM_BLK, PERM_BLK)
        for out, (_, dst, _) in zip(outs, items):
            dst[pl.ds(r0, PERM_BLK), :] = out.astype(dst.dtype)
        return carry

    lax.fori_loop(0, seq // PERM_BLK, body, 0)


def _transpose_blocks(src_ref, vt_ref, seq):
    def body(blk, carry):
        r0 = pl.multiple_of(blk * BLK, BLK)
        vt_ref[blk] = src_ref[pl.ds(r0, BLK), :].astype(F32).T.astype(vt_ref.dtype)
        return carry

    lax.fori_loop(0, seq // BLK, body, 0)


def _dil_windows(q_src, k_src, vt_ref, o_dst, lse_dst, bias_ref, ot_ref, lt_ref, windows):
    scale = B_HEAD_DIM ** -0.5
    c = scale * math.log2(math.e)
    tiles = []
    for rq, nk in windows:
        rk = rq - (nk - BLK)
        for hh in range(B_HEADS):
            cols = slice(hh * B_HEAD_DIM, (hh + 1) * B_HEAD_DIM)
            qh = q_src[pl.ds(rq, BLK), cols]
            kh = k_src[pl.ds(rk, nk), cols]
            tiles.append(_dot_nt(kh, qh) + bias_ref[0, hh, 2 * BLK - nk:, :])
    for wi, (rq, nk) in enumerate(windows):
        qb = rq // BLK if isinstance(rq, int) else lax.shift_right_logical(rq, BLK.bit_length() - 1)
        for hh in range(B_HEADS):
            cols = slice(hh * B_HEAD_DIM, (hh + 1) * B_HEAD_DIM)
            t = tiles[wi * B_HEADS + hh]
            m = jnp.max(t, axis=0, keepdims=True)
            p = jnp.exp2((t - m) * c)
            den = jnp.sum(p, axis=0, keepdims=True)
            if nk == BLK:
                vth = vt_ref[qb, cols, :]
            else:
                vth = jnp.concatenate([vt_ref[qb - 1, cols, :], vt_ref[qb, cols, :]], axis=1)
            ot_ref[wi, cols, :] = _dot(vth, p.astype(BF16)) / den
            lt_ref[wi, hh:hh + 1, :] = m * scale + jnp.log(den)
        o_dst[pl.ds(rq, BLK), :] = ot_ref[wi].T.astype(o_dst.dtype)
        lse_dst[pl.ds(rq, BLK), :] = lt_ref[wi].T


def _dil_streams(q_src, k_src, vt_ref, o_dst, lse_dst, bias_ref, ot_ref, lt_ref, dil, seq):
    stream_len = seq // dil
    nq = stream_len // BLK

    def run(windows):
        _dil_windows(q_src, k_src, vt_ref, o_dst, lse_dst, bias_ref, ot_ref, lt_ref, windows)

    if nq == 1:
        group = math.gcd(DIL_GROUP, dil)

        def stream_group(cg, carry):
            base = pl.multiple_of(cg * (group * stream_len), BLK)
            run([(pl.multiple_of(base + s * stream_len, BLK), BLK) for s in range(group)])
            return carry

        lax.fori_loop(0, dil // group, stream_group, 0)
        return

    def stream(c, carry):
        base = _mult(c * stream_len, BLK)
        n_tail = (nq - 1) % DIL_GROUP
        run([(base, BLK)] + [(_mult(base + (nq - 1 - s) * BLK, BLK), 2 * BLK) for s in range(n_tail)])

        def qgroup(j, carry2):
            rq = pl.multiple_of(base + (1 + DIL_GROUP * j) * BLK, BLK)
            run([(pl.multiple_of(rq + s * BLK, BLK), 2 * BLK) for s in range(DIL_GROUP)])
            return carry2

        n_groups = (nq - 1 - n_tail) // DIL_GROUP
        if n_groups:
            lax.fori_loop(0, n_groups, qgroup, 0)
        return carry

    if dil == 1:
        stream(0, 0)
    else:
        lax.fori_loop(0, dil, stream, 0)


def _head_expand(x, expand):
    hi = x.astype(BF16)
    lo = (x - hi.astype(F32)).astype(BF16)
    return _dot(jnp.concatenate([hi, lo], axis=1), expand)


def _dil_merge(o_src, lse_src, oacc, mrun, lrun, out_ref, first, last, seq):
    er = lax.broadcasted_iota(jnp.int32, (2 * LANES, B_HEADS * B_HEAD_DIM), 0)
    ec = lax.broadcasted_iota(jnp.int32, (2 * LANES, B_HEADS * B_HEAD_DIM), 1)
    expand = jnp.where(lax.shift_right_logical(ec, B_HEAD_DIM.bit_length() - 1) == lax.bitwise_and(er, LANES - 1),
                       jnp.float32(1), jnp.float32(0)).astype(BF16)

    def body(ch, carry):
        r0 = pl.multiple_of(ch * ROW_TM, ROW_TM)
        rows = pl.ds(r0, ROW_TM)
        lse = lse_src[rows, :]
        if first:
            mrun[rows, :] = lse
            lrun[rows, :] = jnp.ones_like(lse)
            oacc[rows, :] = o_src[rows, :].astype(F32)
            return carry
        m_old = mrun[rows, :]
        m_new = jnp.maximum(m_old, lse)
        a = jnp.exp(m_old - m_new)
        bw = jnp.exp(lse - m_new)
        l_new = lrun[rows, :] * a + bw
        if last:
            a = a / l_new
            bw = bw / l_new
        val = oacc[rows, :] * _head_expand(a, expand) + o_src[rows, :].astype(F32) * _head_expand(bw, expand)
        if last:
            out_ref[rows, :] = val.astype(out_ref.dtype)
        else:
            mrun[rows, :] = m_new
            lrun[rows, :] = l_new
            oacc[rows, :] = val
        return carry

    lax.fori_loop(0, seq // ROW_TM, body, 0)


def _dilated_kernel(q_ref, k_ref, v_ref, bias_ref, out_ref,
                    qs, ks, vs, vt, os_, lses, otok, lsetok, ostack, lstack, ot, lt, oacc, mrun, lrun,
                    *, seq):
    g = pl.program_id(1)

    @pl.when(g == 0)
    def _():
        lt[...] = jnp.zeros_like(lt)

    for gi, (_, dil) in enumerate(DILATED_GROUPS):

        @pl.when(g == gi)
        def _(gi=gi, dil=dil):
            first = gi == 0
            last = gi == N_DGROUPS - 1
            if dil == 1:
                _transpose_blocks(v_ref, vt, seq)
                _dil_streams(q_ref, k_ref, vt, otok, lsetok, bias_ref, ot, lt, 1, seq)
            else:
                perm = _perm_matrix(dil, inverse=False).astype(BF16)
                _deinterleave([(q_ref, qs), (k_ref, ks), (v_ref, vs)], perm, dil, seq)
                _transpose_blocks(vs, vt, seq)
                _dil_streams(qs, ks, vt, os_, lses, bias_ref, ot, lt, dil, seq)
                perm_inv = _perm_matrix(dil, inverse=True).astype(BF16)
                _interleave([(os_, otok, ostack), (lses, lsetok, lstack)], perm_inv, dil, seq)
            _dil_merge(otok, lsetok, oacc, mrun, lrun, out_ref, first, last, seq)


def _dilated_call(zg, bias_tiles, *, batch, seq):
    width = B_HEADS * B_HEAD_DIM
    q_col = (GATE_COLS + 3 * A_HEADS * 2 * A_HEAD_DIM) // width
    k_col = q_col + N_DGROUPS
    v_col = k_col + N_DGROUPS
    return pl.pallas_call(
        functools.partial(_dilated_kernel, seq=seq),
        grid=(batch, N_DGROUPS),
        in_specs=[
            pl.BlockSpec((seq, width), lambda b, g: (b, q_col + g)),
            pl.BlockSpec((seq, width), lambda b, g: (b, k_col + g)),
            pl.BlockSpec((seq, width), lambda b, g: (b, v_col + g)),
            pl.BlockSpec((1, B_HEADS, 2 * BLK, BLK), lambda b, g: (g, 0, 0, 0)),
        ],
        out_specs=pl.BlockSpec((seq, width), lambda b, g: (b, 0)),
        out_shape=jax.ShapeDtypeStruct((batch * seq, width), BF16),
        scratch_shapes=[
            pltpu.VMEM((seq, width), BF16), pltpu.VMEM((seq, width), BF16), pltpu.VMEM((seq, width), BF16),
            pltpu.VMEM((seq // BLK, width, BLK), BF16),
            pltpu.VMEM((seq, width), BF16), pltpu.VMEM((seq, LANES), F32),
            pltpu.VMEM((seq, width), BF16), pltpu.VMEM((seq, LANES), F32),
            pltpu.VMEM((PERM_BLK, width), BF16), pltpu.VMEM((PERM_BLK, LANES), F32),
            pltpu.VMEM((DIL_GROUP, width, BLK), F32), pltpu.VMEM((DIL_GROUP, LANES, BLK), F32),
            pltpu.VMEM((seq, width), F32), pltpu.VMEM((seq, LANES), F32), pltpu.VMEM((seq, LANES), F32),
        ],
        compiler_params=_cparams(("parallel", "arbitrary")),
        name="dilated_attn",
    )(zg, zg, zg, bias_tiles)


def _merge_router_kernel(oa_ref, ob_ref, gate_ref, x_ref, pa_ref, pb_ref, wo_ref, g_ref, wr_ref,
                         x1_ref, h2_ref, route_ref, cnt_ref, run_ref):
    tm, d = x_ref.shape

    @pl.when(pl.program_id(0) == 0)
    def _():
        run_ref[...] = jnp.zeros_like(run_ref)

    a = _dot(oa_ref[...], pa_ref[...])
    bm = _dot(ob_ref[...], pb_ref[...])
    merged = gate_ref[:, :d].astype(F32) * a + gate_ref[:, d:].astype(F32) * bm
    x1 = x_ref[...] + _dot(merged.astype(BF16), wo_ref[...])
    x1_ref[...] = x1
    h2 = _rmsnorm_val(x1, g_ref[...], RMS_EPS)
    h2_ref[...] = h2

    h_hi = h2.astype(BF16)
    h_lo = (h2 - h_hi.astype(F32)).astype(BF16)
    lt = (_dot(h_hi, wr_ref[0]) + (_dot(h_hi, wr_ref[1]) + _dot(h_lo, wr_ref[0]))).T
    coarse = [lt[i:i + 1, :] for i in range(N_EXPERT_GROUPS)]
    best = coarse[0]
    gsel = jnp.zeros((1, tm), jnp.int32)
    for i in range(1, N_EXPERT_GROUPS):
        upd = coarse[i] > best
        gsel = jnp.where(upd, i, gsel)
        best = jnp.where(upd, coarse[i], best)
    den = jnp.exp(coarse[0] - best)
    for i in range(1, N_EXPERT_GROUPS):
        den = den + jnp.exp(coarse[i] - best)
    pg = 1.0 / den

    fine = []
    for k in range(EXPERTS_PER_GROUP):
        f = lt[N_EXPERT_GROUPS + k:N_EXPERT_GROUPS + k + 1, :]
        for gi in range(1, N_EXPERT_GROUPS):
            r = N_EXPERT_GROUPS + gi * EXPERTS_PER_GROUP + k
            f = jnp.where(gsel == gi, lt[r:r + 1, :], f)
        fine.append(f)
    v0 = fine[0]
    i0 = jnp.zeros((1, tm), jnp.int32)
    for k in range(1, EXPERTS_PER_GROUP):
        upd = fine[k] > v0
        i0 = jnp.where(upd, k, i0)
        v0 = jnp.where(upd, fine[k], v0)
    v1 = jnp.full((1, tm), -jnp.inf, F32)
    i1 = jnp.zeros((1, tm), jnp.int32)
    for k in range(EXPERTS_PER_GROUP):
        upd = jnp.where(i0 != k, jnp.where(fine[k] > v1, 1, 0), 0) == 1
        i1 = jnp.where(upd, k, i1)
        v1 = jnp.where(upd, fine[k], v1)
    e1w = jnp.exp(v1 - v0)
    gate0 = pg * (1.0 / (1.0 + e1w))
    gate1 = pg * (e1w / (1.0 + e1w))
    e0 = gsel * EXPERTS_PER_GROUP + i0
    e1 = gsel * EXPERTS_PER_GROUP + i1

    eidx = lax.broadcasted_iota(jnp.int32, (N_EXPERTS, tm), 0)
    oh0 = jnp.where(eidx == e0, jnp.float32(1), jnp.float32(0))
    oh1 = jnp.where(eidx == e1, jnp.float32(1), jnp.float32(0))
    ta = lax.broadcasted_iota(jnp.int32, (tm, tm), 0)
    tb = lax.broadcasted_iota(jnp.int32, (tm, tm), 1)
    before = jnp.where(ta < tb, jnp.float32(1), jnp.float32(0)).astype(BF16)
    pre0 = _dot(oh0.astype(BF16), before)
    pre1 = _dot(oh1.astype(BF16), before)
    run = run_ref[:, 0:1]
    tot0 = jnp.sum(oh0, axis=1, keepdims=True)
    tot1 = jnp.sum(oh1, axis=1, keepdims=True)
    rank0 = jnp.sum(oh0 * (run + pre0), axis=0, keepdims=True)
    rank1 = jnp.sum(oh1 * (run + tot0 + pre1), axis=0, keepdims=True)
    new_run = jnp.broadcast_to(run + tot0 + tot1, run_ref.shape)
    run_ref[...] = new_run
    cnt_ref[...] = new_run

    route_ref[0:1, :] = e0.astype(F32)
    route_ref[1:2, :] = e1.astype(F32)
    route_ref[2:3, :] = gate0
    route_ref[3:4, :] = gate1
    route_ref[4:5, :] = rank0
    route_ref[5:6, :] = rank1
    route_ref[6:8, :] = jnp.zeros((2, tm), F32)


def _merge_router_call(oa, ob, zg, x2d, pa, pb, wo, g, wr_t):
    t, d = x2d.shape
    tm = ROW_TM
    const = dict(pipeline_mode=pl.Buffered(1))
    return pl.pallas_call(
        _merge_router_kernel,
        grid=(t // tm,),
        in_specs=[
            pl.BlockSpec((tm, oa.shape[1]), lambda i: (i, 0)),
            pl.BlockSpec((tm, ob.shape[1]), lambda i: (i, 0)),
            pl.BlockSpec((tm, GATE_COLS), lambda i: (i, 0)),
            pl.BlockSpec((tm, d), lambda i: (i, 0)),
            pl.BlockSpec(pa.shape, lambda i: (0, 0), **const),
            pl.BlockSpec(pb.shape, lambda i: (0, 0), **const),
            pl.BlockSpec(wo.shape, lambda i: (0, 0), **const),
            pl.BlockSpec((1, d), lambda i: (0, 0)),
            pl.BlockSpec(wr_t.shape, lambda i: (0, 0, 0), **const),
        ],
        out_specs=[
            pl.BlockSpec((tm, d), lambda i: (i, 0)),
            pl.BlockSpec((tm, d), lambda i: (i, 0)),
            pl.BlockSpec((SUBLANES, tm), lambda i: (0, i)),
            pl.BlockSpec((N_EXPERTS, LANES), lambda i: (0, 0)),
        ],
        out_shape=[
            jax.ShapeDtypeStruct((t, d), F32),
            jax.ShapeDtypeStruct((t, d), F32),
            jax.ShapeDtypeStruct((SUBLANES, t), F32),
            jax.ShapeDtypeStruct((N_EXPERTS, LANES), F32),
        ],
        scratch_shapes=[pltpu.VMEM((N_EXPERTS, LANES), F32)],
        compiler_params=_cparams(("arbitrary",)),
        name="merge_router",
    )(oa, ob, zg, x2d, pa, pb, wo, g, wr_t)


def _row_copy(src, s_row, dst, d_row, sem):
    return pltpu.make_async_copy(src.at[pl.ds(s_row, 1)], dst.at[pl.ds(d_row, 1)], sem)


def _dispatch_kernel(dest_ref, zstart_ref, zlen_ref, h_ref, x_hbm, zero_ref, sem, zsem):
    step = pl.program_id(0)

    def issue(t, carry):
        for k in range(TOP_K):
            _row_copy(h_ref, t, x_hbm, dest_ref[0, k, t], sem.at[k]).start()
        return carry

    lax.fori_loop(0, DISPATCH_TOK, issue, 0, unroll=8)

    @pl.when(step == 0)
    def _():
        zero_ref[...] = jnp.zeros_like(zero_ref)

        def group_copy(r8):
            return pltpu.make_async_copy(
                zero_ref, x_hbm.at[pl.ds(pl.multiple_of(r8, SUBLANES), SUBLANES)], zsem.at[1])

        def segment(e, wait):
            start = zstart_ref[e]
            n_head = jnp.minimum(lax.bitwise_and(-start, SUBLANES - 1), zlen_ref[e])
            n_group = lax.shift_right_logical(zlen_ref[e] - n_head, SUBLANES.bit_length() - 1)

            def head(r, carry):
                cp = _row_copy(zero_ref, 0, x_hbm, 0 if wait else start + r, zsem.at[0])
                cp.wait() if wait else cp.start()
                return carry

            def group(j, carry):
                cp = group_copy(0 if wait else start + n_head + SUBLANES * j)
                cp.wait() if wait else cp.start()
                return carry

            lax.fori_loop(0, n_head, head, 0)
            lax.fori_loop(0, n_group, group, 0)

        lax.fori_loop(0, N_EXPERTS + 1, lambda e, c: (segment(e, False), c)[1], 0)
        lax.fori_loop(0, N_EXPERTS + 1, lambda e, c: (segment(e, True), c)[1], 0)

    for k in range(TOP_K):
        pltpu.make_async_copy(h_ref, x_hbm.at[pl.ds(0, DISPATCH_TOK)], sem.at[k]).wait()


def _dispatch_call(dest_blocks, zstart, zlen, h2, n_rows):
    t, d = h2.shape
    return pl.pallas_call(
        _dispatch_kernel,
        grid=(t // DISPATCH_TOK,),
        in_specs=[
            pl.BlockSpec((1, TOP_K, DISPATCH_TOK), lambda i: (i, 0, 0), memory_space=pltpu.SMEM),
            pl.BlockSpec(memory_space=pltpu.SMEM),
            pl.BlockSpec(memory_space=pltpu.SMEM),
            pl.BlockSpec((DISPATCH_TOK, d), lambda i: (i, 0)),
        ],
        out_specs=pl.BlockSpec(memory_space=pl.ANY),
        out_shape=jax.ShapeDtypeStruct((n_rows, d), F32),
        scratch_shapes=[pltpu.VMEM((SUBLANES, d), F32), pltpu.SemaphoreType.DMA((TOP_K,)),
                        pltpu.SemaphoreType.DMA((2,))],
        compiler_params=_cparams(("arbitrary",)),
        name="dispatch",
    )(dest_blocks, zstart, zlen, h2)


def _expert_kernel(blk_e_ref, blk_valid_ref, x_ref, w1_ref, w3_ref, w2_ref, y_ref):
    i = pl.program_id(0)

    @pl.when(blk_valid_ref[i] == 1)
    def _():
        x = x_ref[...].astype(BF16)
        a = _dot(x, w1_ref[0])
        b = _dot(x, w3_ref[0])
        hdn = (a * _sigmoid(a)) * b
        y_ref[...] = _dot(hdn.astype(BF16), w2_ref[0])

    @pl.when(blk_valid_ref[i] == 0)
    def _():
        y_ref[...] = jnp.zeros_like(y_ref)


def _expert_call(blk_e, blk_valid, xin, w1, w3, w2):
    n_rows, d = xin.shape
    de = w1.shape[2]
    grid_spec = pltpu.PrefetchScalarGridSpec(
        num_scalar_prefetch=2,
        grid=(n_rows // EXP_TM,),
        in_specs=[
            pl.BlockSpec((EXP_TM, d), lambda i, be, bv: (i, 0)),
            pl.BlockSpec((1, d, de), lambda i, be, bv: (be[i], 0, 0)),
            pl.BlockSpec((1, d, de), lambda i, be, bv: (be[i], 0, 0)),
            pl.BlockSpec((1, de, d), lambda i, be, bv: (be[i], 0, 0)),
        ],
        out_specs=pl.BlockSpec((EXP_TM, d), lambda i, be, bv: (i, 0)),
    )
    return pl.pallas_call(
        _expert_kernel,
        grid_spec=grid_spec,
        out_shape=jax.ShapeDtypeStruct((n_rows, d), F32),
        compiler_params=_cparams(("arbitrary",)),
        name="experts",
    )(blk_e, blk_valid, xin, w1, w3, w2)


def _combine_kernel(dest_ref, dnext_ref, y_hbm, x1_ref, gt_ref, p_ref, gple_ref, wg_ref, wp_ref, gfin_ref,
                    o_ref, ybuf, sem, *, final_norm, n_steps):
    tm = x1_ref.shape[0]
    i = pl.program_id(0)
    slot = lax.bitwise_and(i, 1)

    def gather(d_ref, s):
        def issue(t, carry):
            for k in range(TOP_K):
                _row_copy(y_hbm, d_ref[0, k, t], ybuf.at[s, k], t, sem.at[s, k]).start()
            return carry

        lax.fori_loop(0, tm, issue, 0, unroll=8)

    @pl.when(i == 0)
    def _():
        gather(dest_ref, 0)

    @pl.when(i + 1 < n_steps)
    def _():
        gather(dnext_ref, 1 - slot)

    for k in range(TOP_K):
        pltpu.make_async_copy(y_hbm.at[pl.ds(0, tm)], ybuf.at[slot, k], sem.at[slot, k]).wait()

    for r0 in range(0, tm, ROW_TM):
        rows = slice(r0, r0 + ROW_TM)
        pp = _dot(p_ref[rows, :].astype(BF16), wp_ref[...])
        y = ybuf[slot, 0, rows, :] * gt_ref[rows, 2:3] + ybuf[slot, 1, rows, :] * gt_ref[rows, 3:4]
        x2 = x1_ref[rows, :] + y
        hn = _rmsnorm_val(x2, gple_ref[...], RMS_EPS)
        gate = _sigmoid(_dot(hn.astype(BF16), wg_ref[...]))
        x3 = x2 + gate * pp
        if final_norm:
            o_ref[rows, :] = _rmsnorm_val(x3, gfin_ref[...], RMS_EPS)
        else:
            o_ref[rows, :] = x3


def _combine_call(dest_blocks, yb, x1, gates_t, p2d, g_ple, w_gate, w_proj, g_fin, final_norm):
    t, d = x1.shape
    tm = COMBINE_TM
    const = dict(pipeline_mode=pl.Buffered(1))
    return pl.pallas_call(
        functools.partial(_combine_kernel, final_norm=final_norm, n_steps=t // tm),
        grid=(t // tm,),
        in_specs=[
            pl.BlockSpec((1, TOP_K, tm), lambda i: (i, 0, 0), memory_space=pltpu.SMEM),
            pl.BlockSpec((1, TOP_K, tm), lambda i: (jnp.minimum(i + 1, t // tm - 1), 0, 0),
                         memory_space=pltpu.SMEM),
            pl.BlockSpec(memory_space=pl.ANY),
            pl.BlockSpec((tm, d), lambda i: (i, 0)),
            pl.BlockSpec((tm, SUBLANES), lambda i: (i, 0)),
            pl.BlockSpec((tm, p2d.shape[1]), lambda i: (i, 0)),
            pl.BlockSpec((1, d), lambda i: (0, 0)),
            pl.BlockSpec(w_gate.shape, lambda i: (0, 0), **const),
            pl.BlockSpec(w_proj.shape, lambda i: (0, 0), **const),
            pl.BlockSpec((1, d), lambda i: (0, 0)),
        ],
        out_specs=pl.BlockSpec((tm, d), lambda i: (i, 0)),
        out_shape=jax.ShapeDtypeStruct((t, d), F32),
        scratch_shapes=[pltpu.VMEM((2, TOP_K, tm, d), F32), pltpu.SemaphoreType.DMA((2, TOP_K))],
        compiler_params=_cparams(("arbitrary",)),
        name="combine",
    )(dest_blocks, dest_blocks, yb, x1, gates_t, p2d, g_ple, w_gate, w_proj, g_fin)


def _routing_plan(route, counts_f, t):
    counts = counts_f[:, 0].astype(jnp.int32)
    pcounts = ((counts + EXP_TM - 1) // EXP_TM) * EXP_TM
    pends = jnp.cumsum(pcounts)
    pstarts = pends - pcounts
    eid = route[0:2].astype(jnp.int32)
    rank = route[4:6].astype(jnp.int32)
    onehot = eid[..., None] == jnp.arange(N_EXPERTS, dtype=jnp.int32)
    dest = jnp.sum(jnp.where(onehot, pstarts, 0), axis=-1) + rank
    n_rows = TOP_K * t + N_EXPERTS * EXP_TM
    n_blocks = n_rows // EXP_TM
    blk_row = jnp.arange(n_blocks, dtype=jnp.int32) * EXP_TM
    blk_e = jnp.sum((pends[None, :] <= blk_row[:, None]).astype(jnp.int32), axis=1)
    blk_valid = (blk_row < pends[-1]).astype(jnp.int32)
    last_e = jnp.sum((pends <= pends[-1] - 1).astype(jnp.int32))
    blk_e = jnp.minimum(blk_e, last_e)
    zstart = jnp.concatenate([pstarts + counts, pends[-1:]]).astype(jnp.int32)
    zlen = jnp.concatenate([pcounts - counts, n_rows - pends[-1:]]).astype(jnp.int32)
    return dest, blk_e, blk_valid, zstart, zlen, n_rows


def _blocked(dest, tok_per_block):
    t = dest.shape[1]
    return dest.reshape(TOP_K, t // tok_per_block, tok_per_block).transpose(1, 0, 2)


def kernel(x, p, rel_bias, norm_mix_g, w_in, w_gate, lambda_q1, lambda_k1, lambda_q2, lambda_k2, subln_g,
           w_proj_a, w_proj_b, w_out, norm_ffn_g, w_coarse, w_fine, w1, w3, w2, norm_ple_g, w_ple_gate,
           w_ple_proj, final_norm_g):
    batch, seq, d = x.shape
    depth = w_in.shape[0]
    t = batch * seq
    assert seq % PERM_BLK == 0 and t % min(PROJ_TM, t) == 0
    assert all(seq // dil >= BLK and win // dil == BLK for win, dil in DILATED_GROUPS)

    nq = seq // ATT_BLK
    bias_a, w_cat0 = _bias_tiles_call(rel_bias, n_heads=A_HEADS, n_off=nq, tq=ATT_BLK, tk=ATT_BLK,
                                      off_mult=ATT_BLK, off_add=0, dil=1, max_rel=seq, head0=0, name="bias_diff",
                                      key_major=True, mult=A_HEAD_DIM ** 0.5, concat_cast=(w_gate[0], w_in[0]))
    bias_b = jnp.concatenate([
        _bias_tiles_call(rel_bias, n_heads=B_HEADS, n_off=1, tq=2 * BLK, tk=BLK, off_mult=0, off_add=BLK,
                         dil=dil, max_rel=win // dil, head0=A_HEADS + gi * B_HEADS, name="bias_dil%d" % gi,
                         key_major=True, mult=B_HEAD_DIM ** 0.5)[0].reshape(1, B_HEADS, 2 * BLK, BLK)
        for gi, (win, dil) in enumerate(DILATED_GROUPS)], axis=0)

    x2d = x.reshape(t, d)
    for layer in range(depth):
        lam_init = 0.8 - 0.6 * math.exp(-0.3 * layer)
        lam = _lam_call(lambda_q1[layer:layer + 1], lambda_k1[layer:layer + 1],
                        lambda_q2[layer:layer + 1], lambda_k2[layer:layer + 1], lam_init)
        if layer == 0 and w_cat0 is not None:
            w_cat = w_cat0
        else:
            w_cat = jnp.concatenate([w_gate[layer], w_in[layer]], axis=1).astype(BF16)
        zg = _inproj_call(x2d, norm_mix_g[layer:layer + 1], w_cat)
        n_exp, _, d_exp = w1[layer].shape
        oa, (w1b, w3b, w2b) = _diff_attn_call(
            zg, lam, bias_a, subln_g[layer:layer + 1],
            [w1[layer].reshape(n_exp * d, d_exp), w3[layer].reshape(n_exp * d, d_exp),
             w2[layer].reshape(n_exp * d_exp, d)],
            batch=batch, seq=seq, out_scale=1.0 - lam_init)
        ob = _dilated_call(zg, bias_b, batch=batch, seq=seq)

        wr_t = jnp.concatenate([
            w_coarse[layer].T,
            w_fine[layer].transpose(0, 2, 1).reshape(N_EXPERTS, d),
            jnp.zeros((LANES - N_EXPERT_GROUPS - N_EXPERTS, d), F32)], axis=0)
        wr_t = wr_t.T
        wr_hi = wr_t.astype(BF16)
        wr_t = jnp.stack([wr_hi, (wr_t - wr_hi.astype(F32)).astype(BF16)], axis=0)
        x1, h2, route, counts = _merge_router_call(
            oa, ob, zg, x2d, w_proj_a[layer].astype(BF16), w_proj_b[layer].astype(BF16),
            w_out[layer].astype(BF16), norm_ffn_g[layer:layer + 1], wr_t)

        dest, blk_e, blk_valid, zstart, zlen, n_rows = _routing_plan(route, counts, t)
        xin = _dispatch_call(_blocked(dest, DISPATCH_TOK), zstart, zlen, h2, n_rows)
        yb = _expert_call(blk_e, blk_valid, xin, w1b.reshape(n_exp, d, d_exp), w3b.reshape(n_exp, d, d_exp),
                          w2b.reshape(n_exp, d_exp, d))
        x2d = _combine_call(_blocked(dest, COMBINE_TM), yb, x1, route.T, p[layer].reshape(t, -1),
                            norm_ple_g[layer:layer + 1], w_ple_gate[layer].astype(BF16),
                            w_ple_proj[layer].astype(BF16), final_norm_g.reshape(1, d),
                            final_norm=layer == depth - 1)
    return x2d.reshape(batch, seq, d)
```

```python
import functools
import math

import jax
import jax.numpy as jnp
from jax import lax
from jax.experimental import pallas as pl
from jax.experimental.pallas import tpu as pltpu

F32 = jnp.float32
BF16 = jnp.bfloat16

BLK = 128
NEG_INF = -1e30
RMS_EPS = 1e-6
SUBLN_EPS = 1e-5
N_BUCKETS = 32
MAX_DISTANCE = 2048
A_HEADS = 8
A_HEAD_DIM = 128
DILATED_GROUPS = ((128, 1), (512, 4), (2048, 16))
N_DGROUPS = 3
B_HEADS = 8
B_HEAD_DIM = 64
N_EXPERT_GROUPS = 4
EXPERTS_PER_GROUP = 8
N_EXPERTS = 32
TOP_K = 2

LANES = 128
SUBLANES = 8
VMEM_LIMIT = 56 * 1024 * 1024
ATT_BLK = 256
ATT_HEADS = 2
PERM_BLK = 256
GATE_COLS = 4096
PROJ_TN = 512
PROJ_TM = 2048
ROW_TM = 256
COMBINE_TM = 512
EXP_TM = 256
DISPATCH_TOK = 2048
NORM_CHUNK = 128
CAST_SLAB_BYTES = 1 << 20
DIL_GROUP = 8
TOEPLITZ_LANES = 512


def _t5_thresholds():
    max_exact = N_BUCKETS // 2
    out = []
    for k in range(1, N_BUCKETS - max_exact):
        out.append(int(math.ceil(max_exact * (MAX_DISTANCE / max_exact) ** (k / (N_BUCKETS - max_exact)))))
    return tuple(out)


T5_THRESHOLDS = _t5_thresholds()


def _cparams(sem, vmem=VMEM_LIMIT):
    return pltpu.CompilerParams(dimension_semantics=sem, vmem_limit_bytes=vmem)


def _mult(x, m):
    return x if isinstance(x, int) else pl.multiple_of(x, m)


def _sigmoid(x):
    return 0.5 * jnp.tanh(0.5 * x) + 0.5


def _dot(a, b, **kw):
    return jnp.dot(a, b, preferred_element_type=F32, **kw)


def _dot_nt(a, b, **kw):
    return lax.dot_general(a, b, (((1,), (1,)), ((), ())), preferred_element_type=F32, **kw)


def _lam_kernel(q1_ref, k1_ref, q2_ref, k2_ref, o_ref, *, lam_init):
    s1 = jnp.sum(q1_ref[...] * k1_ref[...], axis=-1, keepdims=True)
    s2 = jnp.sum(q2_ref[...] * k2_ref[...], axis=-1, keepdims=True)
    o_ref[...] = jnp.exp(s1) - jnp.exp(s2) + lam_init


def _lam_call(lq1, lk1, lq2, lk2, lam_init):
    return pl.pallas_call(
        functools.partial(_lam_kernel, lam_init=lam_init),
        out_shape=jax.ShapeDtypeStruct((1, 1), F32),
        name="lam",
    )(lq1, lk1, lq2, lk2)


def _bias_tile_kernel(tab_ref, *rest, tq, tk, off_mult, off_add, dil, max_rel, head0, key_major, mult):
    n_pass = max(len(rest) - 2, 0)
    o_ref = rest[n_pass]
    if n_pass:
        col = 0
        for src in rest[:n_pass]:
            rest[-1][:, col:col + src.shape[1]] = src[...].astype(rest[-1].dtype)
            col += src.shape[1]
    h = pl.program_id(0)
    n = pl.program_id(1)
    assert key_major and tq + tk <= TOEPLITZ_LANES
    m = lax.broadcasted_iota(jnp.int32, (8, TOEPLITZ_LANES), 1)
    rel = jnp.where(m < tk, m, m - TOEPLITZ_LANES) + (n * off_mult + off_add)
    dist = rel * dil
    large = jnp.full(m.shape, N_BUCKETS // 2, jnp.int32)
    for thr in T5_THRESHOLDS:
        large = large + jnp.where(dist >= thr, 1, 0)
    bucket = jnp.where(dist < N_BUCKETS // 2, dist, large)
    acc = jnp.zeros(m.shape, F32)
    for b in range(N_BUCKETS):
        acc = jnp.where(bucket == b, tab_ref[b, head0 + h], acc)
    valid = jnp.where(rel >= 0, jnp.where(rel <= max_rel, 1, 0), 0)
    vec = jnp.where(valid == 1, acc * mult, NEG_INF)[0:1, :]
    rows = pltpu.roll(jnp.broadcast_to(vec, (tq, TOEPLITZ_LANES)), 0, 1, stride=1, stride_axis=0)
    o_ref[0, 0] = rows[:, :tk]


def _bias_tiles_call(rel_bias, *, n_heads, n_off, tq, tk, off_mult, off_add, dil, max_rel, head0, name,
                     key_major=False, mult=1.0, concat_cast=()):
    kern = functools.partial(_bias_tile_kernel, tq=tq, tk=tk, off_mult=off_mult, off_add=off_add,
                             dil=dil, max_rel=max_rel, head0=head0, key_major=key_major, mult=mult)
    n_steps = n_heads * n_off
    if concat_cast and concat_cast[0].shape[0] % (n_steps * 16):
        concat_cast = ()
    in_specs = [pl.BlockSpec(memory_space=pltpu.SMEM)]
    out_specs = [pl.BlockSpec((1, 1, tq, tk), lambda h, n: (h, n, 0, 0))]
    out_shape = [jax.ShapeDtypeStruct((n_heads, n_off, tq, tk), F32)]
    if concat_cast:
        rows = concat_cast[0].shape[0]
        cols = sum(w.shape[1] for w in concat_cast)
        in_specs += [pl.BlockSpec((rows // n_steps, w.shape[1]), lambda h, n: (h * n_off + n, 0))
                     for w in concat_cast]
        out_specs.append(pl.BlockSpec((rows // n_steps, cols), lambda h, n: (h * n_off + n, 0)))
        out_shape.append(jax.ShapeDtypeStruct((rows, cols), BF16))
    outs = pl.pallas_call(
        kern,
        grid=(n_heads, n_off),
        in_specs=in_specs,
        out_specs=out_specs,
        out_shape=out_shape,
        compiler_params=_cparams(("parallel", "parallel")),
        name=name,
    )(rel_bias, *concat_cast)
    return outs[0], (outs[1] if concat_cast else None)


def _rmsnorm_rows(x_ref, g_ref, out_ref, eps):
    rows = x_ref.shape[0]
    g = g_ref[...]

    def body(c, carry):
        r0 = pl.multiple_of(c * NORM_CHUNK, NORM_CHUNK)
        x = x_ref[pl.ds(r0, NORM_CHUNK), :]
        ms = jnp.mean(x * x, axis=-1, keepdims=True)
        out_ref[pl.ds(r0, NORM_CHUNK), :] = ((x * lax.rsqrt(ms + eps)) * g).astype(out_ref.dtype)
        return carry

    lax.fori_loop(0, rows // NORM_CHUNK, body, 0)


def _rmsnorm_val(x, g, eps):
    ms = jnp.mean(x * x, axis=-1, keepdims=True)
    return (x * lax.rsqrt(ms + eps)) * g


def _inproj_kernel(x_ref, g_ref, w_ref, o_ref, h_ref, *, n_gate_blocks):
    j = pl.program_id(1)

    @pl.when(j == 0)
    def _():
        _rmsnorm_rows(x_ref, g_ref, h_ref, RMS_EPS)

    acc = _dot(h_ref[...], w_ref[...])

    @pl.when(j < n_gate_blocks)
    def _():
        o_ref[...] = _sigmoid(acc).astype(o_ref.dtype)

    @pl.when(j >= n_gate_blocks)
    def _():
        o_ref[...] = acc.astype(o_ref.dtype)


def _inproj_call(x2d, g, w_cat):
    t, d = x2d.shape
    n = w_cat.shape[1]
    tm = min(PROJ_TM, t)
    return pl.pallas_call(
        functools.partial(_inproj_kernel, n_gate_blocks=GATE_COLS // PROJ_TN),
        grid=(t // tm, n // PROJ_TN),
        in_specs=[
            pl.BlockSpec((tm, d), lambda i, j: (i, 0)),
            pl.BlockSpec((1, d), lambda i, j: (0, 0)),
            pl.BlockSpec((d, PROJ_TN), lambda i, j: (0, j)),
        ],
        out_specs=pl.BlockSpec((tm, PROJ_TN), lambda i, j: (i, j)),
        out_shape=jax.ShapeDtypeStruct((t, n), BF16),
        scratch_shapes=[pltpu.VMEM((tm, d), BF16)],
        compiler_params=_cparams(("parallel", "arbitrary")),
        name="inproj",
    )(x2d, g, w_cat)


def _diff_attn_kernel(lam_ref, q_ref, k_ref, v_ref, bias_ref, g_ref, *rest, out_scale, n_blk, n_cast):
    cast_in = rest[:n_cast]
    o_ref = rest[n_cast]
    cast_out = rest[n_cast + 1:2 * n_cast + 1]
    vt_ref, tbuf_a, tbuf_b, acc = rest[2 * n_cast + 1:]
    for src, dst in zip(cast_in, cast_out):
        dst[...] = src[...].astype(dst.dtype)
    qi = pl.program_id(2)
    width = 2 * A_HEAD_DIM
    n_map = 2 * ATT_HEADS
    c = (A_HEAD_DIM ** -0.5) * math.log2(math.e)

    @pl.when(qi == 0)
    def _():
        def transpose_block(b, carry):
            r0 = pl.multiple_of(b * ATT_BLK, ATT_BLK)
            for hp in range(ATT_HEADS):
                vb = v_ref[pl.ds(r0, ATT_BLK), hp * width:(hp + 1) * width]
                vt_ref[hp, b] = vb.astype(F32).T.astype(BF16)
            return carry

        lax.fori_loop(0, n_blk, transpose_block, 0)

    acc[...] = jnp.zeros_like(acc)

    q0 = pl.multiple_of(qi * ATT_BLK, ATT_BLK)
    map_cols = [slice(mi * A_HEAD_DIM, (mi + 1) * A_HEAD_DIM) for mi in range(n_map)]
    qs = [q_ref[pl.ds(q0, ATT_BLK), cols] for cols in map_cols]

    def scores(ki, dst, qt=None):
        k0 = pl.multiple_of(ki * ATT_BLK, ATT_BLK)
        for mi, cols in enumerate(map_cols):
            if qt is None:
                q, tile = qs[mi], qi - ki
            else:
                q, tile = q_ref[pl.ds(pl.multiple_of(qt * ATT_BLK, ATT_BLK), ATT_BLK), cols], qt - ki
            dst[mi] = _dot_nt(k_ref[pl.ds(k0, ATT_BLK), cols], q) + bias_ref[mi // 2, tile]

    def softmax(t, m, l):
        m_new = jnp.maximum(m, jnp.max(t, axis=0, keepdims=True))
        alpha = jnp.exp2((m - m_new) * c)
        p = jnp.exp2((t - m_new) * c)
        return m_new, alpha * l + jnp.sum(p, axis=0, keepdims=True), alpha, p.astype(BF16)

    first_buf = lax.bitwise_and(lax.shift_right_logical(qi * (qi + 1), 1), 1)

    @pl.when(qi == 0)
    def _():
        scores(0, tbuf_a)

    def step(ki, carry, cur, nxt, last):
        ms, ls = carry
        stats = [softmax(cur[mi], ms[mi], ls[mi]) for mi in range(n_map)]
        if last:
            scores(0, nxt, qt=jnp.minimum(qi + 1, n_blk - 1))
        else:
            scores(ki + 1, nxt)
        for mi in range(n_map):
            acc[mi] = acc[mi] * stats[mi][2] + _dot(vt_ref[mi // 2, ki], stats[mi][3])
        return tuple(st[0] for st in stats), tuple(st[1] for st in stats)

    def body(ki, carry, last=False):
        return lax.cond(lax.bitwise_and(first_buf + ki, 1) == 0,
                        lambda cr: step(ki, cr, tbuf_a, tbuf_b, last),
                        lambda cr: step(ki, cr, tbuf_b, tbuf_a, last), carry)

    minf = (jnp.full((1, ATT_BLK), -jnp.inf, F32),) * n_map
    zero = (jnp.zeros((1, ATT_BLK), F32),) * n_map
    _, ls = body(qi, lax.fori_loop(0, qi, body, (minf, zero)), last=True)
    for hp in range(ATT_HEADS):
        w = acc[2 * hp] / ls[2 * hp] - lam_ref[0, 0] * (acc[2 * hp + 1] / ls[2 * hp + 1])
        ms = jnp.mean(w * w, axis=0, keepdims=True)
        y = ((w * lax.rsqrt(ms + SUBLN_EPS)) * g_ref[...]) * out_scale
        o_ref[:, hp * width:(hp + 1) * width] = y.T.astype(o_ref.dtype)


def _diff_attn_call(zg, lam, bias_tiles, subln_g, passengers, *, batch, seq, out_scale):
    t = batch * seq
    nq = seq // ATT_BLK
    width = 2 * A_HEAD_DIM
    blk_w = ATT_HEADS * width
    q_col = GATE_COLS // blk_w
    k_col = q_col + A_HEADS // ATT_HEADS
    v_col = k_col + A_HEADS // ATT_HEADS
    n_steps = (A_HEADS // ATT_HEADS) * batch * nq
    riders = [w for w in passengers
              if w.shape[0] % (n_steps * 16) == 0 and w.size * 4 // n_steps <= CAST_SLAB_BYTES]

    def slab_spec(w):
        return pl.BlockSpec((w.shape[0] // n_steps, w.shape[1]), lambda h, b, i: ((h * batch + b) * nq + i, 0))

    outs = pl.pallas_call(
        functools.partial(_diff_attn_kernel, out_scale=out_scale, n_blk=nq, n_cast=len(riders)),
        grid=(A_HEADS // ATT_HEADS, batch, nq),
        in_specs=[
            pl.BlockSpec(memory_space=pltpu.SMEM),
            pl.BlockSpec((seq, blk_w), lambda h, b, i: (b, q_col + h)),
            pl.BlockSpec((seq, blk_w), lambda h, b, i: (b, k_col + h)),
            pl.BlockSpec((seq, blk_w), lambda h, b, i: (b, v_col + h)),
            pl.BlockSpec((ATT_HEADS, nq, ATT_BLK, ATT_BLK), lambda h, b, i: (h, 0, 0, 0)),
            pl.BlockSpec((width, 1), lambda h, b, i: (0, 0)),
        ] + [slab_spec(w) for w in riders],
        out_specs=[pl.BlockSpec((ATT_BLK, blk_w), lambda h, b, i: (b * nq + i, h))]
        + [slab_spec(w) for w in riders],
        out_shape=[jax.ShapeDtypeStruct((t, A_HEADS * width), BF16)]
        + [jax.ShapeDtypeStruct(w.shape, BF16) for w in riders],
        scratch_shapes=[pltpu.VMEM((ATT_HEADS, nq, width, ATT_BLK), BF16),
                        pltpu.VMEM((2 * ATT_HEADS, ATT_BLK, ATT_BLK), F32),
                        pltpu.VMEM((2 * ATT_HEADS, ATT_BLK, ATT_BLK), F32),
                        pltpu.VMEM((2 * ATT_HEADS, width, ATT_BLK), F32)],
        compiler_params=_cparams(("parallel", "parallel", "arbitrary")),
        name="diff_attn",
    )(lam, zg, zg, zg, bias_tiles, subln_g.reshape(width, 1), *riders)
    cast = iter(outs[1:])
    return outs[0], [next(cast) if any(w is r for r in riders) else w.astype(BF16) for w in passengers]


def _perm_matrix(dil, inverse):
    w = PERM_BLK // dil
    shift = w.bit_length() - 1
    a = lax.broadcasted_iota(jnp.int32, (PERM_BLK, PERM_BLK), 0)
    b = lax.broadcasted_iota(jnp.int32, (PERM_BLK, PERM_BLK), 1)
    dst, src = (b, a) if inverse else (a, b)
    c = lax.shift_right_logical(dst, shift)
    ll = lax.bitwise_and(dst, w - 1)
    return jnp.where(src == ll * dil + c, jnp.float32(1), jnp.float32(0))


def _deinterleave(pairs, perm, dil, seq):
    w = PERM_BLK // dil
    stream_len = seq // dil

    def body(b8, carry):
        r0 = pl.multiple_of(b8 * PERM_BLK, PERM_BLK)
        ys = [_dot(perm, src[pl.ds(r0, PERM_BLK), :]).astype(dst.dtype) for src, dst in pairs]
        for y, (_, dst) in zip(ys, pairs):
            for c in range(dil):
                d0 = pl.multiple_of(c * stream_len + b8 * w, w)
                dst[pl.ds(d0, w), :] = y[c * w:(c + 1) * w, :]
        return carry

    lax.fori_loop(0, seq // PERM_BLK, body, 0)


def _permute_rows(perm, x):
    if x.dtype == BF16:
        return _dot(perm, x)
    hi = x.astype(BF16)
    r1 = x - hi.astype(F32)
    mid = r1.astype(BF16)
    lo = (r1 - mid.astype(F32)).astype(BF16)
    y = _dot(perm, jnp.concatenate([hi, mid, lo], axis=1))
    n = x.shape[1]
    return y[:, :n] + (y[:, n:2 * n] + y[:, 2 * n:])


def _interleave(items, perm_inv, dil, seq):
    w = PERM_BLK // dil
    stream_len = seq // dil

    def body(b8, carry):
        for src, _, stack in items:
            for c in range(dil):
                s0 = pl.multiple_of(c * stream_len + b8 * w, w)
                stack[c * w:(c + 1) * w, :] = src[pl.ds(s0, w), :]
        outs = [_permute_rows(perm_inv, stack[...]) for _, _, stack in items]
        r0 = pl.multiple_of(b8 * PERM_BLK, PERM_BLK)
        for out, (_, dst, _) in zip(outs, items):
            dst[pl.ds(r0, PERM_BLK), :] = out.astype(dst.dtype)
        return carry

    lax.fori_loop(0, seq // PERM_BLK, body, 0)


def _transpose_blocks(src_ref, vt_ref, seq):
    def body(blk, carry):
        r0 = pl.multiple_of(blk * BLK, BLK)
        vt_ref[blk] = src_ref[pl.ds(r0, BLK), :].astype(F32).T.astype(vt_ref.dtype)
        return carry

    lax.fori_loop(0, seq // BLK, body, 0)


def _dil_windows(q_src, k_src, vt_ref, o_dst, lse_dst, bias_ref, ot_ref, lt_ref, windows):
    scale = B_HEAD_DIM ** -0.5
    c = scale * math.log2(math.e)
    tiles = []
    for rq, nk in windows:
        rk = rq - (nk - BLK)
        for hh in range(B_HEADS):
            cols = slice(hh * B_HEAD_DIM, (hh + 1) * B_HEAD_DIM)
            qh = q_src[pl.ds(rq, BLK), cols]
            kh = k_src[pl.ds(rk, nk), cols]
            tiles.append(_dot_nt(kh, qh) + bias_ref[0, hh, 2 * BLK - nk:, :])
    for wi, (rq, nk) in enumerate(windows):
        qb = rq // BLK if isinstance(rq, int) else lax.shift_right_logical(rq, BLK.bit_length() - 1)
        for hh in range(B_HEADS):
            cols = slice(hh * B_HEAD_DIM, (hh + 1) * B_HEAD_DIM)
            t = tiles[wi * B_HEADS + hh]
            m = jnp.max(t, axis=0, keepdims=True)
            p = jnp.exp2((t - m) * c)
            den = jnp.sum(p, axis=0, keepdims=True)
            if nk == BLK:
                vth = vt_ref[qb, cols, :]
            else:
                vth = jnp.concatenate([vt_ref[qb - 1, cols, :], vt_ref[qb, cols, :]], axis=1)
            ot_ref[wi, cols, :] = _dot(vth, p.astype(BF16)) / den
            lt_ref[wi, hh:hh + 1, :] = m * scale + jnp.log(den)
        o_dst[pl.ds(rq, BLK), :] = ot_ref[wi].T.astype(o_dst.dtype)
        lse_dst[pl.ds(rq, BLK), :] = lt_ref[wi].T


def _dil_streams(q_src, k_src, vt_ref, o_dst, lse_dst, bias_ref, ot_ref, lt_ref, dil, seq):
    stream_len = seq // dil
    nq = stream_len // BLK

    def run(windows):
        _dil_windows(q_src, k_src, vt_ref, o_dst, lse_dst, bias_ref, ot_ref, lt_ref, windows)

    if nq == 1:
        group = math.gcd(DIL_GROUP, dil)

        def stream_group(cg, carry):
            base = pl.multiple_of(cg * (group * stream_len), BLK)
            run([(pl.multiple_of(base + s * stream_len, BLK), BLK) for s in range(group)])
            return carry

        lax.fori_loop(0, dil // group, stream_group, 0)
        return

    def stream(c, carry):
        base = _mult(c * stream_len, BLK)
        n_tail = (nq - 1) % DIL_GROUP
        run([(base, BLK)] + [(_mult(base + (nq - 1 - s) * BLK, BLK), 2 * BLK) for s in range(n_tail)])

        def qgroup(j, carry2):
            rq = pl.multiple_of(base + (1 + DIL_GROUP * j) * BLK, BLK)
            run([(pl.multiple_of(rq + s * BLK, BLK), 2 * BLK) for s in range(DIL_GROUP)])
            return carry2

        n_groups = (nq - 1 - n_tail) // DIL_GROUP
        if n_groups:
            lax.fori_loop(0, n_groups, qgroup, 0)
        return carry

    if dil == 1:
        stream(0, 0)
    else:
        lax.fori_loop(0, dil, stream, 0)


def _head_expand(x, expand):
    hi = x.astype(BF16)
    lo = (x - hi.astype(F32)).astype(BF16)
    return _dot(jnp.concatenate([hi, lo], axis=1), expand)


def _dil_merge(o_src, lse_src, oacc, mrun, lrun, out_ref, first, last, seq):
    er = lax.broadcasted_iota(jnp.int32, (2 * LANES, B_HEADS * B_HEAD_DIM), 0)
    ec = lax.broadcasted_iota(jnp.int32, (2 * LANES, B_HEADS * B_HEAD_DIM), 1)
    expand = jnp.where(lax.shift_right_logical(ec, B_HEAD_DIM.bit_length() - 1) == lax.bitwise_and(er, LANES - 1),
                       jnp.float32(1), jnp.float32(0)).astype(BF16)

    def body(ch, carry):
        r0 = pl.multiple_of(ch * ROW_TM, ROW_TM)
        rows = pl.ds(r0, ROW_TM)
        lse = lse_src[rows, :]
        if first:
            mrun[rows, :] = lse
            lrun[rows, :] = jnp.ones_like(lse)
            oacc[rows, :] = o_src[rows, :].astype(F32)
            return carry
        m_old = mrun[rows, :]
        m_new = jnp.maximum(m_old, lse)
        a = jnp.exp(m_old - m_new)
        bw = jnp.exp(lse - m_new)
        l_new = lrun[rows, :] * a + bw
        if last:
            a = a / l_new
            bw = bw / l_new
        val = oacc[rows, :] * _head_expand(a, expand) + o_src[rows, :].astype(F32) * _head_expand(bw, expand)
        if last:
            out_ref[rows, :] = val.astype(out_ref.dtype)
        else:
            mrun[rows, :] = m_new
            lrun[rows, :] = l_new
            oacc[rows, :] = val
        return carry

    lax.fori_loop(0, seq // ROW_TM, body, 0)


def _dilated_kernel(q_ref, k_ref, v_ref, bias_ref, out_ref,
                    qs, ks, vs, vt, os_, lses, otok, lsetok, ostack, lstack, ot, lt, oacc, mrun, lrun,
                    *, seq):
    g = pl.program_id(1)

    @pl.when(g == 0)
    def _():
        lt[...] = jnp.zeros_like(lt)

    for gi, (_, dil) in enumerate(DILATED_GROUPS):

        @pl.when(g == gi)
        def _(gi=gi, dil=dil):
            first = gi == 0
            last = gi == N_DGROUPS - 1
            if dil == 1:
                _transpose_blocks(v_ref, vt, seq)
                _dil_streams(q_ref, k_ref, vt, otok, lsetok, bias_ref, ot, lt, 1, seq)
            else:
                perm = _perm_matrix(dil, inverse=False).astype(BF16)
                _deinterleave([(q_ref, qs), (k_ref, ks), (v_ref, vs)], perm, dil, seq)
                _transpose_blocks(vs, vt, seq)
                _dil_streams(qs, ks, vt, os_, lses, bias_ref, ot, lt, dil, seq)
                perm_inv = _perm_matrix(dil, inverse=True).astype(BF16)
                _interleave([(os_, otok, ostack), (lses, lsetok, lstack)], perm_inv, dil, seq)
            _dil_merge(otok, lsetok, oacc, mrun, lrun, out_ref, first, last, seq)


def _dilated_call(zg, bias_tiles, *, batch, seq):
    width = B_HEADS * B_HEAD_DIM
    q_col = (GATE_COLS + 3 * A_HEADS * 2 * A_HEAD_DIM) // width
    k_col = q_col + N_DGROUPS
    v_col = k_col + N_DGROUPS
    return pl.pallas_call(
        functools.partial(_dilated_kernel, seq=seq),
        grid=(batch, N_DGROUPS),
        in_specs=[
            pl.BlockSpec((seq, width), lambda b, g: (b, q_col + g)),
            pl.BlockSpec((seq, width), lambda b, g: (b, k_col + g)),
            pl.BlockSpec((seq, width), lambda b, g: (b, v_col + g)),
            pl.BlockSpec((1, B_HEADS, 2 * BLK, BLK), lambda b, g: (g, 0, 0, 0)),
        ],
        out_specs=pl.BlockSpec((seq, width), lambda b, g: (b, 0)),
        out_shape=jax.ShapeDtypeStruct((batch * seq, width), BF16),
        scratch_shapes=[
            pltpu.VMEM((seq, width), BF16), pltpu.VMEM((seq, width), BF16), pltpu.VMEM((seq, width), BF16),
            pltpu.VMEM((seq // BLK, width, BLK), BF16),
            pltpu.VMEM((seq, width), BF16), pltpu.VMEM((seq, LANES), F32),
            pltpu.VMEM((seq, width), BF16), pltpu.VMEM((seq, LANES), F32),
            pltpu.VMEM((PERM_BLK, width), BF16), pltpu.VMEM((PERM_BLK, LANES), F32),
            pltpu.VMEM((DIL_GROUP, width, BLK), F32), pltpu.VMEM((DIL_GROUP, LANES, BLK), F32),
            pltpu.VMEM((seq, width), F32), pltpu.VMEM((seq, LANES), F32), pltpu.VMEM((seq, LANES), F32),
        ],
        compiler_params=_cparams(("parallel", "arbitrary")),
        name="dilated_attn",
    )(zg, zg, zg, bias_tiles)


def _merge_router_kernel(oa_ref, ob_ref, gate_ref, x_ref, pa_ref, pb_ref, wo_ref, g_ref, wr_ref,
                         x1_ref, h2_ref, route_ref, cnt_ref, run_ref):
    tm, d = x_ref.shape

    @pl.when(pl.program_id(0) == 0)
    def _():
        run_ref[...] = jnp.zeros_like(run_ref)

    a = _dot(oa_ref[...], pa_ref[...])
    bm = _dot(ob_ref[...], pb_ref[...])
    merged = gate_ref[:, :d].astype(F32) * a + gate_ref[:, d:].astype(F32) * bm
    x1 = x_ref[...] + _dot(merged.astype(BF16), wo_ref[...])
    x1_ref[...] = x1
    h2 = _rmsnorm_val(x1, g_ref[...], RMS_EPS)
    h2_ref[...] = h2

    h_hi = h2.astype(BF16)
    h_lo = (h2 - h_hi.astype(F32)).astype(BF16)
    lt = (_dot(h_hi, wr_ref[0]) + (_dot(h_hi, wr_ref[1]) + _dot(h_lo, wr_ref[0]))).T
    coarse = [lt[i:i + 1, :] for i in range(N_EXPERT_GROUPS)]
    best = coarse[0]
    gsel = jnp.zeros((1, tm), jnp.int32)
    for i in range(1, N_EXPERT_GROUPS):
        upd = coarse[i] > best
        gsel = jnp.where(upd, i, gsel)
        best = jnp.where(upd, coarse[i], best)
    den = jnp.exp(coarse[0] - best)
    for i in range(1, N_EXPERT_GROUPS):
        den = den + jnp.exp(coarse[i] - best)
    pg = 1.0 / den

    fine = []
    for k in range(EXPERTS_PER_GROUP):
        f = lt[N_EXPERT_GROUPS + k:N_EXPERT_GROUPS + k + 1, :]
        for gi in range(1, N_EXPERT_GROUPS):
            r = N_EXPERT_GROUPS + gi * EXPERTS_PER_GROUP + k
            f = jnp.where(gsel == gi, lt[r:r + 1, :], f)
        fine.append(f)
    v0 = fine[0]
    i0 = jnp.zeros((1, tm), jnp.int32)
    for k in range(1, EXPERTS_PER_GROUP):
        upd = fine[k] > v0
        i0 = jnp.where(upd, k, i0)
        v0 = jnp.where(upd, fine[k], v0)
    v1 = jnp.full((1, tm), -jnp.inf, F32)
    i1 = jnp.zeros((1, tm), jnp.int32)
    for k in range(EXPERTS_PER_GROUP):
        upd = jnp.where(i0 != k, jnp.where(fine[k] > v1, 1, 0), 0) == 1
        i1 = jnp.where(upd, k, i1)
        v1 = jnp.where(upd, fine[k], v1)
    e1w = jnp.exp(v1 - v0)
    gate0 = pg * (1.0 / (1.0 + e1w))
    gate1 = pg * (e1w / (1.0 + e1w))
    e0 = gsel * EXPERTS_PER_GROUP + i0
    e1 = gsel * EXPERTS_PER_GROUP + i1

    eidx = lax.broadcasted_iota(jnp.int32, (N_EXPERTS, tm), 0)
    oh0 = jnp.where(eidx == e0, jnp.float32(1), jnp.float32(0))
    oh1 = jnp.where(eidx == e1, jnp.float32(1), jnp.float32(0))
    ta = lax.broadcasted_iota(jnp.int32, (tm, tm), 0)
    tb = lax.broadcasted_iota(jnp.int32, (tm, tm), 1)
    before = jnp.where(ta < tb, jnp.float32(1), jnp.float32(0)).astype(BF16)
    pre0 = _dot(oh0.astype(BF16), before)
    pre1 = _dot(oh1.astype(BF16), before)
    run = run_ref[:, 0:1]
    tot0 = jnp.sum(oh0, axis=1, keepdims=True)
    tot1 = jnp.sum(oh1, axis=1, keepdims=True)
    rank0 = jnp.sum(oh0 * (run + pre0), axis=0, keepdims=True)
    rank1 = jnp.sum(oh1 * (run + tot0 + pre1), axis=0, keepdims=True)
    new_run = jnp.broadcast_to(run + tot0 + tot1, run_ref.shape)
    run_ref[...] = new_run
    cnt_ref[...] = new_run

    route_ref[0:1, :] = e0.astype(F32)
    route_ref[1:2, :] = e1.astype(F32)
    route_ref[2:3, :] = gate0
    route_ref[3:4, :] = gate1
    route_ref[4:5, :] = rank0
    route_ref[5:6, :] = rank1
    route_ref[6:8, :] = jnp.zeros((2, tm), F32)


def _merge_router_call(oa, ob, zg, x2d, pa, pb, wo, g, wr_t):
    t, d = x2d.shape
    tm = ROW_TM
    const = dict(pipeline_mode=pl.Buffered(1))
    return pl.pallas_call(
        _merge_router_kernel,
        grid=(t // tm,),
        in_specs=[
            pl.BlockSpec((tm, oa.shape[1]), lambda i: (i, 0)),
            pl.BlockSpec((tm, ob.shape[1]), lambda i: (i, 0)),
            pl.BlockSpec((tm, GATE_COLS), lambda i: (i, 0)),
            pl.BlockSpec((tm, d), lambda i: (i, 0)),
            pl.BlockSpec(pa.shape, lambda i: (0, 0), **const),
            pl.BlockSpec(pb.shape, lambda i: (0, 0), **const),
            pl.BlockSpec(wo.shape, lambda i: (0, 0), **const),
            pl.BlockSpec((1, d), lambda i: (0, 0)),
            pl.BlockSpec(wr_t.shape, lambda i: (0, 0, 0), **const),
        ],
        out_specs=[
            pl.BlockSpec((tm, d), lambda i: (i, 0)),
            pl.BlockSpec((tm, d), lambda i: (i, 0)),
            pl.BlockSpec((SUBLANES, tm), lambda i: (0, i)),
            pl.BlockSpec((N_EXPERTS, LANES), lambda i: (0, 0)),
        ],
        out_shape=[
            jax.ShapeDtypeStruct((t, d), F32),
            jax.ShapeDtypeStruct((t, d), F32),
            jax.ShapeDtypeStruct((SUBLANES, t), F32),
            jax.ShapeDtypeStruct((N_EXPERTS, LANES), F32),
        ],
        scratch_shapes=[pltpu.VMEM((N_EXPERTS, LANES), F32)],
        compiler_params=_cparams(("arbitrary",)),
        name="merge_router",
    )(oa, ob, zg, x2d, pa, pb, wo, g, wr_t)


def _row_copy(src, s_row, dst, d_row, sem):
    return pltpu.make_async_copy(src.at[pl.ds(s_row, 1)], dst.at[pl.ds(d_row, 1)], sem)


def _dispatch_kernel(dest_ref, zstart_ref, zlen_ref, h_ref, x_hbm, zero_ref, sem, zsem):
    step = pl.program_id(0)

    def issue(t, carry):
        for k in range(TOP_K):
            _row_copy(h_ref, t, x_hbm, dest_ref[0, k, t], sem.at[k]).start(priority=k)
        return carry

    lax.fori_loop(0, DISPATCH_TOK, issue, 0, unroll=8)

    @pl.when(step == 0)
    def _():
        zero_ref[...] = jnp.zeros_like(zero_ref)

        def group_copy(r8):
            return pltpu.make_async_copy(
                zero_ref, x_hbm.at[pl.ds(pl.multiple_of(r8, SUBLANES), SUBLANES)], zsem.at[1])

        def segment(e, wait):
            start = zstart_ref[e]
            n_head = jnp.minimum(lax.bitwise_and(-start, SUBLANES - 1), zlen_ref[e])
            n_group = lax.shift_right_logical(zlen_ref[e] - n_head, SUBLANES.bit_length() - 1)

            def head(r, carry):
                cp = _row_copy(zero_ref, 0, x_hbm, 0 if wait else start + r, zsem.at[0])
                cp.wait() if wait else cp.start()
                return carry

            def group(j, carry):
                cp = group_copy(0 if wait else start + n_head + SUBLANES * j)
                cp.wait() if wait else cp.start()
                return carry

            lax.fori_loop(0, n_head, head, 0)
            lax.fori_loop(0, n_group, group, 0)

        lax.fori_loop(0, N_EXPERTS + 1, lambda e, c: (segment(e, False), c)[1], 0)
        lax.fori_loop(0, N_EXPERTS + 1, lambda e, c: (segment(e, True), c)[1], 0)

    for k in range(TOP_K):
        pltpu.make_async_copy(h_ref, x_hbm.at[pl.ds(0, DISPATCH_TOK)], sem.at[k]).wait()


def _dispatch_call(dest_blocks, zstart, zlen, h2, n_rows):
    t, d = h2.shape
    return pl.pallas_call(
        _dispatch_kernel,
        grid=(t // DISPATCH_TOK,),
        in_specs=[
            pl.BlockSpec((1, TOP_K, DISPATCH_TOK), lambda i: (i, 0, 0), memory_space=pltpu.SMEM),
            pl.BlockSpec(memory_space=pltpu.SMEM),
            pl.BlockSpec(memory_space=pltpu.SMEM),
            pl.BlockSpec((DISPATCH_TOK, d), lambda i: (i, 0)),
        ],
        out_specs=pl.BlockSpec(memory_space=pl.ANY),
        out_shape=jax.ShapeDtypeStruct((n_rows, d), F32),
        scratch_shapes=[pltpu.VMEM((SUBLANES, d), F32), pltpu.SemaphoreType.DMA((TOP_K,)),
                        pltpu.SemaphoreType.DMA((2,))],
        compiler_params=_cparams(("arbitrary",)),
        name="dispatch",
    )(dest_blocks, zstart, zlen, h2)


def _expert_kernel(blk_e_ref, blk_valid_ref, x_ref, w1_ref, w3_ref, w2_ref, y_ref):
    i = pl.program_id(0)

    @pl.when(blk_valid_ref[i] == 1)
    def _():
        x = x_ref[...].astype(BF16)
        a = _dot(x, w1_ref[0])
        b = _dot(x, w3_ref[0])
        hdn = (a * _sigmoid(a)) * b
        y_ref[...] = _dot(hdn.astype(BF16), w2_ref[0])

    @pl.when(blk_valid_ref[i] == 0)
    def _():
        y_ref[...] = jnp.zeros_like(y_ref)


def _expert_call(blk_e, blk_valid, xin, w1, w3, w2):
    n_rows, d = xin.shape
    de = w1.shape[2]
    grid_spec = pltpu.PrefetchScalarGridSpec(
        num_scalar_prefetch=2,
        grid=(n_rows // EXP_TM,),
        in_specs=[
            pl.BlockSpec((EXP_TM, d), lambda i, be, bv: (i, 0)),
            pl.BlockSpec((1, d, de), lambda i, be, bv: (be[i], 0, 0)),
            pl.BlockSpec((1, d, de), lambda i, be, bv: (be[i], 0, 0)),
            pl.BlockSpec((1, de, d), lambda i, be, bv: (be[i], 0, 0)),
        ],
        out_specs=pl.BlockSpec((EXP_TM, d), lambda i, be, bv: (i, 0)),
    )
    return pl.pallas_call(
        _expert_kernel,
        grid_spec=grid_spec,
        out_shape=jax.ShapeDtypeStruct((n_rows, d), F32),
        compiler_params=_cparams(("arbitrary",)),
        name="experts",
    )(blk_e, blk_valid, xin, w1, w3, w2)


def _combine_kernel(dest_ref, dnext_ref, y_hbm, x1_ref, gt_ref, p_ref, gple_ref, wg_ref, wp_ref, gfin_ref,
                    o_ref, ybuf, sem, *, final_norm, n_steps):
    tm = x1_ref.shape[0]
    i = pl.program_id(0)
    slot = lax.bitwise_and(i, 1)

    def gather(d_ref, s):
        def issue(t, carry):
            for k in range(TOP_K):
                _row_copy(y_hbm, d_ref[0, k, t], ybuf.at[s, k], t, sem.at[s, k]).start(priority=k)
            return carry

        lax.fori_loop(0, tm, issue, 0, unroll=8)

    @pl.when(i == 0)
    def _():
        gather(dest_ref, 0)

    @pl.when(i + 1 < n_steps)
    def _():
        gather(dnext_ref, 1 - slot)

    for k in range(TOP_K):
        pltpu.make_async_copy(y_hbm.at[pl.ds(0, tm)], ybuf.at[slot, k], sem.at[slot, k]).wait()

    for r0 in range(0, tm, ROW_TM):
        rows = slice(r0, r0 + ROW_TM)
        pp = _dot(p_ref[rows, :].astype(BF16), wp_ref[...])
        y = ybuf[slot, 0, rows, :] * gt_ref[rows, 2:3] + ybuf[slot, 1, rows, :] * gt_ref[rows, 3:4]
        x2 = x1_ref[rows, :] + y
        hn = _rmsnorm_val(x2, gple_ref[...], RMS_EPS)
        gate = _sigmoid(_dot(hn.astype(BF16), wg_ref[...]))
        x3 = x2 + gate * pp
        if final_norm:
            o_ref[rows, :] = _rmsnorm_val(x3, gfin_ref[...], RMS_EPS)
        else:
            o_ref[rows, :] = x3


def _combine_call(dest_blocks, yb, x1, gates_t, p2d, g_ple, w_gate, w_proj, g_fin, final_norm):
    t, d = x1.shape
    tm = COMBINE_TM
    const = dict(pipeline_mode=pl.Buffered(1))
    return pl.pallas_call(
        functools.partial(_combine_kernel, final_norm=final_norm, n_steps=t // tm),
        grid=(t // tm,),
        in_specs=[
            pl.BlockSpec((1, TOP_K, tm), lambda i: (i, 0, 0), memory_space=pltpu.SMEM),
            pl.BlockSpec((1, TOP_K, tm), lambda i: (jnp.minimum(i + 1, t // tm - 1), 0, 0),
                         memory_space=pltpu.SMEM),
            pl.BlockSpec(memory_space=pl.ANY),
            pl.BlockSpec((tm, d), lambda i: (i, 0)),
            pl.BlockSpec((tm, SUBLANES), lambda i: (i, 0)),
            pl.BlockSpec((tm, p2d.shape[1]), lambda i: (i, 0)),
            pl.BlockSpec((1, d), lambda i: (0, 0)),
            pl.BlockSpec(w_gate.shape, lambda i: (0, 0), **const),
            pl.BlockSpec(w_proj.shape, lambda i: (0, 0), **const),
            pl.BlockSpec((1, d), lambda i: (0, 0)),
        ],
        out_specs=pl.BlockSpec((tm, d), lambda i: (i, 0)),
        out_shape=jax.ShapeDtypeStruct((t, d), F32),
        scratch_shapes=[pltpu.VMEM((2, TOP_K, tm, d), F32), pltpu.SemaphoreType.DMA((2, TOP_K))],
        compiler_params=_cparams(("arbitrary",)),
        name="combine",
    )(dest_blocks, dest_blocks, yb, x1, gates_t, p2d, g_ple, w_gate, w_proj, g_fin)


def _routing_plan(route, counts_f, t):
    counts = counts_f[:, 0].astype(jnp.int32)
    pcounts = ((counts + EXP_TM - 1) // EXP_TM) * EXP_TM
    pends = jnp.cumsum(pcounts)
    pstarts = pends - pcounts
    eid = route[0:2].astype(jnp.int32)
    rank = route[4:6].astype(jnp.int32)
    onehot = eid[..., None] == jnp.arange(N_EXPERTS, dtype=jnp.int32)
    dest = jnp.sum(jnp.where(onehot, pstarts, 0), axis=-1) + rank
    n_rows = TOP_K * t + N_EXPERTS * EXP_TM
    n_blocks = n_rows // EXP_TM
    blk_row = jnp.arange(n_blocks, dtype=jnp.int32) * EXP_TM
    blk_e = jnp.sum((pends[None, :] <= blk_row[:, None]).astype(jnp.int32), axis=1)
    blk_valid = (blk_row < pends[-1]).astype(jnp.int32)
    last_e = jnp.sum((pends <= pends[-1] - 1).astype(jnp.int32))
    blk_e = jnp.minimum(blk_e, last_e)
    zstart = jnp.concatenate([pstarts + counts, pends[-1:]]).astype(jnp.int32)
    zlen = jnp.concatenate([pcounts - counts, n_rows - pends[-1:]]).astype(jnp.int32)
    return dest, blk_e, blk_valid, zstart, zlen, n_rows


def _blocked(dest, tok_per_block):
    t = dest.shape[1]
    return dest.reshape(TOP_K, t // tok_per_block, tok_per_block).transpose(1, 0, 2)


def kernel(x, p, rel_bias, norm_mix_g, w_in, w_gate, lambda_q1, lambda_k1, lambda_q2, lambda_k2, subln_g,
           w_proj_a, w_proj_b, w_out, norm_ffn_g, w_coarse, w_fine, w1, w3, w2, norm_ple_g, w_ple_gate,
           w_ple_proj, final_norm_g):
    batch, seq, d = x.shape
    depth = w_in.shape[0]
    t = batch * seq
    assert seq % PERM_BLK == 0 and t % min(PROJ_TM, t) == 0
    assert all(seq // dil >= BLK and win // dil == BLK for win, dil in DILATED_GROUPS)

    nq = seq // ATT_BLK
    bias_a, w_cat0 = _bias_tiles_call(rel_bias, n_heads=A_HEADS, n_off=nq, tq=ATT_BLK, tk=ATT_BLK,
                                      off_mult=ATT_BLK, off_add=0, dil=1, max_rel=seq, head0=0, name="bias_diff",
                                      key_major=True, mult=A_HEAD_DIM ** 0.5, concat_cast=(w_gate[0], w_in[0]))
    bias_b = jnp.concatenate([
        _bias_tiles_call(rel_bias, n_heads=B_HEADS, n_off=1, tq=2 * BLK, tk=BLK, off_mult=0, off_add=BLK,
                         dil=dil, max_rel=win // dil, head0=A_HEADS + gi * B_HEADS, name="bias_dil%d" % gi,
                         key_major=True, mult=B_HEAD_DIM ** 0.5)[0].reshape(1, B_HEADS, 2 * BLK, BLK)
        for gi, (win, dil) in enumerate(DILATED_GROUPS)], axis=0)

    x2d = x.reshape(t, d)
    for layer in range(depth):
        lam_init = 0.8 - 0.6 * math.exp(-0.3 * layer)
        lam = _lam_call(lambda_q1[layer:layer + 1], lambda_k1[layer:layer + 1],
                        lambda_q2[layer:layer + 1], lambda_k2[layer:layer + 1], lam_init)
        if layer == 0 and w_cat0 is not None:
            w_cat = w_cat0
        else:
            w_cat = jnp.concatenate([w_gate[layer], w_in[layer]], axis=1).astype(BF16)
        zg = _inproj_call(x2d, norm_mix_g[layer:layer + 1], w_cat)
        n_exp, _, d_exp = w1[layer].shape
        oa, (w1b, w3b, w2b) = _diff_attn_call(
            zg, lam, bias_a, subln_g[layer:layer + 1],
            [w1[layer].reshape(n_exp * d, d_exp), w3[layer].reshape(n_exp * d, d_exp),
             w2[layer].reshape(n_exp * d_exp, d)],
            batch=batch, seq=seq, out_scale=1.0 - lam_init)
        ob = _dilated_call(zg, bias_b, batch=batch, seq=seq)

        wr_t = jnp.concatenate([
            w_coarse[layer].T,
            w_fine[layer].transpose(0, 2, 1).reshape(N_EXPERTS, d),
            jnp.zeros((LANES - N_EXPERT_GROUPS - N_EXPERTS, d), F32)], axis=0)
        wr_t = wr_t.T
        wr_hi = wr_t.astype(BF16)
        wr_t = jnp.stack([wr_hi, (wr_t - wr_hi.astype(F32)).astype(BF16)], axis=0)
        x1, h2, route, counts = _merge_router_call(
            oa, ob, zg, x2d, w_proj_a[layer].astype(BF16), w_proj_b[layer].astype(BF16),
            w_out[layer].astype(BF16), norm_ffn_g[layer:layer + 1], wr_t)

        dest, blk_e, blk_valid, zstart, zlen, n_rows = _routing_plan(route, counts, t)
        xin = _dispatch_call(_blocked(dest, DISPATCH_TOK), zstart, zlen, h2, n_rows)
        yb = _expert_call(blk_e, blk_valid, xin, w1b.reshape(n_exp, d, d_exp), w3b.reshape(n_exp, d, d_exp),
                          w2b.reshape(n_exp, d_exp, d))
        x2d = _combine_call(_blocked(dest, COMBINE_TM), yb, x1, route.T, p[layer].reshape(t, -1),
                            norm_ple_g[layer:layer + 1], w_ple_gate[layer].astype(BF16),
                            w_ple_proj[layer].astype(BF16), final_norm_g.reshape(1, d),
                            final_norm=layer == depth - 1)
    return x2d.reshape(batch, seq, d)
```
